```python
import jax
import jax.numpy as jnp
from jax import lax
import numpy as np

D_MODEL = 1024
BATCH = 2
SEQ = 8192
DEPTH = 2

GRID_W = 64
CTX_LEN = 256

N_EVEN = (DEPTH + 1) // 2
N_ODD = DEPTH // 2
DEEPNORM_ALPHA = (2.0 * DEPTH) ** 0.25
DEEPNORM_BETA = (8.0 * DEPTH) ** -0.25
LN_EPS = 1e-5
NEG_INF = -1e30
F32 = jnp.float32

NA_HEAD_DIM = 64
NA_WIDTH = D_MODEL // 2
NA_HEADS = NA_WIDTH // NA_HEAD_DIM
NA_WIN_ROWS = 8
NA_WIN_COLS = 16
NA_SCALE = NA_HEAD_DIM ** -0.5

RW_HEAD_DIM = 64
RW_WIDTH = D_MODEL // 2
RW_HEADS = RW_WIDTH // RW_HEAD_DIM
RW_DECAY_LORA = 32
RW_AAA_LORA = 32
RW_GATE_LORA = 96
RW_GN_EPS = 64e-5

ML_HEADS = 8
ML_V_DIM = D_MODEL // ML_HEADS
ML_QK_DIM = ML_V_DIM // 2
ML_WIDTH = ML_HEADS * ML_V_DIM
ML_CHUNK = 128
ML_NORM_EPS = 1e-6
ROPE_BASE = 10000.0

N_EXPERTS = 256
TOP_K = 8
N_GROUPS = 8
TOPK_GROUPS = 4
EXPERT_FF = 256
SHARED_FF = 256
ROUTED_SCALE = 2.5
MOE_BLOCK = 128

EVEN_LAYOUT = (
    ('na_q', NA_WIDTH), ('na_k', NA_WIDTH), ('na_v', NA_WIDTH),
    ('rw_r', RW_WIDTH), ('rw_k', RW_WIDTH), ('rw_v', RW_WIDTH),
    ('rw_wf', RW_DECAY_LORA), ('rw_wb', RW_DECAY_LORA),
    ('rw_af', RW_AAA_LORA), ('rw_ab', RW_AAA_LORA), ('rw_g', RW_GATE_LORA),
)
EVEN_COLS = sum(w for _, w in EVEN_LAYOUT)
RW_SHIFT_COLS = sum(w for n, w in EVEN_LAYOUT if n.startswith('rw_'))
ODD_LAYOUT = (
    ('ml_q', ML_HEADS * ML_QK_DIM), ('ml_k', ML_HEADS * ML_QK_DIM),
    ('ml_v', ML_WIDTH), ('ml_o', ML_WIDTH),
    ('ml_if', ML_HEADS), ('ml_ib', ML_HEADS), ('ml_ff', ML_HEADS), ('ml_fb', ML_HEADS),
)
ODD_COLS = sum(w for _, w in ODD_LAYOUT)
EVEN_CTX_STATE_COLS = ('na_k', 'na_v', 'rw_k', 'rw_v', 'rw_wf', 'rw_wb', 'rw_af', 'rw_ab')
ODD_CTX_STATE_COLS = ('ml_k', 'ml_v', 'ml_if', 'ml_ib', 'ml_ff', 'ml_fb')

kernel_name = 'hybrid_natten_rwkv7_mlstm_moe_dit'


def _offsets(layout, prefix=''):
    offs, o = {}, 0
    for name, width in layout:
        if name.startswith(prefix):
            offs[name] = (o, width)
            o += width
    return offs


def project(h, w, layout, names):
    offs = _offsets(layout)
    if len(names) == len(layout):
        y = jnp.einsum('btd,de->bte', h, w)
        return {n: y[..., offs[n][0]:offs[n][0] + offs[n][1]] for n in names}
    return {n: jnp.einsum('btd,de->bte', h, w[:, offs[n][0]:offs[n][0] + offs[n][1]]) for n in names}


def layer_norm(x, g, b):
    xf = x.astype(F32)
    mu = xf.mean(-1, keepdims=True)
    var = jnp.square(xf - mu).mean(-1, keepdims=True)
    return ((xf - mu) * lax.rsqrt(var + LN_EPS) * g + b).astype(x.dtype)


def modulate(h, shift, scale):
    return h * (1.0 + scale) + shift


def centred_shift(p, mu):
    zero = jnp.zeros_like(p[:, :1])
    prev = jnp.concatenate([zero, p[:, :-1]], 1)
    nxt = jnp.concatenate([p[:, 1:], zero], 1)
    return p + mu * (0.5 * (prev + nxt) - p)


def axial_rope(z):
    T, dh = z.shape[1], z.shape[-1]
    half = dh // 2
    nf = half // 2
    t = jnp.arange(T)
    row = (t // GRID_W).astype(F32)
    col = (t % GRID_W).astype(F32)
    inv = ROPE_BASE ** (-jnp.arange(nf, dtype=F32) / nf)

    def rot(u, pos):
        ang = pos[:, None] * inv[None, :]
        cos = jnp.cos(ang)[None, :, None, :]
        sin = jnp.sin(ang)[None, :, None, :]
        u1, u2 = u[..., :nf], u[..., nf:]
        return jnp.concatenate([u1 * cos - u2 * sin, u1 * sin + u2 * cos], -1)

    return jnp.concatenate([rot(z[..., :half], row), rot(z[..., half:], col)], -1).astype(z.dtype)


def neighbourhood_attention(q, k, v, k_ctx, v_ctx, rpb):
    B, N, H, dh = q.shape
    rows = N // GRID_W
    kh = min(NA_WIN_ROWS, rows)
    kw = NA_WIN_COLS
    grid = lambda z: z.reshape(B, rows, GRID_W, H, dh)
    qg, kg, vg = grid(q), grid(k), grid(v)
    r_idx = jnp.arange(rows)
    row_start = jnp.clip(r_idx - kh // 2, 0, rows - kh)
    band_rows = row_start[:, None] + jnp.arange(kh)[None, :]
    k_band = kg[:, band_rows]
    v_band = vg[:, band_rows]
    j_idx = jnp.arange(GRID_W)
    col_start = jnp.clip(j_idx - kw // 2, 0, GRID_W - kw)
    col_in = (j_idx[None, :] >= col_start[:, None]) & (j_idx[None, :] < col_start[:, None] + kw)
    row_off = band_rows - r_idx[:, None] + (NA_WIN_ROWS - 1)
    col_off = jnp.clip(j_idx[None, :] - j_idx[:, None], -(kw - 1), kw - 1) + (kw - 1)
    bias = rpb.astype(F32)[:, row_off[:, None, :, None], col_off[None, :, None, :]]
    bias = jnp.where(col_in[None, None, :, None, :], bias, NEG_INF)
    s_loc = jnp.einsum('brjhd,brachd->bhrjac', qg, k_band).astype(F32) + bias[None]
    s_ctx = jnp.einsum('brjhd,bchd->bhrjc', qg, k_ctx).astype(F32)
    n_loc = kh * GRID_W
    p = jax.nn.softmax(jnp.concatenate([s_loc.reshape(B, H, rows, GRID_W, n_loc), s_ctx], -1), axis=-1).astype(v.dtype)
    p_loc = p[..., :n_loc].reshape(B, H, rows, GRID_W, kh, GRID_W)
    out = (jnp.einsum('bhrjac,brachd->brjhd', p_loc, v_band)
           + jnp.einsum('bhrjc,bchd->brjhd', p[..., n_loc:], v_ctx))
    return out.reshape(B, N, H * dh)


def ctx_attention(q, k, v):
    s = jnp.einsum('bqhd,bkhd->bhqk', q, k).astype(F32)
    p = jax.nn.softmax(s, axis=-1).astype(v.dtype)
    return jnp.einsum('bhqk,bkhd->bqhd', p, v)


def rwkv_prep(t, w0, w_up, a0, a_up, k_k, k_a):
    B, T = t['rw_k'].shape[:2]
    heads = lambda z: z.reshape(B, T, RW_HEADS, RW_HEAD_DIM).astype(F32)
    k = t['rw_k'].astype(F32)
    kk = heads(k * k_k)
    kk = kk / jnp.maximum(jnp.linalg.norm(kk, axis=-1, keepdims=True), 1e-12)
    dirs = []
    for d, (wn, an) in enumerate((('rw_wf', 'rw_af'), ('rw_wb', 'rw_ab'))):
        w_log = -jax.nn.softplus(-(w0[d] + jnp.tanh(t[wn].astype(F32)) @ w_up[d])) - 0.5
        decay = jnp.exp(-jnp.exp(w_log))
        a = jax.nn.sigmoid(a0[d] + t[an].astype(F32) @ a_up[d])
        k_d = k * (1.0 + (a - 1.0) * k_a)
        dirs.append((heads(decay), heads(a), heads(k_d)))
    return kk, heads(t['rw_v']), dirs


def rwkv_scan(s0, decay, a, k, kk, v, r, reverse):
    seq = lambda z: jnp.moveaxis(z, 1, 0)
    xs = (seq(decay), seq(-kk), seq(kk * a), seq(k), seq(v))
    if r is not None:
        xs = xs + (seq(r),)

    def step(s, inp):
        s = (s * inp[0][:, :, None, :]
             + jnp.einsum('bhvk,bhk->bhv', s, inp[1])[..., None] * inp[2][:, :, None, :]
             + inp[4][..., None] * inp[3][:, :, None, :])
        return s, (jnp.einsum('bhvk,bhk->bhv', s, inp[5]) if len(inp) == 6 else None)

    s, ys = lax.scan(step, s0, xs, reverse=reverse)
    return s, (None if r is None else jnp.moveaxis(ys, 0, 1))


def rwkv_readout(y, r, v, dirs, g_low, r_k, g_up, gn_g, gn_b):
    B, T = y.shape[:2]
    mu = y.mean(-1, keepdims=True)
    var = jnp.square(y - mu).mean(-1, keepdims=True)
    yn = ((y - mu) * lax.rsqrt(var + RW_GN_EPS) * gn_g.reshape(RW_HEADS, RW_HEAD_DIM)
          + gn_b.reshape(RW_HEADS, RW_HEAD_DIM))
    bonus = (jnp.sum(r * dirs[0][2] * r_k, -1, keepdims=True)
             + jnp.sum(r * dirs[1][2] * r_k, -1, keepdims=True)) * v
    gate = jax.nn.sigmoid(g_low.astype(F32)) @ g_up
    return (yn + bonus).reshape(B, T, RW_WIDTH) * gate


def ml_prep(t, gate_b, rope, need_q):
    B, T = t['ml_k'].shape[:2]
    heads = lambda z, dh: z.reshape(B, T, ML_HEADS, dh).astype(F32)
    k = heads(t['ml_k'], ML_QK_DIM)
    q = heads(t['ml_q'], ML_QK_DIM) * ML_QK_DIM ** -0.5 if need_q else None
    if rope:
        k = axial_rope(k)
        q = axial_rope(q)
    v = heads(t['ml_v'], ML_V_DIM)
    gb = gate_b.astype(F32)
    bht = lambda z: z.astype(F32).transpose(0, 2, 1)
    ig = (bht(t['ml_if'] + gb[0]), bht(t['ml_ib'] + gb[1]))
    lf = (jax.nn.log_sigmoid(bht(t['ml_ff'] + gb[2])), jax.nn.log_sigmoid(bht(t['ml_fb'] + gb[3])))
    bhtd = lambda z: None if z is None else z.transpose(0, 2, 1, 3)
    return bhtd(q), bhtd(k), bhtd(v), ig, lf


def ml_chunk_states(k, v, ig, lf, state0):
    B, H, T, dk = k.shape
    dv = v.shape[-1]
    L = min(ML_CHUNK, T)
    nc = T // L
    kc = k.reshape(B, H, nc, L, dk)
    vc = v.reshape(B, H, nc, L, dv)
    b = jnp.cumsum(lf.reshape(B, H, nc, L), -1)
    b_end = b[..., -1]
    g = b_end[..., None] - b + ig.reshape(B, H, nc, L)
    m_chunk = g.max(-1)
    wgt = jnp.exp(g - m_chunk[..., None])
    kv = jnp.einsum('bhnl,bhnlk,bhnlv->bhnkv', wgt, kc, vc)
    ks = jnp.einsum('bhnl,bhnlk->bhnk', wgt, kc)

    def step(state, inp):
        c_mem, n_mem, m = state
        be, mc, kv_n, ks_n = inp
        m_new = jnp.maximum(be + m, mc)
        fa = jnp.exp(be + m - m_new)
        fb = jnp.exp(mc - m_new)
        c_new = fa[..., None, None] * c_mem + fb[..., None, None] * kv_n
        n_new = fa[..., None] * n_mem + fb[..., None] * ks_n
        return (c_new, n_new, m_new), state

    xs = tuple(jnp.moveaxis(z, 2, 0) for z in (b_end, m_chunk, kv, ks))
    final, starts = lax.scan(step, state0, xs)
    return tuple(jnp.moveaxis(z, 0, 2) for z in starts), final


def ml_chunk_outputs(q, k, v, ig, lf, starts):
    B, H, T, dk = q.shape
    dv = v.shape[-1]
    L = min(ML_CHUNK, T)
    nc = T // L
    qc = q.reshape(B, H, nc, L, dk)
    kc = k.reshape(B, H, nc, L, dk)
    vc = v.reshape(B, H, nc, L, dv)
    b = jnp.cumsum(lf.reshape(B, H, nc, L), -1)
    c0, n0, m0 = starts
    causal = jnp.tril(jnp.ones((L, L), bool))
    dlog = jnp.where(causal, b[..., :, None] - b[..., None, :] + ig.reshape(B, H, nc, L)[..., None, :], NEG_INF)
    inter = b + m0[..., None]
    m = jnp.maximum(dlog.max(-1), inter)
    dw = jnp.exp(dlog - m[..., None])
    iw = jnp.exp(inter - m)
    s = jnp.einsum('bhntd,bhnsd->bhnts', qc, kc) * dw
    num = jnp.einsum('bhnts,bhnsv->bhntv', s, vc) + iw[..., None] * jnp.einsum('bhntd,bhndv->bhntv', qc, c0)
    den = s.sum(-1) + iw * jnp.einsum('bhntd,bhnd->bhnt', qc, n0)
    h = num / jnp.maximum(jnp.abs(den), jnp.exp(-m))[..., None]
    return h.reshape(B, H, T, dv)


def ml_readout(h, o, norm_g):
    B, H, T, dv = h.shape
    hn = h * lax.rsqrt(jnp.mean(h * h, -1, keepdims=True) + ML_NORM_EPS)
    hn = hn.transpose(0, 2, 1, 3).reshape(B, T, H * dv) * norm_g
    return hn * jax.nn.sigmoid(o.astype(F32))


def even_mixer(a_lat, a_ctx, w_in, w_out, rpb, mu, w0, w_up, a0, a_up, g_up, k_k, k_a, r_k, gn_g, gn_b, need_ctx):
    names = tuple(n for n, _ in EVEN_LAYOUT)
    shift_offs = _offsets(EVEN_LAYOUT, 'rw_')

    def prep(h, cols):
        t = project(h, w_in, EVEN_LAYOUT, cols)
        for n in cols:
            if n in shift_offs:
                o, wd = shift_offs[n]
                t[n] = centred_shift(t[n], mu[o:o + wd])
        return t

    t_lat = prep(a_lat, names)
    t_ctx = prep(a_ctx, names if need_ctx else EVEN_CTX_STATE_COLS)
    B, N = a_lat.shape[:2]
    C = a_ctx.shape[1]
    na_h = lambda z: z.reshape(z.shape[0], z.shape[1], NA_HEADS, NA_HEAD_DIM)
    rw_h = lambda z: z.reshape(z.shape[0], z.shape[1], RW_HEADS, RW_HEAD_DIM).astype(F32)
    k_c, v_c = na_h(t_ctx['na_k']), na_h(t_ctx['na_v'])
    na_lat = neighbourhood_attention(na_h(t_lat['na_q']) * NA_SCALE, na_h(t_lat['na_k']),
                                     na_h(t_lat['na_v']), k_c, v_c, rpb)
    kk_c, vr_c, dirs_c = rwkv_prep(t_ctx, w0, w_up, a0, a_up, k_k, k_a)
    kk_l, vr_l, dirs_l = rwkv_prep(t_lat, w0, w_up, a0, a_up, k_k, k_a)
    r_l = rw_h(t_lat['rw_r'])
    r_c = rw_h(t_ctx['rw_r']) if need_ctx else None
    s0 = jnp.zeros((B, RW_HEADS, RW_HEAD_DIM, RW_HEAD_DIM), F32)
    y_l, y_c = [], []
    for d in range(2):
        s_ctx, yc = rwkv_scan(s0, *dirs_c[d], kk_c, vr_c, r_c, d == 1)
        _, yl = rwkv_scan(s_ctx, *dirs_l[d], kk_l, vr_l, r_l, d == 1)
        y_l.append(yl)
        y_c.append(yc)
    rw_lat = rwkv_readout(y_l[0] + y_l[1], r_l, vr_l, dirs_l, t_lat['rw_g'], r_k, g_up, gn_g, gn_b)
    y_lat = jnp.einsum('btd,de->bte', jnp.concatenate([na_lat.astype(F32), rw_lat], -1), w_out).astype(a_lat.dtype)
    if not need_ctx:
        return y_lat, None
    na_ctx = ctx_attention(na_h(t_ctx['na_q']) * NA_SCALE, k_c, v_c).reshape(B, C, NA_WIDTH)
    rw_ctx = rwkv_readout(y_c[0] + y_c[1], r_c, vr_c, dirs_c, t_ctx['rw_g'], r_k, g_up, gn_g, gn_b)
    y_ctx = jnp.einsum('btd,de->bte', jnp.concatenate([na_ctx.astype(F32), rw_ctx], -1), w_out).astype(a_ctx.dtype)
    return y_lat, y_ctx


def odd_mixer(a_lat, a_ctx, w_in, w_out, gate_b, norm_g, need_ctx):
    names = tuple(n for n, _ in ODD_LAYOUT)
    t_lat = project(a_lat, w_in, ODD_LAYOUT, names)
    t_ctx = project(a_ctx, w_in, ODD_LAYOUT, names if need_ctx else ODD_CTX_STATE_COLS)
    q_l, k_l, v_l, ig_l, lf_l = ml_prep(t_lat, gate_b, True, True)
    q_c, k_c, v_c, ig_c, lf_c = ml_prep(t_ctx, gate_b, False, need_ctx)
    B = a_lat.shape[0]
    zero = (jnp.zeros((B, ML_HEADS, ML_QK_DIM, ML_V_DIM), F32),
            jnp.zeros((B, ML_HEADS, ML_QK_DIM), F32),
            jnp.zeros((B, ML_HEADS), F32))
    h_l, h_c = [], []
    for d in range(2):
        f = (lambda z: jnp.flip(z, 2)) if d == 1 else (lambda z: z)
        starts_c, final_c = ml_chunk_states(f(k_c), f(v_c), f(ig_c[d]), f(lf_c[d]), zero)
        starts_l, _ = ml_chunk_states(f(k_l), f(v_l), f(ig_l[d]), f(lf_l[d]), final_c)
        h_l.append(f(ml_chunk_outputs(f(q_l), f(k_l), f(v_l), f(ig_l[d]), f(lf_l[d]), starts_l)))
        if need_ctx:
            h_c.append(f(ml_chunk_outputs(f(q_c), f(k_c), f(v_c), f(ig_c[d]), f(lf_c[d]), starts_c)))
    y_lat = jnp.einsum('btd,de->bte', ml_readout(h_l[0] + h_l[1], t_lat['ml_o'], norm_g), w_out).astype(a_lat.dtype)
    if not need_ctx:
        return y_lat, None
    y_ctx = jnp.einsum('btd,de->bte', ml_readout(h_c[0] + h_c[1], t_ctx['ml_o'], norm_g), w_out).astype(a_ctx.dtype)
    return y_lat, y_ctx


def swiglu(h, wg, wu, wd):
    return jnp.dot(jax.nn.silu(jnp.dot(h, wg)) * jnp.dot(h, wu), wd)


def moe_ffn(h, router_w, router_b, wg, wu, wd, sg, su, sd):
    T, D = h.shape
    E = router_w.shape[-1]
    s = jax.nn.sigmoid(jnp.dot(h, router_w).astype(F32))
    grp = (s + router_b.astype(F32)).reshape(T, N_GROUPS, E // N_GROUPS)
    g_score = lax.top_k(grp, 2)[0].sum(-1)
    g_keep = jax.nn.one_hot(lax.top_k(g_score, TOPK_GROUPS)[1], N_GROUPS).sum(1) > 0
    choice = jnp.where(g_keep[:, :, None], grp, NEG_INF).reshape(T, E)
    e_idx = lax.top_k(choice, TOP_K)[1]
    w_sel = jnp.take_along_axis(s, e_idx, 1)
    w_sel = w_sel / w_sel.sum(-1, keepdims=True) * ROUTED_SCALE
    n_asg = T * TOP_K
    flat_e = e_idx.reshape(-1)
    order = jnp.argsort(flat_e)
    se = flat_e[order]
    st = (order // TOP_K).astype(jnp.int32)
    sw = w_sel.reshape(-1)[order]
    counts = jnp.bincount(flat_e, length=E)
    padded = (counts + MOE_BLOCK - 1) // MOE_BLOCK * MOE_BLOCK
    ends_pad = jnp.cumsum(padded)
    pos = (ends_pad - padded)[se] + jnp.arange(n_asg) - (jnp.cumsum(counts) - counts)[se]
    n_blocks = -(-n_asg // MOE_BLOCK) + E
    buf_tok = jnp.full((n_blocks * MOE_BLOCK,), T, jnp.int32).at[pos].set(st)
    buf_w = jnp.zeros((n_blocks * MOE_BLOCK,), F32).at[pos].set(sw)
    blk_e = jnp.minimum(jnp.searchsorted(ends_pad, jnp.arange(n_blocks) * MOE_BLOCK, side='right'), E - 1)
    h_pad = jnp.concatenate([h, jnp.zeros((1, D), h.dtype)], 0)

    def block(acc, blk):
        tok, wt, e = blk
        y = swiglu(h_pad[tok], wg[e], wu[e], wd[e]) * wt[:, None]
        return acc.at[tok].add(y.astype(acc.dtype)), None

    acc, _ = lax.scan(block, jnp.zeros((T + 1, D), h.dtype),
                      (buf_tok.reshape(n_blocks, MOE_BLOCK), buf_w.reshape(n_blocks, MOE_BLOCK), blk_e))
    return acc[:T] + swiglu(h, sg, su, sd)


def setup_inputs(seed: int = 0) -> dict:
    key = jax.random.key(seed)
    ks = iter(jax.random.split(key, 48))

    def nrm(shape, scale):
        return jax.random.normal(next(ks), shape, F32) * scale

    D = D_MODEL
    MIX_E = NA_WIDTH + RW_WIDTH
    return {
        'x': nrm((BATCH, SEQ, D), 1.0),
        'c': nrm((BATCH, D), 1.0),
        'ctx': nrm((BATCH, CTX_LEN, D), 1.0),
        'c_ctx': nrm((D,), 1.0),
        'ada_w': nrm((DEPTH, D, 6 * D), 0.5 * D ** -0.5),
        'ada_b': nrm((DEPTH, 6 * D), 0.02),
        'ln_g': 1.0 + nrm((DEPTH, 2, D), 0.02),
        'ln_b': nrm((DEPTH, 2, D), 0.02),
        'ev_w_in': nrm((N_EVEN, D, EVEN_COLS), D ** -0.5),
        'ev_w_out': nrm((N_EVEN, MIX_E, D), DEEPNORM_BETA * MIX_E ** -0.5),
        'na_rpb': nrm((N_EVEN, NA_HEADS, 2 * NA_WIN_ROWS - 1, 2 * NA_WIN_COLS - 1), 0.5),
        'rw_mu': jax.random.uniform(next(ks), (N_EVEN, RW_SHIFT_COLS), F32),
        'rw_w0': nrm((N_EVEN, 2, RW_WIDTH), 1.0) - 2.0,
        'rw_w_up': nrm((N_EVEN, 2, RW_DECAY_LORA, RW_WIDTH), 0.1),
        'rw_a0': nrm((N_EVEN, 2, RW_WIDTH), 0.5),
        'rw_a_up': nrm((N_EVEN, 2, RW_AAA_LORA, RW_WIDTH), RW_AAA_LORA ** -0.5),
        'rw_g_up': nrm((N_EVEN, RW_GATE_LORA, RW_WIDTH), RW_GATE_LORA ** -0.5),
        'rw_k_k': 0.85 + nrm((N_EVEN, RW_WIDTH), 0.05),
        'rw_k_a': 1.0 + nrm((N_EVEN, RW_WIDTH), 0.05),
        'rw_r_k': nrm((N_EVEN, RW_HEADS, RW_HEAD_DIM), 0.1),
        'rw_gn_g': 1.0 + nrm((N_EVEN, RW_WIDTH), 0.02),
        'rw_gn_b': nrm((N_EVEN, RW_WIDTH), 0.02),
        'od_w_in': nrm((N_ODD, D, ODD_COLS), D ** -0.5),
        'od_w_out': nrm((N_ODD, ML_WIDTH, D), DEEPNORM_BETA * ML_WIDTH ** -0.5),
        'ml_gate_b': jnp.concatenate([nrm((N_ODD, 2, ML_HEADS), 0.1),
                                      3.0 + nrm((N_ODD, 2, ML_HEADS), 0.5)], axis=1),
        'ml_norm_g': 1.0 + nrm((N_ODD, ML_WIDTH), 0.02),
        'moe_router': nrm((DEPTH, D, N_EXPERTS), D ** -0.5),
        'moe_bias': nrm((DEPTH, N_EXPERTS), 0.01),
        'moe_w_gate': nrm((DEPTH, N_EXPERTS, D, EXPERT_FF), D ** -0.5),
        'moe_w_up': nrm((DEPTH, N_EXPERTS, D, EXPERT_FF), D ** -0.5),
        'moe_w_down': nrm((DEPTH, N_EXPERTS, EXPERT_FF, D), DEEPNORM_BETA * EXPERT_FF ** -0.5),
        'sh_w_gate': nrm((DEPTH, D, SHARED_FF), D ** -0.5),
        'sh_w_up': nrm((DEPTH, D, SHARED_FF), D ** -0.5),
        'sh_w_down': nrm((DEPTH, SHARED_FF, D), DEEPNORM_BETA * SHARED_FF ** -0.5),
    }


def reference(x, c, ctx, c_ctx, ada_w, ada_b, ln_g, ln_b, ev_w_in, ev_w_out, na_rpb, rw_mu, rw_w0, rw_w_up,
              rw_a0, rw_a_up, rw_g_up, rw_k_k, rw_k_a, rw_r_k, rw_gn_g, rw_gn_b, od_w_in, od_w_out, ml_gate_b,
              ml_norm_g, moe_router, moe_bias, moe_w_gate, moe_w_up, moe_w_down, sh_w_gate, sh_w_up, sh_w_down):
    B, N, D = x.shape
    C = ctx.shape[1]
    silu_c = jax.nn.silu(c)
    silu_cc = jax.nn.silu(c_ctx)
    h_lat, h_ctx = x, ctx
    for l in range(DEPTH):
        last = l == DEPTH - 1
        mods = jnp.split(silu_c @ ada_w[l] + ada_b[l], 6, axis=-1)
        n_cm = 2 if last else 6
        mods_c = jnp.split(silu_cc @ ada_w[l][:, :n_cm * D] + ada_b[l][:n_cm * D], n_cm, axis=-1)
        a_lat = modulate(h_lat, mods[0][:, None], mods[1][:, None])
        a_ctx = modulate(h_ctx, mods_c[0], mods_c[1])
        if l % 2 == 0:
            e = l // 2
            y_lat, y_ctx = even_mixer(a_lat, a_ctx, ev_w_in[e], ev_w_out[e], na_rpb[e], rw_mu[e], rw_w0[e],
                                      rw_w_up[e], rw_a0[e], rw_a_up[e], rw_g_up[e], rw_k_k[e], rw_k_a[e],
                                      rw_r_k[e], rw_gn_g[e], rw_gn_b[e], not last)
        else:
            o = l // 2
            y_lat, y_ctx = odd_mixer(a_lat, a_ctx, od_w_in[o], od_w_out[o], ml_gate_b[o], ml_norm_g[o], not last)
        h_lat = layer_norm(DEEPNORM_ALPHA * h_lat + mods[2][:, None] * y_lat, ln_g[l, 0], ln_b[l, 0])
        f_lat = modulate(h_lat, mods[3][:, None], mods[4][:, None]).reshape(B * N, D)
        moe_args = (moe_router[l], moe_bias[l], moe_w_gate[l], moe_w_up[l], moe_w_down[l],
                    sh_w_gate[l], sh_w_up[l], sh_w_down[l])
        if last:
            ff = moe_ffn(f_lat, *moe_args).reshape(B, N, D)
            h_lat = layer_norm(DEEPNORM_ALPHA * h_lat + mods[5][:, None] * ff, ln_g[l, 1], ln_b[l, 1])
        else:
            h_ctx = layer_norm(DEEPNORM_ALPHA * h_ctx + mods_c[2] * y_ctx, ln_g[l, 0], ln_b[l, 0])
            f_ctx = modulate(h_ctx, mods_c[3], mods_c[4]).reshape(B * C, D)
            ff = moe_ffn(jnp.concatenate([f_lat, f_ctx], 0), *moe_args)
            h_lat = layer_norm(DEEPNORM_ALPHA * h_lat + mods[5][:, None] * ff[:B * N].reshape(B, N, D),
                               ln_g[l, 1], ln_b[l, 1])
            h_ctx = layer_norm(DEEPNORM_ALPHA * h_ctx + mods_c[5] * ff[B * N:].reshape(B, C, D),
                               ln_g[l, 1], ln_b[l, 1])
    return h_lat
```

```python
import functools

import jax
import jax.numpy as jnp
from jax import lax
from jax.experimental import pallas as pl
from jax.experimental.pallas import tpu as pltpu

D_MODEL = 1024
DEPTH = 2
GRID_W = 64

DEEPNORM_ALPHA = (2.0 * DEPTH) ** 0.25
LN_EPS = 1e-5
NEG_INF = -1e30
F32 = jnp.float32
BF16 = jnp.bfloat16

NA_HEAD_DIM = 64
NA_WIDTH = D_MODEL // 2
NA_HEADS = NA_WIDTH // NA_HEAD_DIM
NA_WIN_ROWS = 8
NA_WIN_COLS = 16
NA_SCALE = NA_HEAD_DIM ** -0.5

RW_HEAD_DIM = 64
RW_WIDTH = D_MODEL // 2
RW_HEADS = RW_WIDTH // RW_HEAD_DIM
RW_DECAY_LORA = 32
RW_AAA_LORA = 32
RW_GATE_LORA = 96
RW_GN_EPS = 64e-5

ML_HEADS = 8
ML_V_DIM = D_MODEL // ML_HEADS
ML_QK_DIM = ML_V_DIM // 2
ML_WIDTH = ML_HEADS * ML_V_DIM
ML_CHUNK = 128
ML_NORM_EPS = 1e-6
ROPE_BASE = 10000.0

N_EXPERTS = 256
TOP_K = 8
N_GROUPS = 8
TOPK_GROUPS = 4
EXPERT_FF = 256
ROUTED_SCALE = 2.5
MOE_BLOCK = 128

EVEN_LAYOUT = (
    ('na_q', NA_WIDTH), ('na_k', NA_WIDTH), ('na_v', NA_WIDTH),
    ('rw_r', RW_WIDTH), ('rw_k', RW_WIDTH), ('rw_v', RW_WIDTH),
    ('rw_wf', RW_DECAY_LORA), ('rw_wb', RW_DECAY_LORA),
    ('rw_af', RW_AAA_LORA), ('rw_ab', RW_AAA_LORA), ('rw_g', RW_GATE_LORA),
)
ODD_LAYOUT = (
    ('ml_q', ML_HEADS * ML_QK_DIM), ('ml_k', ML_HEADS * ML_QK_DIM),
    ('ml_v', ML_WIDTH), ('ml_o', ML_WIDTH),
    ('ml_if', ML_HEADS), ('ml_ib', ML_HEADS), ('ml_ff', ML_HEADS), ('ml_fb', ML_HEADS),
)
EVEN_CTX_STATE_COLS = ('na_k', 'na_v', 'rw_k', 'rw_v', 'rw_wf', 'rw_wb', 'rw_af', 'rw_ab')
ODD_CTX_STATE_COLS = ('ml_k', 'ml_v', 'ml_if', 'ml_ib', 'ml_ff', 'ml_fb')

SUBLANES = 8
LANES = 128
VMEM_LIMIT_BYTES = 48 * 1024 * 1024


RW_SCAN_CHAINS = 8
RW_SCAN_TIME = 256


def _segment_sum(x, ones):
    hi = x.astype(BF16)
    lo = (x - hi.astype(F32)).astype(BF16)
    return (jnp.dot(hi, ones, preferred_element_type=F32)
            + jnp.dot(lo, ones, preferred_element_type=F32))


def _rwkv_scan_kernel(w_ref, nkk_ref, beta_ref, kd_ref, v_ref, r_ref, y_ref, s_ref):
    @pl.when(pl.program_id(1) == 0)
    def _():
        s_ref[...] = jnp.zeros_like(s_ref)

    n_chain, n_time = w_ref.shape[0], w_ref.shape[1]
    row = lax.broadcasted_iota(jnp.int32, (LANES, LANES), 0)
    col = lax.broadcasted_iota(jnp.int32, (LANES, LANES), 1)
    ones = ((row // RW_HEAD_DIM) == (col // RW_HEAD_DIM)).astype(BF16)
    vi = lax.broadcasted_iota(jnp.int32, (RW_HEAD_DIM, LANES), 0)
    li = lax.broadcasted_iota(jnp.int32, (RW_HEAD_DIM, LANES), 1)
    diag = (li % RW_HEAD_DIM) == vi

    def sub(i, carry):
        off = pl.multiple_of(i * SUBLANES, SUBLANES)
        for g in range(n_chain):
            sl = (g, pl.ds(off, SUBLANES), slice(None))
            w8, nkk8, beta8 = w_ref[sl], nkk_ref[sl], beta_ref[sl]
            kd8, v8, r8 = kd_ref[sl], v_ref[sl], r_ref[sl]
            s = s_ref[g]
            rows = []
            for t in range(SUBLANES):
                rt = slice(t, t + 1)
                vcol = _segment_sum(jnp.where(diag, v8[rt], 0.0), ones)
                sa = _segment_sum(s * nkk8[rt], ones)
                s = s * w8[rt] + sa * beta8[rt] + vcol * kd8[rt]
                ybc = _segment_sum(s * r8[rt], ones)
                rows.append(jnp.sum(jnp.where(diag, ybc, 0.0), axis=0, keepdims=True))
            s_ref[g] = s
            y_ref[sl] = jnp.concatenate(rows, axis=0)
        return carry

    lax.fori_loop(0, n_time // SUBLANES, sub, 0)


def rwkv_scan_pallas(w, nkk, beta, kd, v, r):
    G, T, _ = w.shape
    gc = RW_SCAN_CHAINS if G % RW_SCAN_CHAINS == 0 else G
    tc = RW_SCAN_TIME if T % RW_SCAN_TIME == 0 else T
    assert T % SUBLANES == 0
    spec = pl.BlockSpec((gc, tc, LANES), lambda g, t: (g, t, 0))
    return pl.pallas_call(
        _rwkv_scan_kernel,
        grid=(G // gc, T // tc),
        in_specs=[spec] * 6,
        out_specs=spec,
        out_shape=jax.ShapeDtypeStruct((G, T, LANES), F32),
        scratch_shapes=[pltpu.VMEM((gc, RW_HEAD_DIM, LANES), F32)],
        compiler_params=pltpu.CompilerParams(
            dimension_semantics=("arbitrary", "arbitrary"), vmem_limit_bytes=VMEM_LIMIT_BYTES),
        name="rwkv_scan",
    )(w, nkk, beta, kd, v, r)


def _expert_block_kernel(blk_e_ref, x_ref, wt_ref, wg_ref, wu_ref, wd_ref, o_ref):
    del blk_e_ref
    x = x_ref[...]
    g = jnp.dot(x, wg_ref[0].astype(BF16), preferred_element_type=F32)
    u = jnp.dot(x, wu_ref[0].astype(BF16), preferred_element_type=F32)
    mid = (g * jax.nn.sigmoid(g) * u).astype(BF16)
    y = jnp.dot(mid, wd_ref[0].astype(BF16), preferred_element_type=F32)
    o_ref[...] = y * wt_ref[...]


def expert_blocks_pallas(xs, ws, blk_e, wg, wu, wd):
    n_rows, D = xs.shape
    nb = n_rows // MOE_BLOCK
    F = wg.shape[-1]
    grid_spec = pltpu.PrefetchScalarGridSpec(
        num_scalar_prefetch=1,
        grid=(nb,),
        in_specs=[
            pl.BlockSpec((MOE_BLOCK, D), lambda i, e: (i, 0)),
            pl.BlockSpec((MOE_BLOCK, 1), lambda i, e: (i, 0)),
            pl.BlockSpec((1, D, F), lambda i, e: (e[i], 0, 0)),
            pl.BlockSpec((1, D, F), lambda i, e: (e[i], 0, 0)),
            pl.BlockSpec((1, F, D), lambda i, e: (e[i], 0, 0)),
        ],
        out_specs=pl.BlockSpec((MOE_BLOCK, D), lambda i, e: (i, 0)),
    )
    return pl.pallas_call(
        _expert_block_kernel,
        grid_spec=grid_spec,
        out_shape=jax.ShapeDtypeStruct((n_rows, D), F32),
        compiler_params=pltpu.CompilerParams(
            dimension_semantics=("arbitrary",), vmem_limit_bytes=VMEM_LIMIT_BYTES),
        name="moe_experts",
    )(blk_e, xs, ws, wg, wu, wd)


def _offsets(layout, prefix=''):
    offs, o = {}, 0
    for name, width in layout:
        if name.startswith(prefix):
            offs[name] = (o, width)
            o += width
    return offs


def project(h, w, layout, names):
    offs = _offsets(layout)
    if len(names) == len(layout):
        y = jnp.einsum('btd,de->bte', h, w)
        return {n: y[..., offs[n][0]:offs[n][0] + offs[n][1]] for n in names}
    return {n: jnp.einsum('btd,de->bte', h, w[:, offs[n][0]:offs[n][0] + offs[n][1]]) for n in names}


def layer_norm(x, g, b):
    xf = x.astype(F32)
    mu = xf.mean(-1, keepdims=True)
    var = jnp.square(xf - mu).mean(-1, keepdims=True)
    return ((xf - mu) * lax.rsqrt(var + LN_EPS) * g + b).astype(x.dtype)


def modulate(h, shift, scale):
    return h * (1.0 + scale) + shift


def centred_shift(p, mu):
    zero = jnp.zeros_like(p[:, :1])
    prev = jnp.concatenate([zero, p[:, :-1]], 1)
    nxt = jnp.concatenate([p[:, 1:], zero], 1)
    return p + mu * (0.5 * (prev + nxt) - p)


def axial_rope(z):
    T, dh = z.shape[1], z.shape[-1]
    half = dh // 2
    nf = half // 2
    t = jnp.arange(T)
    row = (t // GRID_W).astype(F32)
    col = (t % GRID_W).astype(F32)
    inv = ROPE_BASE ** (-jnp.arange(nf, dtype=F32) / nf)

    def rot(u, pos):
        ang = pos[:, None] * inv[None, :]
        cos = jnp.cos(ang)[None, :, None, :]
        sin = jnp.sin(ang)[None, :, None, :]
        u1, u2 = u[..., :nf], u[..., nf:]
        return jnp.concatenate([u1 * cos - u2 * sin, u1 * sin + u2 * cos], -1)

    return jnp.concatenate([rot(z[..., :half], row), rot(z[..., half:], col)], -1).astype(z.dtype)


def neighbourhood_attention(q, k, v, k_ctx, v_ctx, rpb):
    B, N, H, dh = q.shape
    rows = N // GRID_W
    kh = min(NA_WIN_ROWS, rows)
    kw = NA_WIN_COLS
    grid = lambda z: z.reshape(B, rows, GRID_W, H, dh)
    qg, kg, vg = grid(q), grid(k), grid(v)
    r_idx = jnp.arange(rows)
    row_start = jnp.clip(r_idx - kh // 2, 0, rows - kh)
    band_rows = row_start[:, None] + jnp.arange(kh)[None, :]
    k_band = kg[:, band_rows]
    v_band = vg[:, band_rows]
    j_idx = jnp.arange(GRID_W)
    col_start = jnp.clip(j_idx - kw // 2, 0, GRID_W - kw)
    col_in = (j_idx[None, :] >= col_start[:, None]) & (j_idx[None, :] < col_start[:, None] + kw)
    row_off = band_rows - r_idx[:, None] + (NA_WIN_ROWS - 1)
    col_off = jnp.clip(j_idx[None, :] - j_idx[:, None], -(kw - 1), kw - 1) + (kw - 1)
    bias = rpb.astype(F32)[:, row_off[:, None, :, None], col_off[None, :, None, :]]
    bias = jnp.where(col_in[None, None, :, None, :], bias, NEG_INF)
    s_loc = jnp.einsum('brjhd,brachd->bhrjac', qg, k_band).astype(F32) + bias[None]
    s_ctx = jnp.einsum('brjhd,bchd->bhrjc', qg, k_ctx).astype(F32)
    n_loc = kh * GRID_W
    p = jax.nn.softmax(jnp.concatenate([s_loc.reshape(B, H, rows, GRID_W, n_loc), s_ctx], -1), axis=-1).astype(v.dtype)
    p_loc = p[..., :n_loc].reshape(B, H, rows, GRID_W, kh, GRID_W)
    out = (jnp.einsum('bhrjac,brachd->brjhd', p_loc, v_band)
           + jnp.einsum('bhrjc,bchd->brjhd', p[..., n_loc:], v_ctx))
    return out.reshape(B, N, H * dh)


def ctx_attention(q, k, v):
    s = jnp.einsum('bqhd,bkhd->bhqk', q, k).astype(F32)
    p = jax.nn.softmax(s, axis=-1).astype(v.dtype)
    return jnp.einsum('bhqk,bkhd->bqhd', p, v)


def rwkv_prep(t, w0, w_up, a0, a_up, k_k, k_a):
    B, T = t['rw_k'].shape[:2]
    heads = lambda z: z.reshape(B, T, RW_HEADS, RW_HEAD_DIM).astype(F32)
    k = t['rw_k'].astype(F32)
    kk = heads(k * k_k)
    kk = kk / jnp.maximum(jnp.linalg.norm(kk, axis=-1, keepdims=True), 1e-12)
    dirs = []
    for d, (wn, an) in enumerate((('rw_wf', 'rw_af'), ('rw_wb', 'rw_ab'))):
        w_log = -jax.nn.softplus(-(w0[d] + jnp.tanh(t[wn].astype(F32)) @ w_up[d])) - 0.5
        decay = jnp.exp(-jnp.exp(w_log))
        a = jax.nn.sigmoid(a0[d] + t[an].astype(F32) @ a_up[d])
        k_d = k * (1.0 + (a - 1.0) * k_a)
        dirs.append((heads(decay), heads(a), heads(k_d)))
    return kk, heads(t['rw_v']), dirs


def rwkv_readout(y, r, v, dirs, g_low, r_k, g_up, gn_g, gn_b):
    B, T = y.shape[:2]
    mu = y.mean(-1, keepdims=True)
    var = jnp.square(y - mu).mean(-1, keepdims=True)
    yn = ((y - mu) * lax.rsqrt(var + RW_GN_EPS) * gn_g.reshape(RW_HEADS, RW_HEAD_DIM)
          + gn_b.reshape(RW_HEADS, RW_HEAD_DIM))
    bonus = (jnp.sum(r * dirs[0][2] * r_k, -1, keepdims=True)
             + jnp.sum(r * dirs[1][2] * r_k, -1, keepdims=True)) * v
    gate = jax.nn.sigmoid(g_low.astype(F32)) @ g_up
    return (yn + bonus).reshape(B, T, RW_WIDTH) * gate


def ml_prep(t, gate_b, rope, need_q):
    B, T = t['ml_k'].shape[:2]
    heads = lambda z, dh: z.reshape(B, T, ML_HEADS, dh).astype(F32)
    k = heads(t['ml_k'], ML_QK_DIM)
    q = heads(t['ml_q'], ML_QK_DIM) * ML_QK_DIM ** -0.5 if need_q else None
    if rope:
        k = axial_rope(k)
        q = axial_rope(q)
    v = heads(t['ml_v'], ML_V_DIM)
    gb = gate_b.astype(F32)
    bht = lambda z: z.astype(F32).transpose(0, 2, 1)
    ig = (bht(t['ml_if'] + gb[0]), bht(t['ml_ib'] + gb[1]))
    lf = (jax.nn.log_sigmoid(bht(t['ml_ff'] + gb[2])), jax.nn.log_sigmoid(bht(t['ml_fb'] + gb[3])))
    bhtd = lambda z: None if z is None else z.transpose(0, 2, 1, 3)
    return bhtd(q), bhtd(k), bhtd(v), ig, lf


def ml_chunk_states(k, v, ig, lf, state0):
    B, H, T, dk = k.shape
    dv = v.shape[-1]
    L = min(ML_CHUNK, T)
    nc = T // L
    kc = k.reshape(B, H, nc, L, dk)
    vc = v.reshape(B, H, nc, L, dv)
    b = jnp.cumsum(lf.reshape(B, H, nc, L), -1)
    b_end = b[..., -1]
    g = b_end[..., None] - b + ig.reshape(B, H, nc, L)
    m_chunk = g.max(-1)
    wgt = jnp.exp(g - m_chunk[..., None])
    kv = jnp.einsum('bhnl,bhnlk,bhnlv->bhnkv', wgt, kc, vc)
    ks = jnp.einsum('bhnl,bhnlk->bhnk', wgt, kc)

    def step(state, inp):
        c_mem, n_mem, m = state
        be, mc, kv_n, ks_n = inp
        m_new = jnp.maximum(be + m, mc)
        fa = jnp.exp(be + m - m_new)
        fb = jnp.exp(mc - m_new)
        c_new = fa[..., None, None] * c_mem + fb[..., None, None] * kv_n
        n_new = fa[..., None] * n_mem + fb[..., None] * ks_n
        return (c_new, n_new, m_new), state

    xs = tuple(jnp.moveaxis(z, 2, 0) for z in (b_end, m_chunk, kv, ks))
    final, starts = lax.scan(step, state0, xs)
    return tuple(jnp.moveaxis(z, 0, 2) for z in starts), final


def ml_chunk_outputs(q, k, v, ig, lf, starts):
    B, H, T, dk = q.shape
    dv = v.shape[-1]
    L = min(ML_CHUNK, T)
    nc = T // L
    qc = q.reshape(B, H, nc, L, dk)
    kc = k.reshape(B, H, nc, L, dk)
    vc = v.reshape(B, H, nc, L, dv)
    b = jnp.cumsum(lf.reshape(B, H, nc, L), -1)
    c0, n0, m0 = starts
    causal = jnp.tril(jnp.ones((L, L), bool))
    dlog = jnp.where(causal, b[..., :, None] - b[..., None, :] + ig.reshape(B, H, nc, L)[..., None, :], NEG_INF)
    inter = b + m0[..., None]
    m = jnp.maximum(dlog.max(-1), inter)
    dw = jnp.exp(dlog - m[..., None])
    iw = jnp.exp(inter - m)
    s = jnp.einsum('bhntd,bhnsd->bhnts', qc, kc) * dw
    num = jnp.einsum('bhnts,bhnsv->bhntv', s, vc) + iw[..., None] * jnp.einsum('bhntd,bhndv->bhntv', qc, c0)
    den = s.sum(-1) + iw * jnp.einsum('bhntd,bhnd->bhnt', qc, n0)
    h = num / jnp.maximum(jnp.abs(den), jnp.exp(-m))[..., None]
    return h.reshape(B, H, T, dv)


def ml_readout(h, o, norm_g):
    B, H, T, dv = h.shape
    hn = h * lax.rsqrt(jnp.mean(h * h, -1, keepdims=True) + ML_NORM_EPS)
    hn = hn.transpose(0, 2, 1, 3).reshape(B, T, H * dv) * norm_g
    return hn * jax.nn.sigmoid(o.astype(F32))


def _pair_lanes(z):
    B, T, H, K = z.shape
    return z.reshape(B, T, H // 2, 2 * K).transpose(0, 2, 1, 3).reshape(B * (H // 2), T, 2 * K)


def _unpair_lanes(z, B):
    G, T, L = z.shape
    hp = G // B
    return z.reshape(B, hp, T, L).transpose(0, 2, 1, 3).reshape(B, T, hp * 2, L // 2)


def rwkv_bidirectional(kk_c, vr_c, dirs_c, r_c, kk_l, vr_l, dirs_l, r_l):
    B, C = kk_c.shape[:2]
    N = kk_l.shape[1]
    chains = []
    for d in range(2):
        flip = (lambda z: jnp.flip(z, 1)) if d == 1 else (lambda z: z)
        cat = lambda zc, zl: _pair_lanes(jnp.concatenate([flip(zc), flip(zl)], 1))
        decay = cat(dirs_c[d][0], dirs_l[d][0])
        nkk = cat(-kk_c, -kk_l)
        beta = cat(kk_c * dirs_c[d][1], kk_l * dirs_l[d][1])
        kd = cat(dirs_c[d][2], dirs_l[d][2])
        v = cat(vr_c, vr_l)
        r = cat(r_c, r_l)
        chains.append((decay, nkk, beta, kd, v, r))
    args = [jnp.concatenate([chains[0][i], chains[1][i]], 0) for i in range(6)]
    y = rwkv_scan_pallas(*args)
    G = y.shape[0] // 2
    y_f = _unpair_lanes(y[:G], B)
    y_b = _unpair_lanes(y[G:], B)
    y_c = y_f[:, :C] + jnp.flip(y_b[:, :C], 1)
    y_l = y_f[:, C:] + jnp.flip(y_b[:, C:], 1)
    return y_c, y_l


def even_mixer(a_lat, a_ctx, w_in, w_out, rpb, mu, w0, w_up, a0, a_up, g_up, k_k, k_a, r_k, gn_g, gn_b, need_ctx):
    names = tuple(n for n, _ in EVEN_LAYOUT)
    shift_offs = _offsets(EVEN_LAYOUT, 'rw_')

    def prep(h, cols):
        t = project(h, w_in, EVEN_LAYOUT, cols)
        for n in cols:
            if n in shift_offs:
                o, wd = shift_offs[n]
                t[n] = centred_shift(t[n], mu[o:o + wd])
        return t

    t_lat = prep(a_lat, names)
    t_ctx = prep(a_ctx, names)
    B, N = a_lat.shape[:2]
    C = a_ctx.shape[1]
    na_h = lambda z: z.reshape(z.shape[0], z.shape[1], NA_HEADS, NA_HEAD_DIM)
    rw_h = lambda z: z.reshape(z.shape[0], z.shape[1], RW_HEADS, RW_HEAD_DIM).astype(F32)
    k_c, v_c = na_h(t_ctx['na_k']), na_h(t_ctx['na_v'])
    na_lat = neighbourhood_attention(na_h(t_lat['na_q']) * NA_SCALE, na_h(t_lat['na_k']),
                                     na_h(t_lat['na_v']), k_c, v_c, rpb)
    kk_c, vr_c, dirs_c = rwkv_prep(t_ctx, w0, w_up, a0, a_up, k_k, k_a)
    kk_l, vr_l, dirs_l = rwkv_prep(t_lat, w0, w_up, a0, a_up, k_k, k_a)
    r_l = rw_h(t_lat['rw_r'])
    r_c = rw_h(t_ctx['rw_r'])
    y_c, y_l = rwkv_bidirectional(kk_c, vr_c, dirs_c, r_c, kk_l, vr_l, dirs_l, r_l)
    rw_lat = rwkv_readout(y_l, r_l, vr_l, dirs_l, t_lat['rw_g'], r_k, g_up, gn_g, gn_b)
    y_lat = jnp.einsum('btd,de->bte', jnp.concatenate([na_lat.astype(F32), rw_lat], -1), w_out).astype(a_lat.dtype)
    if not need_ctx:
        return y_lat, None
    na_ctx = ctx_attention(na_h(t_ctx['na_q']) * NA_SCALE, k_c, v_c).reshape(B, C, NA_WIDTH)
    rw_ctx = rwkv_readout(y_c, r_c, vr_c, dirs_c, t_ctx['rw_g'], r_k, g_up, gn_g, gn_b)
    y_ctx = jnp.einsum('btd,de->bte', jnp.concatenate([na_ctx.astype(F32), rw_ctx], -1), w_out).astype(a_ctx.dtype)
    return y_lat, y_ctx


def odd_mixer(a_lat, a_ctx, w_in, w_out, gate_b, norm_g, need_ctx):
    names = tuple(n for n, _ in ODD_LAYOUT)
    t_lat = project(a_lat, w_in, ODD_LAYOUT, names)
    t_ctx = project(a_ctx, w_in, ODD_LAYOUT, names if need_ctx else ODD_CTX_STATE_COLS)
    q_l, k_l, v_l, ig_l, lf_l = ml_prep(t_lat, gate_b, True, True)
    q_c, k_c, v_c, ig_c, lf_c = ml_prep(t_ctx, gate_b, False, need_ctx)
    B = a_lat.shape[0]
    zero = (jnp.zeros((B, ML_HEADS, ML_QK_DIM, ML_V_DIM), F32),
            jnp.zeros((B, ML_HEADS, ML_QK_DIM), F32),
            jnp.zeros((B, ML_HEADS), F32))
    h_l, h_c = [], []
    for d in range(2):
        f = (lambda z: jnp.flip(z, 2)) if d == 1 else (lambda z: z)
        starts_c, final_c = ml_chunk_states(f(k_c), f(v_c), f(ig_c[d]), f(lf_c[d]), zero)
        starts_l, _ = ml_chunk_states(f(k_l), f(v_l), f(ig_l[d]), f(lf_l[d]), final_c)
        h_l.append(f(ml_chunk_outputs(f(q_l), f(k_l), f(v_l), f(ig_l[d]), f(lf_l[d]), starts_l)))
        if need_ctx:
            h_c.append(f(ml_chunk_outputs(f(q_c), f(k_c), f(v_c), f(ig_c[d]), f(lf_c[d]), starts_c)))
    y_lat = jnp.einsum('btd,de->bte', ml_readout(h_l[0] + h_l[1], t_lat['ml_o'], norm_g), w_out).astype(a_lat.dtype)
    if not need_ctx:
        return y_lat, None
    y_ctx = jnp.einsum('btd,de->bte', ml_readout(h_c[0] + h_c[1], t_ctx['ml_o'], norm_g), w_out).astype(a_ctx.dtype)
    return y_lat, y_ctx


def swiglu(h, wg, wu, wd):
    return jnp.dot(jax.nn.silu(jnp.dot(h, wg)) * jnp.dot(h, wu), wd)


def moe_ffn(h, router_w, router_b, wg, wu, wd, sg, su, sd):
    T, D = h.shape
    E = router_w.shape[-1]
    s = jax.nn.sigmoid(jnp.dot(h, router_w, precision=lax.Precision.HIGHEST).astype(F32))
    grp = (s + router_b.astype(F32)).reshape(T, N_GROUPS, E // N_GROUPS)
    g_score = lax.top_k(grp, 2)[0].sum(-1)
    g_keep = jax.nn.one_hot(lax.top_k(g_score, TOPK_GROUPS)[1], N_GROUPS).sum(1) > 0
    choice = jnp.where(g_keep[:, :, None], grp, NEG_INF).reshape(T, E)
    e_idx = lax.top_k(choice, TOP_K)[1]
    w_sel = jnp.take_along_axis(s, e_idx, 1)
    w_sel = w_sel / w_sel.sum(-1, keepdims=True) * ROUTED_SCALE
    n_asg = T * TOP_K
    flat_e = e_idx.reshape(-1)
    order = jnp.argsort(flat_e)
    se = flat_e[order]
    st = (order // TOP_K).astype(jnp.int32)
    sw = w_sel.reshape(-1)[order]
    counts = jnp.bincount(flat_e, length=E)
    padded = (counts + MOE_BLOCK - 1) // MOE_BLOCK * MOE_BLOCK
    ends_pad = jnp.cumsum(padded)
    pos = (ends_pad - padded)[se] + jnp.arange(n_asg) - (jnp.cumsum(counts) - counts)[se]
    n_blocks = -(-n_asg // MOE_BLOCK) + E
    buf_tok = jnp.full((n_blocks * MOE_BLOCK,), T, jnp.int32).at[pos].set(st)
    buf_w = jnp.zeros((n_blocks * MOE_BLOCK,), F32).at[pos].set(sw)
    blk_e = jnp.minimum(jnp.searchsorted(ends_pad, jnp.arange(n_blocks) * MOE_BLOCK, side='right'), E - 1)
    h_pad = jnp.concatenate([h, jnp.zeros((1, D), h.dtype)], 0).astype(BF16)
    ys = expert_blocks_pallas(h_pad[buf_tok], buf_w[:, None], blk_e.astype(jnp.int32), wg, wu, wd)
    acc = jnp.zeros((T + 1, D), F32).at[buf_tok].add(ys)
    return acc[:T] + swiglu(h, sg, su, sd)


def kernel(x, c, ctx, c_ctx, ada_w, ada_b, ln_g, ln_b, ev_w_in, ev_w_out, na_rpb, rw_mu, rw_w0, rw_w_up,
           rw_a0, rw_a_up, rw_g_up, rw_k_k, rw_k_a, rw_r_k, rw_gn_g, rw_gn_b, od_w_in, od_w_out, ml_gate_b,
           ml_norm_g, moe_router, moe_bias, moe_w_gate, moe_w_up, moe_w_down, sh_w_gate, sh_w_up, sh_w_down):
    B, N, D = x.shape
    C = ctx.shape[1]
    silu_c = jax.nn.silu(c)
    silu_cc = jax.nn.silu(c_ctx)
    h_lat, h_ctx = x, ctx
    for l in range(DEPTH):
        last = l == DEPTH - 1
        mods = jnp.split(silu_c @ ada_w[l] + ada_b[l], 6, axis=-1)
        n_cm = 2 if last else 6
        mods_c = jnp.split(silu_cc @ ada_w[l][:, :n_cm * D] + ada_b[l][:n_cm * D], n_cm, axis=-1)
        a_lat = modulate(h_lat, mods[0][:, None], mods[1][:, None])
        a_ctx = modulate(h_ctx, mods_c[0], mods_c[1])
        if l % 2 == 0:
            e = l // 2
            y_lat, y_ctx = even_mixer(a_lat, a_ctx, ev_w_in[e], ev_w_out[e], na_rpb[e], rw_mu[e], rw_w0[e],
                                      rw_w_up[e], rw_a0[e], rw_a_up[e], rw_g_up[e], rw_k_k[e], rw_k_a[e],
                                      rw_r_k[e], rw_gn_g[e], rw_gn_b[e], not last)
        else:
            o = l // 2
            y_lat, y_ctx = odd_mixer(a_lat, a_ctx, od_w_in[o], od_w_out[o], ml_gate_b[o], ml_norm_g[o], not last)
        h_lat = layer_norm(DEEPNORM_ALPHA * h_lat + mods[2][:, None] * y_lat, ln_g[l, 0], ln_b[l, 0])
        f_lat = modulate(h_lat, mods[3][:, None], mods[4][:, None]).reshape(B * N, D)
        moe_args = (moe_router[l], moe_bias[l], moe_w_gate[l], moe_w_up[l], moe_w_down[l],
                    sh_w_gate[l], sh_w_up[l], sh_w_down[l])
        if last:
            ff = moe_ffn(f_lat, *moe_args).reshape(B, N, D)
            h_lat = layer_norm(DEEPNORM_ALPHA * h_lat + mods[5][:, None] * ff, ln_g[l, 1], ln_b[l, 1])
        else:
            h_ctx = layer_norm(DEEPNORM_ALPHA * h_ctx + mods_c[2] * y_ctx, ln_g[l, 0], ln_b[l, 0])
            f_ctx = modulate(h_ctx, mods_c[3], mods_c[4]).reshape(B * C, D)
            ff = moe_ffn(jnp.concatenate([f_lat, f_ctx], 0), *moe_args)
            h_lat = layer_norm(DEEPNORM_ALPHA * h_lat + mods[5][:, None] * ff[:B * N].reshape(B, N, D),
                               ln_g[l, 1], ln_b[l, 1])
            h_ctx = layer_norm(DEEPNORM_ALPHA * h_ctx + mods_c[5] * ff[B * N:].reshape(B, C, D),
                               ln_g[l, 1], ln_b[l, 1])
    return h_lat
```

```python
import functools

import jax
import jax.numpy as jnp
from jax import lax
from jax.experimental import pallas as pl
from jax.experimental.pallas import tpu as pltpu

D_MODEL = 1024
DEPTH = 2
GRID_W = 64

DEEPNORM_ALPHA = (2.0 * DEPTH) ** 0.25
LN_EPS = 1e-5
NEG_INF = -1e30
F32 = jnp.float32
BF16 = jnp.bfloat16

NA_HEAD_DIM = 64
NA_WIDTH = D_MODEL // 2
NA_HEADS = NA_WIDTH // NA_HEAD_DIM
NA_WIN_ROWS = 8
NA_WIN_COLS = 16
NA_SCALE = NA_HEAD_DIM ** -0.5

RW_HEAD_DIM = 64
RW_WIDTH = D_MODEL // 2
RW_HEADS = RW_WIDTH // RW_HEAD_DIM
RW_DECAY_LORA = 32
RW_AAA_LORA = 32
RW_GATE_LORA = 96
RW_GN_EPS = 64e-5

ML_HEADS = 8
ML_V_DIM = D_MODEL // ML_HEADS
ML_QK_DIM = ML_V_DIM // 2
ML_WIDTH = ML_HEADS * ML_V_DIM
ML_CHUNK = 128
ML_NORM_EPS = 1e-6
ROPE_BASE = 10000.0

N_EXPERTS = 256
TOP_K = 8
N_GROUPS = 8
TOPK_GROUPS = 4
EXPERT_FF = 256
ROUTED_SCALE = 2.5
MOE_BLOCK = 128

EVEN_LAYOUT = (
    ('na_q', NA_WIDTH), ('na_k', NA_WIDTH), ('na_v', NA_WIDTH),
    ('rw_r', RW_WIDTH), ('rw_k', RW_WIDTH), ('rw_v', RW_WIDTH),
    ('rw_wf', RW_DECAY_LORA), ('rw_wb', RW_DECAY_LORA),
    ('rw_af', RW_AAA_LORA), ('rw_ab', RW_AAA_LORA), ('rw_g', RW_GATE_LORA),
)
ODD_LAYOUT = (
    ('ml_q', ML_HEADS * ML_QK_DIM), ('ml_k', ML_HEADS * ML_QK_DIM),
    ('ml_v', ML_WIDTH), ('ml_o', ML_WIDTH),
    ('ml_if', ML_HEADS), ('ml_ib', ML_HEADS), ('ml_ff', ML_HEADS), ('ml_fb', ML_HEADS),
)
EVEN_CTX_STATE_COLS = ('na_k', 'na_v', 'rw_k', 'rw_v', 'rw_wf', 'rw_wb', 'rw_af', 'rw_ab')
ODD_CTX_STATE_COLS = ('ml_k', 'ml_v', 'ml_if', 'ml_ib', 'ml_ff', 'ml_fb')

SUBLANES = 8
LANES = 128
VMEM_LIMIT_BYTES = 48 * 1024 * 1024


RW_SCAN_CHAINS = 8
RW_SCAN_TIME = 256


def _split_bf16(x):
    hi = x.astype(BF16)
    lo = (x - hi.astype(F32)).astype(BF16)
    return jnp.concatenate([hi, lo], axis=-1)


def _rwkv_scan_kernel(w_ref, nkk_ref, beta_ref, kd_ref, v_ref, r_ref, y_ref, s_ref):
    @pl.when(pl.program_id(1) == 0)
    def _():
        s_ref[...] = jnp.zeros_like(s_ref)

    n_chain, n_time = w_ref.shape[0], w_ref.shape[1]
    rows_all = n_chain * RW_HEAD_DIM
    row = lax.broadcasted_iota(jnp.int32, (2 * LANES, LANES), 0)
    col = lax.broadcasted_iota(jnp.int32, (2 * LANES, LANES), 1)
    ones = (((row % LANES) // RW_HEAD_DIM) == (col // RW_HEAD_DIM)).astype(BF16)
    vi = lax.broadcasted_iota(jnp.int32, (1, RW_HEAD_DIM, LANES), 1)
    li = lax.broadcasted_iota(jnp.int32, (1, RW_HEAD_DIM, LANES), 2)
    diag = (li % RW_HEAD_DIM) == vi

    def seg(x):
        out = jnp.dot(_split_bf16(x.reshape(rows_all, LANES)), ones, preferred_element_type=F32)
        return out.reshape(n_chain, RW_HEAD_DIM, LANES)

    def sub(i, carry):
        sl = (slice(None), pl.ds(pl.multiple_of(i * SUBLANES, SUBLANES), SUBLANES), slice(None))
        w8, nkk8, beta8 = w_ref[sl], nkk_ref[sl], beta_ref[sl]
        kd8, v8, r8 = kd_ref[sl], v_ref[sl], r_ref[sl]
        s = s_ref[...]
        rows = []
        for t in range(SUBLANES):
            rt = (slice(None), slice(t, t + 1), slice(None))
            vcol = seg(jnp.where(diag, v8[rt], 0.0))
            sa = seg(s * nkk8[rt])
            s = s * w8[rt] + sa * beta8[rt] + vcol * kd8[rt]
            ybc = seg(s * r8[rt])
            rows.append(jnp.sum(jnp.where(diag, ybc, 0.0), axis=1, keepdims=True))
        s_ref[...] = s
        y_ref[sl] = jnp.concatenate(rows, axis=1)
        return carry

    lax.fori_loop(0, n_time // SUBLANES, sub, 0)


def rwkv_scan_pallas(w, nkk, beta, kd, v, r):
    G, T, _ = w.shape
    gc = RW_SCAN_CHAINS if G % RW_SCAN_CHAINS == 0 else G
    tc = RW_SCAN_TIME if T % RW_SCAN_TIME == 0 else T
    assert T % SUBLANES == 0
    spec = pl.BlockSpec((gc, tc, LANES), lambda g, t: (g, t, 0))
    return pl.pallas_call(
        _rwkv_scan_kernel,
        grid=(G // gc, T // tc),
        in_specs=[spec] * 6,
        out_specs=spec,
        out_shape=jax.ShapeDtypeStruct((G, T, LANES), F32),
        scratch_shapes=[pltpu.VMEM((gc, RW_HEAD_DIM, LANES), F32)],
        compiler_params=pltpu.CompilerParams(
            dimension_semantics=("arbitrary", "arbitrary"), vmem_limit_bytes=VMEM_LIMIT_BYTES),
        name="rwkv_scan",
    )(w, nkk, beta, kd, v, r)


NA_BAND = NA_WIN_ROWS * GRID_W


def _na_kernel(q_ref, k_ref, v_ref, kc_ref, vc_ref, bias_ref, o_ref):
    r = pl.program_id(1)
    n_rows = pl.num_programs(1)
    row_start = jnp.clip(r - NA_WIN_ROWS // 2, 0, n_rows - NA_WIN_ROWS)
    start = pl.multiple_of(row_start * GRID_W, GRID_W)
    q = (q_ref[0] * NA_SCALE).astype(BF16)
    kb = k_ref[0, pl.ds(start, NA_BAND), :]
    vb = v_ref[0, pl.ds(start, NA_BAND), :]
    kc = kc_ref[0]
    vc = vc_ref[0]
    head_of_lane = lax.broadcasted_iota(jnp.int32, (GRID_W, LANES), 1) // NA_HEAD_DIM
    nt = (((1,), (1,)), ((), ()))
    for p in range(NA_WIDTH // LANES):
        cols = slice(p * LANES, (p + 1) * LANES)
        qp, kp, vp, kcp, vcp = q[:, cols], kb[:, cols], vb[:, cols], kc[:, cols], vc[:, cols]
        outs = []
        for h2 in range(LANES // NA_HEAD_DIM):
            qm = jnp.where(head_of_lane == h2, qp, jnp.zeros_like(qp))
            s_loc = lax.dot_general(qm, kp, nt, preferred_element_type=F32) + bias_ref[0, 2 * p + h2]
            s_ctx = lax.dot_general(qm, kcp, nt, preferred_element_type=F32)
            m = jnp.maximum(jnp.max(s_loc, axis=-1, keepdims=True), jnp.max(s_ctx, axis=-1, keepdims=True))
            e_loc = jnp.exp(s_loc - m)
            e_ctx = jnp.exp(s_ctx - m)
            den = jnp.sum(e_loc, axis=-1, keepdims=True) + jnp.sum(e_ctx, axis=-1, keepdims=True)
            o = (jnp.dot(e_loc.astype(BF16), vp, preferred_element_type=F32)
                 + jnp.dot(e_ctx.astype(BF16), vcp, preferred_element_type=F32))
            outs.append(o / den)
        o_ref[0, :, cols] = jnp.where(head_of_lane == 0, outs[0], outs[1])


def _na_bias_table(rpb):
    kw = NA_WIN_COLS
    j = jnp.arange(GRID_W)
    col_start = jnp.clip(j - kw // 2, 0, GRID_W - kw)
    col_in = (j[None, :] >= col_start[:, None]) & (j[None, :] < col_start[:, None] + kw)
    col_off = jnp.clip(j[None, :] - j[:, None], -(kw - 1), kw - 1) + (kw - 1)
    d = jnp.arange(NA_WIN_ROWS)
    a = jnp.arange(NA_WIN_ROWS)
    row_off = a[None, :] - d[:, None] + (NA_WIN_ROWS - 1)
    tab = rpb.astype(F32)[:, row_off[:, :, None, None], col_off[None, None, :, :]]
    tab = jnp.where(col_in[None, None, None], tab, NEG_INF)
    tab = tab.transpose(1, 0, 3, 2, 4)
    return tab.reshape(NA_WIN_ROWS, NA_HEADS, GRID_W, NA_BAND)


def neighbourhood_attention_pallas(q, k, v, k_ctx, v_ctx, rpb):
    B, N, W = q.shape
    C = k_ctx.shape[1]
    n_rows = N // GRID_W
    assert n_rows >= NA_WIN_ROWS and N % GRID_W == 0
    bias = _na_bias_table(rpb)

    def bias_idx(b, r):
        return (r - jnp.clip(r - NA_WIN_ROWS // 2, 0, n_rows - NA_WIN_ROWS), 0, 0, 0)

    return pl.pallas_call(
        _na_kernel,
        grid=(B, n_rows),
        in_specs=[
            pl.BlockSpec((1, GRID_W, W), lambda b, r: (b, r, 0)),
            pl.BlockSpec((1, N, W), lambda b, r: (b, 0, 0)),
            pl.BlockSpec((1, N, W), lambda b, r: (b, 0, 0)),
            pl.BlockSpec((1, C, W), lambda b, r: (b, 0, 0)),
            pl.BlockSpec((1, C, W), lambda b, r: (b, 0, 0)),
            pl.BlockSpec((1, NA_HEADS, GRID_W, NA_BAND), bias_idx),
        ],
        out_specs=pl.BlockSpec((1, GRID_W, W), lambda b, r: (b, r, 0)),
        out_shape=jax.ShapeDtypeStruct((B, N, W), F32),
        compiler_params=pltpu.CompilerParams(
            dimension_semantics=("arbitrary", "arbitrary"), vmem_limit_bytes=VMEM_LIMIT_BYTES),
        name="na_attention",
    )(q, k.astype(BF16), v.astype(BF16), k_ctx.astype(BF16), v_ctx.astype(BF16), bias)


MOE_TOKEN_TILE = 128


def _swiglu_bf16(x, wg, wu, wd):
    g = jnp.dot(x, wg, preferred_element_type=F32)
    u = jnp.dot(x, wu, preferred_element_type=F32)
    mid = (g * jax.nn.sigmoid(g) * u).astype(BF16)
    return jnp.dot(mid, wd, preferred_element_type=F32)


def _row_copy(src_ref, src_row, dst_ref, dst_row, sem):
    return pltpu.make_async_copy(src_ref.at[pl.ds(src_row, 1), :], dst_ref.at[pl.ds(dst_row, 1), :], sem)


def _dispatch_kernel(pos_ref, f_ref, xs_ref, sem):
    n_tok = f_ref.shape[0]

    def issue(t, carry):
        for k in range(TOP_K):
            _row_copy(f_ref, t, xs_ref, pos_ref[t * TOP_K + k], sem).start()
        return carry

    lax.fori_loop(0, n_tok, issue, 0)

    def drain(t, carry):
        for k in range(TOP_K):
            _row_copy(f_ref, 0, xs_ref, 0, sem).wait()
        return carry

    lax.fori_loop(0, n_tok, drain, 0)


def moe_dispatch_pallas(f, pos_flat):
    T, D = f.shape
    tm = MOE_TOKEN_TILE
    assert T % tm == 0
    return pl.pallas_call(
        _dispatch_kernel,
        grid=(T // tm,),
        in_specs=[
            pl.BlockSpec((tm * TOP_K,), lambda i: (i,), memory_space=pltpu.SMEM),
            pl.BlockSpec((tm, D), lambda i: (i, 0)),
        ],
        out_specs=pl.BlockSpec(memory_space=pl.ANY),
        out_shape=jax.ShapeDtypeStruct((T * TOP_K, D), F32),
        scratch_shapes=[pltpu.SemaphoreType.DMA(())],
        compiler_params=pltpu.CompilerParams(
            dimension_semantics=("arbitrary",), vmem_limit_bytes=VMEM_LIMIT_BYTES),
        name="moe_dispatch",
    )(pos_flat, f)


def _expert_item_kernel(blk_ref, e_ref, lo_ref, hi_ref, first_ref, x_ref, wg_ref, wu_ref, wd_ref, o_ref):
    del e_ref
    i = pl.program_id(0)
    lo, hi = lo_ref[i], hi_ref[i]

    @pl.when(hi > lo)
    def _():
        y = _swiglu_bf16(x_ref[...].astype(BF16), wg_ref[0].astype(BF16), wu_ref[0].astype(BF16),
                         wd_ref[0].astype(BF16))
        rows = blk_ref[i] * MOE_BLOCK + lax.broadcasted_iota(jnp.int32, (MOE_BLOCK, 1), 0)
        y = jnp.where((rows >= lo) & (rows < hi), y, 0.0)

        @pl.when(first_ref[i] == 1)
        def _():
            o_ref[...] = y

        @pl.when(first_ref[i] == 0)
        def _():
            o_ref[...] += y


def moe_experts_pallas(xs, items, wg, wu, wd):
    n_rows, D = xs.shape
    F = wg.shape[-1]
    n_items = items[0].shape[0]
    grid_spec = pltpu.PrefetchScalarGridSpec(
        num_scalar_prefetch=5,
        grid=(n_items,),
        in_specs=[
            pl.BlockSpec((MOE_BLOCK, D), lambda i, blk, e, lo, hi, first: (blk[i], 0)),
            pl.BlockSpec((1, D, F), lambda i, blk, e, lo, hi, first: (e[i], 0, 0)),
            pl.BlockSpec((1, D, F), lambda i, blk, e, lo, hi, first: (e[i], 0, 0)),
            pl.BlockSpec((1, F, D), lambda i, blk, e, lo, hi, first: (e[i], 0, 0)),
        ],
        out_specs=pl.BlockSpec((MOE_BLOCK, D), lambda i, blk, e, lo, hi, first: (blk[i], 0)),
    )
    return pl.pallas_call(
        _expert_item_kernel,
        grid_spec=grid_spec,
        out_shape=jax.ShapeDtypeStruct((n_rows, D), F32),
        compiler_params=pltpu.CompilerParams(
            dimension_semantics=("arbitrary",), vmem_limit_bytes=VMEM_LIMIT_BYTES),
        name="moe_experts",
    )(*items, xs, wg, wu, wd)


def _combine_kernel(pos_ref, w_ref, f_ref, sg_ref, su_ref, sd_ref, ys_ref, o_ref, buf_ref, sem):
    n_tok = f_ref.shape[0]

    def issue(t, carry):
        for k in range(TOP_K):
            pltpu.make_async_copy(ys_ref.at[pl.ds(pos_ref[t * TOP_K + k], 1), :],
                                  buf_ref.at[k, pl.ds(t, 1), :], sem).start()
        return carry

    lax.fori_loop(0, n_tok, issue, 0)
    acc = _swiglu_bf16(f_ref[...].astype(BF16), sg_ref[...], su_ref[...], sd_ref[...])

    def drain(t, carry):
        for k in range(TOP_K):
            pltpu.make_async_copy(ys_ref.at[pl.ds(0, 1), :], buf_ref.at[0, pl.ds(0, 1), :], sem).wait()
        return carry

    lax.fori_loop(0, n_tok, drain, 0)
    w = w_ref[...]
    for k in range(TOP_K):
        acc = acc + w[:, k:k + 1] * buf_ref[k]
    o_ref[...] = acc


def moe_combine_pallas(ys, pos_flat, w_sel, f, sg, su, sd):
    T, D = f.shape
    tm = MOE_TOKEN_TILE
    F = sg.shape[-1]
    full = lambda shape: pl.BlockSpec(shape, lambda i: (0,) * len(shape))
    return pl.pallas_call(
        _combine_kernel,
        grid=(T // tm,),
        in_specs=[
            pl.BlockSpec((tm * TOP_K,), lambda i: (i,), memory_space=pltpu.SMEM),
            pl.BlockSpec((tm, TOP_K), lambda i: (i, 0)),
            pl.BlockSpec((tm, D), lambda i: (i, 0)),
            full((D, F)), full((D, F)), full((F, D)),
            pl.BlockSpec(memory_space=pl.ANY),
        ],
        out_specs=pl.BlockSpec((tm, D), lambda i: (i, 0)),
        out_shape=jax.ShapeDtypeStruct((T, D), F32),
        scratch_shapes=[pltpu.VMEM((TOP_K, tm, D), F32), pltpu.SemaphoreType.DMA(())],
        compiler_params=pltpu.CompilerParams(
            dimension_semantics=("arbitrary",), vmem_limit_bytes=VMEM_LIMIT_BYTES),
        name="moe_combine",
    )(pos_flat, w_sel, f, sg, su, sd, ys)


def _offsets(layout, prefix=''):
    offs, o = {}, 0
    for name, width in layout:
        if name.startswith(prefix):
            offs[name] = (o, width)
            o += width
    return offs


def project(h, w, layout, names):
    offs = _offsets(layout)
    if len(names) == len(layout):
        y = jnp.einsum('btd,de->bte', h, w)
        return {n: y[..., offs[n][0]:offs[n][0] + offs[n][1]] for n in names}
    return {n: jnp.einsum('btd,de->bte', h, w[:, offs[n][0]:offs[n][0] + offs[n][1]]) for n in names}


def layer_norm(x, g, b):
    xf = x.astype(F32)
    mu = xf.mean(-1, keepdims=True)
    var = jnp.square(xf - mu).mean(-1, keepdims=True)
    return ((xf - mu) * lax.rsqrt(var + LN_EPS) * g + b).astype(x.dtype)


def modulate(h, shift, scale):
    return h * (1.0 + scale) + shift


def centred_shift(p, mu):
    zero = jnp.zeros_like(p[:, :1])
    prev = jnp.concatenate([zero, p[:, :-1]], 1)
    nxt = jnp.concatenate([p[:, 1:], zero], 1)
    return p + mu * (0.5 * (prev + nxt) - p)


def axial_rope(z):
    T, dh = z.shape[1], z.shape[-1]
    half = dh // 2
    nf = half // 2
    t = jnp.arange(T)
    row = (t // GRID_W).astype(F32)
    col = (t % GRID_W).astype(F32)
    inv = ROPE_BASE ** (-jnp.arange(nf, dtype=F32) / nf)

    def rot(u, pos):
        ang = pos[:, None] * inv[None, :]
        cos = jnp.cos(ang)[None, :, None, :]
        sin = jnp.sin(ang)[None, :, None, :]
        u1, u2 = u[..., :nf], u[..., nf:]
        return jnp.concatenate([u1 * cos - u2 * sin, u1 * sin + u2 * cos], -1)

    return jnp.concatenate([rot(z[..., :half], row), rot(z[..., half:], col)], -1).astype(z.dtype)


def neighbourhood_attention(q, k, v, k_ctx, v_ctx, rpb):
    B, N, H, dh = q.shape
    rows = N // GRID_W
    kh = min(NA_WIN_ROWS, rows)
    kw = NA_WIN_COLS
    grid = lambda z: z.reshape(B, rows, GRID_W, H, dh)
    qg, kg, vg = grid(q), grid(k), grid(v)
    r_idx = jnp.arange(rows)
    row_start = jnp.clip(r_idx - kh // 2, 0, rows - kh)
    band_rows = row_start[:, None] + jnp.arange(kh)[None, :]
    k_band = kg[:, band_rows]
    v_band = vg[:, band_rows]
    j_idx = jnp.arange(GRID_W)
    col_start = jnp.clip(j_idx - kw // 2, 0, GRID_W - kw)
    col_in = (j_idx[None, :] >= col_start[:, None]) & (j_idx[None, :] < col_start[:, None] + kw)
    row_off = band_rows - r_idx[:, None] + (NA_WIN_ROWS - 1)
    col_off = jnp.clip(j_idx[None, :] - j_idx[:, None], -(kw - 1), kw - 1) + (kw - 1)
    bias = rpb.astype(F32)[:, row_off[:, None, :, None], col_off[None, :, None, :]]
    bias = jnp.where(col_in[None, None, :, None, :], bias, NEG_INF)
    s_loc = jnp.einsum('brjhd,brachd->bhrjac', qg, k_band).astype(F32) + bias[None]
    s_ctx = jnp.einsum('brjhd,bchd->bhrjc', qg, k_ctx).astype(F32)
    n_loc = kh * GRID_W
    p = jax.nn.softmax(jnp.concatenate([s_loc.reshape(B, H, rows, GRID_W, n_loc), s_ctx], -1), axis=-1).astype(v.dtype)
    p_loc = p[..., :n_loc].reshape(B, H, rows, GRID_W, kh, GRID_W)
    out = (jnp.einsum('bhrjac,brachd->brjhd', p_loc, v_band)
           + jnp.einsum('bhrjc,bchd->brjhd', p[..., n_loc:], v_ctx))
    return out.reshape(B, N, H * dh)


def ctx_attention(q, k, v):
    s = jnp.einsum('bqhd,bkhd->bhqk', q, k).astype(F32)
    p = jax.nn.softmax(s, axis=-1).astype(v.dtype)
    return jnp.einsum('bhqk,bkhd->bqhd', p, v)


def rwkv_prep(t, w0, w_up, a0, a_up, k_k, k_a):
    B, T = t['rw_k'].shape[:2]
    heads = lambda z: z.reshape(B, T, RW_HEADS, RW_HEAD_DIM).astype(F32)
    k = t['rw_k'].astype(F32)
    kk = heads(k * k_k)
    kk = kk / jnp.maximum(jnp.linalg.norm(kk, axis=-1, keepdims=True), 1e-12)
    dirs = []
    for d, (wn, an) in enumerate((('rw_wf', 'rw_af'), ('rw_wb', 'rw_ab'))):
        w_log = -jax.nn.softplus(-(w0[d] + jnp.tanh(t[wn].astype(F32)) @ w_up[d])) - 0.5
        decay = jnp.exp(-jnp.exp(w_log))
        a = jax.nn.sigmoid(a0[d] + t[an].astype(F32) @ a_up[d])
        k_d = k * (1.0 + (a - 1.0) * k_a)
        dirs.append((heads(decay), heads(a), heads(k_d)))
    return kk, heads(t['rw_v']), dirs


def rwkv_readout(y, r, v, dirs, g_low, r_k, g_up, gn_g, gn_b):
    B, T = y.shape[:2]
    mu = y.mean(-1, keepdims=True)
    var = jnp.square(y - mu).mean(-1, keepdims=True)
    yn = ((y - mu) * lax.rsqrt(var + RW_GN_EPS) * gn_g.reshape(RW_HEADS, RW_HEAD_DIM)
          + gn_b.reshape(RW_HEADS, RW_HEAD_DIM))
    bonus = (jnp.sum(r * dirs[0][2] * r_k, -1, keepdims=True)
             + jnp.sum(r * dirs[1][2] * r_k, -1, keepdims=True)) * v
    gate = jax.nn.sigmoid(g_low.astype(F32)) @ g_up
    return (yn + bonus).reshape(B, T, RW_WIDTH) * gate


def ml_prep(t, gate_b, rope, need_q):
    B, T = t['ml_k'].shape[:2]
    heads = lambda z, dh: z.reshape(B, T, ML_HEADS, dh).astype(F32)
    k = heads(t['ml_k'], ML_QK_DIM)
    q = heads(t['ml_q'], ML_QK_DIM) * ML_QK_DIM ** -0.5 if need_q else None
    if rope:
        k = axial_rope(k)
        q = axial_rope(q)
    v = heads(t['ml_v'], ML_V_DIM)
    gb = gate_b.astype(F32)
    bht = lambda z: z.astype(F32).transpose(0, 2, 1)
    ig = (bht(t['ml_if'] + gb[0]), bht(t['ml_ib'] + gb[1]))
    lf = (jax.nn.log_sigmoid(bht(t['ml_ff'] + gb[2])), jax.nn.log_sigmoid(bht(t['ml_fb'] + gb[3])))
    bhtd = lambda z: None if z is None else z.transpose(0, 2, 1, 3)
    return bhtd(q), bhtd(k), bhtd(v), ig, lf


def ml_chunk_states(k, v, ig, lf, state0):
    B, H, T, dk = k.shape
    dv = v.shape[-1]
    L = min(ML_CHUNK, T)
    nc = T // L
    kc = k.reshape(B, H, nc, L, dk)
    vc = v.reshape(B, H, nc, L, dv)
    b = jnp.cumsum(lf.reshape(B, H, nc, L), -1)
    b_end = b[..., -1]
    g = b_end[..., None] - b + ig.reshape(B, H, nc, L)
    m_chunk = g.max(-1)
    wgt = jnp.exp(g - m_chunk[..., None])
    kv = jnp.einsum('bhnl,bhnlk,bhnlv->bhnkv', wgt, kc, vc)
    ks = jnp.einsum('bhnl,bhnlk->bhnk', wgt, kc)

    def step(state, inp):
        c_mem, n_mem, m = state
        be, mc, kv_n, ks_n = inp
        m_new = jnp.maximum(be + m, mc)
        fa = jnp.exp(be + m - m_new)
        fb = jnp.exp(mc - m_new)
        c_new = fa[..., None, None] * c_mem + fb[..., None, None] * kv_n
        n_new = fa[..., None] * n_mem + fb[..., None] * ks_n
        return (c_new, n_new, m_new), state

    xs = tuple(jnp.moveaxis(z, 2, 0) for z in (b_end, m_chunk, kv, ks))
    final, starts = lax.scan(step, state0, xs)
    return tuple(jnp.moveaxis(z, 0, 2) for z in starts), final


def ml_chunk_outputs(q, k, v, ig, lf, starts):
    B, H, T, dk = q.shape
    dv = v.shape[-1]
    L = min(ML_CHUNK, T)
    nc = T // L
    qc = q.reshape(B, H, nc, L, dk)
    kc = k.reshape(B, H, nc, L, dk)
    vc = v.reshape(B, H, nc, L, dv)
    b = jnp.cumsum(lf.reshape(B, H, nc, L), -1)
    c0, n0, m0 = starts
    causal = jnp.tril(jnp.ones((L, L), bool))
    dlog = jnp.where(causal, b[..., :, None] - b[..., None, :] + ig.reshape(B, H, nc, L)[..., None, :], NEG_INF)
    inter = b + m0[..., None]
    m = jnp.maximum(dlog.max(-1), inter)
    dw = jnp.exp(dlog - m[..., None])
    iw = jnp.exp(inter - m)
    s = jnp.einsum('bhntd,bhnsd->bhnts', qc, kc) * dw
    num = jnp.einsum('bhnts,bhnsv->bhntv', s, vc) + iw[..., None] * jnp.einsum('bhntd,bhndv->bhntv', qc, c0)
    den = s.sum(-1) + iw * jnp.einsum('bhntd,bhnd->bhnt', qc, n0)
    h = num / jnp.maximum(jnp.abs(den), jnp.exp(-m))[..., None]
    return h.reshape(B, H, T, dv)


def ml_readout(h, o, norm_g):
    B, H, T, dv = h.shape
    hn = h * lax.rsqrt(jnp.mean(h * h, -1, keepdims=True) + ML_NORM_EPS)
    hn = hn.transpose(0, 2, 1, 3).reshape(B, T, H * dv) * norm_g
    return hn * jax.nn.sigmoid(o.astype(F32))


def _pair_lanes(z):
    B, T, H, K = z.shape
    return z.reshape(B, T, H // 2, 2 * K).transpose(0, 2, 1, 3).reshape(B * (H // 2), T, 2 * K)


def _unpair_lanes(z, B):
    G, T, L = z.shape
    hp = G // B
    return z.reshape(B, hp, T, L).transpose(0, 2, 1, 3).reshape(B, T, hp * 2, L // 2)


def rwkv_bidirectional(kk_c, vr_c, dirs_c, r_c, kk_l, vr_l, dirs_l, r_l):
    B, C = kk_c.shape[:2]
    N = kk_l.shape[1]
    chains = []
    for d in range(2):
        flip = (lambda z: jnp.flip(z, 1)) if d == 1 else (lambda z: z)
        cat = lambda zc, zl: _pair_lanes(jnp.concatenate([flip(zc), flip(zl)], 1))
        decay = cat(dirs_c[d][0], dirs_l[d][0])
        nkk = cat(-kk_c, -kk_l)
        beta = cat(kk_c * dirs_c[d][1], kk_l * dirs_l[d][1])
        kd = cat(dirs_c[d][2], dirs_l[d][2])
        v = cat(vr_c, vr_l)
        r = cat(r_c, r_l)
        chains.append((decay, nkk, beta, kd, v, r))
    args = [jnp.concatenate([chains[0][i], chains[1][i]], 0) for i in range(6)]
    y = rwkv_scan_pallas(*args)
    G = y.shape[0] // 2
    y_f = _unpair_lanes(y[:G], B)
    y_b = _unpair_lanes(y[G:], B)
    y_c = y_f[:, :C] + jnp.flip(y_b[:, :C], 1)
    y_l = y_f[:, C:] + jnp.flip(y_b[:, C:], 1)
    return y_c, y_l


def even_mixer(a_lat, a_ctx, w_in, w_out, rpb, mu, w0, w_up, a0, a_up, g_up, k_k, k_a, r_k, gn_g, gn_b, need_ctx):
    names = tuple(n for n, _ in EVEN_LAYOUT)
    shift_offs = _offsets(EVEN_LAYOUT, 'rw_')

    def prep(h, cols):
        t = project(h, w_in, EVEN_LAYOUT, cols)
        for n in cols:
            if n in shift_offs:
                o, wd = shift_offs[n]
                t[n] = centred_shift(t[n], mu[o:o + wd])
        return t

    t_lat = prep(a_lat, names)
    t_ctx = prep(a_ctx, names)
    B, N = a_lat.shape[:2]
    C = a_ctx.shape[1]
    na_h = lambda z: z.reshape(z.shape[0], z.shape[1], NA_HEADS, NA_HEAD_DIM)
    rw_h = lambda z: z.reshape(z.shape[0], z.shape[1], RW_HEADS, RW_HEAD_DIM).astype(F32)
    k_c, v_c = na_h(t_ctx['na_k']), na_h(t_ctx['na_v'])
    na_lat = neighbourhood_attention_pallas(t_lat['na_q'], t_lat['na_k'], t_lat['na_v'],
                                            t_ctx['na_k'], t_ctx['na_v'], rpb)
    kk_c, vr_c, dirs_c = rwkv_prep(t_ctx, w0, w_up, a0, a_up, k_k, k_a)
    kk_l, vr_l, dirs_l = rwkv_prep(t_lat, w0, w_up, a0, a_up, k_k, k_a)
    r_l = rw_h(t_lat['rw_r'])
    r_c = rw_h(t_ctx['rw_r'])
    y_c, y_l = rwkv_bidirectional(kk_c, vr_c, dirs_c, r_c, kk_l, vr_l, dirs_l, r_l)
    rw_lat = rwkv_readout(y_l, r_l, vr_l, dirs_l, t_lat['rw_g'], r_k, g_up, gn_g, gn_b)
    y_lat = jnp.einsum('btd,de->bte', jnp.concatenate([na_lat.astype(F32), rw_lat], -1), w_out).astype(a_lat.dtype)
    if not need_ctx:
        return y_lat, None
    na_ctx = ctx_attention(na_h(t_ctx['na_q']) * NA_SCALE, k_c, v_c).reshape(B, C, NA_WIDTH)
    rw_ctx = rwkv_readout(y_c, r_c, vr_c, dirs_c, t_ctx['rw_g'], r_k, g_up, gn_g, gn_b)
    y_ctx = jnp.einsum('btd,de->bte', jnp.concatenate([na_ctx.astype(F32), rw_ctx], -1), w_out).astype(a_ctx.dtype)
    return y_lat, y_ctx


def odd_mixer(a_lat, a_ctx, w_in, w_out, gate_b, norm_g, need_ctx):
    names = tuple(n for n, _ in ODD_LAYOUT)
    t_lat = project(a_lat, w_in, ODD_LAYOUT, names)
    t_ctx = project(a_ctx, w_in, ODD_LAYOUT, names if need_ctx else ODD_CTX_STATE_COLS)
    q_l, k_l, v_l, ig_l, lf_l = ml_prep(t_lat, gate_b, True, True)
    q_c, k_c, v_c, ig_c, lf_c = ml_prep(t_ctx, gate_b, False, need_ctx)
    B = a_lat.shape[0]
    zero = (jnp.zeros((B, ML_HEADS, ML_QK_DIM, ML_V_DIM), F32),
            jnp.zeros((B, ML_HEADS, ML_QK_DIM), F32),
            jnp.zeros((B, ML_HEADS), F32))
    h_l, h_c = [], []
    for d in range(2):
        f = (lambda z: jnp.flip(z, 2)) if d == 1 else (lambda z: z)
        starts_c, final_c = ml_chunk_states(f(k_c), f(v_c), f(ig_c[d]), f(lf_c[d]), zero)
        starts_l, _ = ml_chunk_states(f(k_l), f(v_l), f(ig_l[d]), f(lf_l[d]), final_c)
        h_l.append(f(ml_chunk_outputs(f(q_l), f(k_l), f(v_l), f(ig_l[d]), f(lf_l[d]), starts_l)))
        if need_ctx:
            h_c.append(f(ml_chunk_outputs(f(q_c), f(k_c), f(v_c), f(ig_c[d]), f(lf_c[d]), starts_c)))
    y_lat = jnp.einsum('btd,de->bte', ml_readout(h_l[0] + h_l[1], t_lat['ml_o'], norm_g), w_out).astype(a_lat.dtype)
    if not need_ctx:
        return y_lat, None
    y_ctx = jnp.einsum('btd,de->bte', ml_readout(h_c[0] + h_c[1], t_ctx['ml_o'], norm_g), w_out).astype(a_ctx.dtype)
    return y_lat, y_ctx


def swiglu(h, wg, wu, wd):
    return jnp.dot(jax.nn.silu(jnp.dot(h, wg)) * jnp.dot(h, wu), wd)


def moe_ffn(h, router_w, router_b, wg, wu, wd, sg, su, sd):
    T, D = h.shape
    E = router_w.shape[-1]
    s = jax.nn.sigmoid(jnp.dot(h, router_w, precision=lax.Precision.HIGHEST).astype(F32))
    grp = (s + router_b.astype(F32)).reshape(T, N_GROUPS, E // N_GROUPS)
    g_score = lax.top_k(grp, 2)[0].sum(-1)
    g_keep = jax.nn.one_hot(lax.top_k(g_score, TOPK_GROUPS)[1], N_GROUPS).sum(1) > 0
    choice = jnp.where(g_keep[:, :, None], grp, NEG_INF).reshape(T, E)
    e_idx = lax.top_k(choice, TOP_K)[1]
    w_sel = jnp.take_along_axis(s, e_idx, 1)
    w_sel = w_sel / w_sel.sum(-1, keepdims=True) * ROUTED_SCALE
    n_asg = T * TOP_K
    assert n_asg % MOE_BLOCK == 0
    i32 = jnp.int32
    flat_e = e_idx.reshape(-1)
    order = jnp.argsort(flat_e)
    pos_flat = jnp.zeros((n_asg,), i32).at[order].set(jnp.arange(n_asg, dtype=i32))
    counts = jnp.bincount(flat_e, length=E).astype(i32)
    ends = jnp.cumsum(counts).astype(i32)
    starts = ends - counts
    nb = n_asg // MOE_BLOCK
    first_blk = starts // MOE_BLOCK
    nblk = jnp.where(counts > 0, (ends - 1) // MOE_BLOCK - first_blk + 1, 0)
    item_ends = jnp.cumsum(nblk).astype(i32)
    item_starts = item_ends - nblk
    n_items = nb + E
    it = jnp.arange(n_items, dtype=i32)
    real = it < item_ends[-1]
    e_of = jnp.minimum(jnp.searchsorted(item_ends, it, side='right'), E - 1).astype(i32)
    e_of = jnp.where(real, e_of, e_of[item_ends[-1] - 1])
    blk = jnp.where(real, first_blk[e_of] + it - item_starts[e_of], nb - 1).astype(i32)
    lo = jnp.where(real, jnp.maximum(starts[e_of], blk * MOE_BLOCK), 0).astype(i32)
    hi = jnp.where(real, jnp.minimum(ends[e_of], (blk + 1) * MOE_BLOCK), 0).astype(i32)
    first = (real & (blk != jnp.concatenate([jnp.full((1,), -1, i32), blk[:-1]]))).astype(i32)
    xs = moe_dispatch_pallas(h, pos_flat)
    ys = moe_experts_pallas(xs, (blk, e_of, lo, hi, first), wg, wu, wd)
    return moe_combine_pallas(ys, pos_flat, w_sel, h, sg.astype(BF16), su.astype(BF16), sd.astype(BF16))


def kernel(x, c, ctx, c_ctx, ada_w, ada_b, ln_g, ln_b, ev_w_in, ev_w_out, na_rpb, rw_mu, rw_w0, rw_w_up,
           rw_a0, rw_a_up, rw_g_up, rw_k_k, rw_k_a, rw_r_k, rw_gn_g, rw_gn_b, od_w_in, od_w_out, ml_gate_b,
           ml_norm_g, moe_router, moe_bias, moe_w_gate, moe_w_up, moe_w_down, sh_w_gate, sh_w_up, sh_w_down):
    B, N, D = x.shape
    C = ctx.shape[1]
    silu_c = jax.nn.silu(c)
    silu_cc = jax.nn.silu(c_ctx)
    h_lat, h_ctx = x, ctx
    for l in range(DEPTH):
        last = l == DEPTH - 1
        mods = jnp.split(silu_c @ ada_w[l] + ada_b[l], 6, axis=-1)
        n_cm = 2 if last else 6
        mods_c = jnp.split(silu_cc @ ada_w[l][:, :n_cm * D] + ada_b[l][:n_cm * D], n_cm, axis=-1)
        a_lat = modulate(h_lat, mods[0][:, None], mods[1][:, None])
        a_ctx = modulate(h_ctx, mods_c[0], mods_c[1])
        if l % 2 == 0:
            e = l // 2
            y_lat, y_ctx = even_mixer(a_lat, a_ctx, ev_w_in[e], ev_w_out[e], na_rpb[e], rw_mu[e], rw_w0[e],
                                      rw_w_up[e], rw_a0[e], rw_a_up[e], rw_g_up[e], rw_k_k[e], rw_k_a[e],
                                      rw_r_k[e], rw_gn_g[e], rw_gn_b[e], not last)
        else:
            o = l // 2
            y_lat, y_ctx = odd_mixer(a_lat, a_ctx, od_w_in[o], od_w_out[o], ml_gate_b[o], ml_norm_g[o], not last)
        h_lat = layer_norm(DEEPNORM_ALPHA * h_lat + mods[2][:, None] * y_lat, ln_g[l, 0], ln_b[l, 0])
        f_lat = modulate(h_lat, mods[3][:, None], mods[4][:, None]).reshape(B * N, D)
        moe_args = (moe_router[l], moe_bias[l], moe_w_gate[l], moe_w_up[l], moe_w_down[l],
                    sh_w_gate[l], sh_w_up[l], sh_w_down[l])
        if last:
            ff = moe_ffn(f_lat, *moe_args).reshape(B, N, D)
            h_lat = layer_norm(DEEPNORM_ALPHA * h_lat + mods[5][:, None] * ff, ln_g[l, 1], ln_b[l, 1])
        else:
            h_ctx = layer_norm(DEEPNORM_ALPHA * h_ctx + mods_c[2] * y_ctx, ln_g[l, 0], ln_b[l, 0])
            f_ctx = modulate(h_ctx, mods_c[3], mods_c[4]).reshape(B * C, D)
            ff = moe_ffn(jnp.concatenate([f_lat, f_ctx], 0), *moe_args)
            h_lat = layer_norm(DEEPNORM_ALPHA * h_lat + mods[5][:, None] * ff[:B * N].reshape(B, N, D),
                               ln_g[l, 1], ln_b[l, 1])
            h_ctx = layer_norm(DEEPNORM_ALPHA * h_ctx + mods_c[5] * ff[B * N:].reshape(B, C, D),
                               ln_g[l, 1], ln_b[l, 1])
    return h_lat
```

```python
import functools

import jax
import jax.numpy as jnp
from jax import lax
from jax.experimental import pallas as pl
from jax.experimental.pallas import tpu as pltpu

D_MODEL = 1024
DEPTH = 2
GRID_W = 64

DEEPNORM_ALPHA = (2.0 * DEPTH) ** 0.25
LN_EPS = 1e-5
NEG_INF = -1e30
F32 = jnp.float32
BF16 = jnp.bfloat16

NA_HEAD_DIM = 64
NA_WIDTH = D_MODEL // 2
NA_HEADS = NA_WIDTH // NA_HEAD_DIM
NA_WIN_ROWS = 8
NA_WIN_COLS = 16
NA_SCALE = NA_HEAD_DIM ** -0.5

RW_HEAD_DIM = 64
RW_WIDTH = D_MODEL // 2
RW_HEADS = RW_WIDTH // RW_HEAD_DIM
RW_DECAY_LORA = 32
RW_AAA_LORA = 32
RW_GATE_LORA = 96
RW_GN_EPS = 64e-5

ML_HEADS = 8
ML_V_DIM = D_MODEL // ML_HEADS
ML_QK_DIM = ML_V_DIM // 2
ML_WIDTH = ML_HEADS * ML_V_DIM
ML_CHUNK = 128
ML_NORM_EPS = 1e-6
ROPE_BASE = 10000.0

N_EXPERTS = 256
TOP_K = 8
N_GROUPS = 8
TOPK_GROUPS = 4
ROUTED_SCALE = 2.5
MOE_BLOCK = 128

ODD_LAYOUT = (
    ('ml_q', ML_HEADS * ML_QK_DIM), ('ml_k', ML_HEADS * ML_QK_DIM),
    ('ml_v', ML_WIDTH), ('ml_o', ML_WIDTH),
    ('ml_if', ML_HEADS), ('ml_ib', ML_HEADS), ('ml_ff', ML_HEADS), ('ml_fb', ML_HEADS),
)
ODD_CTX_STATE_COLS = ('ml_k', 'ml_v', 'ml_if', 'ml_ib', 'ml_ff', 'ml_fb')

SUBLANES = 8
LANES = 128
VMEM_LIMIT_BYTES = 56 * 1024 * 1024

ROW_TILE = 256
N_MODS = 6
RW_COLS = 3 * RW_WIDTH + 2 * LANES
NT_DIMS = (((1,), (1,)), ((), ()))


def _cparams(n_axes):
    return pltpu.CompilerParams(dimension_semantics=("arbitrary",) * n_axes, vmem_limit_bytes=VMEM_LIMIT_BYTES)


def _full_spec(shape):
    return pl.BlockSpec(shape, lambda *_: (0,) * len(shape))


def _split_bf16(x):
    hi = x.astype(BF16)
    lo = (x - hi.astype(F32)).astype(BF16)
    return jnp.concatenate([hi, lo], axis=-1)


def _block_ones(n_rows, n_cols, seg):
    row = lax.broadcasted_iota(jnp.int32, (n_rows, n_cols), 0)
    col = lax.broadcasted_iota(jnp.int32, (n_rows, n_cols), 1)
    return (((row % n_cols) // seg) == (col // seg)).astype(BF16)


def _seg_sum(x, ones2):
    return jnp.dot(_split_bf16(x), ones2, preferred_element_type=F32)


def _mod_index(tiles_per_batch, ctx_tiles, n_batch):
    def idx(i):
        return jnp.where(i % tiles_per_batch < ctx_tiles, n_batch, i // tiles_per_batch)
    return idx


def _mod_spec(chunk, mod_idx):
    return pl.BlockSpec((None, None, 1, D_MODEL), lambda i: (mod_idx(i), chunk, 0, 0))


def _ada_kernel(c_ref, w_ref, b_ref, o_ref):
    c = c_ref[...]
    x = (c * jax.nn.sigmoid(c)).astype(BF16)
    o_ref[...] = jnp.dot(x, w_ref[...].astype(BF16), preferred_element_type=F32) + b_ref[...]


def ada_mods_pallas(cond, w, b):
    n, D = cond.shape
    n_out = w.shape[1]
    tn = 512
    out = pl.pallas_call(
        _ada_kernel,
        grid=(n_out // tn,),
        in_specs=[_full_spec((n, D)), pl.BlockSpec((D, tn), lambda j: (0, j)), pl.BlockSpec((1, tn), lambda j: (0, j))],
        out_specs=pl.BlockSpec((n, tn), lambda j: (0, j)),
        out_shape=jax.ShapeDtypeStruct((n, n_out), F32),
        compiler_params=_cparams(1),
        name="ada_mods",
    )(cond, w, b.reshape(1, n_out))
    return out.reshape(n, N_MODS, 1, D)


def _softplus(x):
    return jnp.maximum(x, 0.0) + jnp.log(1.0 + jnp.exp(-jnp.abs(x)))


def _proj_even_kernel(h_ref, hp_ref, hn_ref, shift_ref, scale_ref, wna_ref, wrw_ref, mu_ref, ones_ref,
                      kk_ref, ka_ref, w0_ref, a0_ref, wup_ref, aup_ref,
                      q_ref, k_ref, v_ref, dec_f_ref, dec_b_ref, beta_f_ref, beta_b_ref, kd_f_ref, kd_b_ref,
                      nkk_ref, rv_ref, rr_ref, glow_ref, *, tiles_per_batch, ctx_tiles):
    i = pl.program_id(0)
    j = i % tiles_per_batch
    first = (j == 0) | (j == ctx_tiles)
    last = (j == ctx_tiles - 1) | (j == tiles_per_batch - 1)
    tm = h_ref.shape[0]
    gain = 1.0 + scale_ref[...]
    shift = shift_ref[...]
    a = h_ref[...] * gain + shift
    a_prev = jnp.where(first, 0.0, hp_ref[SUBLANES - 1:SUBLANES, :] * gain + shift)
    a_next = jnp.where(last, 0.0, hn_ref[0:1, :] * gain + shift)
    rid = lax.broadcasted_iota(jnp.int32, (tm, 1), 0)
    prev = jnp.where(rid == 0, a_prev, pltpu.roll(a, 1, 0))
    nxt = jnp.where(rid == tm - 1, a_next, pltpu.roll(a, tm - 1, 0))
    a16 = a.astype(BF16)
    nb16 = (0.5 * (prev + nxt)).astype(BF16)

    na = jnp.dot(a16, wna_ref[...], preferred_element_type=F32)
    q_ref[...] = (na[:, :NA_WIDTH] * NA_SCALE).astype(BF16)
    k_ref[...] = na[:, NA_WIDTH:2 * NA_WIDTH].astype(BF16)
    v_ref[...] = na[:, 2 * NA_WIDTH:].astype(BF16)

    pa = jnp.dot(a16, wrw_ref[...], preferred_element_type=F32)
    pn = jnp.dot(nb16, wrw_ref[...], preferred_element_type=F32)
    t = pa + mu_ref[...] * (pn - pa)
    r = t[:, :RW_WIDTH]
    k = t[:, RW_WIDTH:2 * RW_WIDTH]
    lora = t[:, 3 * RW_WIDTH:3 * RW_WIDTH + LANES]
    rr_ref[...] = r
    rv_ref[...] = t[:, 2 * RW_WIDTH:3 * RW_WIDTH]
    glow_ref[...] = t[:, 3 * RW_WIDTH + LANES:]

    kk = k * kk_ref[...]
    norm = jnp.sqrt(_seg_sum(kk * kk, ones_ref[...]))
    kk = kk / jnp.maximum(norm, 1e-12)
    nkk_ref[...] = -kk
    lora_t = jnp.tanh(lora).astype(BF16)
    lora16 = lora.astype(BF16)
    outs = ((dec_f_ref, beta_f_ref, kd_f_ref), (dec_b_ref, beta_b_ref, kd_b_ref))
    for d in range(2):
        w_log = -_softplus(-(w0_ref[d:d + 1, :] + jnp.dot(lora_t, wup_ref[d], preferred_element_type=F32))) - 0.5
        a_gate = jax.nn.sigmoid(a0_ref[d:d + 1, :] + jnp.dot(lora16, aup_ref[d], preferred_element_type=F32))
        outs[d][0][...] = jnp.exp(-jnp.exp(w_log))
        outs[d][1][...] = kk * a_gate
        outs[d][2][...] = k * (1.0 + (a_gate - 1.0) * ka_ref[...])


def proj_even_pallas(h, mods, dims, w_in, mu, w0, w_up, a0, a_up, k_k, k_a):
    B, C, N = dims
    R, D = h.shape
    tm = ROW_TILE
    tpb, ctx_tiles = (C + N) // tm, C // tm
    mod_idx = _mod_index(tpb, ctx_tiles, B)
    w_na = w_in[:, :3 * NA_WIDTH].astype(BF16)
    n_rw = w_in.shape[1] - 3 * NA_WIDTH
    w_rw = jnp.pad(w_in[:, 3 * NA_WIDTH:], ((0, 0), (0, RW_COLS - n_rw))).astype(BF16)
    mu_p = jnp.pad(mu, (0, RW_COLS - n_rw)).reshape(1, RW_COLS)
    ones2 = _block_ones(2 * RW_WIDTH, RW_WIDTH, RW_HEAD_DIM)
    lr = RW_DECAY_LORA

    def pad_up(m, first_row):
        out = jnp.zeros((2, LANES, RW_WIDTH), F32)
        for d in range(2):
            out = out.at[d, first_row + d * lr:first_row + (d + 1) * lr].set(m[d])
        return out.astype(BF16)

    row = lambda width: pl.BlockSpec((tm, width), lambda i: (i, 0))
    hb = tm // SUBLANES
    n_hb = R // SUBLANES
    wide = jax.ShapeDtypeStruct((R, RW_WIDTH), F32)
    half = jax.ShapeDtypeStruct((R, NA_WIDTH), BF16)
    return pl.pallas_call(
        functools.partial(_proj_even_kernel, tiles_per_batch=tpb, ctx_tiles=ctx_tiles),
        grid=(R // tm,),
        in_specs=[
            row(D),
            pl.BlockSpec((SUBLANES, D), lambda i: (jnp.maximum(i * hb - 1, 0), 0)),
            pl.BlockSpec((SUBLANES, D), lambda i: (jnp.minimum((i + 1) * hb, n_hb - 1), 0)),
            _mod_spec(0, mod_idx), _mod_spec(1, mod_idx),
            _full_spec((D, 3 * NA_WIDTH)), _full_spec((D, RW_COLS)), _full_spec((1, RW_COLS)),
            _full_spec((2 * RW_WIDTH, RW_WIDTH)),
            _full_spec((1, RW_WIDTH)), _full_spec((1, RW_WIDTH)),
            _full_spec((2, RW_WIDTH)), _full_spec((2, RW_WIDTH)),
            _full_spec((2, LANES, RW_WIDTH)), _full_spec((2, LANES, RW_WIDTH)),
        ],
        out_specs=[row(NA_WIDTH)] * 3 + [row(RW_WIDTH)] * 9 + [row(LANES)],
        out_shape=[half] * 3 + [wide] * 9 + [jax.ShapeDtypeStruct((R, LANES), F32)],
        compiler_params=_cparams(1),
        name="proj_even",
    )(h, h, h, mods, mods, w_na, w_rw, mu_p, ones2, k_k.reshape(1, -1), k_a.reshape(1, -1), w0, a0,
      pad_up(w_up, 0), pad_up(a_up, 2 * lr))


RW_SCAN_TIME = 256


def _rwkv_scan_kernel(wf_ref, bf_ref, kf_ref, nf_ref, vf_ref, rf_ref,
                      wb_ref, bb_ref, kb_ref, nb_ref, vb_ref, rb_ref, yf_ref, yb_ref, s_ref):
    @pl.when(pl.program_id(0) == 0)
    def _():
        s_ref[...] = jnp.zeros_like(s_ref)

    n_batch, n_time, width = wf_ref.shape
    n_pair = width // LANES
    n_dir_chain = n_batch * n_pair
    n_chain = 2 * n_dir_chain
    rows_all = n_chain * RW_HEAD_DIM
    ones2 = _block_ones(2 * LANES, LANES, RW_HEAD_DIM)
    vi = lax.broadcasted_iota(jnp.int32, (1, RW_HEAD_DIM, LANES), 1)
    li = lax.broadcasted_iota(jnp.int32, (1, RW_HEAD_DIM, LANES), 2)
    diag = (li % RW_HEAD_DIM) == vi
    n_sub = n_time // SUBLANES

    def seg(x):
        out = jnp.dot(_split_bf16(x.reshape(rows_all, LANES)), ones2, preferred_element_type=F32)
        return out.reshape(n_chain, RW_HEAD_DIM, LANES)

    def chains(ref, rows):
        x = ref[:, rows, :]
        return [x[b, :, p * LANES:(p + 1) * LANES] for b in range(n_batch) for p in range(n_pair)]

    def sub(i, carry):
        rows_f = pl.ds(pl.multiple_of(i * SUBLANES, SUBLANES), SUBLANES)
        rows_b = pl.ds(pl.multiple_of((n_sub - 1 - i) * SUBLANES, SUBLANES), SUBLANES)
        load = lambda f_ref, b_ref: (jnp.stack(chains(f_ref, rows_f)), jnp.stack(chains(b_ref, rows_b)))
        w8, beta8, kd8 = load(wf_ref, wb_ref), load(bf_ref, bb_ref), load(kf_ref, kb_ref)
        nkk8, v8, r8 = load(nf_ref, nb_ref), load(vf_ref, vb_ref), load(rf_ref, rb_ref)

        def at(pair, t):
            tb = SUBLANES - 1 - t
            return jnp.concatenate([pair[0][:, t:t + 1, :], pair[1][:, tb:tb + 1, :]], axis=0)

        s = s_ref[...]
        rows = []
        for t in range(SUBLANES):
            vcol = seg(jnp.where(diag, at(v8, t), 0.0))
            sa = seg(s * at(nkk8, t))
            s = s * at(w8, t) + sa * at(beta8, t) + vcol * at(kd8, t)
            ybc = seg(s * at(r8, t))
            rows.append(jnp.sum(jnp.where(diag, ybc, 0.0), axis=1, keepdims=True))
        s_ref[...] = s
        y_f = jnp.concatenate([row[:n_dir_chain] for row in rows], axis=1)
        y_b = jnp.concatenate([row[n_dir_chain:] for row in rows[::-1]], axis=1)
        for b in range(n_batch):
            for p in range(n_pair):
                c = b * n_pair + p
                yf_ref[b, rows_f, p * LANES:(p + 1) * LANES] = y_f[c]
                yb_ref[b, rows_b, p * LANES:(p + 1) * LANES] = y_b[c]
        return carry

    lax.fori_loop(0, n_sub, sub, 0)


def rwkv_scan_pallas(dec_f, beta_f, kd_f, dec_b, beta_b, kd_b, nkk, v, r, dims):
    B, C, N = dims
    S = C + N
    tc = RW_SCAN_TIME
    assert C % tc == 0 and N % tc == 0
    n_ctx, n_all = C // tc, S // tc
    as3 = lambda z: z.reshape(B, S, RW_WIDTH)
    fwd = pl.BlockSpec((B, tc, RW_WIDTH), lambda j: (0, j, 0))
    bwd = pl.BlockSpec((B, tc, RW_WIDTH),
                       lambda j: (0, jnp.where(j < n_ctx, n_ctx - 1 - j, n_all - 1 - (j - n_ctx)), 0))
    out = jax.ShapeDtypeStruct((B, S, RW_WIDTH), F32)
    y_f, y_b = pl.pallas_call(
        _rwkv_scan_kernel,
        grid=(n_all,),
        in_specs=[fwd] * 6 + [bwd] * 6,
        out_specs=[fwd, bwd],
        out_shape=[out, out],
        scratch_shapes=[pltpu.VMEM((2 * B * (RW_WIDTH // LANES), RW_HEAD_DIM, LANES), F32)],
        compiler_params=_cparams(1),
        name="rwkv_scan",
    )(as3(dec_f), as3(beta_f), as3(kd_f), as3(nkk), as3(v), as3(r),
      as3(dec_b), as3(beta_b), as3(kd_b), as3(nkk), as3(v), as3(r))
    return y_f.reshape(B * S, RW_WIDTH), y_b.reshape(B * S, RW_WIDTH)


NA_BAND = NA_WIN_ROWS * GRID_W


def _na_row_start(j, ctx_blocks, n_rows):
    r = jnp.maximum(j - ctx_blocks, 0)
    return r, jnp.clip(r - NA_WIN_ROWS // 2, 0, n_rows - NA_WIN_ROWS)


def _na_kernel(q_ref, k_ref, v_ref, bias_ref, o_ref, *, n_ctx):
    j = pl.program_id(1)
    ctx_blocks = n_ctx // GRID_W
    n_rows = pl.num_programs(1) - ctx_blocks
    _, row_start = _na_row_start(j, ctx_blocks, n_rows)
    start = pl.multiple_of(n_ctx + row_start * GRID_W, GRID_W)
    q = q_ref[0]
    kb = k_ref[0, pl.ds(start, NA_BAND), :]
    vb = v_ref[0, pl.ds(start, NA_BAND), :]
    kc = k_ref[0, pl.ds(0, n_ctx), :]
    vc = v_ref[0, pl.ds(0, n_ctx), :]
    head_of_lane = lax.broadcasted_iota(jnp.int32, (GRID_W, LANES), 1) // NA_HEAD_DIM
    for p in range(NA_WIDTH // LANES):
        cols = slice(p * LANES, (p + 1) * LANES)
        qp, kp, vp, kcp, vcp = q[:, cols], kb[:, cols], vb[:, cols], kc[:, cols], vc[:, cols]
        outs = []
        for h2 in range(LANES // NA_HEAD_DIM):
            qm = jnp.where(head_of_lane == h2, qp, jnp.zeros_like(qp))
            s_loc = lax.dot_general(qm, kp, NT_DIMS, preferred_element_type=F32) + bias_ref[0, 2 * p + h2]
            s_ctx = lax.dot_general(qm, kcp, NT_DIMS, preferred_element_type=F32)
            m = jnp.maximum(jnp.max(s_loc, axis=-1, keepdims=True), jnp.max(s_ctx, axis=-1, keepdims=True))
            e_loc = jnp.exp(s_loc - m)
            e_ctx = jnp.exp(s_ctx - m)
            den = jnp.sum(e_loc, axis=-1, keepdims=True) + jnp.sum(e_ctx, axis=-1, keepdims=True)
            o = (jnp.dot(e_loc.astype(BF16), vp, preferred_element_type=F32)
                 + jnp.dot(e_ctx.astype(BF16), vcp, preferred_element_type=F32))
            outs.append(o / den)
        o_ref[0, :, cols] = jnp.where(head_of_lane == 0, outs[0], outs[1])


def _na_bias_table(rpb):
    kw = NA_WIN_COLS
    j = jnp.arange(GRID_W)
    col_start = jnp.clip(j - kw // 2, 0, GRID_W - kw)
    col_in = (j[None, :] >= col_start[:, None]) & (j[None, :] < col_start[:, None] + kw)
    col_off = jnp.clip(j[None, :] - j[:, None], -(kw - 1), kw - 1) + (kw - 1)
    d = jnp.arange(NA_WIN_ROWS)
    a = jnp.arange(NA_WIN_ROWS)
    row_off = a[None, :] - d[:, None] + (NA_WIN_ROWS - 1)
    tab = rpb.astype(F32)[:, row_off[:, :, None, None], col_off[None, None, :, :]]
    tab = jnp.where(col_in[None, None, None], tab, NEG_INF)
    tab = tab.transpose(1, 0, 3, 2, 4).reshape(NA_WIN_ROWS, NA_HEADS, GRID_W, NA_BAND)
    return jnp.concatenate([tab, jnp.full((1,) + tab.shape[1:], NEG_INF, F32)], 0)


def attention_pallas(q, k, v, rpb, dims):
    B, C, N = dims
    S = C + N
    W = NA_WIDTH
    n_rows = N // GRID_W
    ctx_blocks = C // GRID_W
    assert n_rows >= NA_WIN_ROWS and N % GRID_W == 0 and C % GRID_W == 0
    bias = _na_bias_table(rpb)
    as3 = lambda z: z.reshape(B, S, W)

    def bias_idx(b, j):
        r, row_start = _na_row_start(j, ctx_blocks, n_rows)
        return (jnp.where(j < ctx_blocks, NA_WIN_ROWS, r - row_start), 0, 0, 0)

    out = pl.pallas_call(
        functools.partial(_na_kernel, n_ctx=C),
        grid=(B, S // GRID_W),
        in_specs=[
            pl.BlockSpec((1, GRID_W, W), lambda b, j: (b, j, 0)),
            pl.BlockSpec((1, S, W), lambda b, j: (b, 0, 0)),
            pl.BlockSpec((1, S, W), lambda b, j: (b, 0, 0)),
            pl.BlockSpec((1, NA_HEADS, GRID_W, NA_BAND), bias_idx),
        ],
        out_specs=pl.BlockSpec((1, GRID_W, W), lambda b, j: (b, j, 0)),
        out_shape=jax.ShapeDtypeStruct((B, S, W), F32),
        compiler_params=_cparams(2),
        name="na_attention",
    )(as3(q), as3(k), as3(v), bias)
    return out.reshape(B * S, W)


def _layer_norm(x, g, b):
    mu = jnp.mean(x, axis=-1, keepdims=True)
    xc = x - mu
    var = jnp.mean(xc * xc, axis=-1, keepdims=True)
    return xc * lax.rsqrt(var + LN_EPS) * g + b


def _mixer_tail(h, y, gate, ln_g, ln_b, shift, scale, router, h_out_ref, f_ref, s_ref):
    h1 = _layer_norm(DEEPNORM_ALPHA * h + gate * y, ln_g, ln_b)
    f = h1 * (1.0 + scale) + shift
    h_out_ref[...] = h1
    f_ref[...] = f
    s_ref[...] = jax.nn.sigmoid(jnp.dot(f, router, preferred_element_type=F32, precision=lax.Precision.HIGHEST))


def _even_out_kernel(na_ref, yf_ref, yb_ref, r_ref, v_ref, kdf_ref, kdb_ref, glow_ref, h_ref,
                     gate_ref, shift_ref, scale_ref, ones_ref, gng_ref, gnb_ref, rk_ref, gup_ref, wout_ref,
                     lng_ref, lnb_ref, router_ref, h_out_ref, f_ref, s_ref):
    ones2 = ones_ref[...]
    inv = 1.0 / RW_HEAD_DIM
    y = yf_ref[...] + yb_ref[...]
    mu = _seg_sum(y, ones2) * inv
    yc = y - mu
    var = _seg_sum(yc * yc, ones2) * inv
    yn = yc * lax.rsqrt(var + RW_GN_EPS) * gng_ref[...] + gnb_ref[...]
    r = r_ref[...]
    bonus = (_seg_sum(r * kdf_ref[...] * rk_ref[...], ones2) + _seg_sum(r * kdb_ref[...] * rk_ref[...], ones2))
    gate = jnp.dot(jax.nn.sigmoid(glow_ref[...]).astype(BF16), gup_ref[...], preferred_element_type=F32)
    rw = (yn + bonus * v_ref[...]) * gate
    mix = jnp.concatenate([na_ref[...], rw], axis=-1).astype(BF16)
    y_mix = jnp.dot(mix, wout_ref[...], preferred_element_type=F32)
    _mixer_tail(h_ref[...], y_mix, gate_ref[...], lng_ref[...], lnb_ref[...], shift_ref[...], scale_ref[...],
                router_ref[...], h_out_ref, f_ref, s_ref)


def _tail_specs(R, D, E, tm):
    row = lambda width: pl.BlockSpec((tm, width), lambda i: (i, 0))
    return ([row(D), row(D), row(E)],
            [jax.ShapeDtypeStruct((R, D), F32), jax.ShapeDtypeStruct((R, D), F32), jax.ShapeDtypeStruct((R, E), F32)])


def even_out_pallas(na, y_f, y_b, r, v, kd_f, kd_b, glow, h, mods, dims, g_up, r_k, gn_g, gn_b, w_out,
                    ln_g, ln_b, router_w):
    B, C, N = dims
    R, D = h.shape
    E = router_w.shape[1]
    tm = ROW_TILE
    mod_idx = _mod_index((C + N) // tm, C // tm, B)
    row = lambda width: pl.BlockSpec((tm, width), lambda i: (i, 0))
    ones2 = _block_ones(2 * RW_WIDTH, RW_WIDTH, RW_HEAD_DIM)
    g_up_p = jnp.pad(g_up, ((0, LANES - g_up.shape[0]), (0, 0))).astype(BF16)
    vec = lambda z: z.reshape(1, -1)
    out_specs, out_shape = _tail_specs(R, D, E, tm)
    return pl.pallas_call(
        _even_out_kernel,
        grid=(R // tm,),
        in_specs=[row(NA_WIDTH)] + [row(RW_WIDTH)] * 6 + [row(LANES), row(D),
                  _mod_spec(2, mod_idx), _mod_spec(3, mod_idx), _mod_spec(4, mod_idx),
                  _full_spec((2 * RW_WIDTH, RW_WIDTH)),
                  _full_spec((1, RW_WIDTH)), _full_spec((1, RW_WIDTH)), _full_spec((1, RW_WIDTH)),
                  _full_spec((LANES, RW_WIDTH)), _full_spec((D, D)),
                  _full_spec((1, D)), _full_spec((1, D)), _full_spec((D, E))],
        out_specs=out_specs,
        out_shape=out_shape,
        compiler_params=_cparams(1),
        name="even_out",
    )(na, y_f, y_b, r, v, kd_f, kd_b, glow, h, mods, mods, mods, ones2, vec(gn_g), vec(gn_b), vec(r_k),
      g_up_p, w_out.astype(BF16), vec(ln_g), vec(ln_b), router_w)


def _resid_tail_kernel(y_ref, h_ref, gate_ref, shift_ref, scale_ref, lng_ref, lnb_ref, router_ref,
                       h_out_ref, f_ref, s_ref):
    _mixer_tail(h_ref[...], y_ref[...], gate_ref[...], lng_ref[...], lnb_ref[...], shift_ref[...],
                scale_ref[...], router_ref[...], h_out_ref, f_ref, s_ref)


def resid_tail_pallas(y, h, mods, dims, ln_g, ln_b, router_w):
    B, C, N = dims
    R, D = h.shape
    E = router_w.shape[1]
    tm = ROW_TILE
    mod_idx = _mod_index((C + N) // tm, C // tm, B)
    row = lambda width: pl.BlockSpec((tm, width), lambda i: (i, 0))
    vec = lambda z: z.reshape(1, -1)
    out_specs, out_shape = _tail_specs(R, D, E, tm)
    return pl.pallas_call(
        _resid_tail_kernel,
        grid=(R // tm,),
        in_specs=[row(D), row(D), _mod_spec(2, mod_idx), _mod_spec(3, mod_idx), _mod_spec(4, mod_idx),
                  _full_spec((1, D)), _full_spec((1, D)), _full_spec((D, E))],
        out_specs=out_specs,
        out_shape=out_shape,
        compiler_params=_cparams(1),
        name="resid_tail",
    )(y, h, mods, mods, mods, vec(ln_g), vec(ln_b), router_w)


MOE_TOKEN_TILE = 128


def _swiglu_bf16(x, wg, wu, wd):
    g = jnp.dot(x, wg, preferred_element_type=F32)
    u = jnp.dot(x, wu, preferred_element_type=F32)
    mid = (g * jax.nn.sigmoid(g) * u).astype(BF16)
    return jnp.dot(mid, wd, preferred_element_type=F32)


def _row_copy(src_ref, src_row, dst_ref, dst_row, sem):
    return pltpu.make_async_copy(src_ref.at[pl.ds(src_row, 1), :], dst_ref.at[pl.ds(dst_row, 1), :], sem)


def _dispatch_kernel(pos_ref, f_ref, xs_ref, sem):
    n_tok = f_ref.shape[0]

    def issue(t, carry):
        for k in range(TOP_K):
            _row_copy(f_ref, t, xs_ref, pos_ref[t * TOP_K + k], sem).start()
        return carry

    lax.fori_loop(0, n_tok, issue, 0)

    def drain(t, carry):
        for k in range(TOP_K):
            _row_copy(f_ref, 0, xs_ref, 0, sem).wait()
        return carry

    lax.fori_loop(0, n_tok, drain, 0)


def moe_dispatch_pallas(f, pos_flat):
    T, D = f.shape
    tm = MOE_TOKEN_TILE
    assert T % tm == 0
    return pl.pallas_call(
        _dispatch_kernel,
        grid=(T // tm,),
        in_specs=[
            pl.BlockSpec((tm * TOP_K,), lambda i: (i,), memory_space=pltpu.SMEM),
            pl.BlockSpec((tm, D), lambda i: (i, 0)),
        ],
        out_specs=pl.BlockSpec(memory_space=pl.ANY),
        out_shape=jax.ShapeDtypeStruct((T * TOP_K, D), F32),
        scratch_shapes=[pltpu.SemaphoreType.DMA(())],
        compiler_params=_cparams(1),
        name="moe_dispatch",
    )(pos_flat, f)


def _expert_item_kernel(blk_ref, e_ref, lo_ref, hi_ref, first_ref, x_ref, wg_ref, wu_ref, wd_ref, o_ref):
    del e_ref
    i = pl.program_id(0)
    lo, hi = lo_ref[i], hi_ref[i]

    @pl.when(hi > lo)
    def _():
        y = _swiglu_bf16(x_ref[...].astype(BF16), wg_ref[0].astype(BF16), wu_ref[0].astype(BF16),
                         wd_ref[0].astype(BF16))
        rows = blk_ref[i] * MOE_BLOCK + lax.broadcasted_iota(jnp.int32, (MOE_BLOCK, 1), 0)
        y = jnp.where((rows >= lo) & (rows < hi), y, 0.0)

        @pl.when(first_ref[i] == 1)
        def _():
            o_ref[...] = y

        @pl.when(first_ref[i] == 0)
        def _():
            o_ref[...] += y


def moe_experts_pallas(xs, items, wg, wu, wd):
    n_rows, D = xs.shape
    F = wg.shape[-1]
    n_items = items[0].shape[0]
    grid_spec = pltpu.PrefetchScalarGridSpec(
        num_scalar_prefetch=5,
        grid=(n_items,),
        in_specs=[
            pl.BlockSpec((MOE_BLOCK, D), lambda i, blk, e, lo, hi, first: (blk[i], 0)),
            pl.BlockSpec((1, D, F), lambda i, blk, e, lo, hi, first: (e[i], 0, 0)),
            pl.BlockSpec((1, D, F), lambda i, blk, e, lo, hi, first: (e[i], 0, 0)),
            pl.BlockSpec((1, F, D), lambda i, blk, e, lo, hi, first: (e[i], 0, 0)),
        ],
        out_specs=pl.BlockSpec((MOE_BLOCK, D), lambda i, blk, e, lo, hi, first: (blk[i], 0)),
    )
    return pl.pallas_call(
        _expert_item_kernel,
        grid_spec=grid_spec,
        out_shape=jax.ShapeDtypeStruct((n_rows, D), F32),
        compiler_params=_cparams(1),
        name="moe_experts",
    )(*items, xs, wg, wu, wd)


def _combine_kernel(pos_ref, w_ref, f_ref, h_ref, gate_ref, lng_ref, lnb_ref, sg_ref, su_ref, sd_ref, ys_ref,
                    o_ref, buf_ref, sem):
    n_tok = f_ref.shape[0]

    def issue(t, carry):
        for k in range(TOP_K):
            pltpu.make_async_copy(ys_ref.at[pl.ds(pos_ref[t * TOP_K + k], 1), :],
                                  buf_ref.at[k, pl.ds(t, 1), :], sem).start()
        return carry

    lax.fori_loop(0, n_tok, issue, 0)
    acc = _swiglu_bf16(f_ref[...].astype(BF16), sg_ref[...], su_ref[...], sd_ref[...])

    def drain(t, carry):
        for k in range(TOP_K):
            pltpu.make_async_copy(ys_ref.at[pl.ds(0, 1), :], buf_ref.at[0, pl.ds(0, 1), :], sem).wait()
        return carry

    lax.fori_loop(0, n_tok, drain, 0)
    w = w_ref[...]
    for k in range(TOP_K):
        acc = acc + w[:, k:k + 1] * buf_ref[k]
    o_ref[...] = _layer_norm(DEEPNORM_ALPHA * h_ref[...] + gate_ref[...] * acc, lng_ref[...], lnb_ref[...])


def moe_combine_pallas(ys, pos_flat, w_sel, f, h, mods, dims, ln_g, ln_b, sg, su, sd):
    B, C, N = dims
    T, D = f.shape
    tm = MOE_TOKEN_TILE
    F = sg.shape[-1]
    mod_idx = _mod_index((C + N) // tm, C // tm, B)
    vec = lambda z: z.reshape(1, -1)
    return pl.pallas_call(
        _combine_kernel,
        grid=(T // tm,),
        in_specs=[
            pl.BlockSpec((tm * TOP_K,), lambda i: (i,), memory_space=pltpu.SMEM),
            pl.BlockSpec((tm, TOP_K), lambda i: (i, 0)),
            pl.BlockSpec((tm, D), lambda i: (i, 0)),
            pl.BlockSpec((tm, D), lambda i: (i, 0)),
            _mod_spec(5, mod_idx), _full_spec((1, D)), _full_spec((1, D)),
            _full_spec((D, F)), _full_spec((D, F)), _full_spec((F, D)),
            pl.BlockSpec(memory_space=pl.ANY),
        ],
        out_specs=pl.BlockSpec((tm, D), lambda i: (i, 0)),
        out_shape=jax.ShapeDtypeStruct((T, D), F32),
        scratch_shapes=[pltpu.VMEM((TOP_K, tm, D), F32), pltpu.SemaphoreType.DMA(())],
        compiler_params=_cparams(1),
        name="moe_combine",
    )(pos_flat, w_sel, f, h, mods, vec(ln_g), vec(ln_b), sg, su, sd, ys)


def moe_layer(f, s, h, mods, dims, ln_g, ln_b, router_b, wg, wu, wd, sg, su, sd):
    T, D = f.shape
    E = s.shape[-1]
    grp = (s + router_b.astype(F32)).reshape(T, N_GROUPS, E // N_GROUPS)
    g_score = lax.top_k(grp, 2)[0].sum(-1)
    g_keep = jax.nn.one_hot(lax.top_k(g_score, TOPK_GROUPS)[1], N_GROUPS).sum(1) > 0
    choice = jnp.where(g_keep[:, :, None], grp, NEG_INF).reshape(T, E)
    e_idx = lax.top_k(choice, TOP_K)[1]
    w_sel = jnp.take_along_axis(s, e_idx, 1)
    w_sel = w_sel / w_sel.sum(-1, keepdims=True) * ROUTED_SCALE
    n_asg = T * TOP_K
    assert n_asg % MOE_BLOCK == 0
    i32 = jnp.int32
    flat_e = e_idx.reshape(-1)
    order = jnp.argsort(flat_e)
    pos_flat = jnp.zeros((n_asg,), i32).at[order].set(jnp.arange(n_asg, dtype=i32))
    counts = jnp.bincount(flat_e, length=E).astype(i32)
    ends = jnp.cumsum(counts).astype(i32)
    starts = ends - counts
    nb = n_asg // MOE_BLOCK
    first_blk = starts // MOE_BLOCK
    nblk = jnp.where(counts > 0, (ends - 1) // MOE_BLOCK - first_blk + 1, 0)
    item_ends = jnp.cumsum(nblk).astype(i32)
    item_starts = item_ends - nblk
    n_items = nb + E
    it = jnp.arange(n_items, dtype=i32)
    real = it < item_ends[-1]
    e_of = jnp.minimum(jnp.searchsorted(item_ends, it, side='right'), E - 1).astype(i32)
    e_of = jnp.where(real, e_of, e_of[item_ends[-1] - 1])
    blk = jnp.where(real, first_blk[e_of] + it - item_starts[e_of], nb - 1).astype(i32)
    lo = jnp.where(real, jnp.maximum(starts[e_of], blk * MOE_BLOCK), 0).astype(i32)
    hi = jnp.where(real, jnp.minimum(ends[e_of], (blk + 1) * MOE_BLOCK), 0).astype(i32)
    first = (real & (blk != jnp.concatenate([jnp.full((1,), -1, i32), blk[:-1]]))).astype(i32)
    xs = moe_dispatch_pallas(f, pos_flat)
    ys = moe_experts_pallas(xs, (blk, e_of, lo, hi, first), wg, wu, wd)
    return moe_combine_pallas(ys, pos_flat, w_sel, f, h, mods, dims, ln_g, ln_b,
                              sg.astype(BF16), su.astype(BF16), sd.astype(BF16))


def _offsets(layout, prefix=''):
    offs, o = {}, 0
    for name, width in layout:
        if name.startswith(prefix):
            offs[name] = (o, width)
            o += width
    return offs


def project(h, w, layout, names):
    offs = _offsets(layout)
    if len(names) == len(layout):
        y = jnp.einsum('btd,de->bte', h, w)
        return {n: y[..., offs[n][0]:offs[n][0] + offs[n][1]] for n in names}
    return {n: jnp.einsum('btd,de->bte', h, w[:, offs[n][0]:offs[n][0] + offs[n][1]]) for n in names}


def axial_rope(z):
    T, dh = z.shape[1], z.shape[-1]
    half = dh // 2
    nf = half // 2
    t = jnp.arange(T)
    row = (t // GRID_W).astype(F32)
    col = (t % GRID_W).astype(F32)
    inv = ROPE_BASE ** (-jnp.arange(nf, dtype=F32) / nf)

    def rot(u, pos):
        ang = pos[:, None] * inv[None, :]
        cos = jnp.cos(ang)[None, :, None, :]
        sin = jnp.sin(ang)[None, :, None, :]
        u1, u2 = u[..., :nf], u[..., nf:]
        return jnp.concatenate([u1 * cos - u2 * sin, u1 * sin + u2 * cos], -1)

    return jnp.concatenate([rot(z[..., :half], row), rot(z[..., half:], col)], -1).astype(z.dtype)


def ml_prep(t, gate_b, rope, need_q):
    B, T = t['ml_k'].shape[:2]
    heads = lambda z, dh: z.reshape(B, T, ML_HEADS, dh).astype(F32)
    k = heads(t['ml_k'], ML_QK_DIM)
    q = heads(t['ml_q'], ML_QK_DIM) * ML_QK_DIM ** -0.5 if need_q else None
    if rope:
        k = axial_rope(k)
        q = axial_rope(q)
    v = heads(t['ml_v'], ML_V_DIM)
    gb = gate_b.astype(F32)
    bht = lambda z: z.astype(F32).transpose(0, 2, 1)
    ig = (bht(t['ml_if'] + gb[0]), bht(t['ml_ib'] + gb[1]))
    lf = (jax.nn.log_sigmoid(bht(t['ml_ff'] + gb[2])), jax.nn.log_sigmoid(bht(t['ml_fb'] + gb[3])))
    bhtd = lambda z: None if z is None else z.transpose(0, 2, 1, 3)
    return bhtd(q), bhtd(k), bhtd(v), ig, lf


def ml_chunk_states(k, v, ig, lf, state0):
    B, H, T, dk = k.shape
    dv = v.shape[-1]
    L = min(ML_CHUNK, T)
    nc = T // L
    kc = k.reshape(B, H, nc, L, dk)
    vc = v.reshape(B, H, nc, L, dv)
    b = jnp.cumsum(lf.reshape(B, H, nc, L), -1)
    b_end = b[..., -1]
    g = b_end[..., None] - b + ig.reshape(B, H, nc, L)
    m_chunk = g.max(-1)
    wgt = jnp.exp(g - m_chunk[..., None])
    kv = jnp.einsum('bhnl,bhnlk,bhnlv->bhnkv', wgt, kc, vc)
    ks = jnp.einsum('bhnl,bhnlk->bhnk', wgt, kc)

    def step(state, inp):
        c_mem, n_mem, m = state
        be, mc, kv_n, ks_n = inp
        m_new = jnp.maximum(be + m, mc)
        fa = jnp.exp(be + m - m_new)
        fb = jnp.exp(mc - m_new)
        c_new = fa[..., None, None] * c_mem + fb[..., None, None] * kv_n
        n_new = fa[..., None] * n_mem + fb[..., None] * ks_n
        return (c_new, n_new, m_new), state

    xs = tuple(jnp.moveaxis(z, 2, 0) for z in (b_end, m_chunk, kv, ks))
    final, starts = lax.scan(step, state0, xs)
    return tuple(jnp.moveaxis(z, 0, 2) for z in starts), final


def ml_chunk_outputs(q, k, v, ig, lf, starts):
    B, H, T, dk = q.shape
    dv = v.shape[-1]
    L = min(ML_CHUNK, T)
    nc = T // L
    qc = q.reshape(B, H, nc, L, dk)
    kc = k.reshape(B, H, nc, L, dk)
    vc = v.reshape(B, H, nc, L, dv)
    b = jnp.cumsum(lf.reshape(B, H, nc, L), -1)
    c0, n0, m0 = starts
    causal = jnp.tril(jnp.ones((L, L), bool))
    dlog = jnp.where(causal, b[..., :, None] - b[..., None, :] + ig.reshape(B, H, nc, L)[..., None, :], NEG_INF)
    inter = b + m0[..., None]
    m = jnp.maximum(dlog.max(-1), inter)
    dw = jnp.exp(dlog - m[..., None])
    iw = jnp.exp(inter - m)
    s = jnp.einsum('bhntd,bhnsd->bhnts', qc, kc) * dw
    num = jnp.einsum('bhnts,bhnsv->bhntv', s, vc) + iw[..., None] * jnp.einsum('bhntd,bhndv->bhntv', qc, c0)
    den = s.sum(-1) + iw * jnp.einsum('bhntd,bhnd->bhnt', qc, n0)
    h = num / jnp.maximum(jnp.abs(den), jnp.exp(-m))[..., None]
    return h.reshape(B, H, T, dv)


def ml_readout(h, o, norm_g):
    B, H, T, dv = h.shape
    hn = h * lax.rsqrt(jnp.mean(h * h, -1, keepdims=True) + ML_NORM_EPS)
    hn = hn.transpose(0, 2, 1, 3).reshape(B, T, H * dv) * norm_g
    return hn * jax.nn.sigmoid(o.astype(F32))


def odd_mixer(a_lat, a_ctx, w_in, w_out, gate_b, norm_g, need_ctx):
    names = tuple(n for n, _ in ODD_LAYOUT)
    t_lat = project(a_lat, w_in, ODD_LAYOUT, names)
    t_ctx = project(a_ctx, w_in, ODD_LAYOUT, names if need_ctx else ODD_CTX_STATE_COLS)
    q_l, k_l, v_l, ig_l, lf_l = ml_prep(t_lat, gate_b, True, True)
    q_c, k_c, v_c, ig_c, lf_c = ml_prep(t_ctx, gate_b, False, need_ctx)
    B = a_lat.shape[0]
    zero = (jnp.zeros((B, ML_HEADS, ML_QK_DIM, ML_V_DIM), F32),
            jnp.zeros((B, ML_HEADS, ML_QK_DIM), F32),
            jnp.zeros((B, ML_HEADS), F32))
    h_l, h_c = [], []
    for d in range(2):
        f = (lambda z: jnp.flip(z, 2)) if d == 1 else (lambda z: z)
        starts_c, final_c = ml_chunk_states(f(k_c), f(v_c), f(ig_c[d]), f(lf_c[d]), zero)
        starts_l, _ = ml_chunk_states(f(k_l), f(v_l), f(ig_l[d]), f(lf_l[d]), final_c)
        h_l.append(f(ml_chunk_outputs(f(q_l), f(k_l), f(v_l), f(ig_l[d]), f(lf_l[d]), starts_l)))
        if need_ctx:
            h_c.append(f(ml_chunk_outputs(f(q_c), f(k_c), f(v_c), f(ig_c[d]), f(lf_c[d]), starts_c)))
    y_lat = jnp.einsum('btd,de->bte', ml_readout(h_l[0] + h_l[1], t_lat['ml_o'], norm_g), w_out).astype(a_lat.dtype)
    if not need_ctx:
        return y_lat, None
    y_ctx = jnp.einsum('btd,de->bte', ml_readout(h_c[0] + h_c[1], t_ctx['ml_o'], norm_g), w_out).astype(a_ctx.dtype)
    return y_lat, y_ctx


def kernel(x, c, ctx, c_ctx, ada_w, ada_b, ln_g, ln_b, ev_w_in, ev_w_out, na_rpb, rw_mu, rw_w0, rw_w_up,
           rw_a0, rw_a_up, rw_g_up, rw_k_k, rw_k_a, rw_r_k, rw_gn_g, rw_gn_b, od_w_in, od_w_out, ml_gate_b,
           ml_norm_g, moe_router, moe_bias, moe_w_gate, moe_w_up, moe_w_down, sh_w_gate, sh_w_up, sh_w_down):
    B, N, D = x.shape
    C = ctx.shape[1]
    S = C + N
    dims = (B, C, N)
    assert C % ROW_TILE == 0 and N % ROW_TILE == 0 and B + 1 <= SUBLANES
    h = jnp.concatenate([ctx, x], axis=1).reshape(B * S, D)
    cond = jnp.zeros((SUBLANES, D), F32).at[:B].set(c).at[B].set(c_ctx)
    for l in range(DEPTH):
        mods = ada_mods_pallas(cond, ada_w[l], ada_b[l])
        if l % 2 == 0:
            e = l // 2
            (q, k, v, dec_f, dec_b, beta_f, beta_b, kd_f, kd_b, nkk, rv, rr, glow) = proj_even_pallas(
                h, mods, dims, ev_w_in[e], rw_mu[e], rw_w0[e], rw_w_up[e], rw_a0[e], rw_a_up[e],
                rw_k_k[e], rw_k_a[e])
            y_f, y_b = rwkv_scan_pallas(dec_f, beta_f, kd_f, dec_b, beta_b, kd_b, nkk, rv, rr, dims)
            na = attention_pallas(q, k, v, na_rpb[e], dims)
            h, f, s = even_out_pallas(na, y_f, y_b, rr, rv, kd_f, kd_b, glow, h, mods, dims, rw_g_up[e],
                                      rw_r_k[e], rw_gn_g[e], rw_gn_b[e], ev_w_out[e], ln_g[l, 0], ln_b[l, 0],
                                      moe_router[l])
        else:
            o = l // 2
            h3 = h.reshape(B, S, D)
            m3 = mods.reshape(SUBLANES, N_MODS, D)
            a_lat = h3[:, C:] * (1.0 + m3[:B, 1][:, None]) + m3[:B, 0][:, None]
            a_ctx = h3[:, :C] * (1.0 + m3[B, 1]) + m3[B, 0]
            y_lat, _ = odd_mixer(a_lat, a_ctx, od_w_in[o], od_w_out[o], ml_gate_b[o], ml_norm_g[o], False)
            y = jnp.concatenate([jnp.zeros((B, C, D), F32), y_lat], axis=1).reshape(B * S, D)
            h, f, s = resid_tail_pallas(y, h, mods, dims, ln_g[l, 0], ln_b[l, 0], moe_router[l])
        h = moe_layer(f, s, h, mods, dims, ln_g[l, 1], ln_b[l, 1], moe_bias[l], moe_w_gate[l], moe_w_up[l],
                      moe_w_down[l], sh_w_gate[l], sh_w_up[l], sh_w_down[l])
    return h.reshape(B, S, D)[:, C:]
```

```python
import functools

import jax
import jax.numpy as jnp
from jax import lax
from jax.experimental import pallas as pl
from jax.experimental.pallas import tpu as pltpu

D_MODEL = 1024
DEPTH = 2
GRID_W = 64

DEEPNORM_ALPHA = (2.0 * DEPTH) ** 0.25
LN_EPS = 1e-5
NEG_INF = -1e30
F32 = jnp.float32
BF16 = jnp.bfloat16

NA_HEAD_DIM = 64
NA_WIDTH = D_MODEL // 2
NA_HEADS = NA_WIDTH // NA_HEAD_DIM
NA_WIN_ROWS = 8
NA_WIN_COLS = 16
NA_SCALE = NA_HEAD_DIM ** -0.5

RW_HEAD_DIM = 64
RW_WIDTH = D_MODEL // 2
RW_HEADS = RW_WIDTH // RW_HEAD_DIM
RW_DECAY_LORA = 32
RW_AAA_LORA = 32
RW_GATE_LORA = 96
RW_GN_EPS = 64e-5

ML_HEADS = 8
ML_V_DIM = D_MODEL // ML_HEADS
ML_QK_DIM = ML_V_DIM // 2
ML_WIDTH = ML_HEADS * ML_V_DIM
ML_CHUNK = 128
ML_NORM_EPS = 1e-6
ROPE_BASE = 10000.0

N_EXPERTS = 256
TOP_K = 8
N_GROUPS = 8
TOPK_GROUPS = 4
ROUTED_SCALE = 2.5
MOE_BLOCK = 128

ODD_LAYOUT = (
    ('ml_q', ML_HEADS * ML_QK_DIM), ('ml_k', ML_HEADS * ML_QK_DIM),
    ('ml_v', ML_WIDTH), ('ml_o', ML_WIDTH),
    ('ml_if', ML_HEADS), ('ml_ib', ML_HEADS), ('ml_ff', ML_HEADS), ('ml_fb', ML_HEADS),
)
ODD_CTX_STATE_COLS = ('ml_k', 'ml_v', 'ml_if', 'ml_ib', 'ml_ff', 'ml_fb')

SUBLANES = 8
LANES = 128
VMEM_LIMIT_BYTES = 56 * 1024 * 1024

ROW_TILE = 256
N_MODS = 6
RW_COLS = 3 * RW_WIDTH + 2 * LANES
NT_DIMS = (((1,), (1,)), ((), ()))


def _cparams(n_axes):
    return pltpu.CompilerParams(dimension_semantics=("arbitrary",) * n_axes, vmem_limit_bytes=VMEM_LIMIT_BYTES)


def _full_spec(shape):
    return pl.BlockSpec(shape, lambda *_: (0,) * len(shape))


def _split_bf16(x):
    hi = x.astype(BF16)
    lo = (x - hi.astype(F32)).astype(BF16)
    return jnp.concatenate([hi, lo], axis=-1)


def _block_ones(n_rows, n_cols, seg):
    row = lax.broadcasted_iota(jnp.int32, (n_rows, n_cols), 0)
    col = lax.broadcasted_iota(jnp.int32, (n_rows, n_cols), 1)
    return (((row % n_cols) // seg) == (col // seg)).astype(BF16)


def _seg_sum(x, ones2):
    return jnp.dot(_split_bf16(x), ones2, preferred_element_type=F32)


def _mod_index(tiles_per_batch, ctx_tiles, n_batch):
    def idx(i):
        return jnp.where(i % tiles_per_batch < ctx_tiles, n_batch, i // tiles_per_batch)
    return idx


def _mod_spec(chunk, mod_idx):
    return pl.BlockSpec((None, None, 1, D_MODEL), lambda i: (mod_idx(i), chunk, 0, 0))


def _ada_kernel(c_ref, w_ref, b_ref, o_ref):
    c = c_ref[...]
    x = (c * jax.nn.sigmoid(c)).astype(BF16)
    o_ref[...] = jnp.dot(x, w_ref[...].astype(BF16), preferred_element_type=F32) + b_ref[...]


def ada_mods_pallas(cond, w, b):
    n, D = cond.shape
    n_out = w.shape[1]
    tn = 512
    out = pl.pallas_call(
        _ada_kernel,
        grid=(n_out // tn,),
        in_specs=[_full_spec((n, D)), pl.BlockSpec((D, tn), lambda j: (0, j)), pl.BlockSpec((1, tn), lambda j: (0, j))],
        out_specs=pl.BlockSpec((n, tn), lambda j: (0, j)),
        out_shape=jax.ShapeDtypeStruct((n, n_out), F32),
        compiler_params=_cparams(1),
        name="ada_mods",
    )(cond, w, b.reshape(1, n_out))
    return out.reshape(n, N_MODS, 1, D)


def _softplus(x):
    return jnp.maximum(x, 0.0) + jnp.log(1.0 + jnp.exp(-jnp.abs(x)))


def _proj_even_kernel(h_ref, hp_ref, hn_ref, shift_ref, scale_ref, wna_ref, wrw_ref, mu_ref, ones_ref,
                      kk_ref, ka_ref, w0_ref, a0_ref, wup_ref, aup_ref,
                      q_ref, k_ref, v_ref, dec_f_ref, dec_b_ref, beta_f_ref, beta_b_ref, kd_f_ref, kd_b_ref,
                      nkk_ref, rv_ref, rr_ref, glow_ref, *, tiles_per_batch, ctx_tiles):
    i = pl.program_id(0)
    j = i % tiles_per_batch
    first = (j == 0) | (j == ctx_tiles)
    last = (j == ctx_tiles - 1) | (j == tiles_per_batch - 1)
    tm = h_ref.shape[0]
    gain = 1.0 + scale_ref[...]
    shift = shift_ref[...]
    a = h_ref[...] * gain + shift
    a_prev = jnp.where(first, 0.0, hp_ref[SUBLANES - 1:SUBLANES, :] * gain + shift)
    a_next = jnp.where(last, 0.0, hn_ref[0:1, :] * gain + shift)
    rid = lax.broadcasted_iota(jnp.int32, (tm, 1), 0)
    prev = jnp.where(rid == 0, a_prev, pltpu.roll(a, 1, 0))
    nxt = jnp.where(rid == tm - 1, a_next, pltpu.roll(a, tm - 1, 0))
    a16 = a.astype(BF16)
    nb16 = (0.5 * (prev + nxt)).astype(BF16)

    na = jnp.dot(a16, wna_ref[...], preferred_element_type=F32)
    q_ref[...] = (na[:, :NA_WIDTH] * NA_SCALE).astype(BF16)
    k_ref[...] = na[:, NA_WIDTH:2 * NA_WIDTH].astype(BF16)
    v_ref[...] = na[:, 2 * NA_WIDTH:].astype(BF16)

    pa = jnp.dot(a16, wrw_ref[...], preferred_element_type=F32)
    pn = jnp.dot(nb16, wrw_ref[...], preferred_element_type=F32)
    t = pa + mu_ref[...] * (pn - pa)
    r = t[:, :RW_WIDTH]
    k = t[:, RW_WIDTH:2 * RW_WIDTH]
    lora = t[:, 3 * RW_WIDTH:3 * RW_WIDTH + LANES]
    rr_ref[...] = r
    rv_ref[...] = t[:, 2 * RW_WIDTH:3 * RW_WIDTH]
    glow_ref[...] = t[:, 3 * RW_WIDTH + LANES:]

    kk = k * kk_ref[...]
    norm = jnp.sqrt(_seg_sum(kk * kk, ones_ref[...]))
    kk = kk / jnp.maximum(norm, 1e-12)
    nkk_ref[...] = -kk
    lora_t = jnp.tanh(lora).astype(BF16)
    lora16 = lora.astype(BF16)
    outs = ((dec_f_ref, beta_f_ref, kd_f_ref), (dec_b_ref, beta_b_ref, kd_b_ref))
    for d in range(2):
        w_log = -_softplus(-(w0_ref[d:d + 1, :] + jnp.dot(lora_t, wup_ref[d], preferred_element_type=F32))) - 0.5
        a_gate = jax.nn.sigmoid(a0_ref[d:d + 1, :] + jnp.dot(lora16, aup_ref[d], preferred_element_type=F32))
        outs[d][0][...] = jnp.exp(-jnp.exp(w_log))
        outs[d][1][...] = kk * a_gate
        outs[d][2][...] = k * (1.0 + (a_gate - 1.0) * ka_ref[...])


def proj_even_pallas(h, mods, dims, w_in, mu, w0, w_up, a0, a_up, k_k, k_a):
    B, C, N = dims
    R, D = h.shape
    tm = ROW_TILE
    tpb, ctx_tiles = (C + N) // tm, C // tm
    mod_idx = _mod_index(tpb, ctx_tiles, B)
    w_na = w_in[:, :3 * NA_WIDTH].astype(BF16)
    n_rw = w_in.shape[1] - 3 * NA_WIDTH
    w_rw = jnp.pad(w_in[:, 3 * NA_WIDTH:], ((0, 0), (0, RW_COLS - n_rw))).astype(BF16)
    mu_p = jnp.pad(mu, (0, RW_COLS - n_rw)).reshape(1, RW_COLS)
    ones2 = _block_ones(2 * RW_WIDTH, RW_WIDTH, RW_HEAD_DIM)
    lr = RW_DECAY_LORA

    def pad_up(m, first_row):
        out = jnp.zeros((2, LANES, RW_WIDTH), F32)
        for d in range(2):
            out = out.at[d, first_row + d * lr:first_row + (d + 1) * lr].set(m[d])
        return out.astype(BF16)

    row = lambda width: pl.BlockSpec((tm, width), lambda i: (i, 0))
    hb = tm // SUBLANES
    n_hb = R // SUBLANES
    wide = jax.ShapeDtypeStruct((R, RW_WIDTH), F32)
    half = jax.ShapeDtypeStruct((R, NA_WIDTH), BF16)
    return pl.pallas_call(
        functools.partial(_proj_even_kernel, tiles_per_batch=tpb, ctx_tiles=ctx_tiles),
        grid=(R // tm,),
        in_specs=[
            row(D),
            pl.BlockSpec((SUBLANES, D), lambda i: (jnp.maximum(i * hb - 1, 0), 0)),
            pl.BlockSpec((SUBLANES, D), lambda i: (jnp.minimum((i + 1) * hb, n_hb - 1), 0)),
            _mod_spec(0, mod_idx), _mod_spec(1, mod_idx),
            _full_spec((D, 3 * NA_WIDTH)), _full_spec((D, RW_COLS)), _full_spec((1, RW_COLS)),
            _full_spec((2 * RW_WIDTH, RW_WIDTH)),
            _full_spec((1, RW_WIDTH)), _full_spec((1, RW_WIDTH)),
            _full_spec((2, RW_WIDTH)), _full_spec((2, RW_WIDTH)),
            _full_spec((2, LANES, RW_WIDTH)), _full_spec((2, LANES, RW_WIDTH)),
        ],
        out_specs=[row(NA_WIDTH)] * 3 + [row(RW_WIDTH)] * 9 + [row(LANES)],
        out_shape=[half] * 3 + [wide] * 9 + [jax.ShapeDtypeStruct((R, LANES), F32)],
        compiler_params=_cparams(1),
        name="proj_even",
    )(h, h, h, mods, mods, w_na, w_rw, mu_p, ones2, k_k.reshape(1, -1), k_a.reshape(1, -1), w0, a0,
      pad_up(w_up, 0), pad_up(a_up, 2 * lr))


RW_SCAN_TIME = 256


def _rwkv_scan_kernel(wf_ref, bf_ref, kf_ref, nf_ref, vf_ref, rf_ref,
                      wb_ref, bb_ref, kb_ref, nb_ref, vb_ref, rb_ref, yf_ref, yb_ref, s_ref):
    @pl.when(pl.program_id(0) == 0)
    def _():
        s_ref[...] = jnp.zeros_like(s_ref)

    n_batch, n_time, width = wf_ref.shape
    n_pair = width // LANES
    n_dir_chain = n_batch * n_pair
    n_chain = 2 * n_dir_chain
    rows_all = n_chain * RW_HEAD_DIM
    ones2 = _block_ones(2 * LANES, LANES, RW_HEAD_DIM)
    vi = lax.broadcasted_iota(jnp.int32, (1, RW_HEAD_DIM, LANES), 1)
    li = lax.broadcasted_iota(jnp.int32, (1, RW_HEAD_DIM, LANES), 2)
    diag = (li % RW_HEAD_DIM) == vi
    n_sub = n_time // SUBLANES

    def seg(x):
        out = jnp.dot(_split_bf16(x.reshape(rows_all, LANES)), ones2, preferred_element_type=F32)
        return out.reshape(n_chain, RW_HEAD_DIM, LANES)

    def chains(ref, rows):
        x = ref[:, rows, :]
        return [x[b, :, p * LANES:(p + 1) * LANES] for b in range(n_batch) for p in range(n_pair)]

    def sub(i, carry):
        rows_f = pl.ds(pl.multiple_of(i * SUBLANES, SUBLANES), SUBLANES)
        rows_b = pl.ds(pl.multiple_of((n_sub - 1 - i) * SUBLANES, SUBLANES), SUBLANES)
        load = lambda f_ref, b_ref: (jnp.stack(chains(f_ref, rows_f)), jnp.stack(chains(b_ref, rows_b)))
        w8, beta8, kd8 = load(wf_ref, wb_ref), load(bf_ref, bb_ref), load(kf_ref, kb_ref)
        nkk8, v8, r8 = load(nf_ref, nb_ref), load(vf_ref, vb_ref), load(rf_ref, rb_ref)

        def at(pair, t):
            tb = SUBLANES - 1 - t
            return jnp.concatenate([pair[0][:, t:t + 1, :], pair[1][:, tb:tb + 1, :]], axis=0)

        s = s_ref[...]
        rows = []
        for t in range(SUBLANES):
            vcol = seg(jnp.where(diag, at(v8, t), 0.0))
            sa = seg(s * at(nkk8, t))
            s = s * at(w8, t) + sa * at(beta8, t) + vcol * at(kd8, t)
            ybc = seg(s * at(r8, t))
            rows.append(jnp.sum(jnp.where(diag, ybc, 0.0), axis=1, keepdims=True))
        s_ref[...] = s
        y_f = jnp.concatenate([row[:n_dir_chain] for row in rows], axis=1)
        y_b = jnp.concatenate([row[n_dir_chain:] for row in rows[::-1]], axis=1)
        for b in range(n_batch):
            for p in range(n_pair):
                c = b * n_pair + p
                yf_ref[b, rows_f, p * LANES:(p + 1) * LANES] = y_f[c]
                yb_ref[b, rows_b, p * LANES:(p + 1) * LANES] = y_b[c]
        return carry

    lax.fori_loop(0, n_sub, sub, 0)


def rwkv_scan_pallas(dec_f, beta_f, kd_f, dec_b, beta_b, kd_b, nkk, v, r, dims):
    B, C, N = dims
    S = C + N
    tc = RW_SCAN_TIME
    assert C % tc == 0 and N % tc == 0
    n_ctx, n_all = C // tc, S // tc
    as3 = lambda z: z.reshape(B, S, RW_WIDTH)
    fwd = pl.BlockSpec((B, tc, RW_WIDTH), lambda j: (0, j, 0))
    bwd = pl.BlockSpec((B, tc, RW_WIDTH),
                       lambda j: (0, jnp.where(j < n_ctx, n_ctx - 1 - j, n_all - 1 - (j - n_ctx)), 0))
    out = jax.ShapeDtypeStruct((B, S, RW_WIDTH), F32)
    y_f, y_b = pl.pallas_call(
        _rwkv_scan_kernel,
        grid=(n_all,),
        in_specs=[fwd] * 6 + [bwd] * 6,
        out_specs=[fwd, bwd],
        out_shape=[out, out],
        scratch_shapes=[pltpu.VMEM((2 * B * (RW_WIDTH // LANES), RW_HEAD_DIM, LANES), F32)],
        compiler_params=_cparams(1),
        name="rwkv_scan",
    )(as3(dec_f), as3(beta_f), as3(kd_f), as3(nkk), as3(v), as3(r),
      as3(dec_b), as3(beta_b), as3(kd_b), as3(nkk), as3(v), as3(r))
    return y_f.reshape(B * S, RW_WIDTH), y_b.reshape(B * S, RW_WIDTH)


NA_BAND = NA_WIN_ROWS * GRID_W


def _na_row_start(j, ctx_blocks, n_rows):
    r = jnp.maximum(j - ctx_blocks, 0)
    return r, jnp.clip(r - NA_WIN_ROWS // 2, 0, n_rows - NA_WIN_ROWS)


def _na_kernel(q_ref, k_ref, v_ref, bias_ref, o_ref, *, n_ctx):
    j = pl.program_id(1)
    ctx_blocks = n_ctx // GRID_W
    n_rows = pl.num_programs(1) - ctx_blocks
    _, row_start = _na_row_start(j, ctx_blocks, n_rows)
    start = pl.multiple_of(n_ctx + row_start * GRID_W, GRID_W)
    q = q_ref[0]
    kb = k_ref[0, pl.ds(start, NA_BAND), :]
    vb = v_ref[0, pl.ds(start, NA_BAND), :]
    kc = k_ref[0, pl.ds(0, n_ctx), :]
    vc = v_ref[0, pl.ds(0, n_ctx), :]
    head_of_lane = lax.broadcasted_iota(jnp.int32, (GRID_W, LANES), 1) // NA_HEAD_DIM
    for p in range(NA_WIDTH // LANES):
        cols = slice(p * LANES, (p + 1) * LANES)
        qp, kp, vp, kcp, vcp = q[:, cols], kb[:, cols], vb[:, cols], kc[:, cols], vc[:, cols]
        outs = []
        for h2 in range(LANES // NA_HEAD_DIM):
            qm = jnp.where(head_of_lane == h2, qp, jnp.zeros_like(qp))
            s_loc = lax.dot_general(qm, kp, NT_DIMS, preferred_element_type=F32) + bias_ref[0, 2 * p + h2]
            s_ctx = lax.dot_general(qm, kcp, NT_DIMS, preferred_element_type=F32)
            m = jnp.maximum(jnp.max(s_loc, axis=-1, keepdims=True), jnp.max(s_ctx, axis=-1, keepdims=True))
            e_loc = jnp.exp(s_loc - m)
            e_ctx = jnp.exp(s_ctx - m)
            den = jnp.sum(e_loc, axis=-1, keepdims=True) + jnp.sum(e_ctx, axis=-1, keepdims=True)
            o = (jnp.dot(e_loc.astype(BF16), vp, preferred_element_type=F32)
                 + jnp.dot(e_ctx.astype(BF16), vcp, preferred_element_type=F32))
            outs.append(o / den)
        o_ref[0, :, cols] = jnp.where(head_of_lane == 0, outs[0], outs[1])


def _na_bias_table(rpb):
    kw = NA_WIN_COLS
    j = jnp.arange(GRID_W)
    col_start = jnp.clip(j - kw // 2, 0, GRID_W - kw)
    col_in = (j[None, :] >= col_start[:, None]) & (j[None, :] < col_start[:, None] + kw)
    col_off = jnp.clip(j[None, :] - j[:, None], -(kw - 1), kw - 1) + (kw - 1)
    d = jnp.arange(NA_WIN_ROWS)
    a = jnp.arange(NA_WIN_ROWS)
    row_off = a[None, :] - d[:, None] + (NA_WIN_ROWS - 1)
    tab = rpb.astype(F32)[:, row_off[:, :, None, None], col_off[None, None, :, :]]
    tab = jnp.where(col_in[None, None, None], tab, NEG_INF)
    tab = tab.transpose(1, 0, 3, 2, 4).reshape(NA_WIN_ROWS, NA_HEADS, GRID_W, NA_BAND)
    return jnp.concatenate([tab, jnp.full((1,) + tab.shape[1:], NEG_INF, F32)], 0)


def attention_pallas(q, k, v, rpb, dims):
    B, C, N = dims
    S = C + N
    W = NA_WIDTH
    n_rows = N // GRID_W
    ctx_blocks = C // GRID_W
    assert n_rows >= NA_WIN_ROWS and N % GRID_W == 0 and C % GRID_W == 0
    bias = _na_bias_table(rpb)
    as3 = lambda z: z.reshape(B, S, W)

    def bias_idx(b, j):
        r, row_start = _na_row_start(j, ctx_blocks, n_rows)
        return (jnp.where(j < ctx_blocks, NA_WIN_ROWS, r - row_start), 0, 0, 0)

    out = pl.pallas_call(
        functools.partial(_na_kernel, n_ctx=C),
        grid=(B, S // GRID_W),
        in_specs=[
            pl.BlockSpec((1, GRID_W, W), lambda b, j: (b, j, 0)),
            pl.BlockSpec((1, S, W), lambda b, j: (b, 0, 0)),
            pl.BlockSpec((1, S, W), lambda b, j: (b, 0, 0)),
            pl.BlockSpec((1, NA_HEADS, GRID_W, NA_BAND), bias_idx),
        ],
        out_specs=pl.BlockSpec((1, GRID_W, W), lambda b, j: (b, j, 0)),
        out_shape=jax.ShapeDtypeStruct((B, S, W), F32),
        compiler_params=_cparams(2),
        name="na_attention",
    )(as3(q), as3(k), as3(v), bias)
    return out.reshape(B * S, W)


def _layer_norm(x, g, b):
    mu = jnp.mean(x, axis=-1, keepdims=True)
    xc = x - mu
    var = jnp.mean(xc * xc, axis=-1, keepdims=True)
    return xc * lax.rsqrt(var + LN_EPS) * g + b


def _mixer_tail(h, y, gate, ln_g, ln_b, shift, scale, router, h_out_ref, f_ref, s_ref):
    h1 = _layer_norm(DEEPNORM_ALPHA * h + gate * y, ln_g, ln_b)
    f = h1 * (1.0 + scale) + shift
    h_out_ref[...] = h1
    f_ref[...] = f
    s_ref[...] = jax.nn.sigmoid(jnp.dot(f, router, preferred_element_type=F32, precision=lax.Precision.HIGHEST))


def _even_out_kernel(na_ref, yf_ref, yb_ref, r_ref, v_ref, kdf_ref, kdb_ref, glow_ref, h_ref,
                     gate_ref, shift_ref, scale_ref, ones_ref, gng_ref, gnb_ref, rk_ref, gup_ref, wout_ref,
                     lng_ref, lnb_ref, router_ref, h_out_ref, f_ref, s_ref):
    ones2 = ones_ref[...]
    inv = 1.0 / RW_HEAD_DIM
    y = yf_ref[...] + yb_ref[...]
    mu = _seg_sum(y, ones2) * inv
    yc = y - mu
    var = _seg_sum(yc * yc, ones2) * inv
    yn = yc * lax.rsqrt(var + RW_GN_EPS) * gng_ref[...] + gnb_ref[...]
    r = r_ref[...]
    bonus = (_seg_sum(r * kdf_ref[...] * rk_ref[...], ones2) + _seg_sum(r * kdb_ref[...] * rk_ref[...], ones2))
    gate = jnp.dot(jax.nn.sigmoid(glow_ref[...]).astype(BF16), gup_ref[...], preferred_element_type=F32)
    rw = (yn + bonus * v_ref[...]) * gate
    mix = jnp.concatenate([na_ref[...], rw], axis=-1).astype(BF16)
    y_mix = jnp.dot(mix, wout_ref[...], preferred_element_type=F32)
    _mixer_tail(h_ref[...], y_mix, gate_ref[...], lng_ref[...], lnb_ref[...], shift_ref[...], scale_ref[...],
                router_ref[...], h_out_ref, f_ref, s_ref)


def _tail_specs(R, D, E, tm):
    row = lambda width: pl.BlockSpec((tm, width), lambda i: (i, 0))
    return ([row(D), row(D), row(E)],
            [jax.ShapeDtypeStruct((R, D), F32), jax.ShapeDtypeStruct((R, D), F32), jax.ShapeDtypeStruct((R, E), F32)])


def even_out_pallas(na, y_f, y_b, r, v, kd_f, kd_b, glow, h, mods, dims, g_up, r_k, gn_g, gn_b, w_out,
                    ln_g, ln_b, router_w):
    B, C, N = dims
    R, D = h.shape
    E = router_w.shape[1]
    tm = ROW_TILE
    mod_idx = _mod_index((C + N) // tm, C // tm, B)
    row = lambda width: pl.BlockSpec((tm, width), lambda i: (i, 0))
    ones2 = _block_ones(2 * RW_WIDTH, RW_WIDTH, RW_HEAD_DIM)
    g_up_p = jnp.pad(g_up, ((0, LANES - g_up.shape[0]), (0, 0))).astype(BF16)
    vec = lambda z: z.reshape(1, -1)
    out_specs, out_shape = _tail_specs(R, D, E, tm)
    return pl.pallas_call(
        _even_out_kernel,
        grid=(R // tm,),
        in_specs=[row(NA_WIDTH)] + [row(RW_WIDTH)] * 6 + [row(LANES), row(D),
                  _mod_spec(2, mod_idx), _mod_spec(3, mod_idx), _mod_spec(4, mod_idx),
                  _full_spec((2 * RW_WIDTH, RW_WIDTH)),
                  _full_spec((1, RW_WIDTH)), _full_spec((1, RW_WIDTH)), _full_spec((1, RW_WIDTH)),
                  _full_spec((LANES, RW_WIDTH)), _full_spec((D, D)),
                  _full_spec((1, D)), _full_spec((1, D)), _full_spec((D, E))],
        out_specs=out_specs,
        out_shape=out_shape,
        compiler_params=_cparams(1),
        name="even_out",
    )(na, y_f, y_b, r, v, kd_f, kd_b, glow, h, mods, mods, mods, ones2, vec(gn_g), vec(gn_b), vec(r_k),
      g_up_p, w_out.astype(BF16), vec(ln_g), vec(ln_b), router_w)


def _resid_tail_kernel(y_ref, h_ref, gate_ref, shift_ref, scale_ref, lng_ref, lnb_ref, router_ref,
                       h_out_ref, f_ref, s_ref):
    _mixer_tail(h_ref[...], y_ref[...], gate_ref[...], lng_ref[...], lnb_ref[...], shift_ref[...],
                scale_ref[...], router_ref[...], h_out_ref, f_ref, s_ref)


def resid_tail_pallas(y, h, mods, dims, ln_g, ln_b, router_w):
    B, C, N = dims
    R, D = h.shape
    E = router_w.shape[1]
    tm = ROW_TILE
    mod_idx = _mod_index((C + N) // tm, C // tm, B)
    row = lambda width: pl.BlockSpec((tm, width), lambda i: (i, 0))
    vec = lambda z: z.reshape(1, -1)
    out_specs, out_shape = _tail_specs(R, D, E, tm)
    return pl.pallas_call(
        _resid_tail_kernel,
        grid=(R // tm,),
        in_specs=[row(D), row(D), _mod_spec(2, mod_idx), _mod_spec(3, mod_idx), _mod_spec(4, mod_idx),
                  _full_spec((1, D)), _full_spec((1, D)), _full_spec((D, E))],
        out_specs=out_specs,
        out_shape=out_shape,
        compiler_params=_cparams(1),
        name="resid_tail",
    )(y, h, mods, mods, mods, vec(ln_g), vec(ln_b), router_w)


MOE_TOKEN_TILE = 128


def _swiglu_bf16(x, wg, wu, wd):
    g = jnp.dot(x, wg, preferred_element_type=F32)
    u = jnp.dot(x, wu, preferred_element_type=F32)
    mid = (g * jax.nn.sigmoid(g) * u).astype(BF16)
    return jnp.dot(mid, wd, preferred_element_type=F32)


def _row_copy(src_ref, src_row, dst_ref, dst_row, sem):
    return pltpu.make_async_copy(src_ref.at[pl.ds(src_row, 1), :], dst_ref.at[pl.ds(dst_row, 1), :], sem)


def _dispatch_kernel(pos_ref, f_ref, xs_ref, sem):
    n_tok = f_ref.shape[0]

    def issue(t, carry):
        for k in range(TOP_K):
            _row_copy(f_ref, t, xs_ref, pos_ref[t * TOP_K + k], sem).start()
        return carry

    lax.fori_loop(0, n_tok, issue, 0)

    def drain(t, carry):
        for k in range(TOP_K):
            _row_copy(f_ref, 0, xs_ref, 0, sem).wait()
        return carry

    lax.fori_loop(0, n_tok, drain, 0)


def moe_dispatch_pallas(f, pos_flat):
    T, D = f.shape
    tm = MOE_TOKEN_TILE
    assert T % tm == 0
    return pl.pallas_call(
        _dispatch_kernel,
        grid=(T // tm,),
        in_specs=[
            pl.BlockSpec((tm * TOP_K,), lambda i: (i,), memory_space=pltpu.SMEM),
            pl.BlockSpec((tm, D), lambda i: (i, 0)),
        ],
        out_specs=pl.BlockSpec(memory_space=pl.ANY),
        out_shape=jax.ShapeDtypeStruct((T * TOP_K, D), F32),
        scratch_shapes=[pltpu.SemaphoreType.DMA(())],
        compiler_params=_cparams(1),
        name="moe_dispatch",
    )(pos_flat, f)


def _expert_item_kernel(blk_ref, e_ref, lo_ref, hi_ref, first_ref, x_ref, wg_ref, wu_ref, wd_ref, o_ref):
    del e_ref
    i = pl.program_id(0)
    lo, hi = lo_ref[i], hi_ref[i]

    @pl.when(hi > lo)
    def _():
        y = _swiglu_bf16(x_ref[...].astype(BF16), wg_ref[0].astype(BF16), wu_ref[0].astype(BF16),
                         wd_ref[0].astype(BF16))
        rows = blk_ref[i] * MOE_BLOCK + lax.broadcasted_iota(jnp.int32, (MOE_BLOCK, 1), 0)
        y = jnp.where((rows >= lo) & (rows < hi), y, 0.0)

        @pl.when(first_ref[i] == 1)
        def _():
            o_ref[...] = y

        @pl.when(first_ref[i] == 0)
        def _():
            o_ref[...] += y


def moe_experts_pallas(xs, items, wg, wu, wd):
    n_rows, D = xs.shape
    F = wg.shape[-1]
    n_items = items[0].shape[0]
    grid_spec = pltpu.PrefetchScalarGridSpec(
        num_scalar_prefetch=5,
        grid=(n_items,),
        in_specs=[
            pl.BlockSpec((MOE_BLOCK, D), lambda i, blk, e, lo, hi, first: (blk[i], 0)),
            pl.BlockSpec((1, D, F), lambda i, blk, e, lo, hi, first: (e[i], 0, 0)),
            pl.BlockSpec((1, D, F), lambda i, blk, e, lo, hi, first: (e[i], 0, 0)),
            pl.BlockSpec((1, F, D), lambda i, blk, e, lo, hi, first: (e[i], 0, 0)),
        ],
        out_specs=pl.BlockSpec((MOE_BLOCK, D), lambda i, blk, e, lo, hi, first: (blk[i], 0)),
    )
    return pl.pallas_call(
        _expert_item_kernel,
        grid_spec=grid_spec,
        out_shape=jax.ShapeDtypeStruct((n_rows, D), F32),
        compiler_params=_cparams(1),
        name="moe_experts",
    )(*items, xs, wg, wu, wd)


def _combine_kernel(pos_ref, w_ref, f_ref, h_ref, gate_ref, lng_ref, lnb_ref, sg_ref, su_ref, sd_ref, ys_ref,
                    o_ref, buf_ref, sem):
    n_tok = f_ref.shape[0]

    def issue(t, carry):
        for k in range(TOP_K):
            pltpu.make_async_copy(ys_ref.at[pl.ds(pos_ref[t * TOP_K + k], 1), :],
                                  buf_ref.at[k, pl.ds(t, 1), :], sem).start()
        return carry

    lax.fori_loop(0, n_tok, issue, 0)
    acc = _swiglu_bf16(f_ref[...].astype(BF16), sg_ref[...], su_ref[...], sd_ref[...])

    def drain(t, carry):
        for k in range(TOP_K):
            pltpu.make_async_copy(ys_ref.at[pl.ds(0, 1), :], buf_ref.at[0, pl.ds(0, 1), :], sem).wait()
        return carry

    lax.fori_loop(0, n_tok, drain, 0)
    w = w_ref[...]
    for k in range(TOP_K):
        acc = acc + w[:, k:k + 1] * buf_ref[k]
    o_ref[...] = _layer_norm(DEEPNORM_ALPHA * h_ref[...] + gate_ref[...] * acc, lng_ref[...], lnb_ref[...])


def moe_combine_pallas(ys, pos_flat, w_sel, f, h, mods, dims, ln_g, ln_b, sg, su, sd):
    B, C, N = dims
    T, D = f.shape
    tm = MOE_TOKEN_TILE
    F = sg.shape[-1]
    mod_idx = _mod_index((C + N) // tm, C // tm, B)
    vec = lambda z: z.reshape(1, -1)
    return pl.pallas_call(
        _combine_kernel,
        grid=(T // tm,),
        in_specs=[
            pl.BlockSpec((tm * TOP_K,), lambda i: (i,), memory_space=pltpu.SMEM),
            pl.BlockSpec((tm, TOP_K), lambda i: (i, 0)),
            pl.BlockSpec((tm, D), lambda i: (i, 0)),
            pl.BlockSpec((tm, D), lambda i: (i, 0)),
            _mod_spec(5, mod_idx), _full_spec((1, D)), _full_spec((1, D)),
            _full_spec((D, F)), _full_spec((D, F)), _full_spec((F, D)),
            pl.BlockSpec(memory_space=pl.ANY),
        ],
        out_specs=pl.BlockSpec((tm, D), lambda i: (i, 0)),
        out_shape=jax.ShapeDtypeStruct((T, D), F32),
        scratch_shapes=[pltpu.VMEM((TOP_K, tm, D), F32), pltpu.SemaphoreType.DMA(())],
        compiler_params=_cparams(1),
        name="moe_combine",
    )(pos_flat, w_sel, f, h, mods, vec(ln_g), vec(ln_b), sg, su, sd, ys)


REMOVED = -3e38


def _router_kernel(s_ref, bias_ref, e_ref, w_ref, rank_ref, cnt_ref, carry_ref):
    @pl.when(pl.program_id(0) == 0)
    def _():
        carry_ref[...] = jnp.zeros_like(carry_ref)

    s = s_ref[...]
    tm, n_exp = s.shape
    per_group = n_exp // N_GROUPS
    lane_i = lax.broadcasted_iota(jnp.int32, (tm, n_exp), 1)
    lane = lane_i.astype(F32)
    group_of_lane = lane_i // per_group
    big = float(n_exp)
    rmax = lambda z: jnp.max(z, axis=-1, keepdims=True)
    first_at = lambda z, m: jnp.min(jnp.where(z == m, lane, big), axis=-1, keepdims=True)

    grp = s + bias_ref[...]
    g_score = []
    for g in range(N_GROUPS):
        mg = jnp.where(group_of_lane == g, grp, REMOVED)
        m1 = rmax(mg)
        m2 = rmax(jnp.where(lane == first_at(mg, m1), REMOVED, mg))
        g_score.append(m1 + m2)
    choice = jnp.full_like(grp, NEG_INF)
    for g in range(N_GROUPS):
        ahead = jnp.zeros((tm, 1), F32)
        for g2 in range(N_GROUPS):
            if g2 != g:
                beats = (g_score[g2] > g_score[g]) | ((g_score[g2] == g_score[g]) & (g2 < g))
                ahead = ahead + beats.astype(F32)
        choice = jnp.where((group_of_lane == g) & (ahead < TOPK_GROUPS), grp, choice)

    col8 = lax.broadcasted_iota(jnp.int32, (tm, TOP_K), 1)
    e_out = jnp.zeros((tm, TOP_K), F32)
    w_out = jnp.zeros((tm, TOP_K), F32)
    picked = []
    onehot = jnp.zeros((tm, n_exp), F32)
    for k in range(TOP_K):
        idx = first_at(choice, rmax(choice))
        hit = lane == idx
        picked.append(hit)
        onehot = jnp.where(hit, 1.0, onehot)
        e_out = jnp.where(col8 == k, idx, e_out)
        w_out = jnp.where(col8 == k, jnp.sum(jnp.where(hit, s, 0.0), axis=-1, keepdims=True), w_out)
        choice = jnp.where(hit, REMOVED, choice)
    ri = lax.broadcasted_iota(jnp.int32, (tm, tm), 0)
    ci = lax.broadcasted_iota(jnp.int32, (tm, tm), 1)
    before = jnp.dot((ci < ri).astype(BF16), onehot.astype(BF16), preferred_element_type=F32) + carry_ref[0:1, :]
    rank = jnp.zeros((tm, TOP_K), F32)
    for k in range(TOP_K):
        rank = jnp.where(col8 == k, jnp.sum(jnp.where(picked[k], before, 0.0), axis=-1, keepdims=True), rank)
    total = carry_ref[0:1, :] + jnp.sum(onehot, axis=0, keepdims=True)
    carry_ref[...] = jnp.broadcast_to(total, carry_ref.shape)
    cnt_ref[...] = jnp.broadcast_to(total, cnt_ref.shape)
    e_ref[...] = e_out.astype(jnp.int32)
    w_ref[...] = w_out / jnp.sum(w_out, axis=-1, keepdims=True) * ROUTED_SCALE
    rank_ref[...] = rank.astype(jnp.int32)


def router_pallas(s, router_b):
    T, E = s.shape
    tm = ROW_TILE
    row8 = pl.BlockSpec((tm, TOP_K), lambda i: (i, 0))
    e_idx, w_sel, rank, cnt = pl.pallas_call(
        _router_kernel,
        grid=(T // tm,),
        in_specs=[pl.BlockSpec((tm, E), lambda i: (i, 0)), _full_spec((1, E))],
        out_specs=[row8, row8, row8, _full_spec((SUBLANES, E))],
        out_shape=[jax.ShapeDtypeStruct((T, TOP_K), jnp.int32), jax.ShapeDtypeStruct((T, TOP_K), F32),
                   jax.ShapeDtypeStruct((T, TOP_K), jnp.int32), jax.ShapeDtypeStruct((SUBLANES, E), F32)],
        scratch_shapes=[pltpu.VMEM((SUBLANES, E), F32)],
        compiler_params=_cparams(1),
        name="moe_router",
    )(s, router_b.astype(F32).reshape(1, E))
    return e_idx, w_sel, rank, cnt[0].astype(jnp.int32)


def moe_layer(f, s, h, mods, dims, ln_g, ln_b, router_b, wg, wu, wd, sg, su, sd):
    T, D = f.shape
    E = s.shape[-1]
    e_idx, w_sel, rank, counts = router_pallas(s, router_b)
    n_asg = T * TOP_K
    assert n_asg % MOE_BLOCK == 0
    i32 = jnp.int32
    ends = jnp.cumsum(counts).astype(i32)
    starts = ends - counts
    pos_flat = (starts[e_idx] + rank).reshape(-1)
    nb = n_asg // MOE_BLOCK
    first_blk = starts // MOE_BLOCK
    nblk = jnp.where(counts > 0, (ends - 1) // MOE_BLOCK - first_blk + 1, 0)
    item_ends = jnp.cumsum(nblk).astype(i32)
    item_starts = item_ends - nblk
    n_items = nb + E
    it = jnp.arange(n_items, dtype=i32)
    real = it < item_ends[-1]
    e_of = jnp.minimum(jnp.searchsorted(item_ends, it, side='right'), E - 1).astype(i32)
    e_of = jnp.where(real, e_of, e_of[item_ends[-1] - 1])
    blk = jnp.where(real, first_blk[e_of] + it - item_starts[e_of], nb - 1).astype(i32)
    lo = jnp.where(real, jnp.maximum(starts[e_of], blk * MOE_BLOCK), 0).astype(i32)
    hi = jnp.where(real, jnp.minimum(ends[e_of], (blk + 1) * MOE_BLOCK), 0).astype(i32)
    first = (real & (blk != jnp.concatenate([jnp.full((1,), -1, i32), blk[:-1]]))).astype(i32)
    xs = moe_dispatch_pallas(f, pos_flat)
    ys = moe_experts_pallas(xs, (blk, e_of, lo, hi, first), wg, wu, wd)
    return moe_combine_pallas(ys, pos_flat, w_sel, f, h, mods, dims, ln_g, ln_b,
                              sg.astype(BF16), su.astype(BF16), sd.astype(BF16))


ML_QK_WIDTH = ML_HEADS * ML_QK_DIM
ROPE_GROUP = ML_QK_DIM // 4
GATE_IN, GATE_FORGET = 0, 2 * ML_HEADS


def _log_sigmoid(x):
    return -_softplus(-x)


def _proj_odd_kernel(h_ref, shift_ref, scale_ref, wqk_ref, wv_ref, wo_ref, wg_ref, wgt_ref, gb_ref, gbt_ref,
                     cos_ref, sin_ref, q_ref, k_ref, v_ref, o_ref, g_ref, gt_ref):
    a16 = (h_ref[...] * (1.0 + scale_ref[...]) + shift_ref[...]).astype(BF16)
    qk = jnp.dot(a16, wqk_ref[...], preferred_element_type=F32)
    lane = lax.broadcasted_iota(jnp.int32, (1, ML_QK_WIDTH), 1)
    first_of_pair = (lane % (2 * ROPE_GROUP)) < ROPE_GROUP
    cos, sin = cos_ref[...], sin_ref[...]

    def rope(z):
        partner = jnp.where(first_of_pair, pltpu.roll(z, ML_QK_WIDTH - ROPE_GROUP, 1), pltpu.roll(z, ROPE_GROUP, 1))
        return z * cos + partner * sin

    q_ref[...] = rope(qk[:, :ML_QK_WIDTH] * ML_QK_DIM ** -0.5).astype(BF16)
    k_ref[...] = rope(qk[:, ML_QK_WIDTH:]).astype(BF16)
    v_ref[...] = jnp.dot(a16, wv_ref[...], preferred_element_type=F32).astype(BF16)
    o_ref[...] = jnp.dot(a16, wo_ref[...], preferred_element_type=F32)
    g = jnp.dot(a16, wg_ref[...], preferred_element_type=F32) + gb_ref[...]
    gl = lax.broadcasted_iota(jnp.int32, g.shape, 1)
    g_ref[...] = jnp.where((gl >= GATE_FORGET) & (gl < 2 * GATE_FORGET), _log_sigmoid(g), g)
    gt = lax.dot_general(wgt_ref[...], a16, NT_DIMS, preferred_element_type=F32) + gbt_ref[...]
    gs = lax.broadcasted_iota(jnp.int32, gt.shape, 0)
    gt_ref[...] = jnp.where((gs >= GATE_FORGET) & (gs < 2 * GATE_FORGET), _log_sigmoid(gt), gt)


def _rope_tables(C, N):
    t = jnp.arange(N)
    pos = jnp.stack([(t // GRID_W).astype(F32), (t % GRID_W).astype(F32)], 0)
    lane = jnp.arange(ML_QK_WIDTH) % ML_QK_DIM
    inv = ROPE_BASE ** (-(lane % ROPE_GROUP).astype(F32) / ROPE_GROUP)
    ang = pos[lane // (2 * ROPE_GROUP)].T * inv[None, :]
    sign = jnp.where((lane % (2 * ROPE_GROUP)) < ROPE_GROUP, -1.0, 1.0)
    cos = jnp.concatenate([jnp.ones((C, ML_QK_WIDTH), F32), jnp.cos(ang)], 0)
    sin = jnp.concatenate([jnp.zeros((C, ML_QK_WIDTH), F32), jnp.sin(ang) * sign], 0)
    return cos, sin


def proj_odd_pallas(h, mods, dims, w_in, gate_b):
    B, C, N = dims
    R, D = h.shape
    tm = ROW_TILE
    tpb = (C + N) // tm
    mod_idx = _mod_index(tpb, C // tm, B)
    o_qk, o_v, o_o = 2 * ML_QK_WIDTH, 2 * ML_QK_WIDTH + ML_WIDTH, 2 * ML_QK_WIDTH + 2 * ML_WIDTH
    n_gate = w_in.shape[1] - o_o
    w16 = w_in.astype(BF16)
    w_g = jnp.pad(w16[:, o_o:], ((0, 0), (0, LANES - n_gate)))
    gb = jnp.pad(gate_b.astype(F32).reshape(-1), (0, LANES - n_gate))
    cos, sin = _rope_tables(C, N)
    row = lambda width: pl.BlockSpec((tm, width), lambda i: (i, 0))
    seg = pl.BlockSpec((tm, ML_QK_WIDTH), lambda i: (i % tpb, 0))
    return pl.pallas_call(
        _proj_odd_kernel,
        grid=(R // tm,),
        in_specs=[row(D), _mod_spec(0, mod_idx), _mod_spec(1, mod_idx),
                  _full_spec((D, 2 * ML_QK_WIDTH)), _full_spec((D, ML_WIDTH)), _full_spec((D, ML_WIDTH)),
                  _full_spec((D, LANES)), _full_spec((LANES, D)), _full_spec((1, LANES)), _full_spec((LANES, 1)),
                  seg, seg],
        out_specs=[row(ML_QK_WIDTH), row(ML_QK_WIDTH), row(ML_WIDTH), row(ML_WIDTH), row(LANES),
                   pl.BlockSpec((LANES, tm), lambda i: (0, i))],
        out_shape=[jax.ShapeDtypeStruct((R, ML_QK_WIDTH), BF16), jax.ShapeDtypeStruct((R, ML_QK_WIDTH), BF16),
                   jax.ShapeDtypeStruct((R, ML_WIDTH), BF16), jax.ShapeDtypeStruct((R, ML_WIDTH), F32),
                   jax.ShapeDtypeStruct((R, LANES), F32), jax.ShapeDtypeStruct((LANES, R), F32)],
        compiler_params=_cparams(1),
        name="proj_odd",
    )(h, mods, mods, w16[:, :o_qk], w16[:, o_qk:o_v], w16[:, o_v:o_o], w_g, w_g.T, gb.reshape(1, LANES),
      gb.reshape(LANES, 1), cos, sin)


def _split3_bf16(x, axis):
    x1 = x.astype(BF16)
    r1 = x - x1.astype(F32)
    x2 = r1.astype(BF16)
    x3 = (r1 - x2.astype(F32)).astype(BF16)
    return jnp.concatenate([x1, x2, x3], axis=axis)


def _mlstm_kernel(q_ref, k_ref, v_ref, g_ref, gt_ref, h_ref, c_ref, n_ref, m_ref, *, reverse):
    @pl.when(pl.program_id(1) == 0)
    def _():
        c_ref[...] = jnp.zeros_like(c_ref)
        n_ref[...] = jnp.zeros_like(n_ref)
        m_ref[...] = jnp.zeros_like(m_ref)

    L = q_ref.shape[1]
    ti = lax.broadcasted_iota(jnp.int32, (L, L), 0)
    si = lax.broadcasted_iota(jnp.int32, (L, L), 1)
    seen = (si >= ti) if reverse else (si <= ti)
    g = g_ref[0]
    gt = gt_ref[...]
    b_cols3 = jnp.dot(seen.astype(BF16), _split3_bf16(g, 1), preferred_element_type=F32)
    b_cols = b_cols3[:, :LANES] + b_cols3[:, LANES:2 * LANES] + b_cols3[:, 2 * LANES:]
    b_rows3 = lax.dot_general(_split3_bf16(gt, 0), seen.astype(BF16), NT_DIMS, preferred_element_type=F32)
    b_rows = b_rows3[:LANES] + b_rows3[LANES:2 * LANES] + b_rows3[2 * LANES:]
    half = lax.broadcasted_iota(jnp.int32, (1, LANES), 1) // ML_QK_DIM
    row_half = lax.broadcasted_iota(jnp.int32, (LANES, 1), 0) // ML_QK_DIM
    d_off = ML_HEADS if reverse else 0
    tn = (((0,), (0,)), ((), ()))
    for p in range(ML_HEADS // 2):
        qp = q_ref[0, :, p * LANES:(p + 1) * LANES]
        kp = k_ref[0, :, p * LANES:(p + 1) * LANES]
        c_old = c_ref[p]
        n_old = n_ref[p]
        m_old = m_ref[p]
        c16 = c_old.astype(BF16)
        c_new, n_new, m_new_pair = c_old, n_old, m_old
        for h2 in range(2):
            hd = 2 * p + h2
            vh = v_ref[0, :, hd * ML_V_DIM:(hd + 1) * ML_V_DIM]
            gi, gf = GATE_IN + d_off + hd, GATE_FORGET + d_off + hd
            ig_col, ig_row = g[:, gi:gi + 1], gt[gi:gi + 1, :]
            b_col, b_row = b_cols[:, gf:gf + 1], b_rows[gf:gf + 1, :]
            m0 = m_old[:, h2 * ML_QK_DIM:h2 * ML_QK_DIM + 1]
            qm = jnp.where(half == h2, qp, jnp.zeros_like(qp))
            km = jnp.where(half == h2, kp, jnp.zeros_like(kp))
            dlog = jnp.where(seen, b_col - b_row + ig_row, NEG_INF)
            inter = b_col + m0
            m_t = jnp.maximum(jnp.max(dlog, axis=-1, keepdims=True), inter)
            dw = jnp.exp(dlog - m_t)
            iw = jnp.exp(inter - m_t)
            sc = lax.dot_general(qm, kp, NT_DIMS, preferred_element_type=F32) * dw
            num = (jnp.dot(sc.astype(BF16), vh, preferred_element_type=F32)
                   + iw * jnp.dot(qm, c16, preferred_element_type=F32))
            den = (jnp.sum(sc, axis=-1, keepdims=True)
                   + iw * jnp.sum(qm.astype(F32) * n_old, axis=-1, keepdims=True))
            h_ref[0, :, hd * ML_V_DIM:(hd + 1) * ML_V_DIM] = num / jnp.maximum(jnp.abs(den), jnp.exp(-m_t))
            b_end = jnp.sum(g[:, gf:gf + 1], axis=0, keepdims=True)
            g_col = b_end - b_col + ig_col
            m_chunk = jnp.max(g_col, axis=0, keepdims=True)
            kw = km.astype(F32) * jnp.exp(g_col - m_chunk)
            m_new = jnp.maximum(b_end + m0, m_chunk)
            fa = jnp.exp(b_end + m0 - m_new)
            fb = jnp.exp(m_chunk - m_new)
            kv = lax.dot_general(kw.astype(BF16), vh, tn, preferred_element_type=F32)
            c_new = jnp.where(row_half == h2, fa * c_old + fb * kv, c_new)
            n_new = jnp.where(half == h2, fa * n_old + fb * jnp.sum(kw, axis=0, keepdims=True), n_new)
            m_new_pair = jnp.where(half == h2, m_new, m_new_pair)
        c_ref[p] = c_new
        n_ref[p] = n_new
        m_ref[p] = m_new_pair


def mlstm_pallas(q, k, v, g, gt, dims, reverse):
    B, C, N = dims
    S = C + N
    L = ML_CHUNK
    assert C % L == 0 and N % L == 0
    n_ctx, n_all = C // L, S // L
    if reverse:
        chunk = lambda j: jnp.where(j < n_ctx, n_ctx - 1 - j, n_all - 1 - (j - n_ctx))
    else:
        chunk = lambda j: j
    blk = lambda width: pl.BlockSpec((1, L, width), lambda b, j: (b, chunk(j), 0))
    n_pair = ML_HEADS // 2
    out = pl.pallas_call(
        functools.partial(_mlstm_kernel, reverse=reverse),
        grid=(B, n_all),
        in_specs=[blk(ML_QK_WIDTH), blk(ML_QK_WIDTH), blk(ML_WIDTH), blk(LANES),
                  pl.BlockSpec((LANES, L), lambda b, j: (0, b * n_all + chunk(j)))],
        out_specs=blk(ML_WIDTH),
        out_shape=jax.ShapeDtypeStruct((B, S, ML_WIDTH), F32),
        scratch_shapes=[pltpu.VMEM((n_pair, LANES, ML_V_DIM), F32), pltpu.VMEM((n_pair, 1, LANES), F32),
                        pltpu.VMEM((n_pair, 1, LANES), F32)],
        compiler_params=_cparams(2),
        name="mlstm_bwd" if reverse else "mlstm_fwd",
    )(q.reshape(B, S, -1), k.reshape(B, S, -1), v.reshape(B, S, -1), g.reshape(B, S, -1), gt)
    return out.reshape(B * S, ML_WIDTH)


def _odd_out_kernel(hf_ref, hb_ref, o_ref, h_ref, gate_ref, shift_ref, scale_ref, ng_ref, wout_ref,
                    lng_ref, lnb_ref, router_ref, h_out_ref, f_ref, s_ref):
    hs = hf_ref[...] + hb_ref[...]
    parts = []
    for hd in range(ML_HEADS):
        x = hs[:, hd * ML_V_DIM:(hd + 1) * ML_V_DIM]
        parts.append(x * lax.rsqrt(jnp.mean(x * x, axis=-1, keepdims=True) + ML_NORM_EPS))
    hn = jnp.concatenate(parts, axis=-1) * ng_ref[...] * jax.nn.sigmoid(o_ref[...])
    y = jnp.dot(hn.astype(BF16), wout_ref[...], preferred_element_type=F32)
    _mixer_tail(h_ref[...], y, gate_ref[...], lng_ref[...], lnb_ref[...], shift_ref[...], scale_ref[...],
                router_ref[...], h_out_ref, f_ref, s_ref)


def odd_out_pallas(h_f, h_b, o, h, mods, dims, norm_g, w_out, ln_g, ln_b, router_w):
    B, C, N = dims
    R, D = h.shape
    E = router_w.shape[1]
    tm = ROW_TILE
    mod_idx = _mod_index((C + N) // tm, C // tm, B)
    row = lambda width: pl.BlockSpec((tm, width), lambda i: (i, 0))
    vec = lambda z: z.reshape(1, -1)
    out_specs, out_shape = _tail_specs(R, D, E, tm)
    return pl.pallas_call(
        _odd_out_kernel,
        grid=(R // tm,),
        in_specs=[row(ML_WIDTH), row(ML_WIDTH), row(ML_WIDTH), row(D),
                  _mod_spec(2, mod_idx), _mod_spec(3, mod_idx), _mod_spec(4, mod_idx),
                  _full_spec((1, ML_WIDTH)), _full_spec((ML_WIDTH, D)),
                  _full_spec((1, D)), _full_spec((1, D)), _full_spec((D, E))],
        out_specs=out_specs,
        out_shape=out_shape,
        compiler_params=_cparams(1),
        name="odd_out",
    )(h_f, h_b, o, h, mods, mods, mods, vec(norm_g), w_out.astype(BF16), vec(ln_g), vec(ln_b), router_w)


def _offsets(layout, prefix=''):
    offs, o = {}, 0
    for name, width in layout:
        if name.startswith(prefix):
            offs[name] = (o, width)
            o += width
    return offs


def project(h, w, layout, names):
    offs = _offsets(layout)
    if len(names) == len(layout):
        y = jnp.einsum('btd,de->bte', h, w)
        return {n: y[..., offs[n][0]:offs[n][0] + offs[n][1]] for n in names}
    return {n: jnp.einsum('btd,de->bte', h, w[:, offs[n][0]:offs[n][0] + offs[n][1]]) for n in names}


def axial_rope(z):
    T, dh = z.shape[1], z.shape[-1]
    half = dh // 2
    nf = half // 2
    t = jnp.arange(T)
    row = (t // GRID_W).astype(F32)
    col = (t % GRID_W).astype(F32)
    inv = ROPE_BASE ** (-jnp.arange(nf, dtype=F32) / nf)

    def rot(u, pos):
        ang = pos[:, None] * inv[None, :]
        cos = jnp.cos(ang)[None, :, None, :]
        sin = jnp.sin(ang)[None, :, None, :]
        u1, u2 = u[..., :nf], u[..., nf:]
        return jnp.concatenate([u1 * cos - u2 * sin, u1 * sin + u2 * cos], -1)

    return jnp.concatenate([rot(z[..., :half], row), rot(z[..., half:], col)], -1).astype(z.dtype)


def ml_prep(t, gate_b, rope, need_q):
    B, T = t['ml_k'].shape[:2]
    heads = lambda z, dh: z.reshape(B, T, ML_HEADS, dh).astype(F32)
    k = heads(t['ml_k'], ML_QK_DIM)
    q = heads(t['ml_q'], ML_QK_DIM) * ML_QK_DIM ** -0.5 if need_q else None
    if rope:
        k = axial_rope(k)
        q = axial_rope(q)
    v = heads(t['ml_v'], ML_V_DIM)
    gb = gate_b.astype(F32)
    bht = lambda z: z.astype(F32).transpose(0, 2, 1)
    ig = (bht(t['ml_if'] + gb[0]), bht(t['ml_ib'] + gb[1]))
    lf = (jax.nn.log_sigmoid(bht(t['ml_ff'] + gb[2])), jax.nn.log_sigmoid(bht(t['ml_fb'] + gb[3])))
    bhtd = lambda z: None if z is None else z.transpose(0, 2, 1, 3)
    return bhtd(q), bhtd(k), bhtd(v), ig, lf


def ml_chunk_states(k, v, ig, lf, state0):
    B, H, T, dk = k.shape
    dv = v.shape[-1]
    L = min(ML_CHUNK, T)
    nc = T // L
    kc = k.reshape(B, H, nc, L, dk)
    vc = v.reshape(B, H, nc, L, dv)
    b = jnp.cumsum(lf.reshape(B, H, nc, L), -1)
    b_end = b[..., -1]
    g = b_end[..., None] - b + ig.reshape(B, H, nc, L)
    m_chunk = g.max(-1)
    wgt = jnp.exp(g - m_chunk[..., None])
    kv = jnp.einsum('bhnl,bhnlk,bhnlv->bhnkv', wgt, kc, vc)
    ks = jnp.einsum('bhnl,bhnlk->bhnk', wgt, kc)

    def step(state, inp):
        c_mem, n_mem, m = state
        be, mc, kv_n, ks_n = inp
        m_new = jnp.maximum(be + m, mc)
        fa = jnp.exp(be + m - m_new)
        fb = jnp.exp(mc - m_new)
        c_new = fa[..., None, None] * c_mem + fb[..., None, None] * kv_n
        n_new = fa[..., None] * n_mem + fb[..., None] * ks_n
        return (c_new, n_new, m_new), state

    xs = tuple(jnp.moveaxis(z, 2, 0) for z in (b_end, m_chunk, kv, ks))
    final, starts = lax.scan(step, state0, xs)
    return tuple(jnp.moveaxis(z, 0, 2) for z in starts), final


def ml_chunk_outputs(q, k, v, ig, lf, starts):
    B, H, T, dk = q.shape
    dv = v.shape[-1]
    L = min(ML_CHUNK, T)
    nc = T // L
    qc = q.reshape(B, H, nc, L, dk)
    kc = k.reshape(B, H, nc, L, dk)
    vc = v.reshape(B, H, nc, L, dv)
    b = jnp.cumsum(lf.reshape(B, H, nc, L), -1)
    c0, n0, m0 = starts
    causal = jnp.tril(jnp.ones((L, L), bool))
    dlog = jnp.where(causal, b[..., :, None] - b[..., None, :] + ig.reshape(B, H, nc, L)[..., None, :], NEG_INF)
    inter = b + m0[..., None]
    m = jnp.maximum(dlog.max(-1), inter)
    dw = jnp.exp(dlog - m[..., None])
    iw = jnp.exp(inter - m)
    s = jnp.einsum('bhntd,bhnsd->bhnts', qc, kc) * dw
    num = jnp.einsum('bhnts,bhnsv->bhntv', s, vc) + iw[..., None] * jnp.einsum('bhntd,bhndv->bhntv', qc, c0)
    den = s.sum(-1) + iw * jnp.einsum('bhntd,bhnd->bhnt', qc, n0)
    h = num / jnp.maximum(jnp.abs(den), jnp.exp(-m))[..., None]
    return h.reshape(B, H, T, dv)


def ml_readout(h, o, norm_g):
    B, H, T, dv = h.shape
    hn = h * lax.rsqrt(jnp.mean(h * h, -1, keepdims=True) + ML_NORM_EPS)
    hn = hn.transpose(0, 2, 1, 3).reshape(B, T, H * dv) * norm_g
    return hn * jax.nn.sigmoid(o.astype(F32))


def odd_mixer(a_lat, a_ctx, w_in, w_out, gate_b, norm_g, need_ctx):
    names = tuple(n for n, _ in ODD_LAYOUT)
    t_lat = project(a_lat, w_in, ODD_LAYOUT, names)
    t_ctx = project(a_ctx, w_in, ODD_LAYOUT, names if need_ctx else ODD_CTX_STATE_COLS)
    q_l, k_l, v_l, ig_l, lf_l = ml_prep(t_lat, gate_b, True, True)
    q_c, k_c, v_c, ig_c, lf_c = ml_prep(t_ctx, gate_b, False, need_ctx)
    B = a_lat.shape[0]
    zero = (jnp.zeros((B, ML_HEADS, ML_QK_DIM, ML_V_DIM), F32),
            jnp.zeros((B, ML_HEADS, ML_QK_DIM), F32),
            jnp.zeros((B, ML_HEADS), F32))
    h_l, h_c = [], []
    for d in range(2):
        f = (lambda z: jnp.flip(z, 2)) if d == 1 else (lambda z: z)
        starts_c, final_c = ml_chunk_states(f(k_c), f(v_c), f(ig_c[d]), f(lf_c[d]), zero)
        starts_l, _ = ml_chunk_states(f(k_l), f(v_l), f(ig_l[d]), f(lf_l[d]), final_c)
        h_l.append(f(ml_chunk_outputs(f(q_l), f(k_l), f(v_l), f(ig_l[d]), f(lf_l[d]), starts_l)))
        if need_ctx:
            h_c.append(f(ml_chunk_outputs(f(q_c), f(k_c), f(v_c), f(ig_c[d]), f(lf_c[d]), starts_c)))
    y_lat = jnp.einsum('btd,de->bte', ml_readout(h_l[0] + h_l[1], t_lat['ml_o'], norm_g), w_out).astype(a_lat.dtype)
    if not need_ctx:
        return y_lat, None
    y_ctx = jnp.einsum('btd,de->bte', ml_readout(h_c[0] + h_c[1], t_ctx['ml_o'], norm_g), w_out).astype(a_ctx.dtype)
    return y_lat, y_ctx


def kernel(x, c, ctx, c_ctx, ada_w, ada_b, ln_g, ln_b, ev_w_in, ev_w_out, na_rpb, rw_mu, rw_w0, rw_w_up,
           rw_a0, rw_a_up, rw_g_up, rw_k_k, rw_k_a, rw_r_k, rw_gn_g, rw_gn_b, od_w_in, od_w_out, ml_gate_b,
           ml_norm_g, moe_router, moe_bias, moe_w_gate, moe_w_up, moe_w_down, sh_w_gate, sh_w_up, sh_w_down):
    B, N, D = x.shape
    C = ctx.shape[1]
    S = C + N
    dims = (B, C, N)
    assert C % ROW_TILE == 0 and N % ROW_TILE == 0 and B + 1 <= SUBLANES
    h = jnp.concatenate([ctx, x], axis=1).reshape(B * S, D)
    cond = jnp.zeros((SUBLANES, D), F32).at[:B].set(c).at[B].set(c_ctx)
    for l in range(DEPTH):
        mods = ada_mods_pallas(cond, ada_w[l], ada_b[l])
        if l % 2 == 0:
            e = l // 2
            (q, k, v, dec_f, dec_b, beta_f, beta_b, kd_f, kd_b, nkk, rv, rr, glow) = proj_even_pallas(
                h, mods, dims, ev_w_in[e], rw_mu[e], rw_w0[e], rw_w_up[e], rw_a0[e], rw_a_up[e],
                rw_k_k[e], rw_k_a[e])
            y_f, y_b = rwkv_scan_pallas(dec_f, beta_f, kd_f, dec_b, beta_b, kd_b, nkk, rv, rr, dims)
            na = attention_pallas(q, k, v, na_rpb[e], dims)
            h, f, s = even_out_pallas(na, y_f, y_b, rr, rv, kd_f, kd_b, glow, h, mods, dims, rw_g_up[e],
                                      rw_r_k[e], rw_gn_g[e], rw_gn_b[e], ev_w_out[e], ln_g[l, 0], ln_b[l, 0],
                                      moe_router[l])
        else:
            o = l // 2
            q, k, v, og, g, gt = proj_odd_pallas(h, mods, dims, od_w_in[o], ml_gate_b[o])
            h_f = mlstm_pallas(q, k, v, g, gt, dims, False)
            h_b = mlstm_pallas(q, k, v, g, gt, dims, True)
            h, f, s = odd_out_pallas(h_f, h_b, og, h, mods, dims, ml_norm_g[o], od_w_out[o], ln_g[l, 0],
                                     ln_b[l, 0], moe_router[l])
        h = moe_layer(f, s, h, mods, dims, ln_g[l, 1], ln_b[l, 1], moe_bias[l], moe_w_gate[l], moe_w_up[l],
                      moe_w_down[l], sh_w_gate[l], sh_w_up[l], sh_w_down[l])
    return h.reshape(B, S, D)[:, C:]
```

```python
import functools

import jax
import jax.numpy as jnp
import numpy as np
from jax import lax
from jax.experimental import pallas as pl
from jax.experimental.pallas import tpu as pltpu

D_MODEL = 1024
DEPTH = 2
GRID_W = 64

DEEPNORM_ALPHA = (2.0 * DEPTH) ** 0.25
LN_EPS = 1e-5
NEG_INF = -1e30
F32 = jnp.float32
BF16 = jnp.bfloat16

NA_HEAD_DIM = 64
NA_WIDTH = D_MODEL // 2
NA_HEADS = NA_WIDTH // NA_HEAD_DIM
NA_WIN_ROWS = 8
NA_WIN_COLS = 16
NA_SCALE = NA_HEAD_DIM ** -0.5

RW_HEAD_DIM = 64
RW_WIDTH = D_MODEL // 2
RW_HEADS = RW_WIDTH // RW_HEAD_DIM
RW_DECAY_LORA = 32
RW_AAA_LORA = 32
RW_GATE_LORA = 96
RW_GN_EPS = 64e-5

ML_HEADS = 8
ML_V_DIM = D_MODEL // ML_HEADS
ML_QK_DIM = ML_V_DIM // 2
ML_WIDTH = ML_HEADS * ML_V_DIM
ML_CHUNK = 128
ML_NORM_EPS = 1e-6
ROPE_BASE = 10000.0

N_EXPERTS = 256
TOP_K = 8
N_GROUPS = 8
TOPK_GROUPS = 4
ROUTED_SCALE = 2.5
MOE_BLOCK = 128

ODD_LAYOUT = (
    ('ml_q', ML_HEADS * ML_QK_DIM), ('ml_k', ML_HEADS * ML_QK_DIM),
    ('ml_v', ML_WIDTH), ('ml_o', ML_WIDTH),
    ('ml_if', ML_HEADS), ('ml_ib', ML_HEADS), ('ml_ff', ML_HEADS), ('ml_fb', ML_HEADS),
)
ODD_CTX_STATE_COLS = ('ml_k', 'ml_v', 'ml_if', 'ml_ib', 'ml_ff', 'ml_fb')

SUBLANES = 8
LANES = 128
VMEM_LIMIT_BYTES = 56 * 1024 * 1024

ROW_TILE = 256
N_MODS = 6
RW_COLS = 3 * RW_WIDTH + 2 * LANES
NT_DIMS = (((1,), (1,)), ((), ()))


def _cparams(n_axes):
    return pltpu.CompilerParams(dimension_semantics=("arbitrary",) * n_axes, vmem_limit_bytes=VMEM_LIMIT_BYTES)


def _full_spec(shape):
    return pl.BlockSpec(shape, lambda *_: (0,) * len(shape))


def _split_bf16(x):
    hi = x.astype(BF16)
    lo = (x - hi.astype(F32)).astype(BF16)
    return jnp.concatenate([hi, lo], axis=-1)


def _block_ones(n_rows, n_cols, seg):
    row = lax.broadcasted_iota(jnp.int32, (n_rows, n_cols), 0)
    col = lax.broadcasted_iota(jnp.int32, (n_rows, n_cols), 1)
    return (((row % n_cols) // seg) == (col // seg)).astype(BF16)


def _seg_sum(x, ones2):
    return jnp.dot(_split_bf16(x), ones2, preferred_element_type=F32)


def _mod_index(tiles_per_batch, ctx_tiles, n_batch):
    def idx(i):
        return jnp.where(i % tiles_per_batch < ctx_tiles, n_batch, i // tiles_per_batch)
    return idx


def _mod_spec(chunk, mod_idx):
    return pl.BlockSpec((None, None, 1, D_MODEL), lambda i: (mod_idx(i), chunk, 0, 0))


def _ada_kernel(c_ref, w_ref, b_ref, o_ref):
    c = c_ref[...]
    x = (c * jax.nn.sigmoid(c)).astype(BF16)
    o_ref[...] = jnp.dot(x, w_ref[...].astype(BF16), preferred_element_type=F32) + b_ref[...]


def ada_mods_pallas(cond, w, b):
    n, D = cond.shape
    n_out = w.shape[1]
    tn = 512
    out = pl.pallas_call(
        _ada_kernel,
        grid=(n_out // tn,),
        in_specs=[_full_spec((n, D)), pl.BlockSpec((D, tn), lambda j: (0, j)), pl.BlockSpec((1, tn), lambda j: (0, j))],
        out_specs=pl.BlockSpec((n, tn), lambda j: (0, j)),
        out_shape=jax.ShapeDtypeStruct((n, n_out), F32),
        compiler_params=_cparams(1),
        name="ada_mods",
    )(cond, w, b.reshape(1, n_out))
    return out.reshape(n, N_MODS, 1, D)


def _softplus(x):
    return jnp.maximum(x, 0.0) + jnp.log(1.0 + jnp.exp(-jnp.abs(x)))


def _proj_even_kernel(h_ref, hp_ref, hn_ref, shift_ref, scale_ref, wna_ref, wrw_ref, mu_ref, ones_ref,
                      kk_ref, ka_ref, w0_ref, a0_ref, wup_ref, aup_ref,
                      q_ref, k_ref, v_ref, dec_f_ref, dec_b_ref, beta_f_ref, beta_b_ref, kd_f_ref, kd_b_ref,
                      nkk_ref, rv_ref, rr_ref, glow_ref, *, tiles_per_batch, ctx_tiles):
    i = pl.program_id(0)
    j = i % tiles_per_batch
    first = (j == 0) | (j == ctx_tiles)
    last = (j == ctx_tiles - 1) | (j == tiles_per_batch - 1)
    tm = h_ref.shape[0]
    gain = 1.0 + scale_ref[...]
    shift = shift_ref[...]
    a = h_ref[...] * gain + shift
    a_prev = jnp.where(first, 0.0, hp_ref[SUBLANES - 1:SUBLANES, :] * gain + shift)
    a_next = jnp.where(last, 0.0, hn_ref[0:1, :] * gain + shift)
    rid = lax.broadcasted_iota(jnp.int32, (tm, 1), 0)
    prev = jnp.where(rid == 0, a_prev, pltpu.roll(a, 1, 0))
    nxt = jnp.where(rid == tm - 1, a_next, pltpu.roll(a, tm - 1, 0))
    a16 = a.astype(BF16)
    nb16 = (0.5 * (prev + nxt)).astype(BF16)

    na = jnp.dot(a16, wna_ref[...], preferred_element_type=F32)
    q_ref[...] = (na[:, :NA_WIDTH] * NA_SCALE).astype(BF16)
    k_ref[...] = na[:, NA_WIDTH:2 * NA_WIDTH].astype(BF16)
    v_ref[...] = na[:, 2 * NA_WIDTH:].astype(BF16)

    pa = jnp.dot(a16, wrw_ref[...], preferred_element_type=F32)
    pn = jnp.dot(nb16, wrw_ref[...], preferred_element_type=F32)
    t = pa + mu_ref[...] * (pn - pa)
    r = t[:, :RW_WIDTH]
    k = t[:, RW_WIDTH:2 * RW_WIDTH]
    lora = t[:, 3 * RW_WIDTH:3 * RW_WIDTH + LANES]
    rr_ref[...] = r
    rv_ref[...] = t[:, 2 * RW_WIDTH:3 * RW_WIDTH]
    glow_ref[...] = t[:, 3 * RW_WIDTH + LANES:]

    kk = k * kk_ref[...]
    norm = jnp.sqrt(_seg_sum(kk * kk, ones_ref[...]))
    kk = kk / jnp.maximum(norm, 1e-12)
    nkk_ref[...] = -kk
    lora_t = jnp.tanh(lora).astype(BF16)
    lora16 = lora.astype(BF16)
    outs = ((dec_f_ref, beta_f_ref, kd_f_ref), (dec_b_ref, beta_b_ref, kd_b_ref))
    for d in range(2):
        w_log = -_softplus(-(w0_ref[d:d + 1, :] + jnp.dot(lora_t, wup_ref[d], preferred_element_type=F32))) - 0.5
        a_gate = jax.nn.sigmoid(a0_ref[d:d + 1, :] + jnp.dot(lora16, aup_ref[d], preferred_element_type=F32))
        outs[d][0][...] = jnp.exp(-jnp.exp(w_log))
        outs[d][1][...] = kk * a_gate
        outs[d][2][...] = k * (1.0 + (a_gate - 1.0) * ka_ref[...])


def proj_even_pallas(h, mods, dims, w_in, mu, w0, w_up, a0, a_up, k_k, k_a):
    B, C, N = dims
    R, D = h.shape
    tm = ROW_TILE
    tpb, ctx_tiles = (C + N) // tm, C // tm
    mod_idx = _mod_index(tpb, ctx_tiles, B)
    w_na = w_in[:, :3 * NA_WIDTH].astype(BF16)
    n_rw = w_in.shape[1] - 3 * NA_WIDTH
    w_rw = jnp.pad(w_in[:, 3 * NA_WIDTH:], ((0, 0), (0, RW_COLS - n_rw))).astype(BF16)
    mu_p = jnp.pad(mu, (0, RW_COLS - n_rw)).reshape(1, RW_COLS)
    ones2 = _block_ones(2 * RW_WIDTH, RW_WIDTH, RW_HEAD_DIM)
    lr = RW_DECAY_LORA

    def pad_up(m, first_row):
        out = jnp.zeros((2, LANES, RW_WIDTH), F32)
        for d in range(2):
            out = out.at[d, first_row + d * lr:first_row + (d + 1) * lr].set(m[d])
        return out.astype(BF16)

    row = lambda width: pl.BlockSpec((tm, width), lambda i: (i, 0))
    hb = tm // SUBLANES
    n_hb = R // SUBLANES
    wide = jax.ShapeDtypeStruct((R, RW_WIDTH), F32)
    half = jax.ShapeDtypeStruct((R, NA_WIDTH), BF16)
    return pl.pallas_call(
        functools.partial(_proj_even_kernel, tiles_per_batch=tpb, ctx_tiles=ctx_tiles),
        grid=(R // tm,),
        in_specs=[
            row(D),
            pl.BlockSpec((SUBLANES, D), lambda i: (jnp.maximum(i * hb - 1, 0), 0)),
            pl.BlockSpec((SUBLANES, D), lambda i: (jnp.minimum((i + 1) * hb, n_hb - 1), 0)),
            _mod_spec(0, mod_idx), _mod_spec(1, mod_idx),
            _full_spec((D, 3 * NA_WIDTH)), _full_spec((D, RW_COLS)), _full_spec((1, RW_COLS)),
            _full_spec((2 * RW_WIDTH, RW_WIDTH)),
            _full_spec((1, RW_WIDTH)), _full_spec((1, RW_WIDTH)),
            _full_spec((2, RW_WIDTH)), _full_spec((2, RW_WIDTH)),
            _full_spec((2, LANES, RW_WIDTH)), _full_spec((2, LANES, RW_WIDTH)),
        ],
        out_specs=[row(NA_WIDTH)] * 3 + [row(RW_WIDTH)] * 9 + [row(LANES)],
        out_shape=[half] * 3 + [wide] * 9 + [jax.ShapeDtypeStruct((R, LANES), F32)],
        compiler_params=_cparams(1),
        name="proj_even",
    )(h, h, h, mods, mods, w_na, w_rw, mu_p, ones2, k_k.reshape(1, -1), k_a.reshape(1, -1), w0, a0,
      pad_up(w_up, 0), pad_up(a_up, 2 * lr))


RW_SCAN_TIME = 256
RW_SEG_TWO_PIECE = (False, False, False)


def _rwkv_scan_kernel(wf_ref, bf_ref, kf_ref, nf_ref, vf_ref, rf_ref,
                      wb_ref, bb_ref, kb_ref, nb_ref, vb_ref, rb_ref, yf_ref, yb_ref, s_ref):
    @pl.when(pl.program_id(0) == 0)
    def _():
        s_ref[...] = jnp.zeros_like(s_ref)

    n_batch, n_time, width = wf_ref.shape
    n_pair = width // LANES
    n_dir_chain = n_batch * n_pair
    n_chain = 2 * n_dir_chain
    rows_all = n_chain * RW_HEAD_DIM
    ones2 = _block_ones(2 * LANES, LANES, RW_HEAD_DIM)
    vi = lax.broadcasted_iota(jnp.int32, (1, RW_HEAD_DIM, LANES), 1)
    li = lax.broadcasted_iota(jnp.int32, (1, RW_HEAD_DIM, LANES), 2)
    diag = (li % RW_HEAD_DIM) == vi
    n_sub = n_time // SUBLANES

    def seg(x, two_piece):
        x = x.reshape(rows_all, LANES)
        if two_piece:
            out = jnp.dot(_split_bf16(x), ones2, preferred_element_type=F32)
        else:
            out = jnp.dot(x.astype(BF16), ones2[:LANES], preferred_element_type=F32)
        return out.reshape(n_chain, RW_HEAD_DIM, LANES)

    def chains(ref, rows):
        x = ref[:, rows, :]
        return [x[b, :, p * LANES:(p + 1) * LANES] for b in range(n_batch) for p in range(n_pair)]

    def sub(i, carry):
        rows_f = pl.ds(pl.multiple_of(i * SUBLANES, SUBLANES), SUBLANES)
        rows_b = pl.ds(pl.multiple_of((n_sub - 1 - i) * SUBLANES, SUBLANES), SUBLANES)
        load = lambda f_ref, b_ref: (jnp.stack(chains(f_ref, rows_f)), jnp.stack(chains(b_ref, rows_b)))
        w8, beta8, kd8 = load(wf_ref, wb_ref), load(bf_ref, bb_ref), load(kf_ref, kb_ref)
        nkk8, v8, r8 = load(nf_ref, nb_ref), load(vf_ref, vb_ref), load(rf_ref, rb_ref)

        def at(pair, t):
            tb = SUBLANES - 1 - t
            return jnp.concatenate([pair[0][:, t:t + 1, :], pair[1][:, tb:tb + 1, :]], axis=0)

        s = s_ref[...]
        rows = []
        for t in range(SUBLANES):
            vcol = seg(jnp.where(diag, at(v8, t), 0.0), RW_SEG_TWO_PIECE[0])
            sa = seg(s * at(nkk8, t), RW_SEG_TWO_PIECE[1])
            s = s * at(w8, t) + sa * at(beta8, t) + vcol * at(kd8, t)
            ybc = seg(s * at(r8, t), RW_SEG_TWO_PIECE[2])
            rows.append(jnp.sum(jnp.where(diag, ybc, 0.0), axis=1, keepdims=True))
        s_ref[...] = s
        y_f = jnp.concatenate([row[:n_dir_chain] for row in rows], axis=1)
        y_b = jnp.concatenate([row[n_dir_chain:] for row in rows[::-1]], axis=1)
        for b in range(n_batch):
            for p in range(n_pair):
                c = b * n_pair + p
                yf_ref[b, rows_f, p * LANES:(p + 1) * LANES] = y_f[c]
                yb_ref[b, rows_b, p * LANES:(p + 1) * LANES] = y_b[c]
        return carry

    lax.fori_loop(0, n_sub, sub, 0)


def rwkv_scan_pallas(dec_f, beta_f, kd_f, dec_b, beta_b, kd_b, nkk, v, r, dims):
    B, C, N = dims
    S = C + N
    tc = RW_SCAN_TIME
    assert C % tc == 0 and N % tc == 0
    n_ctx, n_all = C // tc, S // tc
    as3 = lambda z: z.reshape(B, S, RW_WIDTH)
    fwd = pl.BlockSpec((B, tc, RW_WIDTH), lambda j: (0, j, 0))
    bwd = pl.BlockSpec((B, tc, RW_WIDTH),
                       lambda j: (0, jnp.where(j < n_ctx, n_ctx - 1 - j, n_all - 1 - (j - n_ctx)), 0))
    out = jax.ShapeDtypeStruct((B, S, RW_WIDTH), F32)
    y_f, y_b = pl.pallas_call(
        _rwkv_scan_kernel,
        grid=(n_all,),
        in_specs=[fwd] * 6 + [bwd] * 6,
        out_specs=[fwd, bwd],
        out_shape=[out, out],
        scratch_shapes=[pltpu.VMEM((2 * B * (RW_WIDTH // LANES), RW_HEAD_DIM, LANES), F32)],
        compiler_params=_cparams(1),
        name="rwkv_scan",
    )(as3(dec_f), as3(beta_f), as3(kd_f), as3(nkk), as3(v), as3(r),
      as3(dec_b), as3(beta_b), as3(kd_b), as3(nkk), as3(v), as3(r))
    return y_f.reshape(B * S, RW_WIDTH), y_b.reshape(B * S, RW_WIDTH)


NA_BAND = NA_WIN_ROWS * GRID_W


def _na_row_start(j, ctx_blocks, n_rows):
    r = jnp.maximum(j - ctx_blocks, 0)
    return r, jnp.clip(r - NA_WIN_ROWS // 2, 0, n_rows - NA_WIN_ROWS)


def _na_kernel(q_ref, k_ref, v_ref, bias_ref, o_ref, *, n_ctx):
    j = pl.program_id(1)
    ctx_blocks = n_ctx // GRID_W
    n_rows = pl.num_programs(1) - ctx_blocks
    _, row_start = _na_row_start(j, ctx_blocks, n_rows)
    start = pl.multiple_of(n_ctx + row_start * GRID_W, GRID_W)
    q = q_ref[0]
    kb = k_ref[0, pl.ds(start, NA_BAND), :]
    vb = v_ref[0, pl.ds(start, NA_BAND), :]
    kc = k_ref[0, pl.ds(0, n_ctx), :]
    vc = v_ref[0, pl.ds(0, n_ctx), :]
    head_of_lane = lax.broadcasted_iota(jnp.int32, (GRID_W, LANES), 1) // NA_HEAD_DIM
    for p in range(NA_WIDTH // LANES):
        cols = slice(p * LANES, (p + 1) * LANES)
        qp, kp, vp, kcp, vcp = q[:, cols], kb[:, cols], vb[:, cols], kc[:, cols], vc[:, cols]
        outs = []
        for h2 in range(LANES // NA_HEAD_DIM):
            qm = jnp.where(head_of_lane == h2, qp, jnp.zeros_like(qp))
            s_loc = lax.dot_general(qm, kp, NT_DIMS, preferred_element_type=F32) + bias_ref[0, 2 * p + h2]
            s_ctx = lax.dot_general(qm, kcp, NT_DIMS, preferred_element_type=F32)
            m = jnp.maximum(jnp.max(s_loc, axis=-1, keepdims=True), jnp.max(s_ctx, axis=-1, keepdims=True))
            e_loc = jnp.exp(s_loc - m)
            e_ctx = jnp.exp(s_ctx - m)
            den = jnp.sum(e_loc, axis=-1, keepdims=True) + jnp.sum(e_ctx, axis=-1, keepdims=True)
            o = (jnp.dot(e_loc.astype(BF16), vp, preferred_element_type=F32)
                 + jnp.dot(e_ctx.astype(BF16), vcp, preferred_element_type=F32))
            outs.append(o / den)
        o_ref[0, :, cols] = jnp.where(head_of_lane == 0, outs[0], outs[1])


def _na_bias_table(rpb):
    kw = NA_WIN_COLS
    n_col_off = 2 * kw - 1
    j = np.arange(GRID_W)
    col_start = np.clip(j - kw // 2, 0, GRID_W - kw)
    col_in = (j[None, :] >= col_start[:, None]) & (j[None, :] < col_start[:, None] + kw)
    col_off = np.clip(j[None, :] - j[:, None], -(kw - 1), kw - 1) + (kw - 1)
    pick = (col_off.reshape(1, -1) == np.arange(n_col_off)[:, None]).astype(np.float32)
    toep = jnp.dot(rpb.astype(F32).reshape(-1, n_col_off), pick, precision=lax.Precision.HIGHEST)
    toep = toep.reshape(NA_HEADS, 2 * NA_WIN_ROWS - 1, GRID_W, GRID_W)
    toep = jnp.where(col_in[None, None], toep, NEG_INF)
    tab = jnp.stack([toep[:, NA_WIN_ROWS - 1 - d:2 * NA_WIN_ROWS - 1 - d] for d in range(NA_WIN_ROWS)], 0)
    tab = tab.transpose(0, 1, 3, 2, 4).reshape(NA_WIN_ROWS, NA_HEADS, GRID_W, NA_BAND)
    return jnp.concatenate([tab, jnp.full((1,) + tab.shape[1:], NEG_INF, F32)], 0)


def attention_pallas(q, k, v, rpb, dims):
    B, C, N = dims
    S = C + N
    W = NA_WIDTH
    n_rows = N // GRID_W
    ctx_blocks = C // GRID_W
    assert n_rows >= NA_WIN_ROWS and N % GRID_W == 0 and C % GRID_W == 0
    bias = _na_bias_table(rpb)
    as3 = lambda z: z.reshape(B, S, W)

    def bias_idx(b, j):
        r, row_start = _na_row_start(j, ctx_blocks, n_rows)
        return (jnp.where(j < ctx_blocks, NA_WIN_ROWS, r - row_start), 0, 0, 0)

    out = pl.pallas_call(
        functools.partial(_na_kernel, n_ctx=C),
        grid=(B, S // GRID_W),
        in_specs=[
            pl.BlockSpec((1, GRID_W, W), lambda b, j: (b, j, 0)),
            pl.BlockSpec((1, S, W), lambda b, j: (b, 0, 0)),
            pl.BlockSpec((1, S, W), lambda b, j: (b, 0, 0)),
            pl.BlockSpec((1, NA_HEADS, GRID_W, NA_BAND), bias_idx),
        ],
        out_specs=pl.BlockSpec((1, GRID_W, W), lambda b, j: (b, j, 0)),
        out_shape=jax.ShapeDtypeStruct((B, S, W), F32),
        compiler_params=_cparams(2),
        name="na_attention",
    )(as3(q), as3(k), as3(v), bias)
    return out.reshape(B * S, W)


def _layer_norm(x, g, b):
    mu = jnp.mean(x, axis=-1, keepdims=True)
    xc = x - mu
    var = jnp.mean(xc * xc, axis=-1, keepdims=True)
    return xc * lax.rsqrt(var + LN_EPS) * g + b


def _mixer_tail(h, y, gate, ln_g, ln_b, shift, scale, router, h_out_ref, f_ref, s_ref):
    h1 = _layer_norm(DEEPNORM_ALPHA * h + gate * y, ln_g, ln_b)
    f = h1 * (1.0 + scale) + shift
    h_out_ref[...] = h1
    f_ref[...] = f
    s_ref[...] = jax.nn.sigmoid(jnp.dot(f, router, preferred_element_type=F32, precision=lax.Precision.HIGHEST))


def _even_out_kernel(na_ref, yf_ref, yb_ref, r_ref, v_ref, kdf_ref, kdb_ref, glow_ref, h_ref,
                     gate_ref, shift_ref, scale_ref, ones_ref, gng_ref, gnb_ref, rk_ref, gup_ref, wout_ref,
                     lng_ref, lnb_ref, router_ref, h_out_ref, f_ref, s_ref):
    ones2 = ones_ref[...]
    inv = 1.0 / RW_HEAD_DIM
    y = yf_ref[...] + yb_ref[...]
    mu = _seg_sum(y, ones2) * inv
    yc = y - mu
    var = _seg_sum(yc * yc, ones2) * inv
    yn = yc * lax.rsqrt(var + RW_GN_EPS) * gng_ref[...] + gnb_ref[...]
    r = r_ref[...]
    bonus = (_seg_sum(r * kdf_ref[...] * rk_ref[...], ones2) + _seg_sum(r * kdb_ref[...] * rk_ref[...], ones2))
    gate = jnp.dot(jax.nn.sigmoid(glow_ref[...]).astype(BF16), gup_ref[...], preferred_element_type=F32)
    rw = (yn + bonus * v_ref[...]) * gate
    mix = jnp.concatenate([na_ref[...], rw], axis=-1).astype(BF16)
    y_mix = jnp.dot(mix, wout_ref[...], preferred_element_type=F32)
    _mixer_tail(h_ref[...], y_mix, gate_ref[...], lng_ref[...], lnb_ref[...], shift_ref[...], scale_ref[...],
                router_ref[...], h_out_ref, f_ref, s_ref)


def _tail_specs(R, D, E, tm):
    row = lambda width: pl.BlockSpec((tm, width), lambda i: (i, 0))
    return ([row(D), row(D), row(E)],
            [jax.ShapeDtypeStruct((R, D), F32), jax.ShapeDtypeStruct((R, D), F32), jax.ShapeDtypeStruct((R, E), F32)])


def even_out_pallas(na, y_f, y_b, r, v, kd_f, kd_b, glow, h, mods, dims, g_up, r_k, gn_g, gn_b, w_out,
                    ln_g, ln_b, router_w):
    B, C, N = dims
    R, D = h.shape
    E = router_w.shape[1]
    tm = ROW_TILE
    mod_idx = _mod_index((C + N) // tm, C // tm, B)
    row = lambda width: pl.BlockSpec((tm, width), lambda i: (i, 0))
    ones2 = _block_ones(2 * RW_WIDTH, RW_WIDTH, RW_HEAD_DIM)
    g_up_p = jnp.pad(g_up, ((0, LANES - g_up.shape[0]), (0, 0))).astype(BF16)
    vec = lambda z: z.reshape(1, -1)
    out_specs, out_shape = _tail_specs(R, D, E, tm)
    return pl.pallas_call(
        _even_out_kernel,
        grid=(R // tm,),
        in_specs=[row(NA_WIDTH)] + [row(RW_WIDTH)] * 6 + [row(LANES), row(D),
                  _mod_spec(2, mod_idx), _mod_spec(3, mod_idx), _mod_spec(4, mod_idx),
                  _full_spec((2 * RW_WIDTH, RW_WIDTH)),
                  _full_spec((1, RW_WIDTH)), _full_spec((1, RW_WIDTH)), _full_spec((1, RW_WIDTH)),
                  _full_spec((LANES, RW_WIDTH)), _full_spec((D, D)),
                  _full_spec((1, D)), _full_spec((1, D)), _full_spec((D, E))],
        out_specs=out_specs,
        out_shape=out_shape,
        compiler_params=_cparams(1),
        name="even_out",
    )(na, y_f, y_b, r, v, kd_f, kd_b, glow, h, mods, mods, mods, ones2, vec(gn_g), vec(gn_b), vec(r_k),
      g_up_p, w_out.astype(BF16), vec(ln_g), vec(ln_b), router_w)


def _resid_tail_kernel(y_ref, h_ref, gate_ref, shift_ref, scale_ref, lng_ref, lnb_ref, router_ref,
                       h_out_ref, f_ref, s_ref):
    _mixer_tail(h_ref[...], y_ref[...], gate_ref[...], lng_ref[...], lnb_ref[...], shift_ref[...],
                scale_ref[...], router_ref[...], h_out_ref, f_ref, s_ref)


def resid_tail_pallas(y, h, mods, dims, ln_g, ln_b, router_w):
    B, C, N = dims
    R, D = h.shape
    E = router_w.shape[1]
    tm = ROW_TILE
    mod_idx = _mod_index((C + N) // tm, C // tm, B)
    row = lambda width: pl.BlockSpec((tm, width), lambda i: (i, 0))
    vec = lambda z: z.reshape(1, -1)
    out_specs, out_shape = _tail_specs(R, D, E, tm)
    return pl.pallas_call(
        _resid_tail_kernel,
        grid=(R // tm,),
        in_specs=[row(D), row(D), _mod_spec(2, mod_idx), _mod_spec(3, mod_idx), _mod_spec(4, mod_idx),
                  _full_spec((1, D)), _full_spec((1, D)), _full_spec((D, E))],
        out_specs=out_specs,
        out_shape=out_shape,
        compiler_params=_cparams(1),
        name="resid_tail",
    )(y, h, mods, mods, mods, vec(ln_g), vec(ln_b), router_w)


MOE_TOKEN_TILE = 128


def _swiglu_bf16(x, wg, wu, wd):
    g = jnp.dot(x, wg, preferred_element_type=F32)
    u = jnp.dot(x, wu, preferred_element_type=F32)
    mid = (g * jax.nn.sigmoid(g) * u).astype(BF16)
    return jnp.dot(mid, wd, preferred_element_type=F32)


def _row_copy(src_ref, src_row, dst_ref, dst_row, sem):
    return pltpu.make_async_copy(src_ref.at[pl.ds(src_row, 1), :], dst_ref.at[pl.ds(dst_row, 1), :], sem)


def _slot(e_ref, rank_ref, starts_ref, i):
    return starts_ref[e_ref[i]] + rank_ref[i]


def _dispatch_kernel(e_ref, rank_ref, starts_ref, f_ref, xs_ref, sem):
    n_tok = f_ref.shape[0]

    def issue(t, carry):
        for k in range(TOP_K):
            _row_copy(f_ref, t, xs_ref, _slot(e_ref, rank_ref, starts_ref, t * TOP_K + k), sem).start()
        return carry

    lax.fori_loop(0, n_tok, issue, 0)

    def drain(t, carry):
        for k in range(TOP_K):
            _row_copy(f_ref, 0, xs_ref, 0, sem).wait()
        return carry

    lax.fori_loop(0, n_tok, drain, 0)


def _slot_specs(tm):
    flat = pl.BlockSpec((tm * TOP_K,), lambda i: (i,), memory_space=pltpu.SMEM)
    return [flat, flat, pl.BlockSpec(memory_space=pltpu.SMEM)]


def moe_dispatch_pallas(f, e_flat, rank_flat, starts):
    T, D = f.shape
    tm = MOE_TOKEN_TILE
    assert T % tm == 0
    return pl.pallas_call(
        _dispatch_kernel,
        grid=(T // tm,),
        in_specs=_slot_specs(tm) + [pl.BlockSpec((tm, D), lambda i: (i, 0))],
        out_specs=pl.BlockSpec(memory_space=pl.ANY),
        out_shape=jax.ShapeDtypeStruct((T * TOP_K, D), F32),
        scratch_shapes=[pltpu.SemaphoreType.DMA(())],
        compiler_params=_cparams(1),
        name="moe_dispatch",
    )(e_flat, rank_flat, starts, f)


def _expert_item_kernel(blk_ref, e_ref, lo_ref, hi_ref, first_ref, x_ref, wg_ref, wu_ref, wd_ref, o_ref,
                        wg16_ref, wu16_ref, wd16_ref):
    i = pl.program_id(0)
    lo, hi = lo_ref[i], hi_ref[i]

    @pl.when((i == 0) | (e_ref[i] != e_ref[jnp.maximum(i - 1, 0)]))
    def _():
        wg16_ref[...] = wg_ref[0, 0].astype(BF16)
        wu16_ref[...] = wu_ref[0, 0].astype(BF16)
        wd16_ref[...] = wd_ref[0, 0].astype(BF16)

    @pl.when(hi > lo)
    def _():
        y = _swiglu_bf16(x_ref[...].astype(BF16), wg16_ref[...], wu16_ref[...], wd16_ref[...])
        rows = blk_ref[i] * MOE_BLOCK + lax.broadcasted_iota(jnp.int32, (MOE_BLOCK, 1), 0)
        y = jnp.where((rows >= lo) & (rows < hi), y, 0.0)

        @pl.when(first_ref[i] == 1)
        def _():
            o_ref[...] = y

        @pl.when(first_ref[i] == 0)
        def _():
            o_ref[...] += y


def moe_experts_pallas(xs, items, layer, wg, wu, wd):
    n_rows, D = xs.shape
    F = wg.shape[-1]
    n_items = items[0].shape[0]
    grid_spec = pltpu.PrefetchScalarGridSpec(
        num_scalar_prefetch=5,
        grid=(n_items,),
        in_specs=[
            pl.BlockSpec((MOE_BLOCK, D), lambda i, blk, e, lo, hi, first: (blk[i], 0)),
            pl.BlockSpec((1, 1, D, F), lambda i, blk, e, lo, hi, first: (layer, e[i], 0, 0)),
            pl.BlockSpec((1, 1, D, F), lambda i, blk, e, lo, hi, first: (layer, e[i], 0, 0)),
            pl.BlockSpec((1, 1, F, D), lambda i, blk, e, lo, hi, first: (layer, e[i], 0, 0)),
        ],
        out_specs=pl.BlockSpec((MOE_BLOCK, D), lambda i, blk, e, lo, hi, first: (blk[i], 0)),
        scratch_shapes=[pltpu.VMEM((D, F), BF16), pltpu.VMEM((D, F), BF16), pltpu.VMEM((F, D), BF16)],
    )
    return pl.pallas_call(
        _expert_item_kernel,
        grid_spec=grid_spec,
        out_shape=jax.ShapeDtypeStruct((n_rows, D), F32),
        compiler_params=_cparams(1),
        name="moe_experts",
    )(*items, xs, wg, wu, wd)


def _combine_kernel(e_ref, rank_ref, starts_ref, w_ref, f_ref, h_ref, gate_ref, lng_ref, lnb_ref,
                    sg_ref, su_ref, sd_ref, ys_ref, o_ref, buf_ref, sem):
    n_tok = f_ref.shape[0]

    def issue(t, carry):
        for k in range(TOP_K):
            pltpu.make_async_copy(ys_ref.at[pl.ds(_slot(e_ref, rank_ref, starts_ref, t * TOP_K + k), 1), :],
                                  buf_ref.at[k, pl.ds(t, 1), :], sem).start()
        return carry

    lax.fori_loop(0, n_tok, issue, 0)
    acc = _swiglu_bf16(f_ref[...].astype(BF16), sg_ref[...], su_ref[...], sd_ref[...])

    def drain(t, carry):
        for k in range(TOP_K):
            pltpu.make_async_copy(ys_ref.at[pl.ds(0, 1), :], buf_ref.at[0, pl.ds(0, 1), :], sem).wait()
        return carry

    lax.fori_loop(0, n_tok, drain, 0)
    w = w_ref[...]
    for k in range(TOP_K):
        acc = acc + w[:, k:k + 1] * buf_ref[k]
    o_ref[...] = _layer_norm(DEEPNORM_ALPHA * h_ref[...] + gate_ref[...] * acc, lng_ref[...], lnb_ref[...])


def moe_combine_pallas(ys, e_flat, rank_flat, starts, w_sel, f, h, mods, dims, ln_g, ln_b, sg, su, sd):
    B, C, N = dims
    T, D = f.shape
    tm = MOE_TOKEN_TILE
    F = sg.shape[-1]
    mod_idx = _mod_index((C + N) // tm, C // tm, B)
    vec = lambda z: z.reshape(1, -1)
    return pl.pallas_call(
        _combine_kernel,
        grid=(T // tm,),
        in_specs=_slot_specs(tm) + [
            pl.BlockSpec((tm, TOP_K), lambda i: (i, 0)),
            pl.BlockSpec((tm, D), lambda i: (i, 0)),
            pl.BlockSpec((tm, D), lambda i: (i, 0)),
            _mod_spec(5, mod_idx), _full_spec((1, D)), _full_spec((1, D)),
            _full_spec((D, F)), _full_spec((D, F)), _full_spec((F, D)),
            pl.BlockSpec(memory_space=pl.ANY),
        ],
        out_specs=pl.BlockSpec((tm, D), lambda i: (i, 0)),
        out_shape=jax.ShapeDtypeStruct((T, D), F32),
        scratch_shapes=[pltpu.VMEM((TOP_K, tm, D), F32), pltpu.SemaphoreType.DMA(())],
        compiler_params=_cparams(1),
        name="moe_combine",
    )(e_flat, rank_flat, starts, w_sel, f, h, mods, vec(ln_g), vec(ln_b), sg, su, sd, ys)


REMOVED = -3e38


def _router_kernel(s_ref, bias_ref, e_ref, w_ref, rank_ref, cnt_ref, carry_ref):
    @pl.when(pl.program_id(0) == 0)
    def _():
        carry_ref[...] = jnp.zeros_like(carry_ref)

    s = s_ref[...]
    tm, n_exp = s.shape
    per_group = n_exp // N_GROUPS
    lane_i = lax.broadcasted_iota(jnp.int32, (tm, n_exp), 1)
    lane = lane_i.astype(F32)
    group_of_lane = lane_i // per_group
    big = float(n_exp)
    rmax = lambda z: jnp.max(z, axis=-1, keepdims=True)
    first_at = lambda z, m: jnp.min(jnp.where(z == m, lane, big), axis=-1, keepdims=True)

    grp = s + bias_ref[...]
    g_score = []
    for g in range(N_GROUPS):
        mg = jnp.where(group_of_lane == g, grp, REMOVED)
        m1 = rmax(mg)
        m2 = rmax(jnp.where(lane == first_at(mg, m1), REMOVED, mg))
        g_score.append(m1 + m2)
    choice = jnp.full_like(grp, NEG_INF)
    for g in range(N_GROUPS):
        ahead = jnp.zeros((tm, 1), F32)
        for g2 in range(N_GROUPS):
            if g2 != g:
                beats = (g_score[g2] > g_score[g]) | ((g_score[g2] == g_score[g]) & (g2 < g))
                ahead = ahead + beats.astype(F32)
        choice = jnp.where((group_of_lane == g) & (ahead < TOPK_GROUPS), grp, choice)

    col8 = lax.broadcasted_iota(jnp.int32, (tm, TOP_K), 1)
    e_out = jnp.zeros((tm, TOP_K), F32)
    w_out = jnp.zeros((tm, TOP_K), F32)
    picked = []
    onehot = jnp.zeros((tm, n_exp), F32)
    for k in range(TOP_K):
        idx = first_at(choice, rmax(choice))
        hit = lane == idx
        picked.append(hit)
        onehot = jnp.where(hit, 1.0, onehot)
        e_out = jnp.where(col8 == k, idx, e_out)
        w_out = jnp.where(col8 == k, jnp.sum(jnp.where(hit, s, 0.0), axis=-1, keepdims=True), w_out)
        choice = jnp.where(hit, REMOVED, choice)
    ri = lax.broadcasted_iota(jnp.int32, (tm, tm), 0)
    ci = lax.broadcasted_iota(jnp.int32, (tm, tm), 1)
    before = jnp.dot((ci < ri).astype(BF16), onehot.astype(BF16), preferred_element_type=F32) + carry_ref[0:1, :]
    rank = jnp.zeros((tm, TOP_K), F32)
    for k in range(TOP_K):
        rank = jnp.where(col8 == k, jnp.sum(jnp.where(picked[k], before, 0.0), axis=-1, keepdims=True), rank)
    total = carry_ref[0:1, :] + jnp.sum(onehot, axis=0, keepdims=True)
    carry_ref[...] = jnp.broadcast_to(total, carry_ref.shape)
    cnt_ref[...] = jnp.broadcast_to(total, cnt_ref.shape)
    e_ref[...] = e_out.astype(jnp.int32)
    w_ref[...] = w_out / jnp.sum(w_out, axis=-1, keepdims=True) * ROUTED_SCALE
    rank_ref[...] = rank.astype(jnp.int32)


def router_pallas(s, router_b):
    T, E = s.shape
    tm = ROW_TILE
    row8 = pl.BlockSpec((tm, TOP_K), lambda i: (i, 0))
    e_idx, w_sel, rank, cnt = pl.pallas_call(
        _router_kernel,
        grid=(T // tm,),
        in_specs=[pl.BlockSpec((tm, E), lambda i: (i, 0)), _full_spec((1, E))],
        out_specs=[row8, row8, row8, _full_spec((SUBLANES, E))],
        out_shape=[jax.ShapeDtypeStruct((T, TOP_K), jnp.int32), jax.ShapeDtypeStruct((T, TOP_K), F32),
                   jax.ShapeDtypeStruct((T, TOP_K), jnp.int32), jax.ShapeDtypeStruct((SUBLANES, E), F32)],
        scratch_shapes=[pltpu.VMEM((SUBLANES, E), F32)],
        compiler_params=_cparams(1),
        name="moe_router",
    )(s, router_b.astype(F32).reshape(1, E))
    return e_idx, w_sel, rank, cnt[0].astype(jnp.int32)


def moe_layer(f, s, h, mods, dims, ln_g, ln_b, router_b, layer, wg, wu, wd, sg, su, sd):
    T, D = f.shape
    E = s.shape[-1]
    e_idx, w_sel, rank, counts = router_pallas(s, router_b)
    n_asg = T * TOP_K
    assert n_asg % MOE_BLOCK == 0
    i32 = jnp.int32
    ends = jnp.cumsum(counts).astype(i32)
    starts = ends - counts
    e_flat, rank_flat = e_idx.reshape(-1), rank.reshape(-1)
    nb = n_asg // MOE_BLOCK
    first_blk = starts // MOE_BLOCK
    nblk = jnp.where(counts > 0, (ends - 1) // MOE_BLOCK - first_blk + 1, 0)
    item_ends = jnp.cumsum(nblk).astype(i32)
    item_starts = item_ends - nblk
    n_items = nb + E
    it = jnp.arange(n_items, dtype=i32)
    real = it < item_ends[-1]
    e_of = jnp.minimum(jnp.searchsorted(item_ends, it, side='right'), E - 1).astype(i32)
    e_of = jnp.where(real, e_of, e_of[item_ends[-1] - 1])
    blk = jnp.where(real, first_blk[e_of] + it - item_starts[e_of], nb - 1).astype(i32)
    lo = jnp.where(real, jnp.maximum(starts[e_of], blk * MOE_BLOCK), 0).astype(i32)
    hi = jnp.where(real, jnp.minimum(ends[e_of], (blk + 1) * MOE_BLOCK), 0).astype(i32)
    first = (real & (blk != jnp.concatenate([jnp.full((1,), -1, i32), blk[:-1]]))).astype(i32)
    xs = moe_dispatch_pallas(f, e_flat, rank_flat, starts)
    ys = moe_experts_pallas(xs, (blk, e_of, lo, hi, first), layer, wg, wu, wd)
    return moe_combine_pallas(ys, e_flat, rank_flat, starts, w_sel, f, h, mods, dims, ln_g, ln_b,
                              sg.astype(BF16), su.astype(BF16), sd.astype(BF16))


ML_QK_WIDTH = ML_HEADS * ML_QK_DIM
ROPE_GROUP = ML_QK_DIM // 4
GATE_IN, GATE_FORGET = 0, 2 * ML_HEADS


def _log_sigmoid(x):
    return -_softplus(-x)


def _proj_odd_kernel(h_ref, shift_ref, scale_ref, wqk_ref, wv_ref, wo_ref, wg_ref, wgt_ref, gb_ref, gbt_ref,
                     cos_ref, sin_ref, q_ref, k_ref, v_ref, o_ref, g_ref, gt_ref):
    a16 = (h_ref[...] * (1.0 + scale_ref[...]) + shift_ref[...]).astype(BF16)
    qk = jnp.dot(a16, wqk_ref[...], preferred_element_type=F32)
    lane = lax.broadcasted_iota(jnp.int32, (1, ML_QK_WIDTH), 1)
    first_of_pair = (lane % (2 * ROPE_GROUP)) < ROPE_GROUP
    cos, sin = cos_ref[...], sin_ref[...]

    def rope(z):
        partner = jnp.where(first_of_pair, pltpu.roll(z, ML_QK_WIDTH - ROPE_GROUP, 1), pltpu.roll(z, ROPE_GROUP, 1))
        return z * cos + partner * sin

    q_ref[...] = rope(qk[:, :ML_QK_WIDTH] * ML_QK_DIM ** -0.5).astype(BF16)
    k_ref[...] = rope(qk[:, ML_QK_WIDTH:]).astype(BF16)
    v_ref[...] = jnp.dot(a16, wv_ref[...], preferred_element_type=F32).astype(BF16)
    o_ref[...] = jnp.dot(a16, wo_ref[...], preferred_element_type=F32)
    g = jnp.dot(a16, wg_ref[...], preferred_element_type=F32) + gb_ref[...]
    gl = lax.broadcasted_iota(jnp.int32, g.shape, 1)
    g_ref[...] = jnp.where((gl >= GATE_FORGET) & (gl < 2 * GATE_FORGET), _log_sigmoid(g), g)
    gt = lax.dot_general(wgt_ref[...], a16, NT_DIMS, preferred_element_type=F32) + gbt_ref[...]
    gs = lax.broadcasted_iota(jnp.int32, gt.shape, 0)
    gt_ref[...] = jnp.where((gs >= GATE_FORGET) & (gs < 2 * GATE_FORGET), _log_sigmoid(gt), gt)


def _rope_tables(C, N):
    t = jnp.arange(N)
    pos = jnp.stack([(t // GRID_W).astype(F32), (t % GRID_W).astype(F32)], 0)
    lane = jnp.arange(ML_QK_WIDTH) % ML_QK_DIM
    inv = ROPE_BASE ** (-(lane % ROPE_GROUP).astype(F32) / ROPE_GROUP)
    ang = pos[lane // (2 * ROPE_GROUP)].T * inv[None, :]
    sign = jnp.where((lane % (2 * ROPE_GROUP)) < ROPE_GROUP, -1.0, 1.0)
    cos = jnp.concatenate([jnp.ones((C, ML_QK_WIDTH), F32), jnp.cos(ang)], 0)
    sin = jnp.concatenate([jnp.zeros((C, ML_QK_WIDTH), F32), jnp.sin(ang) * sign], 0)
    return cos, sin


def proj_odd_pallas(h, mods, dims, w_in, gate_b):
    B, C, N = dims
    R, D = h.shape
    tm = ROW_TILE
    tpb = (C + N) // tm
    mod_idx = _mod_index(tpb, C // tm, B)
    o_qk, o_v, o_o = 2 * ML_QK_WIDTH, 2 * ML_QK_WIDTH + ML_WIDTH, 2 * ML_QK_WIDTH + 2 * ML_WIDTH
    n_gate = w_in.shape[1] - o_o
    w16 = w_in.astype(BF16)
    w_g = jnp.pad(w16[:, o_o:], ((0, 0), (0, LANES - n_gate)))
    gb = jnp.pad(gate_b.astype(F32).reshape(-1), (0, LANES - n_gate))
    cos, sin = _rope_tables(C, N)
    row = lambda width: pl.BlockSpec((tm, width), lambda i: (i, 0))
    seg = pl.BlockSpec((tm, ML_QK_WIDTH), lambda i: (i % tpb, 0))
    return pl.pallas_call(
        _proj_odd_kernel,
        grid=(R // tm,),
        in_specs=[row(D), _mod_spec(0, mod_idx), _mod_spec(1, mod_idx),
                  _full_spec((D, 2 * ML_QK_WIDTH)), _full_spec((D, ML_WIDTH)), _full_spec((D, ML_WIDTH)),
                  _full_spec((D, LANES)), _full_spec((LANES, D)), _full_spec((1, LANES)), _full_spec((LANES, 1)),
                  seg, seg],
        out_specs=[row(ML_QK_WIDTH), row(ML_QK_WIDTH), row(ML_WIDTH), row(ML_WIDTH), row(LANES),
                   pl.BlockSpec((LANES, tm), lambda i: (0, i))],
        out_shape=[jax.ShapeDtypeStruct((R, ML_QK_WIDTH), BF16), jax.ShapeDtypeStruct((R, ML_QK_WIDTH), BF16),
                   jax.ShapeDtypeStruct((R, ML_WIDTH), BF16), jax.ShapeDtypeStruct((R, ML_WIDTH), F32),
                   jax.ShapeDtypeStruct((R, LANES), F32), jax.ShapeDtypeStruct((LANES, R), F32)],
        compiler_params=_cparams(1),
        name="proj_odd",
    )(h, mods, mods, w16[:, :o_qk], w16[:, o_qk:o_v], w16[:, o_v:o_o], w_g, w_g.T, gb.reshape(1, LANES),
      gb.reshape(LANES, 1), cos, sin)


def _split3_bf16(x, axis):
    x1 = x.astype(BF16)
    r1 = x - x1.astype(F32)
    x2 = r1.astype(BF16)
    x3 = (r1 - x2.astype(F32)).astype(BF16)
    return jnp.concatenate([x1, x2, x3], axis=axis)


def _mlstm_kernel(q_ref, k_ref, v_ref, g_ref, gt_ref, h_ref, c_ref, n_ref, m_ref, *, reverse):
    @pl.when(pl.program_id(1) == 0)
    def _():
        c_ref[...] = jnp.zeros_like(c_ref)
        n_ref[...] = jnp.zeros_like(n_ref)
        m_ref[...] = jnp.zeros_like(m_ref)

    L = q_ref.shape[1]
    ti = lax.broadcasted_iota(jnp.int32, (L, L), 0)
    si = lax.broadcasted_iota(jnp.int32, (L, L), 1)
    seen = (si >= ti) if reverse else (si <= ti)
    g = g_ref[0]
    gt = gt_ref[...]
    b_cols3 = jnp.dot(seen.astype(BF16), _split3_bf16(g, 1), preferred_element_type=F32)
    b_cols = b_cols3[:, :LANES] + b_cols3[:, LANES:2 * LANES] + b_cols3[:, 2 * LANES:]
    b_rows3 = lax.dot_general(_split3_bf16(gt, 0), seen.astype(BF16), NT_DIMS, preferred_element_type=F32)
    b_rows = b_rows3[:LANES] + b_rows3[LANES:2 * LANES] + b_rows3[2 * LANES:]
    half = lax.broadcasted_iota(jnp.int32, (1, LANES), 1) // ML_QK_DIM
    row_half = lax.broadcasted_iota(jnp.int32, (LANES, 1), 0) // ML_QK_DIM
    d_off = ML_HEADS if reverse else 0
    tn = (((0,), (0,)), ((), ()))
    for p in range(ML_HEADS // 2):
        qp = q_ref[0, :, p * LANES:(p + 1) * LANES]
        kp = k_ref[0, :, p * LANES:(p + 1) * LANES]
        c_old = c_ref[p]
        n_old = n_ref[p]
        m_old = m_ref[p]
        c16 = c_old.astype(BF16)
        c_new, n_new, m_new_pair = c_old, n_old, m_old
        for h2 in range(2):
            hd = 2 * p + h2
            vh = v_ref[0, :, hd * ML_V_DIM:(hd + 1) * ML_V_DIM]
            gi, gf = GATE_IN + d_off + hd, GATE_FORGET + d_off + hd
            ig_col, ig_row = g[:, gi:gi + 1], gt[gi:gi + 1, :]
            b_col, b_row = b_cols[:, gf:gf + 1], b_rows[gf:gf + 1, :]
            m0 = m_old[:, h2 * ML_QK_DIM:h2 * ML_QK_DIM + 1]
            qm = jnp.where(half == h2, qp, jnp.zeros_like(qp))
            km = jnp.where(half == h2, kp, jnp.zeros_like(kp))
            dlog = jnp.where(seen, b_col - b_row + ig_row, NEG_INF)
            inter = b_col + m0
            m_t = jnp.maximum(jnp.max(dlog, axis=-1, keepdims=True), inter)
            dw = jnp.exp(dlog - m_t)
            iw = jnp.exp(inter - m_t)
            sc = lax.dot_general(qm, kp, NT_DIMS, preferred_element_type=F32) * dw
            num = (jnp.dot(sc.astype(BF16), vh, preferred_element_type=F32)
                   + iw * jnp.dot(qm, c16, preferred_element_type=F32))
            den = (jnp.sum(sc, axis=-1, keepdims=True)
                   + iw * jnp.sum(qm.astype(F32) * n_old, axis=-1, keepdims=True))
            h_ref[0, :, hd * ML_V_DIM:(hd + 1) * ML_V_DIM] = num / jnp.maximum(jnp.abs(den), jnp.exp(-m_t))
            b_end = jnp.sum(g[:, gf:gf + 1], axis=0, keepdims=True)
            g_col = b_end - b_col + ig_col
            m_chunk = jnp.max(g_col, axis=0, keepdims=True)
            kw = km.astype(F32) * jnp.exp(g_col - m_chunk)
            m_new = jnp.maximum(b_end + m0, m_chunk)
            fa = jnp.exp(b_end + m0 - m_new)
            fb = jnp.exp(m_chunk - m_new)
            kv = lax.dot_general(kw.astype(BF16), vh, tn, preferred_element_type=F32)
            c_new = jnp.where(row_half == h2, fa * c_old + fb * kv, c_new)
            n_new = jnp.where(half == h2, fa * n_old + fb * jnp.sum(kw, axis=0, keepdims=True), n_new)
            m_new_pair = jnp.where(half == h2, m_new, m_new_pair)
        c_ref[p] = c_new
        n_ref[p] = n_new
        m_ref[p] = m_new_pair


def mlstm_pallas(q, k, v, g, gt, dims, reverse):
    B, C, N = dims
    S = C + N
    L = ML_CHUNK
    assert C % L == 0 and N % L == 0
    n_ctx, n_all = C // L, S // L
    if reverse:
        chunk = lambda j: jnp.where(j < n_ctx, n_ctx - 1 - j, n_all - 1 - (j - n_ctx))
    else:
        chunk = lambda j: j
    blk = lambda width: pl.BlockSpec((1, L, width), lambda b, j: (b, chunk(j), 0))
    n_pair = ML_HEADS // 2
    out = pl.pallas_call(
        functools.partial(_mlstm_kernel, reverse=reverse),
        grid=(B, n_all),
        in_specs=[blk(ML_QK_WIDTH), blk(ML_QK_WIDTH), blk(ML_WIDTH), blk(LANES),
                  pl.BlockSpec((LANES, L), lambda b, j: (0, b * n_all + chunk(j)))],
        out_specs=blk(ML_WIDTH),
        out_shape=jax.ShapeDtypeStruct((B, S, ML_WIDTH), F32),
        scratch_shapes=[pltpu.VMEM((n_pair, LANES, ML_V_DIM), F32), pltpu.VMEM((n_pair, 1, LANES), F32),
                        pltpu.VMEM((n_pair, 1, LANES), F32)],
        compiler_params=_cparams(2),
        name="mlstm_bwd" if reverse else "mlstm_fwd",
    )(q.reshape(B, S, -1), k.reshape(B, S, -1), v.reshape(B, S, -1), g.reshape(B, S, -1), gt)
    return out.reshape(B * S, ML_WIDTH)


def _odd_out_kernel(hf_ref, hb_ref, o_ref, h_ref, gate_ref, shift_ref, scale_ref, ng_ref, wout_ref,
                    lng_ref, lnb_ref, router_ref, h_out_ref, f_ref, s_ref):
    hs = hf_ref[...] + hb_ref[...]
    parts = []
    for hd in range(ML_HEADS):
        x = hs[:, hd * ML_V_DIM:(hd + 1) * ML_V_DIM]
        parts.append(x * lax.rsqrt(jnp.mean(x * x, axis=-1, keepdims=True) + ML_NORM_EPS))
    hn = jnp.concatenate(parts, axis=-1) * ng_ref[...] * jax.nn.sigmoid(o_ref[...])
    y = jnp.dot(hn.astype(BF16), wout_ref[...], preferred_element_type=F32)
    _mixer_tail(h_ref[...], y, gate_ref[...], lng_ref[...], lnb_ref[...], shift_ref[...], scale_ref[...],
                router_ref[...], h_out_ref, f_ref, s_ref)


def odd_out_pallas(h_f, h_b, o, h, mods, dims, norm_g, w_out, ln_g, ln_b, router_w):
    B, C, N = dims
    R, D = h.shape
    E = router_w.shape[1]
    tm = ROW_TILE
    mod_idx = _mod_index((C + N) // tm, C // tm, B)
    row = lambda width: pl.BlockSpec((tm, width), lambda i: (i, 0))
    vec = lambda z: z.reshape(1, -1)
    out_specs, out_shape = _tail_specs(R, D, E, tm)
    return pl.pallas_call(
        _odd_out_kernel,
        grid=(R // tm,),
        in_specs=[row(ML_WIDTH), row(ML_WIDTH), row(ML_WIDTH), row(D),
                  _mod_spec(2, mod_idx), _mod_spec(3, mod_idx), _mod_spec(4, mod_idx),
                  _full_spec((1, ML_WIDTH)), _full_spec((ML_WIDTH, D)),
                  _full_spec((1, D)), _full_spec((1, D)), _full_spec((D, E))],
        out_specs=out_specs,
        out_shape=out_shape,
        compiler_params=_cparams(1),
        name="odd_out",
    )(h_f, h_b, o, h, mods, mods, mods, vec(norm_g), w_out.astype(BF16), vec(ln_g), vec(ln_b), router_w)


def _offsets(layout, prefix=''):
    offs, o = {}, 0
    for name, width in layout:
        if name.startswith(prefix):
            offs[name] = (o, width)
            o += width
    return offs


def project(h, w, layout, names):
    offs = _offsets(layout)
    if len(names) == len(layout):
        y = jnp.einsum('btd,de->bte', h, w)
        return {n: y[..., offs[n][0]:offs[n][0] + offs[n][1]] for n in names}
    return {n: jnp.einsum('btd,de->bte', h, w[:, offs[n][0]:offs[n][0] + offs[n][1]]) for n in names}


def axial_rope(z):
    T, dh = z.shape[1], z.shape[-1]
    half = dh // 2
    nf = half // 2
    t = jnp.arange(T)
    row = (t // GRID_W).astype(F32)
    col = (t % GRID_W).astype(F32)
    inv = ROPE_BASE ** (-jnp.arange(nf, dtype=F32) / nf)

    def rot(u, pos):
        ang = pos[:, None] * inv[None, :]
        cos = jnp.cos(ang)[None, :, None, :]
        sin = jnp.sin(ang)[None, :, None, :]
        u1, u2 = u[..., :nf], u[..., nf:]
        return jnp.concatenate([u1 * cos - u2 * sin, u1 * sin + u2 * cos], -1)

    return jnp.concatenate([rot(z[..., :half], row), rot(z[..., half:], col)], -1).astype(z.dtype)


def ml_prep(t, gate_b, rope, need_q):
    B, T = t['ml_k'].shape[:2]
    heads = lambda z, dh: z.reshape(B, T, ML_HEADS, dh).astype(F32)
    k = heads(t['ml_k'], ML_QK_DIM)
    q = heads(t['ml_q'], ML_QK_DIM) * ML_QK_DIM ** -0.5 if need_q else None
    if rope:
        k = axial_rope(k)
        q = axial_rope(q)
    v = heads(t['ml_v'], ML_V_DIM)
    gb = gate_b.astype(F32)
    bht = lambda z: z.astype(F32).transpose(0, 2, 1)
    ig = (bht(t['ml_if'] + gb[0]), bht(t['ml_ib'] + gb[1]))
    lf = (jax.nn.log_sigmoid(bht(t['ml_ff'] + gb[2])), jax.nn.log_sigmoid(bht(t['ml_fb'] + gb[3])))
    bhtd = lambda z: None if z is None else z.transpose(0, 2, 1, 3)
    return bhtd(q), bhtd(k), bhtd(v), ig, lf


def ml_chunk_states(k, v, ig, lf, state0):
    B, H, T, dk = k.shape
    dv = v.shape[-1]
    L = min(ML_CHUNK, T)
    nc = T // L
    kc = k.reshape(B, H, nc, L, dk)
    vc = v.reshape(B, H, nc, L, dv)
    b = jnp.cumsum(lf.reshape(B, H, nc, L), -1)
    b_end = b[..., -1]
    g = b_end[..., None] - b + ig.reshape(B, H, nc, L)
    m_chunk = g.max(-1)
    wgt = jnp.exp(g - m_chunk[..., None])
    kv = jnp.einsum('bhnl,bhnlk,bhnlv->bhnkv', wgt, kc, vc)
    ks = jnp.einsum('bhnl,bhnlk->bhnk', wgt, kc)

    def step(state, inp):
        c_mem, n_mem, m = state
        be, mc, kv_n, ks_n = inp
        m_new = jnp.maximum(be + m, mc)
        fa = jnp.exp(be + m - m_new)
        fb = jnp.exp(mc - m_new)
        c_new = fa[..., None, None] * c_mem + fb[..., None, None] * kv_n
        n_new = fa[..., None] * n_mem + fb[..., None] * ks_n
        return (c_new, n_new, m_new), state

    xs = tuple(jnp.moveaxis(z, 2, 0) for z in (b_end, m_chunk, kv, ks))
    final, starts = lax.scan(step, state0, xs)
    return tuple(jnp.moveaxis(z, 0, 2) for z in starts), final


def ml_chunk_outputs(q, k, v, ig, lf, starts):
    B, H, T, dk = q.shape
    dv = v.shape[-1]
    L = min(ML_CHUNK, T)
    nc = T // L
    qc = q.reshape(B, H, nc, L, dk)
    kc = k.reshape(B, H, nc, L, dk)
    vc = v.reshape(B, H, nc, L, dv)
    b = jnp.cumsum(lf.reshape(B, H, nc, L), -1)
    c0, n0, m0 = starts
    causal = jnp.tril(jnp.ones((L, L), bool))
    dlog = jnp.where(causal, b[..., :, None] - b[..., None, :] + ig.reshape(B, H, nc, L)[..., None, :], NEG_INF)
    inter = b + m0[..., None]
    m = jnp.maximum(dlog.max(-1), inter)
    dw = jnp.exp(dlog - m[..., None])
    iw = jnp.exp(inter - m)
    s = jnp.einsum('bhntd,bhnsd->bhnts', qc, kc) * dw
    num = jnp.einsum('bhnts,bhnsv->bhntv', s, vc) + iw[..., None] * jnp.einsum('bhntd,bhndv->bhntv', qc, c0)
    den = s.sum(-1) + iw * jnp.einsum('bhntd,bhnd->bhnt', qc, n0)
    h = num / jnp.maximum(jnp.abs(den), jnp.exp(-m))[..., None]
    return h.reshape(B, H, T, dv)


def ml_readout(h, o, norm_g):
    B, H, T, dv = h.shape
    hn = h * lax.rsqrt(jnp.mean(h * h, -1, keepdims=True) + ML_NORM_EPS)
    hn = hn.transpose(0, 2, 1, 3).reshape(B, T, H * dv) * norm_g
    return hn * jax.nn.sigmoid(o.astype(F32))


def odd_mixer(a_lat, a_ctx, w_in, w_out, gate_b, norm_g, need_ctx):
    names = tuple(n for n, _ in ODD_LAYOUT)
    t_lat = project(a_lat, w_in, ODD_LAYOUT, names)
    t_ctx = project(a_ctx, w_in, ODD_LAYOUT, names if need_ctx else ODD_CTX_STATE_COLS)
    q_l, k_l, v_l, ig_l, lf_l = ml_prep(t_lat, gate_b, True, True)
    q_c, k_c, v_c, ig_c, lf_c = ml_prep(t_ctx, gate_b, False, need_ctx)
    B = a_lat.shape[0]
    zero = (jnp.zeros((B, ML_HEADS, ML_QK_DIM, ML_V_DIM), F32),
            jnp.zeros((B, ML_HEADS, ML_QK_DIM), F32),
            jnp.zeros((B, ML_HEADS), F32))
    h_l, h_c = [], []
    for d in range(2):
        f = (lambda z: jnp.flip(z, 2)) if d == 1 else (lambda z: z)
        starts_c, final_c = ml_chunk_states(f(k_c), f(v_c), f(ig_c[d]), f(lf_c[d]), zero)
        starts_l, _ = ml_chunk_states(f(k_l), f(v_l), f(ig_l[d]), f(lf_l[d]), final_c)
        h_l.append(f(ml_chunk_outputs(f(q_l), f(k_l), f(v_l), f(ig_l[d]), f(lf_l[d]), starts_l)))
        if need_ctx:
            h_c.append(f(ml_chunk_outputs(f(q_c), f(k_c), f(v_c), f(ig_c[d]), f(lf_c[d]), starts_c)))
    y_lat = jnp.einsum('btd,de->bte', ml_readout(h_l[0] + h_l[1], t_lat['ml_o'], norm_g), w_out).astype(a_lat.dtype)
    if not need_ctx:
        return y_lat, None
    y_ctx = jnp.einsum('btd,de->bte', ml_readout(h_c[0] + h_c[1], t_ctx['ml_o'], norm_g), w_out).astype(a_ctx.dtype)
    return y_lat, y_ctx


def kernel(x, c, ctx, c_ctx, ada_w, ada_b, ln_g, ln_b, ev_w_in, ev_w_out, na_rpb, rw_mu, rw_w0, rw_w_up,
           rw_a0, rw_a_up, rw_g_up, rw_k_k, rw_k_a, rw_r_k, rw_gn_g, rw_gn_b, od_w_in, od_w_out, ml_gate_b,
           ml_norm_g, moe_router, moe_bias, moe_w_gate, moe_w_up, moe_w_down, sh_w_gate, sh_w_up, sh_w_down):
    B, N, D = x.shape
    C = ctx.shape[1]
    S = C + N
    dims = (B, C, N)
    assert C % ROW_TILE == 0 and N % ROW_TILE == 0 and B + 1 <= SUBLANES
    h = jnp.concatenate([ctx, x], axis=1).reshape(B * S, D)
    cond = jnp.zeros((SUBLANES, D), F32).at[:B].set(c).at[B].set(c_ctx)
    for l in range(DEPTH):
        mods = ada_mods_pallas(cond, ada_w[l], ada_b[l])
        if l % 2 == 0:
            e = l // 2
            (q, k, v, dec_f, dec_b, beta_f, beta_b, kd_f, kd_b, nkk, rv, rr, glow) = proj_even_pallas(
                h, mods, dims, ev_w_in[e], rw_mu[e], rw_w0[e], rw_w_up[e], rw_a0[e], rw_a_up[e],
                rw_k_k[e], rw_k_a[e])
            y_f, y_b = rwkv_scan_pallas(dec_f, beta_f, kd_f, dec_b, beta_b, kd_b, nkk, rv, rr, dims)
            na = attention_pallas(q, k, v, na_rpb[e], dims)
            h, f, s = even_out_pallas(na, y_f, y_b, rr, rv, kd_f, kd_b, glow, h, mods, dims, rw_g_up[e],
                                      rw_r_k[e], rw_gn_g[e], rw_gn_b[e], ev_w_out[e], ln_g[l, 0], ln_b[l, 0],
                                      moe_router[l])
        else:
            o = l // 2
            q, k, v, og, g, gt = proj_odd_pallas(h, mods, dims, od_w_in[o], ml_gate_b[o])
            h_f = mlstm_pallas(q, k, v, g, gt, dims, False)
            h_b = mlstm_pallas(q, k, v, g, gt, dims, True)
            h, f, s = odd_out_pallas(h_f, h_b, og, h, mods, dims, ml_norm_g[o], od_w_out[o], ln_g[l, 0],
                                     ln_b[l, 0], moe_router[l])
        h = moe_layer(f, s, h, mods, dims, ln_g[l, 1], ln_b[l, 1], moe_bias[l], l, moe_w_gate, moe_w_up,
                      moe_w_down, sh_w_gate[l], sh_w_up[l], sh_w_down[l])
    return h.reshape(B, S, D)[:, C:]
```

```python
import functools

import jax
import jax.numpy as jnp
import numpy as np
from jax import lax
from jax.experimental import pallas as pl
from jax.experimental.pallas import tpu as pltpu

D_MODEL = 1024
DEPTH = 2
GRID_W = 64

DEEPNORM_ALPHA = (2.0 * DEPTH) ** 0.25
LN_EPS = 1e-5
NEG_INF = -1e30
F32 = jnp.float32
BF16 = jnp.bfloat16

NA_HEAD_DIM = 64
NA_WIDTH = D_MODEL // 2
NA_HEADS = NA_WIDTH // NA_HEAD_DIM
NA_WIN_ROWS = 8
NA_WIN_COLS = 16
NA_SCALE = NA_HEAD_DIM ** -0.5

RW_HEAD_DIM = 64
RW_WIDTH = D_MODEL // 2
RW_HEADS = RW_WIDTH // RW_HEAD_DIM
RW_DECAY_LORA = 32
RW_AAA_LORA = 32
RW_GATE_LORA = 96
RW_GN_EPS = 64e-5

ML_HEADS = 8
ML_V_DIM = D_MODEL // ML_HEADS
ML_QK_DIM = ML_V_DIM // 2
ML_WIDTH = ML_HEADS * ML_V_DIM
ML_CHUNK = 128
ML_NORM_EPS = 1e-6
ROPE_BASE = 10000.0

N_EXPERTS = 256
TOP_K = 8
N_GROUPS = 8
TOPK_GROUPS = 4
ROUTED_SCALE = 2.5
MOE_BLOCK = 256

ODD_LAYOUT = (
    ('ml_q', ML_HEADS * ML_QK_DIM), ('ml_k', ML_HEADS * ML_QK_DIM),
    ('ml_v', ML_WIDTH), ('ml_o', ML_WIDTH),
    ('ml_if', ML_HEADS), ('ml_ib', ML_HEADS), ('ml_ff', ML_HEADS), ('ml_fb', ML_HEADS),
)
ODD_CTX_STATE_COLS = ('ml_k', 'ml_v', 'ml_if', 'ml_ib', 'ml_ff', 'ml_fb')

SUBLANES = 8
LANES = 128
VMEM_LIMIT_BYTES = 56 * 1024 * 1024

ROW_TILE = 256
N_MODS = 6
RW_COLS = 3 * RW_WIDTH + 2 * LANES
NT_DIMS = (((1,), (1,)), ((), ()))


def _cparams(n_axes):
    return pltpu.CompilerParams(dimension_semantics=("arbitrary",) * n_axes, vmem_limit_bytes=VMEM_LIMIT_BYTES)


def _full_spec(shape):
    return pl.BlockSpec(shape, lambda *_: (0,) * len(shape))


def _split_bf16(x):
    hi = x.astype(BF16)
    lo = (x - hi.astype(F32)).astype(BF16)
    return jnp.concatenate([hi, lo], axis=-1)


def _block_ones(n_rows, n_cols, seg):
    row = lax.broadcasted_iota(jnp.int32, (n_rows, n_cols), 0)
    col = lax.broadcasted_iota(jnp.int32, (n_rows, n_cols), 1)
    return (((row % n_cols) // seg) == (col // seg)).astype(BF16)


def _seg_sum(x, ones2):
    return jnp.dot(_split_bf16(x), ones2, preferred_element_type=F32)


def _mod_index(tiles_per_batch, ctx_tiles, n_batch):
    def idx(i):
        return jnp.where(i % tiles_per_batch < ctx_tiles, n_batch, i // tiles_per_batch)
    return idx


def _mod_spec(chunk, mod_idx):
    return pl.BlockSpec((None, None, 1, D_MODEL), lambda i: (mod_idx(i), chunk, 0, 0))


def _ada_kernel(c_ref, w_ref, b_ref, o_ref):
    c = c_ref[...]
    x = (c * jax.nn.sigmoid(c)).astype(BF16)
    o_ref[...] = jnp.dot(x, w_ref[...].astype(BF16), preferred_element_type=F32) + b_ref[...]


def ada_mods_pallas(cond, w, b):
    n, D = cond.shape
    n_out = w.shape[1]
    tn = 512
    out = pl.pallas_call(
        _ada_kernel,
        grid=(n_out // tn,),
        in_specs=[_full_spec((n, D)), pl.BlockSpec((D, tn), lambda j: (0, j)), pl.BlockSpec((1, tn), lambda j: (0, j))],
        out_specs=pl.BlockSpec((n, tn), lambda j: (0, j)),
        out_shape=jax.ShapeDtypeStruct((n, n_out), F32),
        compiler_params=_cparams(1),
        name="ada_mods",
    )(cond, w, b.reshape(1, n_out))
    return out.reshape(n, N_MODS, 1, D)


def _softplus(x):
    return jnp.maximum(x, 0.0) + jnp.log(1.0 + jnp.exp(-jnp.abs(x)))


def _proj_even_kernel(h_ref, hp_ref, hn_ref, shift_ref, scale_ref, wna_ref, wrw_ref, mu_ref, ones_ref,
                      kk_ref, ka_ref, w0_ref, a0_ref, wup_ref, aup_ref,
                      q_ref, k_ref, v_ref, dec_f_ref, dec_b_ref, beta_f_ref, beta_b_ref, kd_f_ref, kd_b_ref,
                      nkk_ref, rv_ref, rr_ref, glow_ref, *, tiles_per_batch, ctx_tiles):
    i = pl.program_id(0)
    j = i % tiles_per_batch
    first = (j == 0) | (j == ctx_tiles)
    last = (j == ctx_tiles - 1) | (j == tiles_per_batch - 1)
    tm = h_ref.shape[0]
    gain = 1.0 + scale_ref[...]
    shift = shift_ref[...]
    a = h_ref[...] * gain + shift
    a_prev = jnp.where(first, 0.0, hp_ref[SUBLANES - 1:SUBLANES, :] * gain + shift)
    a_next = jnp.where(last, 0.0, hn_ref[0:1, :] * gain + shift)
    rid = lax.broadcasted_iota(jnp.int32, (tm, 1), 0)
    prev = jnp.where(rid == 0, a_prev, pltpu.roll(a, 1, 0))
    nxt = jnp.where(rid == tm - 1, a_next, pltpu.roll(a, tm - 1, 0))
    a16 = a.astype(BF16)
    nb16 = (0.5 * (prev + nxt)).astype(BF16)

    na = jnp.dot(a16, wna_ref[...], preferred_element_type=F32)
    q_ref[...] = (na[:, :NA_WIDTH] * NA_SCALE).astype(BF16)
    k_ref[...] = na[:, NA_WIDTH:2 * NA_WIDTH].astype(BF16)
    v_ref[...] = na[:, 2 * NA_WIDTH:].astype(BF16)

    pa = jnp.dot(a16, wrw_ref[...], preferred_element_type=F32)
    pn = jnp.dot(nb16, wrw_ref[...], preferred_element_type=F32)
    t = pa + mu_ref[...] * (pn - pa)
    r = t[:, :RW_WIDTH]
    k = t[:, RW_WIDTH:2 * RW_WIDTH]
    lora = t[:, 3 * RW_WIDTH:3 * RW_WIDTH + LANES]
    rr_ref[...] = r
    rv_ref[...] = t[:, 2 * RW_WIDTH:3 * RW_WIDTH]
    glow_ref[...] = t[:, 3 * RW_WIDTH + LANES:]

    kk = k * kk_ref[...]
    norm = jnp.sqrt(_seg_sum(kk * kk, ones_ref[...]))
    kk = kk / jnp.maximum(norm, 1e-12)
    nkk_ref[...] = -kk
    lora_t = jnp.tanh(lora).astype(BF16)
    lora16 = lora.astype(BF16)
    outs = ((dec_f_ref, beta_f_ref, kd_f_ref), (dec_b_ref, beta_b_ref, kd_b_ref))
    for d in range(2):
        w_log = -_softplus(-(w0_ref[d:d + 1, :] + jnp.dot(lora_t, wup_ref[d], preferred_element_type=F32))) - 0.5
        a_gate = jax.nn.sigmoid(a0_ref[d:d + 1, :] + jnp.dot(lora16, aup_ref[d], preferred_element_type=F32))
        outs[d][0][...] = jnp.exp(-jnp.exp(w_log))
        outs[d][1][...] = kk * a_gate
        outs[d][2][...] = k * (1.0 + (a_gate - 1.0) * ka_ref[...])


def proj_even_pallas(h, mods, dims, w_in, mu, w0, w_up, a0, a_up, k_k, k_a):
    B, C, N = dims
    R, D = h.shape
    tm = ROW_TILE
    tpb, ctx_tiles = (C + N) // tm, C // tm
    mod_idx = _mod_index(tpb, ctx_tiles, B)
    w_na = w_in[:, :3 * NA_WIDTH].astype(BF16)
    n_rw = w_in.shape[1] - 3 * NA_WIDTH
    w_rw = jnp.pad(w_in[:, 3 * NA_WIDTH:], ((0, 0), (0, RW_COLS - n_rw))).astype(BF16)
    mu_p = jnp.pad(mu, (0, RW_COLS - n_rw)).reshape(1, RW_COLS)
    ones2 = _block_ones(2 * RW_WIDTH, RW_WIDTH, RW_HEAD_DIM)
    lr = RW_DECAY_LORA

    def pad_up(m, first_row):
        out = jnp.zeros((2, LANES, RW_WIDTH), F32)
        for d in range(2):
            out = out.at[d, first_row + d * lr:first_row + (d + 1) * lr].set(m[d])
        return out.astype(BF16)

    row = lambda width: pl.BlockSpec((tm, width), lambda i: (i, 0))
    hb = tm // SUBLANES
    n_hb = R // SUBLANES
    wide = jax.ShapeDtypeStruct((R, RW_WIDTH), F32)
    half = jax.ShapeDtypeStruct((R, NA_WIDTH), BF16)
    return pl.pallas_call(
        functools.partial(_proj_even_kernel, tiles_per_batch=tpb, ctx_tiles=ctx_tiles),
        grid=(R // tm,),
        in_specs=[
            row(D),
            pl.BlockSpec((SUBLANES, D), lambda i: (jnp.maximum(i * hb - 1, 0), 0)),
            pl.BlockSpec((SUBLANES, D), lambda i: (jnp.minimum((i + 1) * hb, n_hb - 1), 0)),
            _mod_spec(0, mod_idx), _mod_spec(1, mod_idx),
            _full_spec((D, 3 * NA_WIDTH)), _full_spec((D, RW_COLS)), _full_spec((1, RW_COLS)),
            _full_spec((2 * RW_WIDTH, RW_WIDTH)),
            _full_spec((1, RW_WIDTH)), _full_spec((1, RW_WIDTH)),
            _full_spec((2, RW_WIDTH)), _full_spec((2, RW_WIDTH)),
            _full_spec((2, LANES, RW_WIDTH)), _full_spec((2, LANES, RW_WIDTH)),
        ],
        out_specs=[row(NA_WIDTH)] * 3 + [row(RW_WIDTH)] * 9 + [row(LANES)],
        out_shape=[half] * 3 + [wide] * 9 + [jax.ShapeDtypeStruct((R, LANES), F32)],
        compiler_params=_cparams(1),
        name="proj_even",
    )(h, h, h, mods, mods, w_na, w_rw, mu_p, ones2, k_k.reshape(1, -1), k_a.reshape(1, -1), w0, a0,
      pad_up(w_up, 0), pad_up(a_up, 2 * lr))


RW_SCAN_TIME = 256
RW_SEG_TWO_PIECE = (False, False, False)


def _rwkv_scan_kernel(wf_ref, bf_ref, kf_ref, nf_ref, vf_ref, rf_ref,
                      wb_ref, bb_ref, kb_ref, nb_ref, vb_ref, rb_ref, yf_ref, yb_ref, s_ref):
    @pl.when(pl.program_id(0) == 0)
    def _():
        s_ref[...] = jnp.zeros_like(s_ref)

    n_batch, n_time, width = wf_ref.shape
    n_pair = width // LANES
    n_dir_chain = n_batch * n_pair
    n_chain = 2 * n_dir_chain
    rows_all = n_chain * RW_HEAD_DIM
    ones2 = _block_ones(2 * LANES, LANES, RW_HEAD_DIM)
    vi = lax.broadcasted_iota(jnp.int32, (1, RW_HEAD_DIM, LANES), 1)
    li = lax.broadcasted_iota(jnp.int32, (1, RW_HEAD_DIM, LANES), 2)
    diag = (li % RW_HEAD_DIM) == vi
    n_sub = n_time // SUBLANES

    def seg(x, two_piece):
        x = x.reshape(rows_all, LANES)
        if two_piece:
            out = jnp.dot(_split_bf16(x), ones2, preferred_element_type=F32)
        else:
            out = jnp.dot(x.astype(BF16), ones2[:LANES], preferred_element_type=F32)
        return out.reshape(n_chain, RW_HEAD_DIM, LANES)

    def chains(ref, rows):
        x = ref[:, rows, :]
        return [x[b, :, p * LANES:(p + 1) * LANES] for b in range(n_batch) for p in range(n_pair)]

    def sub(i, carry):
        rows_f = pl.ds(pl.multiple_of(i * SUBLANES, SUBLANES), SUBLANES)
        rows_b = pl.ds(pl.multiple_of((n_sub - 1 - i) * SUBLANES, SUBLANES), SUBLANES)
        load = lambda f_ref, b_ref: (jnp.stack(chains(f_ref, rows_f)), jnp.stack(chains(b_ref, rows_b)))
        w8, beta8, kd8 = load(wf_ref, wb_ref), load(bf_ref, bb_ref), load(kf_ref, kb_ref)
        nkk8, v8, r8 = load(nf_ref, nb_ref), load(vf_ref, vb_ref), load(rf_ref, rb_ref)

        def at(pair, t):
            tb = SUBLANES - 1 - t
            return jnp.concatenate([pair[0][:, t:t + 1, :], pair[1][:, tb:tb + 1, :]], axis=0)

        s = s_ref[...]
        rows = []
        for t in range(SUBLANES):
            vcol = seg(jnp.where(diag, at(v8, t), 0.0), RW_SEG_TWO_PIECE[0])
            sa = seg(s * at(nkk8, t), RW_SEG_TWO_PIECE[1])
            s = s * at(w8, t) + sa * at(beta8, t) + vcol * at(kd8, t)
            ybc = seg(s * at(r8, t), RW_SEG_TWO_PIECE[2])
            rows.append(jnp.sum(jnp.where(diag, ybc, 0.0), axis=1, keepdims=True))
        s_ref[...] = s
        y_f = jnp.concatenate([row[:n_dir_chain] for row in rows], axis=1)
        y_b = jnp.concatenate([row[n_dir_chain:] for row in rows[::-1]], axis=1)
        for b in range(n_batch):
            for p in range(n_pair):
                c = b * n_pair + p
                yf_ref[b, rows_f, p * LANES:(p + 1) * LANES] = y_f[c]
                yb_ref[b, rows_b, p * LANES:(p + 1) * LANES] = y_b[c]
        return carry

    lax.fori_loop(0, n_sub, sub, 0)


def rwkv_scan_pallas(dec_f, beta_f, kd_f, dec_b, beta_b, kd_b, nkk, v, r, dims):
    B, C, N = dims
    S = C + N
    tc = RW_SCAN_TIME
    assert C % tc == 0 and N % tc == 0
    n_ctx, n_all = C // tc, S // tc
    as3 = lambda z: z.reshape(B, S, RW_WIDTH)
    fwd = pl.BlockSpec((B, tc, RW_WIDTH), lambda j: (0, j, 0))
    bwd = pl.BlockSpec((B, tc, RW_WIDTH),
                       lambda j: (0, jnp.where(j < n_ctx, n_ctx - 1 - j, n_all - 1 - (j - n_ctx)), 0))
    out = jax.ShapeDtypeStruct((B, S, RW_WIDTH), F32)
    y_f, y_b = pl.pallas_call(
        _rwkv_scan_kernel,
        grid=(n_all,),
        in_specs=[fwd] * 6 + [bwd] * 6,
        out_specs=[fwd, bwd],
        out_shape=[out, out],
        scratch_shapes=[pltpu.VMEM((2 * B * (RW_WIDTH // LANES), RW_HEAD_DIM, LANES), F32)],
        compiler_params=_cparams(1),
        name="rwkv_scan",
    )(as3(dec_f), as3(beta_f), as3(kd_f), as3(nkk), as3(v), as3(r),
      as3(dec_b), as3(beta_b), as3(kd_b), as3(nkk), as3(v), as3(r))
    return y_f.reshape(B * S, RW_WIDTH), y_b.reshape(B * S, RW_WIDTH)


NA_BAND = NA_WIN_ROWS * GRID_W


def _na_row_start(j, ctx_blocks, n_rows):
    r = jnp.maximum(j - ctx_blocks, 0)
    return r, jnp.clip(r - NA_WIN_ROWS // 2, 0, n_rows - NA_WIN_ROWS)


def _na_kernel(q_ref, k_ref, v_ref, bias_ref, o_ref, *, n_ctx):
    j = pl.program_id(1)
    ctx_blocks = n_ctx // GRID_W
    n_rows = pl.num_programs(1) - ctx_blocks
    _, row_start = _na_row_start(j, ctx_blocks, n_rows)
    start = pl.multiple_of(n_ctx + row_start * GRID_W, GRID_W)
    q = q_ref[0]
    kb = k_ref[0, pl.ds(start, NA_BAND), :]
    vb = v_ref[0, pl.ds(start, NA_BAND), :]
    kc = k_ref[0, pl.ds(0, n_ctx), :]
    vc = v_ref[0, pl.ds(0, n_ctx), :]
    head_of_lane = lax.broadcasted_iota(jnp.int32, (GRID_W, LANES), 1) // NA_HEAD_DIM
    for p in range(NA_WIDTH // LANES):
        cols = slice(p * LANES, (p + 1) * LANES)
        qp, kp, vp, kcp, vcp = q[:, cols], kb[:, cols], vb[:, cols], kc[:, cols], vc[:, cols]
        outs = []
        for h2 in range(LANES // NA_HEAD_DIM):
            qm = jnp.where(head_of_lane == h2, qp, jnp.zeros_like(qp))
            s_loc = lax.dot_general(qm, kp, NT_DIMS, preferred_element_type=F32) + bias_ref[0, 2 * p + h2]
            s_ctx = lax.dot_general(qm, kcp, NT_DIMS, preferred_element_type=F32)
            m = jnp.maximum(jnp.max(s_loc, axis=-1, keepdims=True), jnp.max(s_ctx, axis=-1, keepdims=True))
            e_loc = jnp.exp(s_loc - m)
            e_ctx = jnp.exp(s_ctx - m)
            den = jnp.sum(e_loc, axis=-1, keepdims=True) + jnp.sum(e_ctx, axis=-1, keepdims=True)
            o = (jnp.dot(e_loc.astype(BF16), vp, preferred_element_type=F32)
                 + jnp.dot(e_ctx.astype(BF16), vcp, preferred_element_type=F32))
            outs.append(o / den)
        o_ref[0, :, cols] = jnp.where(head_of_lane == 0, outs[0], outs[1])


def _na_bias_table(rpb):
    kw = NA_WIN_COLS
    n_col_off = 2 * kw - 1
    j = np.arange(GRID_W)
    col_start = np.clip(j - kw // 2, 0, GRID_W - kw)
    col_in = (j[None, :] >= col_start[:, None]) & (j[None, :] < col_start[:, None] + kw)
    col_off = np.clip(j[None, :] - j[:, None], -(kw - 1), kw - 1) + (kw - 1)
    pick = (col_off.reshape(1, -1) == np.arange(n_col_off)[:, None]).astype(np.float32)
    toep = jnp.dot(rpb.astype(F32).reshape(-1, n_col_off), pick, precision=lax.Precision.HIGHEST)
    toep = toep.reshape(NA_HEADS, 2 * NA_WIN_ROWS - 1, GRID_W, GRID_W)
    toep = jnp.where(col_in[None, None], toep, NEG_INF)
    tab = jnp.stack([toep[:, NA_WIN_ROWS - 1 - d:2 * NA_WIN_ROWS - 1 - d] for d in range(NA_WIN_ROWS)], 0)
    tab = tab.transpose(0, 1, 3, 2, 4).reshape(NA_WIN_ROWS, NA_HEADS, GRID_W, NA_BAND)
    return jnp.concatenate([tab, jnp.full((1,) + tab.shape[1:], NEG_INF, F32)], 0)


def attention_pallas(q, k, v, rpb, dims):
    B, C, N = dims
    S = C + N
    W = NA_WIDTH
    n_rows = N // GRID_W
    ctx_blocks = C // GRID_W
    assert n_rows >= NA_WIN_ROWS and N % GRID_W == 0 and C % GRID_W == 0
    bias = _na_bias_table(rpb)
    as3 = lambda z: z.reshape(B, S, W)

    def bias_idx(b, j):
        r, row_start = _na_row_start(j, ctx_blocks, n_rows)
        return (jnp.where(j < ctx_blocks, NA_WIN_ROWS, r - row_start), 0, 0, 0)

    out = pl.pallas_call(
        functools.partial(_na_kernel, n_ctx=C),
        grid=(B, S // GRID_W),
        in_specs=[
            pl.BlockSpec((1, GRID_W, W), lambda b, j: (b, j, 0)),
            pl.BlockSpec((1, S, W), lambda b, j: (b, 0, 0)),
            pl.BlockSpec((1, S, W), lambda b, j: (b, 0, 0)),
            pl.BlockSpec((1, NA_HEADS, GRID_W, NA_BAND), bias_idx),
        ],
        out_specs=pl.BlockSpec((1, GRID_W, W), lambda b, j: (b, j, 0)),
        out_shape=jax.ShapeDtypeStruct((B, S, W), F32),
        compiler_params=_cparams(2),
        name="na_attention",
    )(as3(q), as3(k), as3(v), bias)
    return out.reshape(B * S, W)


def _layer_norm(x, g, b):
    mu = jnp.mean(x, axis=-1, keepdims=True)
    xc = x - mu
    var = jnp.mean(xc * xc, axis=-1, keepdims=True)
    return xc * lax.rsqrt(var + LN_EPS) * g + b


def _mixer_tail(h, y, gate, ln_g, ln_b, shift, scale, router, h_out_ref, f_ref, s_ref):
    h1 = _layer_norm(DEEPNORM_ALPHA * h + gate * y, ln_g, ln_b)
    f = h1 * (1.0 + scale) + shift
    h_out_ref[...] = h1
    f_ref[...] = f
    s_ref[...] = jax.nn.sigmoid(jnp.dot(f, router, preferred_element_type=F32, precision=lax.Precision.HIGHEST))


def _even_out_kernel(na_ref, yf_ref, yb_ref, r_ref, v_ref, kdf_ref, kdb_ref, glow_ref, h_ref,
                     gate_ref, shift_ref, scale_ref, ones_ref, gng_ref, gnb_ref, rk_ref, gup_ref, wout_ref,
                     lng_ref, lnb_ref, router_ref, h_out_ref, f_ref, s_ref):
    ones2 = ones_ref[...]
    inv = 1.0 / RW_HEAD_DIM
    y = yf_ref[...] + yb_ref[...]
    mu = _seg_sum(y, ones2) * inv
    yc = y - mu
    var = _seg_sum(yc * yc, ones2) * inv
    yn = yc * lax.rsqrt(var + RW_GN_EPS) * gng_ref[...] + gnb_ref[...]
    r = r_ref[...]
    bonus = (_seg_sum(r * kdf_ref[...] * rk_ref[...], ones2) + _seg_sum(r * kdb_ref[...] * rk_ref[...], ones2))
    gate = jnp.dot(jax.nn.sigmoid(glow_ref[...]).astype(BF16), gup_ref[...], preferred_element_type=F32)
    rw = (yn + bonus * v_ref[...]) * gate
    mix = jnp.concatenate([na_ref[...], rw], axis=-1).astype(BF16)
    y_mix = jnp.dot(mix, wout_ref[...], preferred_element_type=F32)
    _mixer_tail(h_ref[...], y_mix, gate_ref[...], lng_ref[...], lnb_ref[...], shift_ref[...], scale_ref[...],
                router_ref[...], h_out_ref, f_ref, s_ref)


def _tail_specs(R, D, E, tm):
    row = lambda width: pl.BlockSpec((tm, width), lambda i: (i, 0))
    return ([row(D), row(D), row(E)],
            [jax.ShapeDtypeStruct((R, D), F32), jax.ShapeDtypeStruct((R, D), F32), jax.ShapeDtypeStruct((R, E), F32)])


def even_out_pallas(na, y_f, y_b, r, v, kd_f, kd_b, glow, h, mods, dims, g_up, r_k, gn_g, gn_b, w_out,
                    ln_g, ln_b, router_w):
    B, C, N = dims
    R, D = h.shape
    E = router_w.shape[1]
    tm = ROW_TILE
    mod_idx = _mod_index((C + N) // tm, C // tm, B)
    row = lambda width: pl.BlockSpec((tm, width), lambda i: (i, 0))
    ones2 = _block_ones(2 * RW_WIDTH, RW_WIDTH, RW_HEAD_DIM)
    g_up_p = jnp.pad(g_up, ((0, LANES - g_up.shape[0]), (0, 0))).astype(BF16)
    vec = lambda z: z.reshape(1, -1)
    out_specs, out_shape = _tail_specs(R, D, E, tm)
    return pl.pallas_call(
        _even_out_kernel,
        grid=(R // tm,),
        in_specs=[row(NA_WIDTH)] + [row(RW_WIDTH)] * 6 + [row(LANES), row(D),
                  _mod_spec(2, mod_idx), _mod_spec(3, mod_idx), _mod_spec(4, mod_idx),
                  _full_spec((2 * RW_WIDTH, RW_WIDTH)),
                  _full_spec((1, RW_WIDTH)), _full_spec((1, RW_WIDTH)), _full_spec((1, RW_WIDTH)),
                  _full_spec((LANES, RW_WIDTH)), _full_spec((D, D)),
                  _full_spec((1, D)), _full_spec((1, D)), _full_spec((D, E))],
        out_specs=out_specs,
        out_shape=out_shape,
        compiler_params=_cparams(1),
        name="even_out",
    )(na, y_f, y_b, r, v, kd_f, kd_b, glow, h, mods, mods, mods, ones2, vec(gn_g), vec(gn_b), vec(r_k),
      g_up_p, w_out.astype(BF16), vec(ln_g), vec(ln_b), router_w)


def _resid_tail_kernel(y_ref, h_ref, gate_ref, shift_ref, scale_ref, lng_ref, lnb_ref, router_ref,
                       h_out_ref, f_ref, s_ref):
    _mixer_tail(h_ref[...], y_ref[...], gate_ref[...], lng_ref[...], lnb_ref[...], shift_ref[...],
                scale_ref[...], router_ref[...], h_out_ref, f_ref, s_ref)


def resid_tail_pallas(y, h, mods, dims, ln_g, ln_b, router_w):
    B, C, N = dims
    R, D = h.shape
    E = router_w.shape[1]
    tm = ROW_TILE
    mod_idx = _mod_index((C + N) // tm, C // tm, B)
    row = lambda width: pl.BlockSpec((tm, width), lambda i: (i, 0))
    vec = lambda z: z.reshape(1, -1)
    out_specs, out_shape = _tail_specs(R, D, E, tm)
    return pl.pallas_call(
        _resid_tail_kernel,
        grid=(R // tm,),
        in_specs=[row(D), row(D), _mod_spec(2, mod_idx), _mod_spec(3, mod_idx), _mod_spec(4, mod_idx),
                  _full_spec((1, D)), _full_spec((1, D)), _full_spec((D, E))],
        out_specs=out_specs,
        out_shape=out_shape,
        compiler_params=_cparams(1),
        name="resid_tail",
    )(y, h, mods, mods, mods, vec(ln_g), vec(ln_b), router_w)


MOE_TOKEN_TILE = 256


def _swiglu_bf16(x, wg, wu, wd):
    g = jnp.dot(x, wg, preferred_element_type=F32)
    u = jnp.dot(x, wu, preferred_element_type=F32)
    mid = (g * jax.nn.sigmoid(g) * u).astype(BF16)
    return jnp.dot(mid, wd, preferred_element_type=F32)


def _row_copy(src_ref, src_row, dst_ref, dst_row, sem):
    return pltpu.make_async_copy(src_ref.at[pl.ds(src_row, 1), :], dst_ref.at[pl.ds(dst_row, 1), :], sem)


def _slot(e_ref, rank_ref, starts_ref, i):
    return starts_ref[e_ref[i]] + rank_ref[i]


def _dispatch_kernel(e_ref, rank_ref, starts_ref, f_ref, xs_ref, sem):
    n_tok = f_ref.shape[0]

    def issue(t, carry):
        for k in range(TOP_K):
            _row_copy(f_ref, t, xs_ref, _slot(e_ref, rank_ref, starts_ref, t * TOP_K + k), sem).start()
        return carry

    lax.fori_loop(0, n_tok, issue, 0)

    def drain(t, carry):
        for k in range(TOP_K):
            _row_copy(f_ref, 0, xs_ref, 0, sem).wait()
        return carry

    lax.fori_loop(0, n_tok, drain, 0)


def _slot_specs(tm):
    flat = pl.BlockSpec((tm * TOP_K,), lambda i: (i,), memory_space=pltpu.SMEM)
    return [flat, flat, pl.BlockSpec(memory_space=pltpu.SMEM)]


def moe_dispatch_pallas(f, e_flat, rank_flat, starts):
    T, D = f.shape
    tm = MOE_TOKEN_TILE
    assert T % tm == 0
    return pl.pallas_call(
        _dispatch_kernel,
        grid=(T // tm,),
        in_specs=_slot_specs(tm) + [pl.BlockSpec((tm, D), lambda i: (i, 0))],
        out_specs=pl.BlockSpec(memory_space=pl.ANY),
        out_shape=jax.ShapeDtypeStruct((T * TOP_K, D), F32),
        scratch_shapes=[pltpu.SemaphoreType.DMA(())],
        compiler_params=_cparams(1),
        name="moe_dispatch",
    )(e_flat, rank_flat, starts, f)


def _expert_item_kernel(blk_ref, e_ref, lo_ref, hi_ref, first_ref, x_ref, wg_ref, wu_ref, wd_ref, o_ref,
                        wg16_ref, wu16_ref, wd16_ref):
    i = pl.program_id(0)
    lo, hi = lo_ref[i], hi_ref[i]

    @pl.when((i == 0) | (e_ref[i] != e_ref[jnp.maximum(i - 1, 0)]))
    def _():
        wg16_ref[...] = wg_ref[0, 0].astype(BF16)
        wu16_ref[...] = wu_ref[0, 0].astype(BF16)
        wd16_ref[...] = wd_ref[0, 0].astype(BF16)

    @pl.when(hi > lo)
    def _():
        y = _swiglu_bf16(x_ref[...].astype(BF16), wg16_ref[...], wu16_ref[...], wd16_ref[...])
        rows = blk_ref[i] * MOE_BLOCK + lax.broadcasted_iota(jnp.int32, (MOE_BLOCK, 1), 0)
        y = jnp.where((rows >= lo) & (rows < hi), y, 0.0)

        @pl.when(first_ref[i] == 1)
        def _():
            o_ref[...] = y

        @pl.when(first_ref[i] == 0)
        def _():
            o_ref[...] += y


def moe_experts_pallas(xs, items, layer, wg, wu, wd):
    n_rows, D = xs.shape
    F = wg.shape[-1]
    n_items = items[0].shape[0]
    grid_spec = pltpu.PrefetchScalarGridSpec(
        num_scalar_prefetch=5,
        grid=(n_items,),
        in_specs=[
            pl.BlockSpec((MOE_BLOCK, D), lambda i, blk, e, lo, hi, first: (blk[i], 0)),
            pl.BlockSpec((1, 1, D, F), lambda i, blk, e, lo, hi, first: (layer, e[i], 0, 0)),
            pl.BlockSpec((1, 1, D, F), lambda i, blk, e, lo, hi, first: (layer, e[i], 0, 0)),
            pl.BlockSpec((1, 1, F, D), lambda i, blk, e, lo, hi, first: (layer, e[i], 0, 0)),
        ],
        out_specs=pl.BlockSpec((MOE_BLOCK, D), lambda i, blk, e, lo, hi, first: (blk[i], 0)),
        scratch_shapes=[pltpu.VMEM((D, F), BF16), pltpu.VMEM((D, F), BF16), pltpu.VMEM((F, D), BF16)],
    )
    return pl.pallas_call(
        _expert_item_kernel,
        grid_spec=grid_spec,
        out_shape=jax.ShapeDtypeStruct((n_rows, D), F32),
        compiler_params=_cparams(1),
        name="moe_experts",
    )(*items, xs, wg, wu, wd)


def _combine_kernel(e_ref, rank_ref, starts_ref, w_ref, f_ref, h_ref, gate_ref, lng_ref, lnb_ref,
                    sg_ref, su_ref, sd_ref, ys_ref, o_ref, buf_ref, sem):
    n_tok = f_ref.shape[0]

    def issue(t, carry):
        for k in range(TOP_K):
            pltpu.make_async_copy(ys_ref.at[pl.ds(_slot(e_ref, rank_ref, starts_ref, t * TOP_K + k), 1), :],
                                  buf_ref.at[k, pl.ds(t, 1), :], sem).start()
        return carry

    lax.fori_loop(0, n_tok, issue, 0)
    acc = _swiglu_bf16(f_ref[...].astype(BF16), sg_ref[...], su_ref[...], sd_ref[...])

    def drain(t, carry):
        for k in range(TOP_K):
            pltpu.make_async_copy(ys_ref.at[pl.ds(0, 1), :], buf_ref.at[0, pl.ds(0, 1), :], sem).wait()
        return carry

    lax.fori_loop(0, n_tok, drain, 0)
    w = w_ref[...]
    for k in range(TOP_K):
        acc = acc + w[:, k:k + 1] * buf_ref[k]
    o_ref[...] = _layer_norm(DEEPNORM_ALPHA * h_ref[...] + gate_ref[...] * acc, lng_ref[...], lnb_ref[...])


def moe_combine_pallas(ys, e_flat, rank_flat, starts, w_sel, f, h, mods, dims, ln_g, ln_b, sg, su, sd):
    B, C, N = dims
    T, D = f.shape
    tm = MOE_TOKEN_TILE
    F = sg.shape[-1]
    mod_idx = _mod_index((C + N) // tm, C // tm, B)
    vec = lambda z: z.reshape(1, -1)
    return pl.pallas_call(
        _combine_kernel,
        grid=(T // tm,),
        in_specs=_slot_specs(tm) + [
            pl.BlockSpec((tm, TOP_K), lambda i: (i, 0)),
            pl.BlockSpec((tm, D), lambda i: (i, 0)),
            pl.BlockSpec((tm, D), lambda i: (i, 0)),
            _mod_spec(5, mod_idx), _full_spec((1, D)), _full_spec((1, D)),
            _full_spec((D, F)), _full_spec((D, F)), _full_spec((F, D)),
            pl.BlockSpec(memory_space=pl.ANY),
        ],
        out_specs=pl.BlockSpec((tm, D), lambda i: (i, 0)),
        out_shape=jax.ShapeDtypeStruct((T, D), F32),
        scratch_shapes=[pltpu.VMEM((TOP_K, tm, D), F32), pltpu.SemaphoreType.DMA(())],
        compiler_params=_cparams(1),
        name="moe_combine",
    )(e_flat, rank_flat, starts, w_sel, f, h, mods, vec(ln_g), vec(ln_b), sg, su, sd, ys)


REMOVED = -3e38


def _router_kernel(s_ref, bias_ref, e_ref, w_ref, rank_ref, cnt_ref, carry_ref):
    @pl.when(pl.program_id(0) == 0)
    def _():
        carry_ref[...] = jnp.zeros_like(carry_ref)

    s = s_ref[...]
    tm, n_exp = s.shape
    per_group = n_exp // N_GROUPS
    lane_i = lax.broadcasted_iota(jnp.int32, (tm, n_exp), 1)
    lane = lane_i.astype(F32)
    group_of_lane = lane_i // per_group
    big = float(n_exp)
    rmax = lambda z: jnp.max(z, axis=-1, keepdims=True)
    first_at = lambda z, m: jnp.min(jnp.where(z == m, lane, big), axis=-1, keepdims=True)

    grp = s + bias_ref[...]
    g_score = []
    for g in range(N_GROUPS):
        mg = jnp.where(group_of_lane == g, grp, REMOVED)
        m1 = rmax(mg)
        m2 = rmax(jnp.where(lane == first_at(mg, m1), REMOVED, mg))
        g_score.append(m1 + m2)
    choice = jnp.full_like(grp, NEG_INF)
    for g in range(N_GROUPS):
        ahead = jnp.zeros((tm, 1), F32)
        for g2 in range(N_GROUPS):
            if g2 != g:
                beats = (g_score[g2] > g_score[g]) | ((g_score[g2] == g_score[g]) & (g2 < g))
                ahead = ahead + beats.astype(F32)
        choice = jnp.where((group_of_lane == g) & (ahead < TOPK_GROUPS), grp, choice)

    col8 = lax.broadcasted_iota(jnp.int32, (tm, TOP_K), 1)
    e_out = jnp.zeros((tm, TOP_K), F32)
    w_out = jnp.zeros((tm, TOP_K), F32)
    picked = []
    onehot = jnp.zeros((tm, n_exp), F32)
    for k in range(TOP_K):
        idx = first_at(choice, rmax(choice))
        hit = lane == idx
        picked.append(hit)
        onehot = jnp.where(hit, 1.0, onehot)
        e_out = jnp.where(col8 == k, idx, e_out)
        w_out = jnp.where(col8 == k, jnp.sum(jnp.where(hit, s, 0.0), axis=-1, keepdims=True), w_out)
        choice = jnp.where(hit, REMOVED, choice)
    ri = lax.broadcasted_iota(jnp.int32, (tm, tm), 0)
    ci = lax.broadcasted_iota(jnp.int32, (tm, tm), 1)
    before = jnp.dot((ci < ri).astype(BF16), onehot.astype(BF16), preferred_element_type=F32) + carry_ref[0:1, :]
    rank = jnp.zeros((tm, TOP_K), F32)
    for k in range(TOP_K):
        rank = jnp.where(col8 == k, jnp.sum(jnp.where(picked[k], before, 0.0), axis=-1, keepdims=True), rank)
    total = carry_ref[0:1, :] + jnp.sum(onehot, axis=0, keepdims=True)
    carry_ref[...] = jnp.broadcast_to(total, carry_ref.shape)
    cnt_ref[...] = jnp.broadcast_to(total, cnt_ref.shape)
    e_ref[...] = e_out.astype(jnp.int32)
    w_ref[...] = w_out / jnp.sum(w_out, axis=-1, keepdims=True) * ROUTED_SCALE
    rank_ref[...] = rank.astype(jnp.int32)


def router_pallas(s, router_b):
    T, E = s.shape
    tm = ROW_TILE
    row8 = pl.BlockSpec((tm, TOP_K), lambda i: (i, 0))
    e_idx, w_sel, rank, cnt = pl.pallas_call(
        _router_kernel,
        grid=(T // tm,),
        in_specs=[pl.BlockSpec((tm, E), lambda i: (i, 0)), _full_spec((1, E))],
        out_specs=[row8, row8, row8, _full_spec((SUBLANES, E))],
        out_shape=[jax.ShapeDtypeStruct((T, TOP_K), jnp.int32), jax.ShapeDtypeStruct((T, TOP_K), F32),
                   jax.ShapeDtypeStruct((T, TOP_K), jnp.int32), jax.ShapeDtypeStruct((SUBLANES, E), F32)],
        scratch_shapes=[pltpu.VMEM((SUBLANES, E), F32)],
        compiler_params=_cparams(1),
        name="moe_router",
    )(s, router_b.astype(F32).reshape(1, E))
    return e_idx, w_sel, rank, cnt[0].astype(jnp.int32)


def moe_layer(f, s, h, mods, dims, ln_g, ln_b, router_b, layer, wg, wu, wd, sg, su, sd):
    T, D = f.shape
    E = s.shape[-1]
    e_idx, w_sel, rank, counts = router_pallas(s, router_b)
    n_asg = T * TOP_K
    assert n_asg % MOE_BLOCK == 0
    i32 = jnp.int32
    ends = jnp.cumsum(counts).astype(i32)
    starts = ends - counts
    e_flat, rank_flat = e_idx.reshape(-1), rank.reshape(-1)
    nb = n_asg // MOE_BLOCK
    first_blk = starts // MOE_BLOCK
    nblk = jnp.where(counts > 0, (ends - 1) // MOE_BLOCK - first_blk + 1, 0)
    item_ends = jnp.cumsum(nblk).astype(i32)
    item_starts = item_ends - nblk
    n_items = nb + E
    it = jnp.arange(n_items, dtype=i32)
    real = it < item_ends[-1]
    e_of = jnp.minimum(jnp.searchsorted(item_ends, it, side='right'), E - 1).astype(i32)
    e_of = jnp.where(real, e_of, e_of[item_ends[-1] - 1])
    blk = jnp.where(real, first_blk[e_of] + it - item_starts[e_of], nb - 1).astype(i32)
    lo = jnp.where(real, jnp.maximum(starts[e_of], blk * MOE_BLOCK), 0).astype(i32)
    hi = jnp.where(real, jnp.minimum(ends[e_of], (blk + 1) * MOE_BLOCK), 0).astype(i32)
    first = (real & (blk != jnp.concatenate([jnp.full((1,), -1, i32), blk[:-1]]))).astype(i32)
    xs = moe_dispatch_pallas(f, e_flat, rank_flat, starts)
    ys = moe_experts_pallas(xs, (blk, e_of, lo, hi, first), layer, wg, wu, wd)
    return moe_combine_pallas(ys, e_flat, rank_flat, starts, w_sel, f, h, mods, dims, ln_g, ln_b,
                              sg.astype(BF16), su.astype(BF16), sd.astype(BF16))


ML_QK_WIDTH = ML_HEADS * ML_QK_DIM
ROPE_GROUP = ML_QK_DIM // 4
GATE_IN, GATE_FORGET = 0, 2 * ML_HEADS


def _log_sigmoid(x):
    return -_softplus(-x)


def _proj_odd_kernel(h_ref, shift_ref, scale_ref, wqk_ref, wv_ref, wo_ref, wg_ref, wgt_ref, gb_ref, gbt_ref,
                     cos_ref, sin_ref, q_ref, k_ref, v_ref, o_ref, g_ref, gt_ref):
    a16 = (h_ref[...] * (1.0 + scale_ref[...]) + shift_ref[...]).astype(BF16)
    qk = jnp.dot(a16, wqk_ref[...], preferred_element_type=F32)
    lane = lax.broadcasted_iota(jnp.int32, (1, ML_QK_WIDTH), 1)
    first_of_pair = (lane % (2 * ROPE_GROUP)) < ROPE_GROUP
    cos, sin = cos_ref[...], sin_ref[...]

    def rope(z):
        partner = jnp.where(first_of_pair, pltpu.roll(z, ML_QK_WIDTH - ROPE_GROUP, 1), pltpu.roll(z, ROPE_GROUP, 1))
        return z * cos + partner * sin

    q_ref[...] = rope(qk[:, :ML_QK_WIDTH] * ML_QK_DIM ** -0.5).astype(BF16)
    k_ref[...] = rope(qk[:, ML_QK_WIDTH:]).astype(BF16)
    v_ref[...] = jnp.dot(a16, wv_ref[...], preferred_element_type=F32).astype(BF16)
    o_ref[...] = jnp.dot(a16, wo_ref[...], preferred_element_type=F32)
    g = jnp.dot(a16, wg_ref[...], preferred_element_type=F32) + gb_ref[...]
    gl = lax.broadcasted_iota(jnp.int32, g.shape, 1)
    g_ref[...] = jnp.where((gl >= GATE_FORGET) & (gl < 2 * GATE_FORGET), _log_sigmoid(g), g)
    gt = lax.dot_general(wgt_ref[...], a16, NT_DIMS, preferred_element_type=F32) + gbt_ref[...]
    gs = lax.broadcasted_iota(jnp.int32, gt.shape, 0)
    gt_ref[...] = jnp.where((gs >= GATE_FORGET) & (gs < 2 * GATE_FORGET), _log_sigmoid(gt), gt)


def _rope_tables(C, N):
    t = jnp.arange(N)
    pos = jnp.stack([(t // GRID_W).astype(F32), (t % GRID_W).astype(F32)], 0)
    lane = jnp.arange(ML_QK_WIDTH) % ML_QK_DIM
    inv = ROPE_BASE ** (-(lane % ROPE_GROUP).astype(F32) / ROPE_GROUP)
    ang = pos[lane // (2 * ROPE_GROUP)].T * inv[None, :]
    sign = jnp.where((lane % (2 * ROPE_GROUP)) < ROPE_GROUP, -1.0, 1.0)
    cos = jnp.concatenate([jnp.ones((C, ML_QK_WIDTH), F32), jnp.cos(ang)], 0)
    sin = jnp.concatenate([jnp.zeros((C, ML_QK_WIDTH), F32), jnp.sin(ang) * sign], 0)
    return cos, sin


def proj_odd_pallas(h, mods, dims, w_in, gate_b):
    B, C, N = dims
    R, D = h.shape
    tm = ROW_TILE
    tpb = (C + N) // tm
    mod_idx = _mod_index(tpb, C // tm, B)
    o_qk, o_v, o_o = 2 * ML_QK_WIDTH, 2 * ML_QK_WIDTH + ML_WIDTH, 2 * ML_QK_WIDTH + 2 * ML_WIDTH
    n_gate = w_in.shape[1] - o_o
    w16 = w_in.astype(BF16)
    w_g = jnp.pad(w16[:, o_o:], ((0, 0), (0, LANES - n_gate)))
    gb = jnp.pad(gate_b.astype(F32).reshape(-1), (0, LANES - n_gate))
    cos, sin = _rope_tables(C, N)
    row = lambda width: pl.BlockSpec((tm, width), lambda i: (i, 0))
    seg = pl.BlockSpec((tm, ML_QK_WIDTH), lambda i: (i % tpb, 0))
    return pl.pallas_call(
        _proj_odd_kernel,
        grid=(R // tm,),
        in_specs=[row(D), _mod_spec(0, mod_idx), _mod_spec(1, mod_idx),
                  _full_spec((D, 2 * ML_QK_WIDTH)), _full_spec((D, ML_WIDTH)), _full_spec((D, ML_WIDTH)),
                  _full_spec((D, LANES)), _full_spec((LANES, D)), _full_spec((1, LANES)), _full_spec((LANES, 1)),
                  seg, seg],
        out_specs=[row(ML_QK_WIDTH), row(ML_QK_WIDTH), row(ML_WIDTH), row(ML_WIDTH), row(LANES),
                   pl.BlockSpec((LANES, tm), lambda i: (0, i))],
        out_shape=[jax.ShapeDtypeStruct((R, ML_QK_WIDTH), BF16), jax.ShapeDtypeStruct((R, ML_QK_WIDTH), BF16),
                   jax.ShapeDtypeStruct((R, ML_WIDTH), BF16), jax.ShapeDtypeStruct((R, ML_WIDTH), F32),
                   jax.ShapeDtypeStruct((R, LANES), F32), jax.ShapeDtypeStruct((LANES, R), F32)],
        compiler_params=_cparams(1),
        name="proj_odd",
    )(h, mods, mods, w16[:, :o_qk], w16[:, o_qk:o_v], w16[:, o_v:o_o], w_g, w_g.T, gb.reshape(1, LANES),
      gb.reshape(LANES, 1), cos, sin)


def _split3_bf16(x, axis):
    x1 = x.astype(BF16)
    r1 = x - x1.astype(F32)
    x2 = r1.astype(BF16)
    x3 = (r1 - x2.astype(F32)).astype(BF16)
    return jnp.concatenate([x1, x2, x3], axis=axis)


def _mlstm_kernel(q_ref, k_ref, v_ref, g_ref, gt_ref, h_ref, c_ref, n_ref, m_ref, *, reverse):
    @pl.when(pl.program_id(1) == 0)
    def _():
        c_ref[...] = jnp.zeros_like(c_ref)
        n_ref[...] = jnp.zeros_like(n_ref)
        m_ref[...] = jnp.zeros_like(m_ref)

    L = q_ref.shape[1]
    ti = lax.broadcasted_iota(jnp.int32, (L, L), 0)
    si = lax.broadcasted_iota(jnp.int32, (L, L), 1)
    seen = (si >= ti) if reverse else (si <= ti)
    g = g_ref[0]
    gt = gt_ref[...]
    b_cols3 = jnp.dot(seen.astype(BF16), _split3_bf16(g, 1), preferred_element_type=F32)
    b_cols = b_cols3[:, :LANES] + b_cols3[:, LANES:2 * LANES] + b_cols3[:, 2 * LANES:]
    b_rows3 = lax.dot_general(_split3_bf16(gt, 0), seen.astype(BF16), NT_DIMS, preferred_element_type=F32)
    b_rows = b_rows3[:LANES] + b_rows3[LANES:2 * LANES] + b_rows3[2 * LANES:]
    half = lax.broadcasted_iota(jnp.int32, (1, LANES), 1) // ML_QK_DIM
    row_half = lax.broadcasted_iota(jnp.int32, (LANES, 1), 0) // ML_QK_DIM
    d_off = ML_HEADS if reverse else 0
    tn = (((0,), (0,)), ((), ()))
    for p in range(ML_HEADS // 2):
        qp = q_ref[0, :, p * LANES:(p + 1) * LANES]
        kp = k_ref[0, :, p * LANES:(p + 1) * LANES]
        c_old = c_ref[p]
        n_old = n_ref[p]
        m_old = m_ref[p]
        c16 = c_old.astype(BF16)
        c_new, n_new, m_new_pair = c_old, n_old, m_old
        for h2 in range(2):
            hd = 2 * p + h2
            vh = v_ref[0, :, hd * ML_V_DIM:(hd + 1) * ML_V_DIM]
            gi, gf = GATE_IN + d_off + hd, GATE_FORGET + d_off + hd
            ig_col, ig_row = g[:, gi:gi + 1], gt[gi:gi + 1, :]
            b_col, b_row = b_cols[:, gf:gf + 1], b_rows[gf:gf + 1, :]
            m0 = m_old[:, h2 * ML_QK_DIM:h2 * ML_QK_DIM + 1]
            qm = jnp.where(half == h2, qp, jnp.zeros_like(qp))
            km = jnp.where(half == h2, kp, jnp.zeros_like(kp))
            dlog = jnp.where(seen, b_col - b_row + ig_row, NEG_INF)
            inter = b_col + m0
            m_t = jnp.maximum(jnp.max(dlog, axis=-1, keepdims=True), inter)
            dw = jnp.exp(dlog - m_t)
            iw = jnp.exp(inter - m_t)
            sc = lax.dot_general(qm, kp, NT_DIMS, preferred_element_type=F32) * dw
            num = (jnp.dot(sc.astype(BF16), vh, preferred_element_type=F32)
                   + iw * jnp.dot(qm, c16, preferred_element_type=F32))
            den = (jnp.sum(sc, axis=-1, keepdims=True)
                   + iw * jnp.sum(qm.astype(F32) * n_old, axis=-1, keepdims=True))
            h_ref[0, :, hd * ML_V_DIM:(hd + 1) * ML_V_DIM] = num / jnp.maximum(jnp.abs(den), jnp.exp(-m_t))
            b_end = jnp.sum(g[:, gf:gf + 1], axis=0, keepdims=True)
            g_col = b_end - b_col + ig_col
            m_chunk = jnp.max(g_col, axis=0, keepdims=True)
            kw = km.astype(F32) * jnp.exp(g_col - m_chunk)
            m_new = jnp.maximum(b_end + m0, m_chunk)
            fa = jnp.exp(b_end + m0 - m_new)
            fb = jnp.exp(m_chunk - m_new)
            kv = lax.dot_general(kw.astype(BF16), vh, tn, preferred_element_type=F32)
            c_new = jnp.where(row_half == h2, fa * c_old + fb * kv, c_new)
            n_new = jnp.where(half == h2, fa * n_old + fb * jnp.sum(kw, axis=0, keepdims=True), n_new)
            m_new_pair = jnp.where(half == h2, m_new, m_new_pair)
        c_ref[p] = c_new
        n_ref[p] = n_new
        m_ref[p] = m_new_pair


def mlstm_pallas(q, k, v, g, gt, dims, reverse):
    B, C, N = dims
    S = C + N
    L = ML_CHUNK
    assert C % L == 0 and N % L == 0
    n_ctx, n_all = C // L, S // L
    if reverse:
        chunk = lambda j: jnp.where(j < n_ctx, n_ctx - 1 - j, n_all - 1 - (j - n_ctx))
    else:
        chunk = lambda j: j
    blk = lambda width: pl.BlockSpec((1, L, width), lambda b, j: (b, chunk(j), 0))
    n_pair = ML_HEADS // 2
    out = pl.pallas_call(
        functools.partial(_mlstm_kernel, reverse=reverse),
        grid=(B, n_all),
        in_specs=[blk(ML_QK_WIDTH), blk(ML_QK_WIDTH), blk(ML_WIDTH), blk(LANES),
                  pl.BlockSpec((LANES, L), lambda b, j: (0, b * n_all + chunk(j)))],
        out_specs=blk(ML_WIDTH),
        out_shape=jax.ShapeDtypeStruct((B, S, ML_WIDTH), F32),
        scratch_shapes=[pltpu.VMEM((n_pair, LANES, ML_V_DIM), F32), pltpu.VMEM((n_pair, 1, LANES), F32),
                        pltpu.VMEM((n_pair, 1, LANES), F32)],
        compiler_params=_cparams(2),
        name="mlstm_bwd" if reverse else "mlstm_fwd",
    )(q.reshape(B, S, -1), k.reshape(B, S, -1), v.reshape(B, S, -1), g.reshape(B, S, -1), gt)
    return out.reshape(B * S, ML_WIDTH)


def _odd_out_kernel(hf_ref, hb_ref, o_ref, h_ref, gate_ref, shift_ref, scale_ref, ng_ref, wout_ref,
                    lng_ref, lnb_ref, router_ref, h_out_ref, f_ref, s_ref):
    hs = hf_ref[...] + hb_ref[...]
    parts = []
    for hd in range(ML_HEADS):
        x = hs[:, hd * ML_V_DIM:(hd + 1) * ML_V_DIM]
        parts.append(x * lax.rsqrt(jnp.mean(x * x, axis=-1, keepdims=True) + ML_NORM_EPS))
    hn = jnp.concatenate(parts, axis=-1) * ng_ref[...] * jax.nn.sigmoid(o_ref[...])
    y = jnp.dot(hn.astype(BF16), wout_ref[...], preferred_element_type=F32)
    _mixer_tail(h_ref[...], y, gate_ref[...], lng_ref[...], lnb_ref[...], shift_ref[...], scale_ref[...],
                router_ref[...], h_out_ref, f_ref, s_ref)


def odd_out_pallas(h_f, h_b, o, h, mods, dims, norm_g, w_out, ln_g, ln_b, router_w):
    B, C, N = dims
    R, D = h.shape
    E = router_w.shape[1]
    tm = ROW_TILE
    mod_idx = _mod_index((C + N) // tm, C // tm, B)
    row = lambda width: pl.BlockSpec((tm, width), lambda i: (i, 0))
    vec = lambda z: z.reshape(1, -1)
    out_specs, out_shape = _tail_specs(R, D, E, tm)
    return pl.pallas_call(
        _odd_out_kernel,
        grid=(R // tm,),
        in_specs=[row(ML_WIDTH), row(ML_WIDTH), row(ML_WIDTH), row(D),
                  _mod_spec(2, mod_idx), _mod_spec(3, mod_idx), _mod_spec(4, mod_idx),
                  _full_spec((1, ML_WIDTH)), _full_spec((ML_WIDTH, D)),
                  _full_spec((1, D)), _full_spec((1, D)), _full_spec((D, E))],
        out_specs=out_specs,
        out_shape=out_shape,
        compiler_params=_cparams(1),
        name="odd_out",
    )(h_f, h_b, o, h, mods, mods, mods, vec(norm_g), w_out.astype(BF16), vec(ln_g), vec(ln_b), router_w)


def _offsets(layout, prefix=''):
    offs, o = {}, 0
    for name, width in layout:
        if name.startswith(prefix):
            offs[name] = (o, width)
            o += width
    return offs


def project(h, w, layout, names):
    offs = _offsets(layout)
    if len(names) == len(layout):
        y = jnp.einsum('btd,de->bte', h, w)
        return {n: y[..., offs[n][0]:offs[n][0] + offs[n][1]] for n in names}
    return {n: jnp.einsum('btd,de->bte', h, w[:, offs[n][0]:offs[n][0] + offs[n][1]]) for n in names}


def axial_rope(z):
    T, dh = z.shape[1], z.shape[-1]
    half = dh // 2
    nf = half // 2
    t = jnp.arange(T)
    row = (t // GRID_W).astype(F32)
    col = (t % GRID_W).astype(F32)
    inv = ROPE_BASE ** (-jnp.arange(nf, dtype=F32) / nf)

    def rot(u, pos):
        ang = pos[:, None] * inv[None, :]
        cos = jnp.cos(ang)[None, :, None, :]
        sin = jnp.sin(ang)[None, :, None, :]
        u1, u2 = u[..., :nf], u[..., nf:]
        return jnp.concatenate([u1 * cos - u2 * sin, u1 * sin + u2 * cos], -1)

    return jnp.concatenate([rot(z[..., :half], row), rot(z[..., half:], col)], -1).astype(z.dtype)


def ml_prep(t, gate_b, rope, need_q):
    B, T = t['ml_k'].shape[:2]
    heads = lambda z, dh: z.reshape(B, T, ML_HEADS, dh).astype(F32)
    k = heads(t['ml_k'], ML_QK_DIM)
    q = heads(t['ml_q'], ML_QK_DIM) * ML_QK_DIM ** -0.5 if need_q else None
    if rope:
        k = axial_rope(k)
        q = axial_rope(q)
    v = heads(t['ml_v'], ML_V_DIM)
    gb = gate_b.astype(F32)
    bht = lambda z: z.astype(F32).transpose(0, 2, 1)
    ig = (bht(t['ml_if'] + gb[0]), bht(t['ml_ib'] + gb[1]))
    lf = (jax.nn.log_sigmoid(bht(t['ml_ff'] + gb[2])), jax.nn.log_sigmoid(bht(t['ml_fb'] + gb[3])))
    bhtd = lambda z: None if z is None else z.transpose(0, 2, 1, 3)
    return bhtd(q), bhtd(k), bhtd(v), ig, lf


def ml_chunk_states(k, v, ig, lf, state0):
    B, H, T, dk = k.shape
    dv = v.shape[-1]
    L = min(ML_CHUNK, T)
    nc = T // L
    kc = k.reshape(B, H, nc, L, dk)
    vc = v.reshape(B, H, nc, L, dv)
    b = jnp.cumsum(lf.reshape(B, H, nc, L), -1)
    b_end = b[..., -1]
    g = b_end[..., None] - b + ig.reshape(B, H, nc, L)
    m_chunk = g.max(-1)
    wgt = jnp.exp(g - m_chunk[..., None])
    kv = jnp.einsum('bhnl,bhnlk,bhnlv->bhnkv', wgt, kc, vc)
    ks = jnp.einsum('bhnl,bhnlk->bhnk', wgt, kc)

    def step(state, inp):
        c_mem, n_mem, m = state
        be, mc, kv_n, ks_n = inp
        m_new = jnp.maximum(be + m, mc)
        fa = jnp.exp(be + m - m_new)
        fb = jnp.exp(mc - m_new)
        c_new = fa[..., None, None] * c_mem + fb[..., None, None] * kv_n
        n_new = fa[..., None] * n_mem + fb[..., None] * ks_n
        return (c_new, n_new, m_new), state

    xs = tuple(jnp.moveaxis(z, 2, 0) for z in (b_end, m_chunk, kv, ks))
    final, starts = lax.scan(step, state0, xs)
    return tuple(jnp.moveaxis(z, 0, 2) for z in starts), final


def ml_chunk_outputs(q, k, v, ig, lf, starts):
    B, H, T, dk = q.shape
    dv = v.shape[-1]
    L = min(ML_CHUNK, T)
    nc = T // L
    qc = q.reshape(B, H, nc, L, dk)
    kc = k.reshape(B, H, nc, L, dk)
    vc = v.reshape(B, H, nc, L, dv)
    b = jnp.cumsum(lf.reshape(B, H, nc, L), -1)
    c0, n0, m0 = starts
    causal = jnp.tril(jnp.ones((L, L), bool))
    dlog = jnp.where(causal, b[..., :, None] - b[..., None, :] + ig.reshape(B, H, nc, L)[..., None, :], NEG_INF)
    inter = b + m0[..., None]
    m = jnp.maximum(dlog.max(-1), inter)
    dw = jnp.exp(dlog - m[..., None])
    iw = jnp.exp(inter - m)
    s = jnp.einsum('bhntd,bhnsd->bhnts', qc, kc) * dw
    num = jnp.einsum('bhnts,bhnsv->bhntv', s, vc) + iw[..., None] * jnp.einsum('bhntd,bhndv->bhntv', qc, c0)
    den = s.sum(-1) + iw * jnp.einsum('bhntd,bhnd->bhnt', qc, n0)
    h = num / jnp.maximum(jnp.abs(den), jnp.exp(-m))[..., None]
    return h.reshape(B, H, T, dv)


def ml_readout(h, o, norm_g):
    B, H, T, dv = h.shape
    hn = h * lax.rsqrt(jnp.mean(h * h, -1, keepdims=True) + ML_NORM_EPS)
    hn = hn.transpose(0, 2, 1, 3).reshape(B, T, H * dv) * norm_g
    return hn * jax.nn.sigmoid(o.astype(F32))


def odd_mixer(a_lat, a_ctx, w_in, w_out, gate_b, norm_g, need_ctx):
    names = tuple(n for n, _ in ODD_LAYOUT)
    t_lat = project(a_lat, w_in, ODD_LAYOUT, names)
    t_ctx = project(a_ctx, w_in, ODD_LAYOUT, names if need_ctx else ODD_CTX_STATE_COLS)
    q_l, k_l, v_l, ig_l, lf_l = ml_prep(t_lat, gate_b, True, True)
    q_c, k_c, v_c, ig_c, lf_c = ml_prep(t_ctx, gate_b, False, need_ctx)
    B = a_lat.shape[0]
    zero = (jnp.zeros((B, ML_HEADS, ML_QK_DIM, ML_V_DIM), F32),
            jnp.zeros((B, ML_HEADS, ML_QK_DIM), F32),
            jnp.zeros((B, ML_HEADS), F32))
    h_l, h_c = [], []
    for d in range(2):
        f = (lambda z: jnp.flip(z, 2)) if d == 1 else (lambda z: z)
        starts_c, final_c = ml_chunk_states(f(k_c), f(v_c), f(ig_c[d]), f(lf_c[d]), zero)
        starts_l, _ = ml_chunk_states(f(k_l), f(v_l), f(ig_l[d]), f(lf_l[d]), final_c)
        h_l.append(f(ml_chunk_outputs(f(q_l), f(k_l), f(v_l), f(ig_l[d]), f(lf_l[d]), starts_l)))
        if need_ctx:
            h_c.append(f(ml_chunk_outputs(f(q_c), f(k_c), f(v_c), f(ig_c[d]), f(lf_c[d]), starts_c)))
    y_lat = jnp.einsum('btd,de->bte', ml_readout(h_l[0] + h_l[1], t_lat['ml_o'], norm_g), w_out).astype(a_lat.dtype)
    if not need_ctx:
        return y_lat, None
    y_ctx = jnp.einsum('btd,de->bte', ml_readout(h_c[0] + h_c[1], t_ctx['ml_o'], norm_g), w_out).astype(a_ctx.dtype)
    return y_lat, y_ctx


def kernel(x, c, ctx, c_ctx, ada_w, ada_b, ln_g, ln_b, ev_w_in, ev_w_out, na_rpb, rw_mu, rw_w0, rw_w_up,
           rw_a0, rw_a_up, rw_g_up, rw_k_k, rw_k_a, rw_r_k, rw_gn_g, rw_gn_b, od_w_in, od_w_out, ml_gate_b,
           ml_norm_g, moe_router, moe_bias, moe_w_gate, moe_w_up, moe_w_down, sh_w_gate, sh_w_up, sh_w_down):
    B, N, D = x.shape
    C = ctx.shape[1]
    S = C + N
    dims = (B, C, N)
    assert C % ROW_TILE == 0 and N % ROW_TILE == 0 and B + 1 <= SUBLANES
    h = jnp.concatenate([ctx, x], axis=1).reshape(B * S, D)
    cond = jnp.zeros((SUBLANES, D), F32).at[:B].set(c).at[B].set(c_ctx)
    for l in range(DEPTH):
        mods = ada_mods_pallas(cond, ada_w[l], ada_b[l])
        if l % 2 == 0:
            e = l // 2
            (q, k, v, dec_f, dec_b, beta_f, beta_b, kd_f, kd_b, nkk, rv, rr, glow) = proj_even_pallas(
                h, mods, dims, ev_w_in[e], rw_mu[e], rw_w0[e], rw_w_up[e], rw_a0[e], rw_a_up[e],
                rw_k_k[e], rw_k_a[e])
            y_f, y_b = rwkv_scan_pallas(dec_f, beta_f, kd_f, dec_b, beta_b, kd_b, nkk, rv, rr, dims)
            na = attention_pallas(q, k, v, na_rpb[e], dims)
            h, f, s = even_out_pallas(na, y_f, y_b, rr, rv, kd_f, kd_b, glow, h, mods, dims, rw_g_up[e],
                                      rw_r_k[e], rw_gn_g[e], rw_gn_b[e], ev_w_out[e], ln_g[l, 0], ln_b[l, 0],
                                      moe_router[l])
        else:
            o = l // 2
            q, k, v, og, g, gt = proj_odd_pallas(h, mods, dims, od_w_in[o], ml_gate_b[o])
            h_f = mlstm_pallas(q, k, v, g, gt, dims, False)
            h_b = mlstm_pallas(q, k, v, g, gt, dims, True)
            h, f, s = odd_out_pallas(h_f, h_b, og, h, mods, dims, ml_norm_g[o], od_w_out[o], ln_g[l, 0],
                                     ln_b[l, 0], moe_router[l])
        h = moe_layer(f, s, h, mods, dims, ln_g[l, 1], ln_b[l, 1], moe_bias[l], l, moe_w_gate, moe_w_up,
                      moe_w_down, sh_w_gate[l], sh_w_up[l], sh_w_down[l])
    return h.reshape(B, S, D)[:, C:]
```

```python
import functools

import jax
import jax.numpy as jnp
import numpy as np
from jax import lax
from jax.experimental import pallas as pl
from jax.experimental.pallas import tpu as pltpu

D_MODEL = 1024
DEPTH = 2
GRID_W = 64

DEEPNORM_ALPHA = (2.0 * DEPTH) ** 0.25
LN_EPS = 1e-5
NEG_INF = -1e30
F32 = jnp.float32
BF16 = jnp.bfloat16

NA_HEAD_DIM = 64
NA_WIDTH = D_MODEL // 2
NA_HEADS = NA_WIDTH // NA_HEAD_DIM
NA_WIN_ROWS = 8
NA_WIN_COLS = 16
NA_SCALE = NA_HEAD_DIM ** -0.5

RW_HEAD_DIM = 64
RW_WIDTH = D_MODEL // 2
RW_HEADS = RW_WIDTH // RW_HEAD_DIM
RW_DECAY_LORA = 32
RW_AAA_LORA = 32
RW_GATE_LORA = 96
RW_GN_EPS = 64e-5

ML_HEADS = 8
ML_V_DIM = D_MODEL // ML_HEADS
ML_QK_DIM = ML_V_DIM // 2
ML_WIDTH = ML_HEADS * ML_V_DIM
ML_CHUNK = 128
ML_NORM_EPS = 1e-6
ROPE_BASE = 10000.0

N_EXPERTS = 256
TOP_K = 8
N_GROUPS = 8
TOPK_GROUPS = 4
ROUTED_SCALE = 2.5
MOE_BLOCK = 256

ODD_LAYOUT = (
    ('ml_q', ML_HEADS * ML_QK_DIM), ('ml_k', ML_HEADS * ML_QK_DIM),
    ('ml_v', ML_WIDTH), ('ml_o', ML_WIDTH),
    ('ml_if', ML_HEADS), ('ml_ib', ML_HEADS), ('ml_ff', ML_HEADS), ('ml_fb', ML_HEADS),
)
ODD_CTX_STATE_COLS = ('ml_k', 'ml_v', 'ml_if', 'ml_ib', 'ml_ff', 'ml_fb')

SUBLANES = 8
LANES = 128
VMEM_LIMIT_BYTES = 56 * 1024 * 1024

ROW_TILE = 256
N_MODS = 6
RW_COLS = 3 * RW_WIDTH + 2 * LANES
NT_DIMS = (((1,), (1,)), ((), ()))


def _cparams(n_axes):
    return pltpu.CompilerParams(dimension_semantics=("arbitrary",) * n_axes, vmem_limit_bytes=VMEM_LIMIT_BYTES)


def _full_spec(shape):
    return pl.BlockSpec(shape, lambda *_: (0,) * len(shape))


def _split_bf16(x):
    hi = x.astype(BF16)
    lo = (x - hi.astype(F32)).astype(BF16)
    return jnp.concatenate([hi, lo], axis=-1)


def _block_ones(n_rows, n_cols, seg):
    row = lax.broadcasted_iota(jnp.int32, (n_rows, n_cols), 0)
    col = lax.broadcasted_iota(jnp.int32, (n_rows, n_cols), 1)
    return (((row % n_cols) // seg) == (col // seg)).astype(BF16)


def _seg_sum(x, ones2):
    return jnp.dot(_split_bf16(x), ones2, preferred_element_type=F32)


def _mod_index(tiles_per_batch, ctx_tiles, n_batch):
    def idx(i):
        return jnp.where(i % tiles_per_batch < ctx_tiles, n_batch, i // tiles_per_batch)
    return idx


def _mod_spec(chunk, mod_idx):
    return pl.BlockSpec((None, None, 1, D_MODEL), lambda i: (mod_idx(i), chunk, 0, 0))


def _ada_kernel(c_ref, w_ref, b_ref, o_ref):
    c = c_ref[...]
    x = (c * jax.nn.sigmoid(c)).astype(BF16)
    o_ref[...] = jnp.dot(x, w_ref[...].astype(BF16), preferred_element_type=F32) + b_ref[...]


def ada_mods_pallas(cond, w, b):
    n, D = cond.shape
    n_out = w.shape[1]
    tn = 512
    out = pl.pallas_call(
        _ada_kernel,
        grid=(n_out // tn,),
        in_specs=[_full_spec((n, D)), pl.BlockSpec((D, tn), lambda j: (0, j)), pl.BlockSpec((1, tn), lambda j: (0, j))],
        out_specs=pl.BlockSpec((n, tn), lambda j: (0, j)),
        out_shape=jax.ShapeDtypeStruct((n, n_out), F32),
        compiler_params=_cparams(1),
        name="ada_mods",
    )(cond, w, b.reshape(1, n_out))
    return out.reshape(n, N_MODS, 1, D)


def _softplus(x):
    return jnp.maximum(x, 0.0) + jnp.log(1.0 + jnp.exp(-jnp.abs(x)))


def _proj_even_kernel(h_ref, hp_ref, hn_ref, shift_ref, scale_ref, wna_ref, wrw_ref, mu_ref, ones_ref,
                      kk_ref, ka_ref, w0_ref, a0_ref, wup_ref, aup_ref,
                      q_ref, k_ref, v_ref, dec_f_ref, dec_b_ref, beta_f_ref, beta_b_ref, kd_f_ref, kd_b_ref,
                      nkk_ref, rv_ref, rr_ref, glow_ref, *, tiles_per_batch, ctx_tiles):
    i = pl.program_id(0)
    j = i % tiles_per_batch
    first = (j == 0) | (j == ctx_tiles)
    last = (j == ctx_tiles - 1) | (j == tiles_per_batch - 1)
    tm = h_ref.shape[0]
    gain = 1.0 + scale_ref[...]
    shift = shift_ref[...]
    a = h_ref[...] * gain + shift
    a_prev = jnp.where(first, 0.0, hp_ref[SUBLANES - 1:SUBLANES, :] * gain + shift)
    a_next = jnp.where(last, 0.0, hn_ref[0:1, :] * gain + shift)
    rid = lax.broadcasted_iota(jnp.int32, (tm, 1), 0)
    prev = jnp.where(rid == 0, a_prev, pltpu.roll(a, 1, 0))
    nxt = jnp.where(rid == tm - 1, a_next, pltpu.roll(a, tm - 1, 0))
    a16 = a.astype(BF16)
    nb16 = (0.5 * (prev + nxt)).astype(BF16)

    na = jnp.dot(a16, wna_ref[...], preferred_element_type=F32)
    q_ref[...] = (na[:, :NA_WIDTH] * NA_SCALE).astype(BF16)
    k_ref[...] = na[:, NA_WIDTH:2 * NA_WIDTH].astype(BF16)
    v_ref[...] = na[:, 2 * NA_WIDTH:].astype(BF16)

    pa = jnp.dot(a16, wrw_ref[...], preferred_element_type=F32)
    pn = jnp.dot(nb16, wrw_ref[...], preferred_element_type=F32)
    t = pa + mu_ref[...] * (pn - pa)
    r = t[:, :RW_WIDTH]
    k = t[:, RW_WIDTH:2 * RW_WIDTH]
    lora = t[:, 3 * RW_WIDTH:3 * RW_WIDTH + LANES]
    rr_ref[...] = r
    rv_ref[...] = t[:, 2 * RW_WIDTH:3 * RW_WIDTH]
    glow_ref[...] = t[:, 3 * RW_WIDTH + LANES:]

    kk = k * kk_ref[...]
    norm = jnp.sqrt(_seg_sum(kk * kk, ones_ref[...]))
    kk = kk / jnp.maximum(norm, 1e-12)
    nkk_ref[...] = -kk
    lora_t = jnp.tanh(lora).astype(BF16)
    lora16 = lora.astype(BF16)
    outs = ((dec_f_ref, beta_f_ref, kd_f_ref), (dec_b_ref, beta_b_ref, kd_b_ref))
    for d in range(2):
        w_log = -_softplus(-(w0_ref[d:d + 1, :] + jnp.dot(lora_t, wup_ref[d], preferred_element_type=F32))) - 0.5
        a_gate = jax.nn.sigmoid(a0_ref[d:d + 1, :] + jnp.dot(lora16, aup_ref[d], preferred_element_type=F32))
        outs[d][0][...] = jnp.exp(-jnp.exp(w_log))
        outs[d][1][...] = kk * a_gate
        outs[d][2][...] = k * (1.0 + (a_gate - 1.0) * ka_ref[...])


def proj_even_pallas(h, mods, dims, w_in, mu, w0, w_up, a0, a_up, k_k, k_a):
    B, C, N = dims
    R, D = h.shape
    tm = ROW_TILE
    tpb, ctx_tiles = (C + N) // tm, C // tm
    mod_idx = _mod_index(tpb, ctx_tiles, B)
    w_na = w_in[:, :3 * NA_WIDTH].astype(BF16)
    n_rw = w_in.shape[1] - 3 * NA_WIDTH
    w_rw = jnp.pad(w_in[:, 3 * NA_WIDTH:], ((0, 0), (0, RW_COLS - n_rw))).astype(BF16)
    mu_p = jnp.pad(mu, (0, RW_COLS - n_rw)).reshape(1, RW_COLS)
    ones2 = _block_ones(2 * RW_WIDTH, RW_WIDTH, RW_HEAD_DIM)
    lr = RW_DECAY_LORA

    def pad_up(m, first_row):
        out = jnp.zeros((2, LANES, RW_WIDTH), F32)
        for d in range(2):
            out = out.at[d, first_row + d * lr:first_row + (d + 1) * lr].set(m[d])
        return out.astype(BF16)

    row = lambda width: pl.BlockSpec((tm, width), lambda i: (i, 0))
    hb = tm // SUBLANES
    n_hb = R // SUBLANES
    wide = jax.ShapeDtypeStruct((R, RW_WIDTH), F32)
    half = jax.ShapeDtypeStruct((R, NA_WIDTH), BF16)
    return pl.pallas_call(
        functools.partial(_proj_even_kernel, tiles_per_batch=tpb, ctx_tiles=ctx_tiles),
        grid=(R // tm,),
        in_specs=[
            row(D),
            pl.BlockSpec((SUBLANES, D), lambda i: (jnp.maximum(i * hb - 1, 0), 0)),
            pl.BlockSpec((SUBLANES, D), lambda i: (jnp.minimum((i + 1) * hb, n_hb - 1), 0)),
            _mod_spec(0, mod_idx), _mod_spec(1, mod_idx),
            _full_spec((D, 3 * NA_WIDTH)), _full_spec((D, RW_COLS)), _full_spec((1, RW_COLS)),
            _full_spec((2 * RW_WIDTH, RW_WIDTH)),
            _full_spec((1, RW_WIDTH)), _full_spec((1, RW_WIDTH)),
            _full_spec((2, RW_WIDTH)), _full_spec((2, RW_WIDTH)),
            _full_spec((2, LANES, RW_WIDTH)), _full_spec((2, LANES, RW_WIDTH)),
        ],
        out_specs=[row(NA_WIDTH)] * 3 + [row(RW_WIDTH)] * 9 + [row(LANES)],
        out_shape=[half] * 3 + [wide] * 9 + [jax.ShapeDtypeStruct((R, LANES), F32)],
        compiler_params=_cparams(1),
        name="proj_even",
    )(h, h, h, mods, mods, w_na, w_rw, mu_p, ones2, k_k.reshape(1, -1), k_a.reshape(1, -1), w0, a0,
      pad_up(w_up, 0), pad_up(a_up, 2 * lr))


RW_SCAN_TIME = 256


def _rwkv_scan_kernel(wf_ref, bf_ref, kf_ref, nf_ref, vf_ref, rf_ref,
                      wb_ref, bb_ref, kb_ref, nb_ref, vb_ref, rb_ref, yf_ref, yb_ref, s_ref):
    @pl.when(pl.program_id(0) == 0)
    def _():
        s_ref[...] = jnp.zeros_like(s_ref)

    n_batch, n_time, width = wf_ref.shape
    n_pair = width // LANES
    n_dir_chain = n_batch * n_pair
    n_chain = 2 * n_dir_chain
    rows_all = n_chain * RW_HEAD_DIM
    ones = _block_ones(LANES, LANES, RW_HEAD_DIM)
    vi = lax.broadcasted_iota(jnp.int32, (1, RW_HEAD_DIM, LANES), 1)
    li = lax.broadcasted_iota(jnp.int32, (1, RW_HEAD_DIM, LANES), 2)
    diag = (li % RW_HEAD_DIM) == vi
    n_sub = n_time // SUBLANES

    def seg(x):
        out = jnp.dot(x.reshape(rows_all, LANES).astype(BF16), ones, preferred_element_type=F32)
        return out.reshape(n_chain, RW_HEAD_DIM, LANES)

    def chains(ref, rows):
        x = ref[:, rows, :]
        return [x[b, :, p * LANES:(p + 1) * LANES] for b in range(n_batch) for p in range(n_pair)]

    def sub(i, carry):
        rows_f = pl.ds(pl.multiple_of(i * SUBLANES, SUBLANES), SUBLANES)
        rows_b = pl.ds(pl.multiple_of((n_sub - 1 - i) * SUBLANES, SUBLANES), SUBLANES)
        load = lambda f_ref, b_ref: (jnp.stack(chains(f_ref, rows_f)), jnp.stack(chains(b_ref, rows_b)))
        w8, beta8, kd8 = load(wf_ref, wb_ref), load(bf_ref, bb_ref), load(kf_ref, kb_ref)
        nkk8, v8, r8 = load(nf_ref, nb_ref), load(vf_ref, vb_ref), load(rf_ref, rb_ref)

        def at(pair, t):
            tb = SUBLANES - 1 - t
            return jnp.concatenate([pair[0][:, t:t + 1, :], pair[1][:, tb:tb + 1, :]], axis=0)

        s = s_ref[...]
        rows = []
        for t in range(SUBLANES):
            vcol = seg(jnp.where(diag, at(v8, t), 0.0))
            sa = seg(s * at(nkk8, t))
            s = s * at(w8, t) + sa * at(beta8, t) + vcol * at(kd8, t)
            ybc = seg(s * at(r8, t))
            rows.append(jnp.sum(jnp.where(diag, ybc, 0.0), axis=1, keepdims=True))
        s_ref[...] = s
        y_f = jnp.concatenate([row[:n_dir_chain] for row in rows], axis=1)
        y_b = jnp.concatenate([row[n_dir_chain:] for row in rows[::-1]], axis=1)
        for b in range(n_batch):
            for p in range(n_pair):
                c = b * n_pair + p
                yf_ref[b, rows_f, p * LANES:(p + 1) * LANES] = y_f[c]
                yb_ref[b, rows_b, p * LANES:(p + 1) * LANES] = y_b[c]
        return carry

    lax.fori_loop(0, n_sub, sub, 0)


def rwkv_scan_pallas(dec_f, beta_f, kd_f, dec_b, beta_b, kd_b, nkk, v, r, dims):
    B, C, N = dims
    S = C + N
    tc = RW_SCAN_TIME
    assert C % tc == 0 and N % tc == 0
    n_ctx, n_all = C // tc, S // tc
    as3 = lambda z: z.reshape(B, S, RW_WIDTH)
    fwd = pl.BlockSpec((B, tc, RW_WIDTH), lambda j: (0, j, 0))
    bwd = pl.BlockSpec((B, tc, RW_WIDTH),
                       lambda j: (0, jnp.where(j < n_ctx, n_ctx - 1 - j, n_all - 1 - (j - n_ctx)), 0))
    out = jax.ShapeDtypeStruct((B, S, RW_WIDTH), F32)
    y_f, y_b = pl.pallas_call(
        _rwkv_scan_kernel,
        grid=(n_all,),
        in_specs=[fwd] * 6 + [bwd] * 6,
        out_specs=[fwd, bwd],
        out_shape=[out, out],
        scratch_shapes=[pltpu.VMEM((2 * B * (RW_WIDTH // LANES), RW_HEAD_DIM, LANES), F32)],
        compiler_params=_cparams(1),
        name="rwkv_scan",
    )(as3(dec_f), as3(beta_f), as3(kd_f), as3(nkk), as3(v), as3(r),
      as3(dec_b), as3(beta_b), as3(kd_b), as3(nkk), as3(v), as3(r))
    return y_f.reshape(B * S, RW_WIDTH), y_b.reshape(B * S, RW_WIDTH)


NA_BAND = NA_WIN_ROWS * GRID_W


def _na_row_start(j, ctx_blocks, n_rows):
    r = jnp.maximum(j - ctx_blocks, 0)
    return r, jnp.clip(r - NA_WIN_ROWS // 2, 0, n_rows - NA_WIN_ROWS)


def _na_kernel(q_ref, k_ref, v_ref, bias_ref, o_ref, *, n_ctx):
    j = pl.program_id(1)
    ctx_blocks = n_ctx // GRID_W
    n_rows = pl.num_programs(1) - ctx_blocks
    _, row_start = _na_row_start(j, ctx_blocks, n_rows)
    start = pl.multiple_of(n_ctx + row_start * GRID_W, GRID_W)
    q = q_ref[0]
    kb = k_ref[0, pl.ds(start, NA_BAND), :]
    vb = v_ref[0, pl.ds(start, NA_BAND), :]
    kc = k_ref[0, pl.ds(0, n_ctx), :]
    vc = v_ref[0, pl.ds(0, n_ctx), :]
    head_of_lane = lax.broadcasted_iota(jnp.int32, (GRID_W, LANES), 1) // NA_HEAD_DIM
    heads = [(p, h2) for p in range(NA_WIDTH // LANES) for h2 in range(LANES // NA_HEAD_DIM)]
    cols = lambda p: slice(p * LANES, (p + 1) * LANES)
    scores = []
    for p, h2 in heads:
        qm = jnp.where(head_of_lane == h2, q[:, cols(p)], jnp.zeros((GRID_W, LANES), BF16))
        s_loc = lax.dot_general(qm, kb[:, cols(p)], NT_DIMS, preferred_element_type=F32)
        s_ctx = lax.dot_general(qm, kc[:, cols(p)], NT_DIMS, preferred_element_type=F32)
        scores.append((s_loc + bias_ref[0, 2 * p + h2], s_ctx))
    probs = []
    for s_loc, s_ctx in scores:
        m = jnp.maximum(jnp.max(s_loc, axis=-1, keepdims=True), jnp.max(s_ctx, axis=-1, keepdims=True))
        e_loc = jnp.exp(s_loc - m)
        e_ctx = jnp.exp(s_ctx - m)
        den = jnp.sum(e_loc, axis=-1, keepdims=True) + jnp.sum(e_ctx, axis=-1, keepdims=True)
        probs.append((e_loc.astype(BF16), e_ctx.astype(BF16), den))
    outs = []
    for (p, h2), (e_loc, e_ctx, den) in zip(heads, probs):
        o = (jnp.dot(e_loc, vb[:, cols(p)], preferred_element_type=F32)
             + jnp.dot(e_ctx, vc[:, cols(p)], preferred_element_type=F32))
        outs.append(o / den)
    for p in range(NA_WIDTH // LANES):
        o_ref[0, :, cols(p)] = jnp.where(head_of_lane == 0, outs[2 * p], outs[2 * p + 1])


def _na_bias_table(rpb):
    kw = NA_WIN_COLS
    n_col_off = 2 * kw - 1
    j = np.arange(GRID_W)
    col_start = np.clip(j - kw // 2, 0, GRID_W - kw)
    col_in = (j[None, :] >= col_start[:, None]) & (j[None, :] < col_start[:, None] + kw)
    col_off = np.clip(j[None, :] - j[:, None], -(kw - 1), kw - 1) + (kw - 1)
    pick = (col_off.reshape(1, -1) == np.arange(n_col_off)[:, None]).astype(np.float32)
    toep = jnp.dot(rpb.astype(F32).reshape(-1, n_col_off), pick, precision=lax.Precision.HIGHEST)
    toep = toep.reshape(NA_HEADS, 2 * NA_WIN_ROWS - 1, GRID_W, GRID_W)
    toep = jnp.where(col_in[None, None], toep, NEG_INF)
    tab = jnp.stack([toep[:, NA_WIN_ROWS - 1 - d:2 * NA_WIN_ROWS - 1 - d] for d in range(NA_WIN_ROWS)], 0)
    tab = tab.transpose(0, 1, 3, 2, 4).reshape(NA_WIN_ROWS, NA_HEADS, GRID_W, NA_BAND)
    return jnp.concatenate([tab, jnp.full((1,) + tab.shape[1:], NEG_INF, F32)], 0)


def attention_pallas(q, k, v, rpb, dims):
    B, C, N = dims
    S = C + N
    W = NA_WIDTH
    n_rows = N // GRID_W
    ctx_blocks = C // GRID_W
    assert n_rows >= NA_WIN_ROWS and N % GRID_W == 0 and C % GRID_W == 0
    bias = _na_bias_table(rpb)
    as3 = lambda z: z.reshape(B, S, W)

    def bias_idx(b, j):
        r, row_start = _na_row_start(j, ctx_blocks, n_rows)
        return (jnp.where(j < ctx_blocks, NA_WIN_ROWS, r - row_start), 0, 0, 0)

    out = pl.pallas_call(
        functools.partial(_na_kernel, n_ctx=C),
        grid=(B, S // GRID_W),
        in_specs=[
            pl.BlockSpec((1, GRID_W, W), lambda b, j: (b, j, 0)),
            pl.BlockSpec((1, S, W), lambda b, j: (b, 0, 0)),
            pl.BlockSpec((1, S, W), lambda b, j: (b, 0, 0)),
            pl.BlockSpec((1, NA_HEADS, GRID_W, NA_BAND), bias_idx),
        ],
        out_specs=pl.BlockSpec((1, GRID_W, W), lambda b, j: (b, j, 0)),
        out_shape=jax.ShapeDtypeStruct((B, S, W), F32),
        compiler_params=_cparams(2),
        name="na_attention",
    )(as3(q), as3(k), as3(v), bias)
    return out.reshape(B * S, W)


def _layer_norm(x, g, b):
    mu = jnp.mean(x, axis=-1, keepdims=True)
    xc = x - mu
    var = jnp.mean(xc * xc, axis=-1, keepdims=True)
    return xc * lax.rsqrt(var + LN_EPS) * g + b


def _mixer_tail(h, y, gate, ln_g, ln_b, shift, scale, router, h_out_ref, f_ref, s_ref):
    h1 = _layer_norm(DEEPNORM_ALPHA * h + gate * y, ln_g, ln_b)
    f = h1 * (1.0 + scale) + shift
    h_out_ref[...] = h1
    f_ref[...] = f
    s_ref[...] = jax.nn.sigmoid(jnp.dot(f, router, preferred_element_type=F32, precision=lax.Precision.HIGHEST))


def _even_out_kernel(na_ref, yf_ref, yb_ref, r_ref, v_ref, kdf_ref, kdb_ref, glow_ref, h_ref,
                     gate_ref, shift_ref, scale_ref, ones_ref, gng_ref, gnb_ref, rk_ref, gup_ref, wout_ref,
                     lng_ref, lnb_ref, router_ref, h_out_ref, f_ref, s_ref):
    ones2 = ones_ref[...]
    inv = 1.0 / RW_HEAD_DIM
    y = yf_ref[...] + yb_ref[...]
    mu = _seg_sum(y, ones2) * inv
    yc = y - mu
    var = _seg_sum(yc * yc, ones2) * inv
    yn = yc * lax.rsqrt(var + RW_GN_EPS) * gng_ref[...] + gnb_ref[...]
    r = r_ref[...]
    bonus = (_seg_sum(r * kdf_ref[...] * rk_ref[...], ones2) + _seg_sum(r * kdb_ref[...] * rk_ref[...], ones2))
    gate = jnp.dot(jax.nn.sigmoid(glow_ref[...]).astype(BF16), gup_ref[...], preferred_element_type=F32)
    rw = (yn + bonus * v_ref[...]) * gate
    mix = jnp.concatenate([na_ref[...], rw], axis=-1).astype(BF16)
    y_mix = jnp.dot(mix, wout_ref[...], preferred_element_type=F32)
    _mixer_tail(h_ref[...], y_mix, gate_ref[...], lng_ref[...], lnb_ref[...], shift_ref[...], scale_ref[...],
                router_ref[...], h_out_ref, f_ref, s_ref)


def _tail_specs(R, D, E, tm):
    row = lambda width: pl.BlockSpec((tm, width), lambda i: (i, 0))
    return ([row(D), row(D), row(E)],
            [jax.ShapeDtypeStruct((R, D), F32), jax.ShapeDtypeStruct((R, D), F32), jax.ShapeDtypeStruct((R, E), F32)])


def even_out_pallas(na, y_f, y_b, r, v, kd_f, kd_b, glow, h, mods, dims, g_up, r_k, gn_g, gn_b, w_out,
                    ln_g, ln_b, router_w):
    B, C, N = dims
    R, D = h.shape
    E = router_w.shape[1]
    tm = ROW_TILE
    mod_idx = _mod_index((C + N) // tm, C // tm, B)
    row = lambda width: pl.BlockSpec((tm, width), lambda i: (i, 0))
    ones2 = _block_ones(2 * RW_WIDTH, RW_WIDTH, RW_HEAD_DIM)
    g_up_p = jnp.pad(g_up, ((0, LANES - g_up.shape[0]), (0, 0))).astype(BF16)
    vec = lambda z: z.reshape(1, -1)
    out_specs, out_shape = _tail_specs(R, D, E, tm)
    return pl.pallas_call(
        _even_out_kernel,
        grid=(R // tm,),
        in_specs=[row(NA_WIDTH)] + [row(RW_WIDTH)] * 6 + [row(LANES), row(D),
                  _mod_spec(2, mod_idx), _mod_spec(3, mod_idx), _mod_spec(4, mod_idx),
                  _full_spec((2 * RW_WIDTH, RW_WIDTH)),
                  _full_spec((1, RW_WIDTH)), _full_spec((1, RW_WIDTH)), _full_spec((1, RW_WIDTH)),
                  _full_spec((LANES, RW_WIDTH)), _full_spec((D, D)),
                  _full_spec((1, D)), _full_spec((1, D)), _full_spec((D, E))],
        out_specs=out_specs,
        out_shape=out_shape,
        compiler_params=_cparams(1),
        name="even_out",
    )(na, y_f, y_b, r, v, kd_f, kd_b, glow, h, mods, mods, mods, ones2, vec(gn_g), vec(gn_b), vec(r_k),
      g_up_p, w_out.astype(BF16), vec(ln_g), vec(ln_b), router_w)


def _resid_tail_kernel(y_ref, h_ref, gate_ref, shift_ref, scale_ref, lng_ref, lnb_ref, router_ref,
                       h_out_ref, f_ref, s_ref):
    _mixer_tail(h_ref[...], y_ref[...], gate_ref[...], lng_ref[...], lnb_ref[...], shift_ref[...],
                scale_ref[...], router_ref[...], h_out_ref, f_ref, s_ref)


def resid_tail_pallas(y, h, mods, dims, ln_g, ln_b, router_w):
    B, C, N = dims
    R, D = h.shape
    E = router_w.shape[1]
    tm = ROW_TILE
    mod_idx = _mod_index((C + N) // tm, C // tm, B)
    row = lambda width: pl.BlockSpec((tm, width), lambda i: (i, 0))
    vec = lambda z: z.reshape(1, -1)
    out_specs, out_shape = _tail_specs(R, D, E, tm)
    return pl.pallas_call(
        _resid_tail_kernel,
        grid=(R // tm,),
        in_specs=[row(D), row(D), _mod_spec(2, mod_idx), _mod_spec(3, mod_idx), _mod_spec(4, mod_idx),
                  _full_spec((1, D)), _full_spec((1, D)), _full_spec((D, E))],
        out_specs=out_specs,
        out_shape=out_shape,
        compiler_params=_cparams(1),
        name="resid_tail",
    )(y, h, mods, mods, mods, vec(ln_g), vec(ln_b), router_w)


MOE_TOKEN_TILE = 256


def _swiglu_bf16(x, wg, wu, wd):
    g = jnp.dot(x, wg, preferred_element_type=F32)
    u = jnp.dot(x, wu, preferred_element_type=F32)
    mid = (g * jax.nn.sigmoid(g) * u).astype(BF16)
    return jnp.dot(mid, wd, preferred_element_type=F32)


def _row_copy(src_ref, src_row, dst_ref, dst_row, sem):
    return pltpu.make_async_copy(src_ref.at[pl.ds(src_row, 1), :], dst_ref.at[pl.ds(dst_row, 1), :], sem)


def _slot(e_ref, rank_ref, starts_ref, i):
    return starts_ref[e_ref[i]] + rank_ref[i]


def _dispatch_kernel(e_ref, rank_ref, starts_ref, f_ref, xs_ref, sem):
    n_tok = f_ref.shape[0]

    def issue(t, carry):
        for k in range(TOP_K):
            _row_copy(f_ref, t, xs_ref, _slot(e_ref, rank_ref, starts_ref, t * TOP_K + k), sem).start()
        return carry

    lax.fori_loop(0, n_tok, issue, 0)

    def drain(t, carry):
        for k in range(TOP_K):
            _row_copy(f_ref, 0, xs_ref, 0, sem).wait()
        return carry

    lax.fori_loop(0, n_tok, drain, 0)


def _slot_specs(tm):
    flat = pl.BlockSpec((tm * TOP_K,), lambda i: (i,), memory_space=pltpu.SMEM)
    return [flat, flat, pl.BlockSpec(memory_space=pltpu.SMEM)]


def moe_dispatch_pallas(f, e_flat, rank_flat, starts):
    T, D = f.shape
    tm = MOE_TOKEN_TILE
    assert T % tm == 0
    return pl.pallas_call(
        _dispatch_kernel,
        grid=(T // tm,),
        in_specs=_slot_specs(tm) + [pl.BlockSpec((tm, D), lambda i: (i, 0))],
        out_specs=pl.BlockSpec(memory_space=pl.ANY),
        out_shape=jax.ShapeDtypeStruct((T * TOP_K, D), F32),
        scratch_shapes=[pltpu.SemaphoreType.DMA(())],
        compiler_params=_cparams(1),
        name="moe_dispatch",
    )(e_flat, rank_flat, starts, f)


def _expert_item_kernel(blk_ref, e_ref, lo_ref, hi_ref, first_ref, x_ref, wg_ref, wu_ref, wd_ref, o_ref,
                        wg16_ref, wu16_ref, wd16_ref):
    i = pl.program_id(0)
    lo, hi = lo_ref[i], hi_ref[i]

    @pl.when((i == 0) | (e_ref[i] != e_ref[jnp.maximum(i - 1, 0)]))
    def _():
        wg16_ref[...] = wg_ref[0, 0].astype(BF16)
        wu16_ref[...] = wu_ref[0, 0].astype(BF16)
        wd16_ref[...] = wd_ref[0, 0].astype(BF16)

    @pl.when(hi > lo)
    def _():
        y = _swiglu_bf16(x_ref[...].astype(BF16), wg16_ref[...], wu16_ref[...], wd16_ref[...])
        rows = blk_ref[i] * MOE_BLOCK + lax.broadcasted_iota(jnp.int32, (MOE_BLOCK, 1), 0)
        y = jnp.where((rows >= lo) & (rows < hi), y, 0.0)

        @pl.when(first_ref[i] == 1)
        def _():
            o_ref[...] = y

        @pl.when(first_ref[i] == 0)
        def _():
            o_ref[...] += y


def moe_experts_pallas(xs, items, layer, wg, wu, wd):
    n_rows, D = xs.shape
    F = wg.shape[-1]
    n_items = items[0].shape[0]
    grid_spec = pltpu.PrefetchScalarGridSpec(
        num_scalar_prefetch=5,
        grid=(n_items,),
        in_specs=[
            pl.BlockSpec((MOE_BLOCK, D), lambda i, blk, e, lo, hi, first: (blk[i], 0)),
            pl.BlockSpec((1, 1, D, F), lambda i, blk, e, lo, hi, first: (layer, e[i], 0, 0)),
            pl.BlockSpec((1, 1, D, F), lambda i, blk, e, lo, hi, first: (layer, e[i], 0, 0)),
            pl.BlockSpec((1, 1, F, D), lambda i, blk, e, lo, hi, first: (layer, e[i], 0, 0)),
        ],
        out_specs=pl.BlockSpec((MOE_BLOCK, D), lambda i, blk, e, lo, hi, first: (blk[i], 0)),
        scratch_shapes=[pltpu.VMEM((D, F), BF16), pltpu.VMEM((D, F), BF16), pltpu.VMEM((F, D), BF16)],
    )
    return pl.pallas_call(
        _expert_item_kernel,
        grid_spec=grid_spec,
        out_shape=jax.ShapeDtypeStruct((n_rows, D), F32),
        compiler_params=_cparams(1),
        name="moe_experts",
    )(*items, xs, wg, wu, wd)


def _combine_kernel(e_ref, rank_ref, starts_ref, w_ref, f_ref, h_ref, gate_ref, lng_ref, lnb_ref,
                    sg_ref, su_ref, sd_ref, ys_ref, o_ref, buf_ref, sem):
    n_tok = f_ref.shape[0]

    def issue(t, carry):
        for k in range(TOP_K):
            pltpu.make_async_copy(ys_ref.at[pl.ds(_slot(e_ref, rank_ref, starts_ref, t * TOP_K + k), 1), :],
                                  buf_ref.at[k, pl.ds(t, 1), :], sem).start()
        return carry

    lax.fori_loop(0, n_tok, issue, 0)
    acc = _swiglu_bf16(f_ref[...].astype(BF16), sg_ref[...], su_ref[...], sd_ref[...])

    def drain(t, carry):
        for k in range(TOP_K):
            pltpu.make_async_copy(ys_ref.at[pl.ds(0, 1), :], buf_ref.at[0, pl.ds(0, 1), :], sem).wait()
        return carry

    lax.fori_loop(0, n_tok, drain, 0)
    w = w_ref[...]
    for k in range(TOP_K):
        acc = acc + w[:, k:k + 1] * buf_ref[k]
    o_ref[...] = _layer_norm(DEEPNORM_ALPHA * h_ref[...] + gate_ref[...] * acc, lng_ref[...], lnb_ref[...])


def moe_combine_pallas(ys, e_flat, rank_flat, starts, w_sel, f, h, mods, dims, ln_g, ln_b, sg, su, sd):
    B, C, N = dims
    T, D = f.shape
    tm = MOE_TOKEN_TILE
    F = sg.shape[-1]
    mod_idx = _mod_index((C + N) // tm, C // tm, B)
    vec = lambda z: z.reshape(1, -1)
    return pl.pallas_call(
        _combine_kernel,
        grid=(T // tm,),
        in_specs=_slot_specs(tm) + [
            pl.BlockSpec((tm, TOP_K), lambda i: (i, 0)),
            pl.BlockSpec((tm, D), lambda i: (i, 0)),
            pl.BlockSpec((tm, D), lambda i: (i, 0)),
            _mod_spec(5, mod_idx), _full_spec((1, D)), _full_spec((1, D)),
            _full_spec((D, F)), _full_spec((D, F)), _full_spec((F, D)),
            pl.BlockSpec(memory_space=pl.ANY),
        ],
        out_specs=pl.BlockSpec((tm, D), lambda i: (i, 0)),
        out_shape=jax.ShapeDtypeStruct((T, D), F32),
        scratch_shapes=[pltpu.VMEM((TOP_K, tm, D), F32), pltpu.SemaphoreType.DMA(())],
        compiler_params=_cparams(1),
        name="moe_combine",
    )(e_flat, rank_flat, starts, w_sel, f, h, mods, vec(ln_g), vec(ln_b), sg, su, sd, ys)


REMOVED = -3e38


def _router_kernel(s_ref, bias_ref, e_ref, w_ref, rank_ref, cnt_ref, carry_ref):
    @pl.when(pl.program_id(0) == 0)
    def _():
        carry_ref[...] = jnp.zeros_like(carry_ref)

    s = s_ref[...]
    tm, n_exp = s.shape
    per_group = n_exp // N_GROUPS
    lane_i = lax.broadcasted_iota(jnp.int32, (tm, n_exp), 1)
    lane = lane_i.astype(F32)
    group_of_lane = lane_i // per_group
    big = float(n_exp)
    rmax = lambda z: jnp.max(z, axis=-1, keepdims=True)
    first_at = lambda z, m: jnp.min(jnp.where(z == m, lane, big), axis=-1, keepdims=True)

    grp = s + bias_ref[...]
    g_score = []
    for g in range(N_GROUPS):
        mg = jnp.where(group_of_lane == g, grp, REMOVED)
        m1 = rmax(mg)
        m2 = rmax(jnp.where(lane == first_at(mg, m1), REMOVED, mg))
        g_score.append(m1 + m2)
    choice = jnp.full_like(grp, NEG_INF)
    for g in range(N_GROUPS):
        ahead = jnp.zeros((tm, 1), F32)
        for g2 in range(N_GROUPS):
            if g2 != g:
                beats = (g_score[g2] > g_score[g]) | ((g_score[g2] == g_score[g]) & (g2 < g))
                ahead = ahead + beats.astype(F32)
        choice = jnp.where((group_of_lane == g) & (ahead < TOPK_GROUPS), grp, choice)

    col8 = lax.broadcasted_iota(jnp.int32, (tm, TOP_K), 1)
    e_out = jnp.zeros((tm, TOP_K), F32)
    w_out = jnp.zeros((tm, TOP_K), F32)
    picked = []
    onehot = jnp.zeros((tm, n_exp), F32)
    for k in range(TOP_K):
        idx = first_at(choice, rmax(choice))
        hit = lane == idx
        picked.append(hit)
        onehot = jnp.where(hit, 1.0, onehot)
        e_out = jnp.where(col8 == k, idx, e_out)
        w_out = jnp.where(col8 == k, jnp.sum(jnp.where(hit, s, 0.0), axis=-1, keepdims=True), w_out)
        choice = jnp.where(hit, REMOVED, choice)
    ri = lax.broadcasted_iota(jnp.int32, (tm, tm), 0)
    ci = lax.broadcasted_iota(jnp.int32, (tm, tm), 1)
    before = jnp.dot((ci < ri).astype(BF16), onehot.astype(BF16), preferred_element_type=F32) + carry_ref[0:1, :]
    rank = jnp.zeros((tm, TOP_K), F32)
    for k in range(TOP_K):
        rank = jnp.where(col8 == k, jnp.sum(jnp.where(picked[k], before, 0.0), axis=-1, keepdims=True), rank)
    total = carry_ref[0:1, :] + jnp.sum(onehot, axis=0, keepdims=True)
    carry_ref[...] = jnp.broadcast_to(total, carry_ref.shape)
    cnt_ref[...] = jnp.broadcast_to(total, cnt_ref.shape)
    e_ref[...] = e_out.astype(jnp.int32)
    w_ref[...] = w_out / jnp.sum(w_out, axis=-1, keepdims=True) * ROUTED_SCALE
    rank_ref[...] = rank.astype(jnp.int32)


def router_pallas(s, router_b):
    T, E = s.shape
    tm = ROW_TILE
    row8 = pl.BlockSpec((tm, TOP_K), lambda i: (i, 0))
    e_idx, w_sel, rank, cnt = pl.pallas_call(
        _router_kernel,
        grid=(T // tm,),
        in_specs=[pl.BlockSpec((tm, E), lambda i: (i, 0)), _full_spec((1, E))],
        out_specs=[row8, row8, row8, _full_spec((SUBLANES, E))],
        out_shape=[jax.ShapeDtypeStruct((T, TOP_K), jnp.int32), jax.ShapeDtypeStruct((T, TOP_K), F32),
                   jax.ShapeDtypeStruct((T, TOP_K), jnp.int32), jax.ShapeDtypeStruct((SUBLANES, E), F32)],
        scratch_shapes=[pltpu.VMEM((SUBLANES, E), F32)],
        compiler_params=_cparams(1),
        name="moe_router",
    )(s, router_b.astype(F32).reshape(1, E))
    return e_idx, w_sel, rank, cnt[0].astype(jnp.int32)


def moe_layer(f, s, h, mods, dims, ln_g, ln_b, router_b, layer, wg, wu, wd, sg, su, sd):
    T, D = f.shape
    E = s.shape[-1]
    e_idx, w_sel, rank, counts = router_pallas(s, router_b)
    n_asg = T * TOP_K
    assert n_asg % MOE_BLOCK == 0
    i32 = jnp.int32
    ends = jnp.cumsum(counts).astype(i32)
    starts = ends - counts
    e_flat, rank_flat = e_idx.reshape(-1), rank.reshape(-1)
    nb = n_asg // MOE_BLOCK
    first_blk = starts // MOE_BLOCK
    nblk = jnp.where(counts > 0, (ends - 1) // MOE_BLOCK - first_blk + 1, 0)
    item_ends = jnp.cumsum(nblk).astype(i32)
    item_starts = item_ends - nblk
    n_items = nb + E
    it = jnp.arange(n_items, dtype=i32)
    real = it < item_ends[-1]
    e_of = jnp.sum((item_ends[None, :] <= jnp.where(real, it, item_ends[-1] - 1)[:, None]).astype(i32), axis=1)
    is_e = e_of[:, None] == jnp.arange(E, dtype=i32)[None, :]
    pick = lambda tab: jnp.sum(jnp.where(is_e, tab[None, :], 0), axis=1)
    blk = jnp.where(real, pick(first_blk) + it - pick(item_starts), nb - 1).astype(i32)
    lo = jnp.where(real, jnp.maximum(pick(starts), blk * MOE_BLOCK), 0).astype(i32)
    hi = jnp.where(real, jnp.minimum(pick(ends), (blk + 1) * MOE_BLOCK), 0).astype(i32)
    first = (real & (blk != jnp.concatenate([jnp.full((1,), -1, i32), blk[:-1]]))).astype(i32)
    xs = moe_dispatch_pallas(f, e_flat, rank_flat, starts)
    ys = moe_experts_pallas(xs, (blk, e_of, lo, hi, first), layer, wg, wu, wd)
    return moe_combine_pallas(ys, e_flat, rank_flat, starts, w_sel, f, h, mods, dims, ln_g, ln_b,
                              sg.astype(BF16), su.astype(BF16), sd.astype(BF16))


ML_QK_WIDTH = ML_HEADS * ML_QK_DIM
ROPE_GROUP = ML_QK_DIM // 4
GATE_IN, GATE_FORGET = 0, 2 * ML_HEADS


def _log_sigmoid(x):
    return -_softplus(-x)


def _proj_odd_kernel(h_ref, shift_ref, scale_ref, wqk_ref, wv_ref, wo_ref, wg_ref, wgt_ref, gb_ref, gbt_ref,
                     cos_ref, sin_ref, q_ref, k_ref, v_ref, o_ref, g_ref, gt_ref):
    a16 = (h_ref[...] * (1.0 + scale_ref[...]) + shift_ref[...]).astype(BF16)
    qk = jnp.dot(a16, wqk_ref[...], preferred_element_type=F32)
    lane = lax.broadcasted_iota(jnp.int32, (1, ML_QK_WIDTH), 1)
    first_of_pair = (lane % (2 * ROPE_GROUP)) < ROPE_GROUP
    cos, sin = cos_ref[...], sin_ref[...]

    def rope(z):
        partner = jnp.where(first_of_pair, pltpu.roll(z, ML_QK_WIDTH - ROPE_GROUP, 1), pltpu.roll(z, ROPE_GROUP, 1))
        return z * cos + partner * sin

    q_ref[...] = rope(qk[:, :ML_QK_WIDTH] * ML_QK_DIM ** -0.5).astype(BF16)
    k_ref[...] = rope(qk[:, ML_QK_WIDTH:]).astype(BF16)
    v_ref[...] = jnp.dot(a16, wv_ref[...], preferred_element_type=F32).astype(BF16)
    o_ref[...] = jnp.dot(a16, wo_ref[...], preferred_element_type=F32)
    g = jnp.dot(a16, wg_ref[...], preferred_element_type=F32) + gb_ref[...]
    gl = lax.broadcasted_iota(jnp.int32, g.shape, 1)
    g_ref[...] = jnp.where((gl >= GATE_FORGET) & (gl < 2 * GATE_FORGET), _log_sigmoid(g), g)
    gt = lax.dot_general(wgt_ref[...], a16, NT_DIMS, preferred_element_type=F32) + gbt_ref[...]
    gs = lax.broadcasted_iota(jnp.int32, gt.shape, 0)
    gt_ref[...] = jnp.where((gs >= GATE_FORGET) & (gs < 2 * GATE_FORGET), _log_sigmoid(gt), gt)


def _rope_tables(C, N):
    t = jnp.arange(N)
    pos = jnp.stack([(t // GRID_W).astype(F32), (t % GRID_W).astype(F32)], 0)
    lane = jnp.arange(ML_QK_WIDTH) % ML_QK_DIM
    inv = ROPE_BASE ** (-(lane % ROPE_GROUP).astype(F32) / ROPE_GROUP)
    ang = pos[lane // (2 * ROPE_GROUP)].T * inv[None, :]
    sign = jnp.where((lane % (2 * ROPE_GROUP)) < ROPE_GROUP, -1.0, 1.0)
    cos = jnp.concatenate([jnp.ones((C, ML_QK_WIDTH), F32), jnp.cos(ang)], 0)
    sin = jnp.concatenate([jnp.zeros((C, ML_QK_WIDTH), F32), jnp.sin(ang) * sign], 0)
    return cos, sin


def proj_odd_pallas(h, mods, dims, w_in, gate_b):
    B, C, N = dims
    R, D = h.shape
    tm = ROW_TILE
    tpb = (C + N) // tm
    mod_idx = _mod_index(tpb, C // tm, B)
    o_qk, o_v, o_o = 2 * ML_QK_WIDTH, 2 * ML_QK_WIDTH + ML_WIDTH, 2 * ML_QK_WIDTH + 2 * ML_WIDTH
    n_gate = w_in.shape[1] - o_o
    w16 = w_in.astype(BF16)
    w_g = jnp.pad(w16[:, o_o:], ((0, 0), (0, LANES - n_gate)))
    gb = jnp.pad(gate_b.astype(F32).reshape(-1), (0, LANES - n_gate))
    cos, sin = _rope_tables(C, N)
    row = lambda width: pl.BlockSpec((tm, width), lambda i: (i, 0))
    seg = pl.BlockSpec((tm, ML_QK_WIDTH), lambda i: (i % tpb, 0))
    return pl.pallas_call(
        _proj_odd_kernel,
        grid=(R // tm,),
        in_specs=[row(D), _mod_spec(0, mod_idx), _mod_spec(1, mod_idx),
                  _full_spec((D, 2 * ML_QK_WIDTH)), _full_spec((D, ML_WIDTH)), _full_spec((D, ML_WIDTH)),
                  _full_spec((D, LANES)), _full_spec((LANES, D)), _full_spec((1, LANES)), _full_spec((LANES, 1)),
                  seg, seg],
        out_specs=[row(ML_QK_WIDTH), row(ML_QK_WIDTH), row(ML_WIDTH), row(ML_WIDTH), row(LANES),
                   pl.BlockSpec((LANES, tm), lambda i: (0, i))],
        out_shape=[jax.ShapeDtypeStruct((R, ML_QK_WIDTH), BF16), jax.ShapeDtypeStruct((R, ML_QK_WIDTH), BF16),
                   jax.ShapeDtypeStruct((R, ML_WIDTH), BF16), jax.ShapeDtypeStruct((R, ML_WIDTH), F32),
                   jax.ShapeDtypeStruct((R, LANES), F32), jax.ShapeDtypeStruct((LANES, R), F32)],
        compiler_params=_cparams(1),
        name="proj_odd",
    )(h, mods, mods, w16[:, :o_qk], w16[:, o_qk:o_v], w16[:, o_v:o_o], w_g, w_g.T, gb.reshape(1, LANES),
      gb.reshape(LANES, 1), cos, sin)


def _split3_bf16(x, axis):
    x1 = x.astype(BF16)
    r1 = x - x1.astype(F32)
    x2 = r1.astype(BF16)
    x3 = (r1 - x2.astype(F32)).astype(BF16)
    return jnp.concatenate([x1, x2, x3], axis=axis)


def _mlstm_kernel(q_ref, k_ref, v_ref, g_ref, gt_ref, h_ref, c_ref, n_ref, m_ref, *, reverse):
    @pl.when(pl.program_id(1) == 0)
    def _():
        c_ref[...] = jnp.zeros_like(c_ref)
        n_ref[...] = jnp.zeros_like(n_ref)
        m_ref[...] = jnp.zeros_like(m_ref)

    L = q_ref.shape[1]
    ti = lax.broadcasted_iota(jnp.int32, (L, L), 0)
    si = lax.broadcasted_iota(jnp.int32, (L, L), 1)
    seen = (si >= ti) if reverse else (si <= ti)
    g = g_ref[0]
    gt = gt_ref[...]
    b_cols3 = jnp.dot(seen.astype(BF16), _split3_bf16(g, 1), preferred_element_type=F32)
    b_cols = b_cols3[:, :LANES] + b_cols3[:, LANES:2 * LANES] + b_cols3[:, 2 * LANES:]
    b_rows3 = lax.dot_general(_split3_bf16(gt, 0), seen.astype(BF16), NT_DIMS, preferred_element_type=F32)
    b_rows = b_rows3[:LANES] + b_rows3[LANES:2 * LANES] + b_rows3[2 * LANES:]
    half = lax.broadcasted_iota(jnp.int32, (1, LANES), 1) // ML_QK_DIM
    row_half = lax.broadcasted_iota(jnp.int32, (LANES, 1), 0) // ML_QK_DIM
    d_off = ML_HEADS if reverse else 0
    tn = (((0,), (0,)), ((), ()))
    heads = [(hd // 2, hd % 2) for hd in range(ML_HEADS)]
    pair = lambda ref, p: ref[0, :, p * LANES:(p + 1) * LANES]
    value = lambda hd: v_ref[0, :, hd * ML_V_DIM:(hd + 1) * ML_V_DIM]

    decay = []
    for hd, (p, h2) in enumerate(heads):
        gi, gf = GATE_IN + d_off + hd, GATE_FORGET + d_off + hd
        ig_col, ig_row = g[:, gi:gi + 1], gt[gi:gi + 1, :]
        b_col, b_row = b_cols[:, gf:gf + 1], b_rows[gf:gf + 1, :]
        m0 = m_ref[p][:, h2 * ML_QK_DIM:h2 * ML_QK_DIM + 1]
        dlog = jnp.where(seen, b_col - b_row + ig_row, NEG_INF)
        inter = b_col + m0
        m_t = jnp.maximum(jnp.max(dlog, axis=-1, keepdims=True), inter)
        b_end = jnp.sum(g[:, gf:gf + 1], axis=0, keepdims=True)
        g_col = b_end - b_col + ig_col
        m_chunk = jnp.max(g_col, axis=0, keepdims=True)
        m_new = jnp.maximum(b_end + m0, m_chunk)
        decay.append(dict(dw=jnp.exp(dlog - m_t), iw=jnp.exp(inter - m_t), floor=jnp.exp(-m_t),
                          kw=jnp.exp(g_col - m_chunk), m_new=m_new,
                          fa=jnp.exp(b_end + m0 - m_new), fb=jnp.exp(m_chunk - m_new)))

    prods = []
    for p, h2 in heads:
        qm = jnp.where(half == h2, pair(q_ref, p), jnp.zeros((L, LANES), BF16))
        qk = lax.dot_general(qm, pair(k_ref, p), NT_DIMS, preferred_element_type=F32)
        qc = jnp.dot(qm, c_ref[p].astype(BF16), preferred_element_type=F32)
        qn = jnp.sum(qm.astype(F32) * n_ref[p], axis=-1, keepdims=True)
        prods.append((qk, qc, qn))

    for hd, ((qk, qc, qn), dc) in enumerate(zip(prods, decay)):
        sc = qk * dc['dw']
        num = jnp.dot(sc.astype(BF16), value(hd), preferred_element_type=F32) + dc['iw'] * qc
        den = jnp.sum(sc, axis=-1, keepdims=True) + dc['iw'] * qn
        h_ref[0, :, hd * ML_V_DIM:(hd + 1) * ML_V_DIM] = num / jnp.maximum(jnp.abs(den), dc['floor'])

    for p in range(ML_HEADS // 2):
        c_old, n_old, m_old = c_ref[p], n_ref[p], m_ref[p]
        c_new, n_new, m_new_pair = c_old, n_old, m_old
        for h2 in range(2):
            hd = 2 * p + h2
            dc = decay[hd]
            kw = jnp.where(half == h2, pair(k_ref, p), jnp.zeros((L, LANES), BF16)).astype(F32) * dc['kw']
            kv = lax.dot_general(kw.astype(BF16), value(hd), tn, preferred_element_type=F32)
            c_new = jnp.where(row_half == h2, dc['fa'] * c_old + dc['fb'] * kv, c_new)
            n_new = jnp.where(half == h2, dc['fa'] * n_old + dc['fb'] * jnp.sum(kw, axis=0, keepdims=True), n_new)
            m_new_pair = jnp.where(half == h2, dc['m_new'], m_new_pair)
        c_ref[p] = c_new
        n_ref[p] = n_new
        m_ref[p] = m_new_pair


def mlstm_pallas(q, k, v, g, gt, dims, reverse):
    B, C, N = dims
    S = C + N
    L = ML_CHUNK
    assert C % L == 0 and N % L == 0
    n_ctx, n_all = C // L, S // L
    if reverse:
        chunk = lambda j: jnp.where(j < n_ctx, n_ctx - 1 - j, n_all - 1 - (j - n_ctx))
    else:
        chunk = lambda j: j
    blk = lambda width: pl.BlockSpec((1, L, width), lambda b, j: (b, chunk(j), 0))
    n_pair = ML_HEADS // 2
    out = pl.pallas_call(
        functools.partial(_mlstm_kernel, reverse=reverse),
        grid=(B, n_all),
        in_specs=[blk(ML_QK_WIDTH), blk(ML_QK_WIDTH), blk(ML_WIDTH), blk(LANES),
                  pl.BlockSpec((LANES, L), lambda b, j: (0, b * n_all + chunk(j)))],
        out_specs=blk(ML_WIDTH),
        out_shape=jax.ShapeDtypeStruct((B, S, ML_WIDTH), F32),
        scratch_shapes=[pltpu.VMEM((n_pair, LANES, ML_V_DIM), F32), pltpu.VMEM((n_pair, 1, LANES), F32),
                        pltpu.VMEM((n_pair, 1, LANES), F32)],
        compiler_params=_cparams(2),
        name="mlstm_bwd" if reverse else "mlstm_fwd",
    )(q.reshape(B, S, -1), k.reshape(B, S, -1), v.reshape(B, S, -1), g.reshape(B, S, -1), gt)
    return out.reshape(B * S, ML_WIDTH)


def _odd_out_kernel(hf_ref, hb_ref, o_ref, h_ref, gate_ref, shift_ref, scale_ref, ng_ref, wout_ref,
                    lng_ref, lnb_ref, router_ref, h_out_ref, f_ref, s_ref):
    hs = hf_ref[...] + hb_ref[...]
    parts = []
    for hd in range(ML_HEADS):
        x = hs[:, hd * ML_V_DIM:(hd + 1) * ML_V_DIM]
        parts.append(x * lax.rsqrt(jnp.mean(x * x, axis=-1, keepdims=True) + ML_NORM_EPS))
    hn = jnp.concatenate(parts, axis=-1) * ng_ref[...] * jax.nn.sigmoid(o_ref[...])
    y = jnp.dot(hn.astype(BF16), wout_ref[...], preferred_element_type=F32)
    _mixer_tail(h_ref[...], y, gate_ref[...], lng_ref[...], lnb_ref[...], shift_ref[...], scale_ref[...],
                router_ref[...], h_out_ref, f_ref, s_ref)


def odd_out_pallas(h_f, h_b, o, h, mods, dims, norm_g, w_out, ln_g, ln_b, router_w):
    B, C, N = dims
    R, D = h.shape
    E = router_w.shape[1]
    tm = ROW_TILE
    mod_idx = _mod_index((C + N) // tm, C // tm, B)
    row = lambda width: pl.BlockSpec((tm, width), lambda i: (i, 0))
    vec = lambda z: z.reshape(1, -1)
    out_specs, out_shape = _tail_specs(R, D, E, tm)
    return pl.pallas_call(
        _odd_out_kernel,
        grid=(R // tm,),
        in_specs=[row(ML_WIDTH), row(ML_WIDTH), row(ML_WIDTH), row(D),
                  _mod_spec(2, mod_idx), _mod_spec(3, mod_idx), _mod_spec(4, mod_idx),
                  _full_spec((1, ML_WIDTH)), _full_spec((ML_WIDTH, D)),
                  _full_spec((1, D)), _full_spec((1, D)), _full_spec((D, E))],
        out_specs=out_specs,
        out_shape=out_shape,
        compiler_params=_cparams(1),
        name="odd_out",
    )(h_f, h_b, o, h, mods, mods, mods, vec(norm_g), w_out.astype(BF16), vec(ln_g), vec(ln_b), router_w)


def _offsets(layout, prefix=''):
    offs, o = {}, 0
    for name, width in layout:
        if name.startswith(prefix):
            offs[name] = (o, width)
            o += width
    return offs


def project(h, w, layout, names):
    offs = _offsets(layout)
    if len(names) == len(layout):
        y = jnp.einsum('btd,de->bte', h, w)
        return {n: y[..., offs[n][0]:offs[n][0] + offs[n][1]] for n in names}
    return {n: jnp.einsum('btd,de->bte', h, w[:, offs[n][0]:offs[n][0] + offs[n][1]]) for n in names}


def axial_rope(z):
    T, dh = z.shape[1], z.shape[-1]
    half = dh // 2
    nf = half // 2
    t = jnp.arange(T)
    row = (t // GRID_W).astype(F32)
    col = (t % GRID_W).astype(F32)
    inv = ROPE_BASE ** (-jnp.arange(nf, dtype=F32) / nf)

    def rot(u, pos):
        ang = pos[:, None] * inv[None, :]
        cos = jnp.cos(ang)[None, :, None, :]
        sin = jnp.sin(ang)[None, :, None, :]
        u1, u2 = u[..., :nf], u[..., nf:]
        return jnp.concatenate([u1 * cos - u2 * sin, u1 * sin + u2 * cos], -1)

    return jnp.concatenate([rot(z[..., :half], row), rot(z[..., half:], col)], -1).astype(z.dtype)


def ml_prep(t, gate_b, rope, need_q):
    B, T = t['ml_k'].shape[:2]
    heads = lambda z, dh: z.reshape(B, T, ML_HEADS, dh).astype(F32)
    k = heads(t['ml_k'], ML_QK_DIM)
    q = heads(t['ml_q'], ML_QK_DIM) * ML_QK_DIM ** -0.5 if need_q else None
    if rope:
        k = axial_rope(k)
        q = axial_rope(q)
    v = heads(t['ml_v'], ML_V_DIM)
    gb = gate_b.astype(F32)
    bht = lambda z: z.astype(F32).transpose(0, 2, 1)
    ig = (bht(t['ml_if'] + gb[0]), bht(t['ml_ib'] + gb[1]))
    lf = (jax.nn.log_sigmoid(bht(t['ml_ff'] + gb[2])), jax.nn.log_sigmoid(bht(t['ml_fb'] + gb[3])))
    bhtd = lambda z: None if z is None else z.transpose(0, 2, 1, 3)
    return bhtd(q), bhtd(k), bhtd(v), ig, lf


def ml_chunk_states(k, v, ig, lf, state0):
    B, H, T, dk = k.shape
    dv = v.shape[-1]
    L = min(ML_CHUNK, T)
    nc = T // L
    kc = k.reshape(B, H, nc, L, dk)
    vc = v.reshape(B, H, nc, L, dv)
    b = jnp.cumsum(lf.reshape(B, H, nc, L), -1)
    b_end = b[..., -1]
    g = b_end[..., None] - b + ig.reshape(B, H, nc, L)
    m_chunk = g.max(-1)
    wgt = jnp.exp(g - m_chunk[..., None])
    kv = jnp.einsum('bhnl,bhnlk,bhnlv->bhnkv', wgt, kc, vc)
    ks = jnp.einsum('bhnl,bhnlk->bhnk', wgt, kc)

    def step(state, inp):
        c_mem, n_mem, m = state
        be, mc, kv_n, ks_n = inp
        m_new = jnp.maximum(be + m, mc)
        fa = jnp.exp(be + m - m_new)
        fb = jnp.exp(mc - m_new)
        c_new = fa[..., None, None] * c_mem + fb[..., None, None] * kv_n
        n_new = fa[..., None] * n_mem + fb[..., None] * ks_n
        return (c_new, n_new, m_new), state

    xs = tuple(jnp.moveaxis(z, 2, 0) for z in (b_end, m_chunk, kv, ks))
    final, starts = lax.scan(step, state0, xs)
    return tuple(jnp.moveaxis(z, 0, 2) for z in starts), final


def ml_chunk_outputs(q, k, v, ig, lf, starts):
    B, H, T, dk = q.shape
    dv = v.shape[-1]
    L = min(ML_CHUNK, T)
    nc = T // L
    qc = q.reshape(B, H, nc, L, dk)
    kc = k.reshape(B, H, nc, L, dk)
    vc = v.reshape(B, H, nc, L, dv)
    b = jnp.cumsum(lf.reshape(B, H, nc, L), -1)
    c0, n0, m0 = starts
    causal = jnp.tril(jnp.ones((L, L), bool))
    dlog = jnp.where(causal, b[..., :, None] - b[..., None, :] + ig.reshape(B, H, nc, L)[..., None, :], NEG_INF)
    inter = b + m0[..., None]
    m = jnp.maximum(dlog.max(-1), inter)
    dw = jnp.exp(dlog - m[..., None])
    iw = jnp.exp(inter - m)
    s = jnp.einsum('bhntd,bhnsd->bhnts', qc, kc) * dw
    num = jnp.einsum('bhnts,bhnsv->bhntv', s, vc) + iw[..., None] * jnp.einsum('bhntd,bhndv->bhntv', qc, c0)
    den = s.sum(-1) + iw * jnp.einsum('bhntd,bhnd->bhnt', qc, n0)
    h = num / jnp.maximum(jnp.abs(den), jnp.exp(-m))[..., None]
    return h.reshape(B, H, T, dv)


def ml_readout(h, o, norm_g):
    B, H, T, dv = h.shape
    hn = h * lax.rsqrt(jnp.mean(h * h, -1, keepdims=True) + ML_NORM_EPS)
    hn = hn.transpose(0, 2, 1, 3).reshape(B, T, H * dv) * norm_g
    return hn * jax.nn.sigmoid(o.astype(F32))


def odd_mixer(a_lat, a_ctx, w_in, w_out, gate_b, norm_g, need_ctx):
    names = tuple(n for n, _ in ODD_LAYOUT)
    t_lat = project(a_lat, w_in, ODD_LAYOUT, names)
    t_ctx = project(a_ctx, w_in, ODD_LAYOUT, names if need_ctx else ODD_CTX_STATE_COLS)
    q_l, k_l, v_l, ig_l, lf_l = ml_prep(t_lat, gate_b, True, True)
    q_c, k_c, v_c, ig_c, lf_c = ml_prep(t_ctx, gate_b, False, need_ctx)
    B = a_lat.shape[0]
    zero = (jnp.zeros((B, ML_HEADS, ML_QK_DIM, ML_V_DIM), F32),
            jnp.zeros((B, ML_HEADS, ML_QK_DIM), F32),
            jnp.zeros((B, ML_HEADS), F32))
    h_l, h_c = [], []
    for d in range(2):
        f = (lambda z: jnp.flip(z, 2)) if d == 1 else (lambda z: z)
        starts_c, final_c = ml_chunk_states(f(k_c), f(v_c), f(ig_c[d]), f(lf_c[d]), zero)
        starts_l, _ = ml_chunk_states(f(k_l), f(v_l), f(ig_l[d]), f(lf_l[d]), final_c)
        h_l.append(f(ml_chunk_outputs(f(q_l), f(k_l), f(v_l), f(ig_l[d]), f(lf_l[d]), starts_l)))
        if need_ctx:
            h_c.append(f(ml_chunk_outputs(f(q_c), f(k_c), f(v_c), f(ig_c[d]), f(lf_c[d]), starts_c)))
    y_lat = jnp.einsum('btd,de->bte', ml_readout(h_l[0] + h_l[1], t_lat['ml_o'], norm_g), w_out).astype(a_lat.dtype)
    if not need_ctx:
        return y_lat, None
    y_ctx = jnp.einsum('btd,de->bte', ml_readout(h_c[0] + h_c[1], t_ctx['ml_o'], norm_g), w_out).astype(a_ctx.dtype)
    return y_lat, y_ctx


def kernel(x, c, ctx, c_ctx, ada_w, ada_b, ln_g, ln_b, ev_w_in, ev_w_out, na_rpb, rw_mu, rw_w0, rw_w_up,
           rw_a0, rw_a_up, rw_g_up, rw_k_k, rw_k_a, rw_r_k, rw_gn_g, rw_gn_b, od_w_in, od_w_out, ml_gate_b,
           ml_norm_g, moe_router, moe_bias, moe_w_gate, moe_w_up, moe_w_down, sh_w_gate, sh_w_up, sh_w_down):
    B, N, D = x.shape
    C = ctx.shape[1]
    S = C + N
    dims = (B, C, N)
    assert C % ROW_TILE == 0 and N % ROW_TILE == 0 and B + 1 <= SUBLANES
    h = jnp.concatenate([ctx, x], axis=1).reshape(B * S, D)
    cond = jnp.zeros((SUBLANES, D), F32).at[:B].set(c).at[B].set(c_ctx)
    for l in range(DEPTH):
        mods = ada_mods_pallas(cond, ada_w[l], ada_b[l])
        if l % 2 == 0:
            e = l // 2
            (q, k, v, dec_f, dec_b, beta_f, beta_b, kd_f, kd_b, nkk, rv, rr, glow) = proj_even_pallas(
                h, mods, dims, ev_w_in[e], rw_mu[e], rw_w0[e], rw_w_up[e], rw_a0[e], rw_a_up[e],
                rw_k_k[e], rw_k_a[e])
            y_f, y_b = rwkv_scan_pallas(dec_f, beta_f, kd_f, dec_b, beta_b, kd_b, nkk, rv, rr, dims)
            na = attention_pallas(q, k, v, na_rpb[e], dims)
            h, f, s = even_out_pallas(na, y_f, y_b, rr, rv, kd_f, kd_b, glow, h, mods, dims, rw_g_up[e],
                                      rw_r_k[e], rw_gn_g[e], rw_gn_b[e], ev_w_out[e], ln_g[l, 0], ln_b[l, 0],
                                      moe_router[l])
        else:
            o = l // 2
            q, k, v, og, g, gt = proj_odd_pallas(h, mods, dims, od_w_in[o], ml_gate_b[o])
            h_f = mlstm_pallas(q, k, v, g, gt, dims, False)
            h_b = mlstm_pallas(q, k, v, g, gt, dims, True)
            h, f, s = odd_out_pallas(h_f, h_b, og, h, mods, dims, ml_norm_g[o], od_w_out[o], ln_g[l, 0],
                                     ln_b[l, 0], moe_router[l])
        h = moe_layer(f, s, h, mods, dims, ln_g[l, 1], ln_b[l, 1], moe_bias[l], l, moe_w_gate, moe_w_up,
                      moe_w_down, sh_w_gate[l], sh_w_up[l], sh_w_down[l])
    return h.reshape(B, S, D)[:, C:]
```

```python
import functools

import jax
import jax.numpy as jnp
import numpy as np
from jax import lax
from jax.experimental import pallas as pl
from jax.experimental.pallas import tpu as pltpu

D_MODEL = 1024
DEPTH = 2
GRID_W = 64

DEEPNORM_ALPHA = (2.0 * DEPTH) ** 0.25
LN_EPS = 1e-5
NEG_INF = -1e30
F32 = jnp.float32
BF16 = jnp.bfloat16

NA_HEAD_DIM = 64
NA_WIDTH = D_MODEL // 2
NA_HEADS = NA_WIDTH // NA_HEAD_DIM
NA_WIN_ROWS = 8
NA_WIN_COLS = 16
NA_SCALE = NA_HEAD_DIM ** -0.5

RW_HEAD_DIM = 64
RW_WIDTH = D_MODEL // 2
RW_HEADS = RW_WIDTH // RW_HEAD_DIM
RW_DECAY_LORA = 32
RW_AAA_LORA = 32
RW_GATE_LORA = 96
RW_GN_EPS = 64e-5

ML_HEADS = 8
ML_V_DIM = D_MODEL // ML_HEADS
ML_QK_DIM = ML_V_DIM // 2
ML_WIDTH = ML_HEADS * ML_V_DIM
ML_CHUNK = 128
ML_NORM_EPS = 1e-6
ROPE_BASE = 10000.0

N_EXPERTS = 256
TOP_K = 8
N_GROUPS = 8
TOPK_GROUPS = 4
ROUTED_SCALE = 2.5
MOE_BLOCK = 256

ODD_LAYOUT = (
    ('ml_q', ML_HEADS * ML_QK_DIM), ('ml_k', ML_HEADS * ML_QK_DIM),
    ('ml_v', ML_WIDTH), ('ml_o', ML_WIDTH),
    ('ml_if', ML_HEADS), ('ml_ib', ML_HEADS), ('ml_ff', ML_HEADS), ('ml_fb', ML_HEADS),
)
ODD_CTX_STATE_COLS = ('ml_k', 'ml_v', 'ml_if', 'ml_ib', 'ml_ff', 'ml_fb')

SUBLANES = 8
LANES = 128
VMEM_LIMIT_BYTES = 56 * 1024 * 1024

ROW_TILE = 256
N_MODS = 6
RW_COLS = 3 * RW_WIDTH + 2 * LANES
NT_DIMS = (((1,), (1,)), ((), ()))


def _cparams(n_axes):
    return pltpu.CompilerParams(dimension_semantics=("arbitrary",) * n_axes, vmem_limit_bytes=VMEM_LIMIT_BYTES)


def _full_spec(shape):
    return pl.BlockSpec(shape, lambda *_: (0,) * len(shape))


def _split_bf16(x):
    hi = x.astype(BF16)
    lo = (x - hi.astype(F32)).astype(BF16)
    return jnp.concatenate([hi, lo], axis=-1)


def _block_ones(n_rows, n_cols, seg):
    row = lax.broadcasted_iota(jnp.int32, (n_rows, n_cols), 0)
    col = lax.broadcasted_iota(jnp.int32, (n_rows, n_cols), 1)
    return (((row % n_cols) // seg) == (col // seg)).astype(BF16)


def _seg_sum(x, ones2):
    return jnp.dot(_split_bf16(x), ones2, preferred_element_type=F32)


def _mod_index(tiles_per_batch, ctx_tiles, n_batch):
    def idx(i):
        return jnp.where(i % tiles_per_batch < ctx_tiles, n_batch, i // tiles_per_batch)
    return idx


def _mod_spec(chunk, mod_idx):
    return pl.BlockSpec((None, None, 1, D_MODEL), lambda i: (mod_idx(i), chunk, 0, 0))


def _ada_kernel(c_ref, w_ref, b_ref, o_ref):
    c = c_ref[...]
    x = (c * jax.nn.sigmoid(c)).astype(BF16)
    o_ref[...] = jnp.dot(x, w_ref[...].astype(BF16), preferred_element_type=F32) + b_ref[...]


def ada_mods_pallas(cond, w, b):
    n, D = cond.shape
    n_out = w.shape[1]
    tn = 512
    out = pl.pallas_call(
        _ada_kernel,
        grid=(n_out // tn,),
        in_specs=[_full_spec((n, D)), pl.BlockSpec((D, tn), lambda j: (0, j)), pl.BlockSpec((1, tn), lambda j: (0, j))],
        out_specs=pl.BlockSpec((n, tn), lambda j: (0, j)),
        out_shape=jax.ShapeDtypeStruct((n, n_out), F32),
        compiler_params=_cparams(1),
        name="ada_mods",
    )(cond, w, b.reshape(1, n_out))
    return out.reshape(n, N_MODS, 1, D)


def _softplus(x):
    return jnp.maximum(x, 0.0) + jnp.log(1.0 + jnp.exp(-jnp.abs(x)))


def _proj_even_kernel(h_ref, hp_ref, hn_ref, shift_ref, scale_ref, wna_ref, wrw_ref, mu_ref, ones_ref,
                      kk_ref, ka_ref, w0_ref, a0_ref, wup_ref, aup_ref,
                      q_ref, k_ref, v_ref, dec_f_ref, dec_b_ref, beta_f_ref, beta_b_ref, kd_f_ref, kd_b_ref,
                      nkk_ref, rv_ref, rr_ref, glow_ref, *, tiles_per_batch, ctx_tiles):
    i = pl.program_id(0)
    j = i % tiles_per_batch
    first = (j == 0) | (j == ctx_tiles)
    last = (j == ctx_tiles - 1) | (j == tiles_per_batch - 1)
    tm = h_ref.shape[0]
    gain = 1.0 + scale_ref[...]
    shift = shift_ref[...]
    a = h_ref[...] * gain + shift
    a_prev = jnp.where(first, 0.0, hp_ref[SUBLANES - 1:SUBLANES, :] * gain + shift)
    a_next = jnp.where(last, 0.0, hn_ref[0:1, :] * gain + shift)
    rid = lax.broadcasted_iota(jnp.int32, (tm, 1), 0)
    prev = jnp.where(rid == 0, a_prev, pltpu.roll(a, 1, 0))
    nxt = jnp.where(rid == tm - 1, a_next, pltpu.roll(a, tm - 1, 0))
    a16 = a.astype(BF16)
    nb16 = (0.5 * (prev + nxt)).astype(BF16)

    na = jnp.dot(a16, wna_ref[...], preferred_element_type=F32)
    q_ref[...] = (na[:, :NA_WIDTH] * NA_SCALE).astype(BF16)
    k_ref[...] = na[:, NA_WIDTH:2 * NA_WIDTH].astype(BF16)
    v_ref[...] = na[:, 2 * NA_WIDTH:].astype(BF16)

    pa = jnp.dot(a16, wrw_ref[...], preferred_element_type=F32)
    pn = jnp.dot(nb16, wrw_ref[...], preferred_element_type=F32)
    t = pa + mu_ref[...] * (pn - pa)
    r = t[:, :RW_WIDTH]
    k = t[:, RW_WIDTH:2 * RW_WIDTH]
    lora = t[:, 3 * RW_WIDTH:3 * RW_WIDTH + LANES]
    rr_ref[...] = r
    rv_ref[...] = t[:, 2 * RW_WIDTH:3 * RW_WIDTH]
    glow_ref[...] = t[:, 3 * RW_WIDTH + LANES:]

    kk = k * kk_ref[...]
    norm = jnp.sqrt(_seg_sum(kk * kk, ones_ref[...]))
    kk = kk / jnp.maximum(norm, 1e-12)
    nkk_ref[...] = -kk
    lora_t = jnp.tanh(lora).astype(BF16)
    lora16 = lora.astype(BF16)
    outs = ((dec_f_ref, beta_f_ref, kd_f_ref), (dec_b_ref, beta_b_ref, kd_b_ref))
    for d in range(2):
        w_log = -_softplus(-(w0_ref[d:d + 1, :] + jnp.dot(lora_t, wup_ref[d], preferred_element_type=F32))) - 0.5
        a_gate = jax.nn.sigmoid(a0_ref[d:d + 1, :] + jnp.dot(lora16, aup_ref[d], preferred_element_type=F32))
        outs[d][0][...] = jnp.exp(-jnp.exp(w_log))
        outs[d][1][...] = kk * a_gate
        outs[d][2][...] = k * (1.0 + (a_gate - 1.0) * ka_ref[...])


def proj_even_pallas(h, mods, dims, w_in, mu, w0, w_up, a0, a_up, k_k, k_a):
    B, C, N = dims
    R, D = h.shape
    tm = ROW_TILE
    tpb, ctx_tiles = (C + N) // tm, C // tm
    mod_idx = _mod_index(tpb, ctx_tiles, B)
    w_na = w_in[:, :3 * NA_WIDTH].astype(BF16)
    n_rw = w_in.shape[1] - 3 * NA_WIDTH
    w_rw = jnp.pad(w_in[:, 3 * NA_WIDTH:], ((0, 0), (0, RW_COLS - n_rw))).astype(BF16)
    mu_p = jnp.pad(mu, (0, RW_COLS - n_rw)).reshape(1, RW_COLS)
    ones2 = _block_ones(2 * RW_WIDTH, RW_WIDTH, RW_HEAD_DIM)
    lr = RW_DECAY_LORA

    def pad_up(m, first_row):
        out = jnp.zeros((2, LANES, RW_WIDTH), F32)
        for d in range(2):
            out = out.at[d, first_row + d * lr:first_row + (d + 1) * lr].set(m[d])
        return out.astype(BF16)

    row = lambda width: pl.BlockSpec((tm, width), lambda i: (i, 0))
    hb = tm // SUBLANES
    n_hb = R // SUBLANES
    wide = jax.ShapeDtypeStruct((R, RW_WIDTH), F32)
    half = jax.ShapeDtypeStruct((R, NA_WIDTH), BF16)
    return pl.pallas_call(
        functools.partial(_proj_even_kernel, tiles_per_batch=tpb, ctx_tiles=ctx_tiles),
        grid=(R // tm,),
        in_specs=[
            row(D),
            pl.BlockSpec((SUBLANES, D), lambda i: (jnp.maximum(i * hb - 1, 0), 0)),
            pl.BlockSpec((SUBLANES, D), lambda i: (jnp.minimum((i + 1) * hb, n_hb - 1), 0)),
            _mod_spec(0, mod_idx), _mod_spec(1, mod_idx),
            _full_spec((D, 3 * NA_WIDTH)), _full_spec((D, RW_COLS)), _full_spec((1, RW_COLS)),
            _full_spec((2 * RW_WIDTH, RW_WIDTH)),
            _full_spec((1, RW_WIDTH)), _full_spec((1, RW_WIDTH)),
            _full_spec((2, RW_WIDTH)), _full_spec((2, RW_WIDTH)),
            _full_spec((2, LANES, RW_WIDTH)), _full_spec((2, LANES, RW_WIDTH)),
        ],
        out_specs=[row(NA_WIDTH)] * 3 + [row(RW_WIDTH)] * 9 + [row(LANES)],
        out_shape=[half] * 3 + [wide] * 9 + [jax.ShapeDtypeStruct((R, LANES), F32)],
        compiler_params=_cparams(1),
        name="proj_even",
    )(h, h, h, mods, mods, w_na, w_rw, mu_p, ones2, k_k.reshape(1, -1), k_a.reshape(1, -1), w0, a0,
      pad_up(w_up, 0), pad_up(a_up, 2 * lr))


RW_SCAN_TIME = 256


def _rwkv_scan_kernel(wf_ref, bf_ref, kf_ref, nf_ref, vf_ref, rf_ref,
                      wb_ref, bb_ref, kb_ref, nb_ref, vb_ref, rb_ref, yf_ref, yb_ref, s_ref):
    @pl.when(pl.program_id(0) == 0)
    def _():
        s_ref[...] = jnp.zeros_like(s_ref)

    n_batch, n_time, width = wf_ref.shape
    n_pair = width // LANES
    n_dir_chain = n_batch * n_pair
    n_chain = 2 * n_dir_chain
    rows_all = n_chain * RW_HEAD_DIM
    ones = _block_ones(LANES, LANES, RW_HEAD_DIM)
    vi = lax.broadcasted_iota(jnp.int32, (1, RW_HEAD_DIM, LANES), 1)
    li = lax.broadcasted_iota(jnp.int32, (1, RW_HEAD_DIM, LANES), 2)
    diag = (li % RW_HEAD_DIM) == vi
    n_sub = n_time // SUBLANES

    def seg(x):
        out = jnp.dot(x.reshape(rows_all, LANES).astype(BF16), ones, preferred_element_type=F32)
        return out.reshape(n_chain, RW_HEAD_DIM, LANES)

    def chains(ref, rows):
        x = ref[:, rows, :]
        return [x[b, :, p * LANES:(p + 1) * LANES] for b in range(n_batch) for p in range(n_pair)]

    def sub(i, carry):
        rows_f = pl.ds(pl.multiple_of(i * SUBLANES, SUBLANES), SUBLANES)
        rows_b = pl.ds(pl.multiple_of((n_sub - 1 - i) * SUBLANES, SUBLANES), SUBLANES)
        load = lambda f_ref, b_ref: (jnp.stack(chains(f_ref, rows_f)), jnp.stack(chains(b_ref, rows_b)))
        w8, beta8, kd8 = load(wf_ref, wb_ref), load(bf_ref, bb_ref), load(kf_ref, kb_ref)
        nkk8, v8, r8 = load(nf_ref, nb_ref), load(vf_ref, vb_ref), load(rf_ref, rb_ref)

        def at(pair, t):
            tb = SUBLANES - 1 - t
            return jnp.concatenate([pair[0][:, t:t + 1, :], pair[1][:, tb:tb + 1, :]], axis=0)

        s = s_ref[...]
        rows = []
        for t in range(SUBLANES):
            vcol = seg(jnp.where(diag, at(v8, t), 0.0))
            sa = seg(s * at(nkk8, t))
            s = s * at(w8, t) + sa * at(beta8, t) + vcol * at(kd8, t)
            ybc = seg(s * at(r8, t))
            rows.append(jnp.sum(jnp.where(diag, ybc, 0.0), axis=1, keepdims=True))
        s_ref[...] = s
        y_f = jnp.concatenate([row[:n_dir_chain] for row in rows], axis=1)
        y_b = jnp.concatenate([row[n_dir_chain:] for row in rows[::-1]], axis=1)
        for b in range(n_batch):
            for p in range(n_pair):
                c = b * n_pair + p
                yf_ref[b, rows_f, p * LANES:(p + 1) * LANES] = y_f[c]
                yb_ref[b, rows_b, p * LANES:(p + 1) * LANES] = y_b[c]
        return carry

    lax.fori_loop(0, n_sub, sub, 0)


def rwkv_scan_pallas(dec_f, beta_f, kd_f, dec_b, beta_b, kd_b, nkk, v, r, dims):
    B, C, N = dims
    S = C + N
    tc = RW_SCAN_TIME
    assert C % tc == 0 and N % tc == 0
    n_ctx, n_all = C // tc, S // tc
    as3 = lambda z: z.reshape(B, S, RW_WIDTH)
    fwd = pl.BlockSpec((B, tc, RW_WIDTH), lambda j: (0, j, 0))
    bwd = pl.BlockSpec((B, tc, RW_WIDTH),
                       lambda j: (0, jnp.where(j < n_ctx, n_ctx - 1 - j, n_all - 1 - (j - n_ctx)), 0))
    out = jax.ShapeDtypeStruct((B, S, RW_WIDTH), F32)
    y_f, y_b = pl.pallas_call(
        _rwkv_scan_kernel,
        grid=(n_all,),
        in_specs=[fwd] * 6 + [bwd] * 6,
        out_specs=[fwd, bwd],
        out_shape=[out, out],
        scratch_shapes=[pltpu.VMEM((2 * B * (RW_WIDTH // LANES), RW_HEAD_DIM, LANES), F32)],
        compiler_params=_cparams(1),
        name="rwkv_scan",
    )(as3(dec_f), as3(beta_f), as3(kd_f), as3(nkk), as3(v), as3(r),
      as3(dec_b), as3(beta_b), as3(kd_b), as3(nkk), as3(v), as3(r))
    return y_f.reshape(B * S, RW_WIDTH), y_b.reshape(B * S, RW_WIDTH)


NA_BAND = NA_WIN_ROWS * GRID_W


def _na_row_start(j, ctx_blocks, n_rows):
    r = jnp.maximum(j - ctx_blocks, 0)
    return r, jnp.clip(r - NA_WIN_ROWS // 2, 0, n_rows - NA_WIN_ROWS)


def _na_kernel(q_ref, k_ref, v_ref, bias_ref, o_ref, *, n_ctx):
    j = pl.program_id(1)
    ctx_blocks = n_ctx // GRID_W
    n_rows = pl.num_programs(1) - ctx_blocks
    _, row_start = _na_row_start(j, ctx_blocks, n_rows)
    start = pl.multiple_of(n_ctx + row_start * GRID_W, GRID_W)
    q = q_ref[0]
    kb = k_ref[0, pl.ds(start, NA_BAND), :]
    vb = v_ref[0, pl.ds(start, NA_BAND), :]
    kc = k_ref[0, pl.ds(0, n_ctx), :]
    vc = v_ref[0, pl.ds(0, n_ctx), :]
    head_of_lane = lax.broadcasted_iota(jnp.int32, (GRID_W, LANES), 1) // NA_HEAD_DIM
    heads = [(p, h2) for p in range(NA_WIDTH // LANES) for h2 in range(LANES // NA_HEAD_DIM)]
    cols = lambda p: slice(p * LANES, (p + 1) * LANES)
    scores = []
    for p, h2 in heads:
        qm = jnp.where(head_of_lane == h2, q[:, cols(p)], jnp.zeros((GRID_W, LANES), BF16))
        s_loc = lax.dot_general(qm, kb[:, cols(p)], NT_DIMS, preferred_element_type=F32)
        s_ctx = lax.dot_general(qm, kc[:, cols(p)], NT_DIMS, preferred_element_type=F32)
        scores.append((s_loc + bias_ref[0, 2 * p + h2], s_ctx))
    probs = []
    for s_loc, s_ctx in scores:
        m = jnp.maximum(jnp.max(s_loc, axis=-1, keepdims=True), jnp.max(s_ctx, axis=-1, keepdims=True))
        e_loc = jnp.exp(s_loc - m)
        e_ctx = jnp.exp(s_ctx - m)
        den = jnp.sum(e_loc, axis=-1, keepdims=True) + jnp.sum(e_ctx, axis=-1, keepdims=True)
        probs.append((e_loc.astype(BF16), e_ctx.astype(BF16), den))
    outs = []
    for (p, h2), (e_loc, e_ctx, den) in zip(heads, probs):
        o = (jnp.dot(e_loc, vb[:, cols(p)], preferred_element_type=F32)
             + jnp.dot(e_ctx, vc[:, cols(p)], preferred_element_type=F32))
        outs.append(o / den)
    for p in range(NA_WIDTH // LANES):
        o_ref[0, :, cols(p)] = jnp.where(head_of_lane == 0, outs[2 * p], outs[2 * p + 1])


def _na_bias_table(rpb):
    kw = NA_WIN_COLS
    n_col_off = 2 * kw - 1
    j = np.arange(GRID_W)
    col_start = np.clip(j - kw // 2, 0, GRID_W - kw)
    col_in = (j[None, :] >= col_start[:, None]) & (j[None, :] < col_start[:, None] + kw)
    col_off = np.clip(j[None, :] - j[:, None], -(kw - 1), kw - 1) + (kw - 1)
    pick = (col_off.reshape(1, -1) == np.arange(n_col_off)[:, None]).astype(np.float32)
    toep = jnp.dot(rpb.astype(F32).reshape(-1, n_col_off), pick, precision=lax.Precision.HIGHEST)
    toep = toep.reshape(NA_HEADS, 2 * NA_WIN_ROWS - 1, GRID_W, GRID_W)
    toep = jnp.where(col_in[None, None], toep, NEG_INF)
    tab = jnp.stack([toep[:, NA_WIN_ROWS - 1 - d:2 * NA_WIN_ROWS - 1 - d] for d in range(NA_WIN_ROWS)], 0)
    tab = tab.transpose(0, 1, 3, 2, 4).reshape(NA_WIN_ROWS, NA_HEADS, GRID_W, NA_BAND)
    return jnp.concatenate([tab, jnp.full((1,) + tab.shape[1:], NEG_INF, F32)], 0)


def attention_pallas(q, k, v, rpb, dims):
    B, C, N = dims
    S = C + N
    W = NA_WIDTH
    n_rows = N // GRID_W
    ctx_blocks = C // GRID_W
    assert n_rows >= NA_WIN_ROWS and N % GRID_W == 0 and C % GRID_W == 0
    bias = _na_bias_table(rpb)
    as3 = lambda z: z.reshape(B, S, W)

    def bias_idx(b, j):
        r, row_start = _na_row_start(j, ctx_blocks, n_rows)
        return (jnp.where(j < ctx_blocks, NA_WIN_ROWS, r - row_start), 0, 0, 0)

    out = pl.pallas_call(
        functools.partial(_na_kernel, n_ctx=C),
        grid=(B, S // GRID_W),
        in_specs=[
            pl.BlockSpec((1, GRID_W, W), lambda b, j: (b, j, 0)),
            pl.BlockSpec((1, S, W), lambda b, j: (b, 0, 0)),
            pl.BlockSpec((1, S, W), lambda b, j: (b, 0, 0)),
            pl.BlockSpec((1, NA_HEADS, GRID_W, NA_BAND), bias_idx),
        ],
        out_specs=pl.BlockSpec((1, GRID_W, W), lambda b, j: (b, j, 0)),
        out_shape=jax.ShapeDtypeStruct((B, S, W), F32),
        compiler_params=_cparams(2),
        name="na_attention",
    )(as3(q), as3(k), as3(v), bias)
    return out.reshape(B * S, W)


def _layer_norm(x, g, b):
    mu = jnp.mean(x, axis=-1, keepdims=True)
    xc = x - mu
    var = jnp.mean(xc * xc, axis=-1, keepdims=True)
    return xc * lax.rsqrt(var + LN_EPS) * g + b


HIGH_HALF = 0xFFFF0000


def _pack_halves(x):
    w = x.shape[-1] // 2
    bits = lambda z: lax.bitcast_convert_type(z.astype(BF16).astype(F32), jnp.uint32)
    return (bits(x[:, :w]) >> 16) | (bits(x[:, w:]) & jnp.uint32(HIGH_HALF))


def _unpack_halves(p):
    return (lax.bitcast_convert_type(p << 16, F32), lax.bitcast_convert_type(p & jnp.uint32(HIGH_HALF), F32))


def _mixer_tail(h, y, gate, ln_g, ln_b, shift, scale, router, h_out_ref, f_ref, s_ref):
    h1 = _layer_norm(DEEPNORM_ALPHA * h + gate * y, ln_g, ln_b)
    f = h1 * (1.0 + scale) + shift
    h_out_ref[...] = h1
    f_ref[...] = _pack_halves(f)
    s_ref[...] = jax.nn.sigmoid(jnp.dot(f, router, preferred_element_type=F32, precision=lax.Precision.HIGHEST))


def _even_out_kernel(na_ref, yf_ref, yb_ref, r_ref, v_ref, kdf_ref, kdb_ref, glow_ref, h_ref,
                     gate_ref, shift_ref, scale_ref, ones_ref, gng_ref, gnb_ref, rk_ref, gup_ref, wout_ref,
                     lng_ref, lnb_ref, router_ref, h_out_ref, f_ref, s_ref):
    ones2 = ones_ref[...]
    inv = 1.0 / RW_HEAD_DIM
    y = yf_ref[...] + yb_ref[...]
    mu = _seg_sum(y, ones2) * inv
    yc = y - mu
    var = _seg_sum(yc * yc, ones2) * inv
    yn = yc * lax.rsqrt(var + RW_GN_EPS) * gng_ref[...] + gnb_ref[...]
    r = r_ref[...]
    bonus = (_seg_sum(r * kdf_ref[...] * rk_ref[...], ones2) + _seg_sum(r * kdb_ref[...] * rk_ref[...], ones2))
    gate = jnp.dot(jax.nn.sigmoid(glow_ref[...]).astype(BF16), gup_ref[...], preferred_element_type=F32)
    rw = (yn + bonus * v_ref[...]) * gate
    mix = jnp.concatenate([na_ref[...], rw], axis=-1).astype(BF16)
    y_mix = jnp.dot(mix, wout_ref[...], preferred_element_type=F32)
    _mixer_tail(h_ref[...], y_mix, gate_ref[...], lng_ref[...], lnb_ref[...], shift_ref[...], scale_ref[...],
                router_ref[...], h_out_ref, f_ref, s_ref)


def _tail_specs(R, D, E, tm):
    row = lambda width: pl.BlockSpec((tm, width), lambda i: (i, 0))
    return ([row(D), row(D // 2), row(E)],
            [jax.ShapeDtypeStruct((R, D), F32), jax.ShapeDtypeStruct((R, D // 2), jnp.uint32),
             jax.ShapeDtypeStruct((R, E), F32)])


def even_out_pallas(na, y_f, y_b, r, v, kd_f, kd_b, glow, h, mods, dims, g_up, r_k, gn_g, gn_b, w_out,
                    ln_g, ln_b, router_w):
    B, C, N = dims
    R, D = h.shape
    E = router_w.shape[1]
    tm = ROW_TILE
    mod_idx = _mod_index((C + N) // tm, C // tm, B)
    row = lambda width: pl.BlockSpec((tm, width), lambda i: (i, 0))
    ones2 = _block_ones(2 * RW_WIDTH, RW_WIDTH, RW_HEAD_DIM)
    g_up_p = jnp.pad(g_up, ((0, LANES - g_up.shape[0]), (0, 0))).astype(BF16)
    vec = lambda z: z.reshape(1, -1)
    out_specs, out_shape = _tail_specs(R, D, E, tm)
    return pl.pallas_call(
        _even_out_kernel,
        grid=(R // tm,),
        in_specs=[row(NA_WIDTH)] + [row(RW_WIDTH)] * 6 + [row(LANES), row(D),
                  _mod_spec(2, mod_idx), _mod_spec(3, mod_idx), _mod_spec(4, mod_idx),
                  _full_spec((2 * RW_WIDTH, RW_WIDTH)),
                  _full_spec((1, RW_WIDTH)), _full_spec((1, RW_WIDTH)), _full_spec((1, RW_WIDTH)),
                  _full_spec((LANES, RW_WIDTH)), _full_spec((D, D)),
                  _full_spec((1, D)), _full_spec((1, D)), _full_spec((D, E))],
        out_specs=out_specs,
        out_shape=out_shape,
        compiler_params=_cparams(1),
        name="even_out",
    )(na, y_f, y_b, r, v, kd_f, kd_b, glow, h, mods, mods, mods, ones2, vec(gn_g), vec(gn_b), vec(r_k),
      g_up_p, w_out.astype(BF16), vec(ln_g), vec(ln_b), router_w)


def _resid_tail_kernel(y_ref, h_ref, gate_ref, shift_ref, scale_ref, lng_ref, lnb_ref, router_ref,
                       h_out_ref, f_ref, s_ref):
    _mixer_tail(h_ref[...], y_ref[...], gate_ref[...], lng_ref[...], lnb_ref[...], shift_ref[...],
                scale_ref[...], router_ref[...], h_out_ref, f_ref, s_ref)


def resid_tail_pallas(y, h, mods, dims, ln_g, ln_b, router_w):
    B, C, N = dims
    R, D = h.shape
    E = router_w.shape[1]
    tm = ROW_TILE
    mod_idx = _mod_index((C + N) // tm, C // tm, B)
    row = lambda width: pl.BlockSpec((tm, width), lambda i: (i, 0))
    vec = lambda z: z.reshape(1, -1)
    out_specs, out_shape = _tail_specs(R, D, E, tm)
    return pl.pallas_call(
        _resid_tail_kernel,
        grid=(R // tm,),
        in_specs=[row(D), row(D), _mod_spec(2, mod_idx), _mod_spec(3, mod_idx), _mod_spec(4, mod_idx),
                  _full_spec((1, D)), _full_spec((1, D)), _full_spec((D, E))],
        out_specs=out_specs,
        out_shape=out_shape,
        compiler_params=_cparams(1),
        name="resid_tail",
    )(y, h, mods, mods, mods, vec(ln_g), vec(ln_b), router_w)


MOE_TOKEN_TILE = 256


def _swiglu_packed(p, wg_ref, wu_ref, wd_ref):
    lo, hi = _unpack_halves(p)
    lo, hi = lo.astype(BF16), hi.astype(BF16)
    w = p.shape[-1]
    proj = lambda ref: (jnp.dot(lo, ref[:w, :], preferred_element_type=F32)
                        + jnp.dot(hi, ref[w:, :], preferred_element_type=F32))
    g, u = proj(wg_ref), proj(wu_ref)
    mid = (g * jax.nn.sigmoid(g) * u).astype(BF16)
    return jnp.dot(mid, wd_ref[...], preferred_element_type=F32)


def _row_copy(src_ref, src_row, dst_ref, dst_row, sem):
    return pltpu.make_async_copy(src_ref.at[pl.ds(src_row, 1), :], dst_ref.at[pl.ds(dst_row, 1), :], sem)


def _slot(e_ref, rank_ref, starts_ref, i):
    return starts_ref[e_ref[i]] + rank_ref[i]


def _dispatch_kernel(e_ref, rank_ref, starts_ref, f_ref, xs_ref, sem):
    n_tok = f_ref.shape[0]

    def issue(t, carry):
        for k in range(TOP_K):
            _row_copy(f_ref, t, xs_ref, _slot(e_ref, rank_ref, starts_ref, t * TOP_K + k), sem).start()
        return carry

    lax.fori_loop(0, n_tok, issue, 0)

    def drain(t, carry):
        for k in range(TOP_K):
            _row_copy(f_ref, 0, xs_ref, 0, sem).wait()
        return carry

    lax.fori_loop(0, n_tok, drain, 0)


def _slot_specs(tm):
    flat = pl.BlockSpec((tm * TOP_K,), lambda i: (i,), memory_space=pltpu.SMEM)
    return [flat, flat, pl.BlockSpec(memory_space=pltpu.SMEM)]


def moe_dispatch_pallas(f, e_flat, rank_flat, starts):
    T, D = f.shape
    tm = MOE_TOKEN_TILE
    assert T % tm == 0
    return pl.pallas_call(
        _dispatch_kernel,
        grid=(T // tm,),
        in_specs=_slot_specs(tm) + [pl.BlockSpec((tm, D), lambda i: (i, 0))],
        out_specs=pl.BlockSpec(memory_space=pl.ANY),
        out_shape=jax.ShapeDtypeStruct((T * TOP_K, D), f.dtype),
        scratch_shapes=[pltpu.SemaphoreType.DMA(())],
        compiler_params=_cparams(1),
        name="moe_dispatch",
    )(e_flat, rank_flat, starts, f)


def _expert_item_kernel(blk_ref, e_ref, lo_ref, hi_ref, first_ref, x_ref, wg_ref, wu_ref, wd_ref, o_ref,
                        wg16_ref, wu16_ref, wd16_ref, acc_ref):
    i = pl.program_id(0)
    lo, hi = lo_ref[i], hi_ref[i]

    @pl.when((i == 0) | (e_ref[i] != e_ref[jnp.maximum(i - 1, 0)]))
    def _():
        wg16_ref[...] = wg_ref[0, 0].astype(BF16)
        wu16_ref[...] = wu_ref[0, 0].astype(BF16)
        wd16_ref[...] = wd_ref[0, 0].astype(BF16)

    @pl.when(hi > lo)
    def _():
        y = _swiglu_packed(x_ref[...], wg16_ref, wu16_ref, wd16_ref)
        rows = blk_ref[i] * MOE_BLOCK + lax.broadcasted_iota(jnp.int32, (MOE_BLOCK, 1), 0)
        y = jnp.where((rows >= lo) & (rows < hi), y, 0.0)

        @pl.when(first_ref[i] == 1)
        def _():
            acc_ref[...] = y

        @pl.when(first_ref[i] == 0)
        def _():
            acc_ref[...] += y

        o_ref[...] = _pack_halves(acc_ref[...])


def moe_experts_pallas(xs, items, layer, wg, wu, wd):
    n_rows, half = xs.shape
    D = 2 * half
    F = wg.shape[-1]
    n_items = items[0].shape[0]
    grid_spec = pltpu.PrefetchScalarGridSpec(
        num_scalar_prefetch=5,
        grid=(n_items,),
        in_specs=[
            pl.BlockSpec((MOE_BLOCK, half), lambda i, blk, e, lo, hi, first: (blk[i], 0)),
            pl.BlockSpec((1, 1, D, F), lambda i, blk, e, lo, hi, first: (layer, e[i], 0, 0)),
            pl.BlockSpec((1, 1, D, F), lambda i, blk, e, lo, hi, first: (layer, e[i], 0, 0)),
            pl.BlockSpec((1, 1, F, D), lambda i, blk, e, lo, hi, first: (layer, e[i], 0, 0)),
        ],
        out_specs=pl.BlockSpec((MOE_BLOCK, half), lambda i, blk, e, lo, hi, first: (blk[i], 0)),
        scratch_shapes=[pltpu.VMEM((D, F), BF16), pltpu.VMEM((D, F), BF16), pltpu.VMEM((F, D), BF16),
                        pltpu.VMEM((MOE_BLOCK, D), F32)],
    )
    return pl.pallas_call(
        _expert_item_kernel,
        grid_spec=grid_spec,
        out_shape=jax.ShapeDtypeStruct((n_rows, half), jnp.uint32),
        compiler_params=_cparams(1),
        name="moe_experts",
    )(*items, xs, wg, wu, wd)


def _combine_kernel(e_ref, rank_ref, e_next_ref, rank_next_ref, starts_ref, w_ref, f_ref, h_ref, gate_ref,
                    lng_ref, lnb_ref, sg_ref, su_ref, sd_ref, ys_ref, o_ref, buf_a, buf_b, acc_ref, sems):
    i = pl.program_id(0)
    n_tok = f_ref.shape[0]

    def gather(e_r, rank_r, t, buf, sem):
        for k in range(TOP_K):
            pltpu.make_async_copy(ys_ref.at[pl.ds(_slot(e_r, rank_r, starts_ref, t * TOP_K + k), 1), :],
                                  buf.at[k, pl.ds(t, 1), :], sem).start()

    def drain(buf, sem):
        def body(t, carry):
            for k in range(TOP_K):
                pltpu.make_async_copy(ys_ref.at[pl.ds(0, 1), :], buf.at[0, pl.ds(0, 1), :], sem).wait()
            return carry
        lax.fori_loop(0, n_tok, body, 0)

    @pl.when(i == 0)
    def _():
        def first(t, carry):
            gather(e_ref, rank_ref, t, buf_a, sems.at[0])
            return carry
        lax.fori_loop(0, n_tok, first, 0)

    acc_ref[...] = _swiglu_packed(f_ref[...], sg_ref, su_ref, sd_ref)
    half = f_ref.shape[1]

    def reduce_tile(cur, cur_sem, nxt, nxt_sem):
        drain(cur, cur_sem)

        def chunk(c, carry):
            base = pl.multiple_of(c * SUBLANES, SUBLANES)
            for t8 in range(SUBLANES):
                gather(e_next_ref, rank_next_ref, base + t8, nxt, nxt_sem)
            rows = pl.ds(base, SUBLANES)
            w = w_ref[rows, :]
            acc_lo, acc_hi = acc_ref[rows, :half], acc_ref[rows, half:]
            for k in range(TOP_K):
                lo, hi = _unpack_halves(cur[k, rows, :])
                acc_lo = acc_lo + w[:, k:k + 1] * lo
                acc_hi = acc_hi + w[:, k:k + 1] * hi
            acc = jnp.concatenate([acc_lo, acc_hi], axis=1)
            o_ref[rows, :] = _layer_norm(DEEPNORM_ALPHA * h_ref[rows, :] + gate_ref[...] * acc, lng_ref[...],
                                         lnb_ref[...])
            return carry

        lax.fori_loop(0, n_tok // SUBLANES, chunk, 0)

        @pl.when(i == pl.num_programs(0) - 1)
        def _():
            drain(nxt, nxt_sem)

    @pl.when(i % 2 == 0)
    def _():
        reduce_tile(buf_a, sems.at[0], buf_b, sems.at[1])

    @pl.when(i % 2 == 1)
    def _():
        reduce_tile(buf_b, sems.at[1], buf_a, sems.at[0])


def moe_combine_pallas(ys, e_flat, rank_flat, starts, w_sel, f, h, mods, dims, ln_g, ln_b, sg, su, sd):
    B, C, N = dims
    T, D = h.shape
    half = D // 2
    tm = MOE_TOKEN_TILE
    F = sg.shape[-1]
    n_tiles = T // tm
    mod_idx = _mod_index((C + N) // tm, C // tm, B)
    vec = lambda z: z.reshape(1, -1)
    flat = lambda idx: pl.BlockSpec((tm * TOP_K,), idx, memory_space=pltpu.SMEM)
    return pl.pallas_call(
        _combine_kernel,
        grid=(n_tiles,),
        in_specs=[
            flat(lambda i: (i,)), flat(lambda i: (i,)),
            flat(lambda i: (jnp.minimum(i + 1, n_tiles - 1),)), flat(lambda i: (jnp.minimum(i + 1, n_tiles - 1),)),
            pl.BlockSpec(memory_space=pltpu.SMEM),
            pl.BlockSpec((tm, TOP_K), lambda i: (i, 0)),
            pl.BlockSpec((tm, half), lambda i: (i, 0)),
            pl.BlockSpec((tm, D), lambda i: (i, 0)),
            _mod_spec(5, mod_idx), _full_spec((1, D)), _full_spec((1, D)),
            _full_spec((D, F)), _full_spec((D, F)), _full_spec((F, D)),
            pl.BlockSpec(memory_space=pl.ANY),
        ],
        out_specs=pl.BlockSpec((tm, D), lambda i: (i, 0)),
        out_shape=jax.ShapeDtypeStruct((T, D), F32),
        scratch_shapes=[pltpu.VMEM((TOP_K, tm, half), jnp.uint32), pltpu.VMEM((TOP_K, tm, half), jnp.uint32),
                        pltpu.VMEM((tm, D), F32), pltpu.SemaphoreType.DMA((2,))],
        compiler_params=_cparams(1),
        name="moe_combine",
    )(e_flat, rank_flat, e_flat, rank_flat, starts, w_sel, f, h, mods, vec(ln_g), vec(ln_b), sg, su, sd, ys)


REMOVED = -3e38


def _router_kernel(s_ref, bias_ref, e_ref, w_ref, rank_ref, cnt_ref, carry_ref):
    @pl.when(pl.program_id(0) == 0)
    def _():
        carry_ref[...] = jnp.zeros_like(carry_ref)

    s = s_ref[...]
    tm, n_exp = s.shape
    per_group = n_exp // N_GROUPS
    lane_i = lax.broadcasted_iota(jnp.int32, (tm, n_exp), 1)
    lane = lane_i.astype(F32)
    group_of_lane = lane_i // per_group
    big = float(n_exp)
    rmax = lambda z: jnp.max(z, axis=-1, keepdims=True)
    first_at = lambda z, m: jnp.min(jnp.where(z == m, lane, big), axis=-1, keepdims=True)

    grp = s + bias_ref[...]
    g_score = []
    for g in range(N_GROUPS):
        mg = jnp.where(group_of_lane == g, grp, REMOVED)
        m1 = rmax(mg)
        m2 = rmax(jnp.where(lane == first_at(mg, m1), REMOVED, mg))
        g_score.append(m1 + m2)
    choice = jnp.full_like(grp, NEG_INF)
    for g in range(N_GROUPS):
        ahead = jnp.zeros((tm, 1), F32)
        for g2 in range(N_GROUPS):
            if g2 != g:
                beats = (g_score[g2] > g_score[g]) | ((g_score[g2] == g_score[g]) & (g2 < g))
                ahead = ahead + beats.astype(F32)
        choice = jnp.where((group_of_lane == g) & (ahead < TOPK_GROUPS), grp, choice)

    col8 = lax.broadcasted_iota(jnp.int32, (tm, TOP_K), 1)
    e_out = jnp.zeros((tm, TOP_K), F32)
    w_out = jnp.zeros((tm, TOP_K), F32)
    picked = []
    onehot = jnp.zeros((tm, n_exp), F32)
    for k in range(TOP_K):
        idx = first_at(choice, rmax(choice))
        hit = lane == idx
        picked.append(hit)
        onehot = jnp.where(hit, 1.0, onehot)
        e_out = jnp.where(col8 == k, idx, e_out)
        w_out = jnp.where(col8 == k, jnp.sum(jnp.where(hit, s, 0.0), axis=-1, keepdims=True), w_out)
        choice = jnp.where(hit, REMOVED, choice)
    ri = lax.broadcasted_iota(jnp.int32, (tm, tm), 0)
    ci = lax.broadcasted_iota(jnp.int32, (tm, tm), 1)
    before = jnp.dot((ci < ri).astype(BF16), onehot.astype(BF16), preferred_element_type=F32) + carry_ref[0:1, :]
    rank = jnp.zeros((tm, TOP_K), F32)
    for k in range(TOP_K):
        rank = jnp.where(col8 == k, jnp.sum(jnp.where(picked[k], before, 0.0), axis=-1, keepdims=True), rank)
    total = carry_ref[0:1, :] + jnp.sum(onehot, axis=0, keepdims=True)
    carry_ref[...] = jnp.broadcast_to(total, carry_ref.shape)
    cnt_ref[...] = jnp.broadcast_to(total, cnt_ref.shape)
    e_ref[...] = e_out.astype(jnp.int32)
    w_ref[...] = w_out / jnp.sum(w_out, axis=-1, keepdims=True) * ROUTED_SCALE
    rank_ref[...] = rank.astype(jnp.int32)


def router_pallas(s, router_b):
    T, E = s.shape
    tm = ROW_TILE
    row8 = pl.BlockSpec((tm, TOP_K), lambda i: (i, 0))
    e_idx, w_sel, rank, cnt = pl.pallas_call(
        _router_kernel,
        grid=(T // tm,),
        in_specs=[pl.BlockSpec((tm, E), lambda i: (i, 0)), _full_spec((1, E))],
        out_specs=[row8, row8, row8, _full_spec((SUBLANES, E))],
        out_shape=[jax.ShapeDtypeStruct((T, TOP_K), jnp.int32), jax.ShapeDtypeStruct((T, TOP_K), F32),
                   jax.ShapeDtypeStruct((T, TOP_K), jnp.int32), jax.ShapeDtypeStruct((SUBLANES, E), F32)],
        scratch_shapes=[pltpu.VMEM((SUBLANES, E), F32)],
        compiler_params=_cparams(1),
        name="moe_router",
    )(s, router_b.astype(F32).reshape(1, E))
    return e_idx, w_sel, rank, cnt[0].astype(jnp.int32)


def moe_layer(f, s, h, mods, dims, ln_g, ln_b, router_b, layer, wg, wu, wd, sg, su, sd):
    T, D = f.shape
    E = s.shape[-1]
    e_idx, w_sel, rank, counts = router_pallas(s, router_b)
    n_asg = T * TOP_K
    assert n_asg % MOE_BLOCK == 0
    i32 = jnp.int32
    ends = jnp.cumsum(counts).astype(i32)
    starts = ends - counts
    e_flat, rank_flat = e_idx.reshape(-1), rank.reshape(-1)
    nb = n_asg // MOE_BLOCK
    first_blk = starts // MOE_BLOCK
    nblk = jnp.where(counts > 0, (ends - 1) // MOE_BLOCK - first_blk + 1, 0)
    item_ends = jnp.cumsum(nblk).astype(i32)
    item_starts = item_ends - nblk
    n_items = nb + E
    it = jnp.arange(n_items, dtype=i32)
    real = it < item_ends[-1]
    e_of = jnp.sum((item_ends[None, :] <= jnp.where(real, it, item_ends[-1] - 1)[:, None]).astype(i32), axis=1)
    is_e = e_of[:, None] == jnp.arange(E, dtype=i32)[None, :]
    pick = lambda tab: jnp.sum(jnp.where(is_e, tab[None, :], 0), axis=1)
    blk = jnp.where(real, pick(first_blk) + it - pick(item_starts), nb - 1).astype(i32)
    lo = jnp.where(real, jnp.maximum(pick(starts), blk * MOE_BLOCK), 0).astype(i32)
    hi = jnp.where(real, jnp.minimum(pick(ends), (blk + 1) * MOE_BLOCK), 0).astype(i32)
    first = (real & (blk != jnp.concatenate([jnp.full((1,), -1, i32), blk[:-1]]))).astype(i32)
    xs = moe_dispatch_pallas(f, e_flat, rank_flat, starts)
    ys = moe_experts_pallas(xs, (blk, e_of, lo, hi, first), layer, wg, wu, wd)
    return moe_combine_pallas(ys, e_flat, rank_flat, starts, w_sel, f, h, mods, dims, ln_g, ln_b,
                              sg.astype(BF16), su.astype(BF16), sd.astype(BF16))


ML_QK_WIDTH = ML_HEADS * ML_QK_DIM
ROPE_GROUP = ML_QK_DIM // 4
GATE_IN, GATE_FORGET = 0, 2 * ML_HEADS


def _log_sigmoid(x):
    return -_softplus(-x)


def _proj_odd_kernel(h_ref, shift_ref, scale_ref, wqk_ref, wv_ref, wo_ref, wg_ref, wgt_ref, gb_ref, gbt_ref,
                     cos_ref, sin_ref, q_ref, k_ref, v_ref, o_ref, g_ref, gt_ref):
    a16 = (h_ref[...] * (1.0 + scale_ref[...]) + shift_ref[...]).astype(BF16)
    qk = jnp.dot(a16, wqk_ref[...], preferred_element_type=F32)
    lane = lax.broadcasted_iota(jnp.int32, (1, ML_QK_WIDTH), 1)
    first_of_pair = (lane % (2 * ROPE_GROUP)) < ROPE_GROUP
    cos, sin = cos_ref[...], sin_ref[...]

    def rope(z):
        partner = jnp.where(first_of_pair, pltpu.roll(z, ML_QK_WIDTH - ROPE_GROUP, 1), pltpu.roll(z, ROPE_GROUP, 1))
        return z * cos + partner * sin

    q_ref[...] = rope(qk[:, :ML_QK_WIDTH] * ML_QK_DIM ** -0.5).astype(BF16)
    k_ref[...] = rope(qk[:, ML_QK_WIDTH:]).astype(BF16)
    v_ref[...] = jnp.dot(a16, wv_ref[...], preferred_element_type=F32).astype(BF16)
    o_ref[...] = jnp.dot(a16, wo_ref[...], preferred_element_type=F32)
    g = jnp.dot(a16, wg_ref[...], preferred_element_type=F32) + gb_ref[...]
    gl = lax.broadcasted_iota(jnp.int32, g.shape, 1)
    g_ref[...] = jnp.where((gl >= GATE_FORGET) & (gl < 2 * GATE_FORGET), _log_sigmoid(g), g)
    gt = lax.dot_general(wgt_ref[...], a16, NT_DIMS, preferred_element_type=F32) + gbt_ref[...]
    gs = lax.broadcasted_iota(jnp.int32, gt.shape, 0)
    gt_ref[...] = jnp.where((gs >= GATE_FORGET) & (gs < 2 * GATE_FORGET), _log_sigmoid(gt), gt)


def _rope_tables(C, N):
    t = jnp.arange(N)
    pos = jnp.stack([(t // GRID_W).astype(F32), (t % GRID_W).astype(F32)], 0)
    lane = jnp.arange(ML_QK_WIDTH) % ML_QK_DIM
    inv = ROPE_BASE ** (-(lane % ROPE_GROUP).astype(F32) / ROPE_GROUP)
    ang = pos[lane // (2 * ROPE_GROUP)].T * inv[None, :]
    sign = jnp.where((lane % (2 * ROPE_GROUP)) < ROPE_GROUP, -1.0, 1.0)
    cos = jnp.concatenate([jnp.ones((C, ML_QK_WIDTH), F32), jnp.cos(ang)], 0)
    sin = jnp.concatenate([jnp.zeros((C, ML_QK_WIDTH), F32), jnp.sin(ang) * sign], 0)
    return cos, sin


def proj_odd_pallas(h, mods, dims, w_in, gate_b):
    B, C, N = dims
    R, D = h.shape
    tm = ROW_TILE
    tpb = (C + N) // tm
    mod_idx = _mod_index(tpb, C // tm, B)
    o_qk, o_v, o_o = 2 * ML_QK_WIDTH, 2 * ML_QK_WIDTH + ML_WIDTH, 2 * ML_QK_WIDTH + 2 * ML_WIDTH
    n_gate = w_in.shape[1] - o_o
    w16 = w_in.astype(BF16)
    w_g = jnp.pad(w16[:, o_o:], ((0, 0), (0, LANES - n_gate)))
    gb = jnp.pad(gate_b.astype(F32).reshape(-1), (0, LANES - n_gate))
    cos, sin = _rope_tables(C, N)
    row = lambda width: pl.BlockSpec((tm, width), lambda i: (i, 0))
    seg = pl.BlockSpec((tm, ML_QK_WIDTH), lambda i: (i % tpb, 0))
    return pl.pallas_call(
        _proj_odd_kernel,
        grid=(R // tm,),
        in_specs=[row(D), _mod_spec(0, mod_idx), _mod_spec(1, mod_idx),
                  _full_spec((D, 2 * ML_QK_WIDTH)), _full_spec((D, ML_WIDTH)), _full_spec((D, ML_WIDTH)),
                  _full_spec((D, LANES)), _full_spec((LANES, D)), _full_spec((1, LANES)), _full_spec((LANES, 1)),
                  seg, seg],
        out_specs=[row(ML_QK_WIDTH), row(ML_QK_WIDTH), row(ML_WIDTH), row(ML_WIDTH), row(LANES),
                   pl.BlockSpec((LANES, tm), lambda i: (0, i))],
        out_shape=[jax.ShapeDtypeStruct((R, ML_QK_WIDTH), BF16), jax.ShapeDtypeStruct((R, ML_QK_WIDTH), BF16),
                   jax.ShapeDtypeStruct((R, ML_WIDTH), BF16), jax.ShapeDtypeStruct((R, ML_WIDTH), F32),
                   jax.ShapeDtypeStruct((R, LANES), F32), jax.ShapeDtypeStruct((LANES, R), F32)],
        compiler_params=_cparams(1),
        name="proj_odd",
    )(h, mods, mods, w16[:, :o_qk], w16[:, o_qk:o_v], w16[:, o_v:o_o], w_g, w_g.T, gb.reshape(1, LANES),
      gb.reshape(LANES, 1), cos, sin)


def _split3_bf16(x, axis):
    x1 = x.astype(BF16)
    r1 = x - x1.astype(F32)
    x2 = r1.astype(BF16)
    x3 = (r1 - x2.astype(F32)).astype(BF16)
    return jnp.concatenate([x1, x2, x3], axis=axis)


def _mlstm_kernel(q_ref, k_ref, v_ref, g_ref, gt_ref, h_ref, c_ref, n_ref, m_ref, *, reverse):
    @pl.when(pl.program_id(1) == 0)
    def _():
        c_ref[...] = jnp.zeros_like(c_ref)
        n_ref[...] = jnp.zeros_like(n_ref)
        m_ref[...] = jnp.zeros_like(m_ref)

    L = q_ref.shape[1]
    ti = lax.broadcasted_iota(jnp.int32, (L, L), 0)
    si = lax.broadcasted_iota(jnp.int32, (L, L), 1)
    seen = (si >= ti) if reverse else (si <= ti)
    g = g_ref[0]
    gt = gt_ref[...]
    b_cols3 = jnp.dot(seen.astype(BF16), _split3_bf16(g, 1), preferred_element_type=F32)
    b_cols = b_cols3[:, :LANES] + b_cols3[:, LANES:2 * LANES] + b_cols3[:, 2 * LANES:]
    b_rows3 = lax.dot_general(_split3_bf16(gt, 0), seen.astype(BF16), NT_DIMS, preferred_element_type=F32)
    b_rows = b_rows3[:LANES] + b_rows3[LANES:2 * LANES] + b_rows3[2 * LANES:]
    half = lax.broadcasted_iota(jnp.int32, (1, LANES), 1) // ML_QK_DIM
    row_half = lax.broadcasted_iota(jnp.int32, (LANES, 1), 0) // ML_QK_DIM
    d_off = ML_HEADS if reverse else 0
    tn = (((0,), (0,)), ((), ()))
    heads = [(hd // 2, hd % 2) for hd in range(ML_HEADS)]
    pair = lambda ref, p: ref[0, :, p * LANES:(p + 1) * LANES]
    value = lambda hd: v_ref[0, :, hd * ML_V_DIM:(hd + 1) * ML_V_DIM]

    decay = []
    for hd, (p, h2) in enumerate(heads):
        gi, gf = GATE_IN + d_off + hd, GATE_FORGET + d_off + hd
        ig_col, ig_row = g[:, gi:gi + 1], gt[gi:gi + 1, :]
        b_col, b_row = b_cols[:, gf:gf + 1], b_rows[gf:gf + 1, :]
        m0 = m_ref[p][:, h2 * ML_QK_DIM:h2 * ML_QK_DIM + 1]
        dlog = jnp.where(seen, b_col - b_row + ig_row, NEG_INF)
        inter = b_col + m0
        m_t = jnp.maximum(jnp.max(dlog, axis=-1, keepdims=True), inter)
        b_end = jnp.sum(g[:, gf:gf + 1], axis=0, keepdims=True)
        g_col = b_end - b_col + ig_col
        m_chunk = jnp.max(g_col, axis=0, keepdims=True)
        m_new = jnp.maximum(b_end + m0, m_chunk)
        decay.append(dict(dw=jnp.exp(dlog - m_t), iw=jnp.exp(inter - m_t), floor=jnp.exp(-m_t),
                          kw=jnp.exp(g_col - m_chunk), m_new=m_new,
                          fa=jnp.exp(b_end + m0 - m_new), fb=jnp.exp(m_chunk - m_new)))

    prods = []
    for p, h2 in heads:
        qm = jnp.where(half == h2, pair(q_ref, p), jnp.zeros((L, LANES), BF16))
        qk = lax.dot_general(qm, pair(k_ref, p), NT_DIMS, preferred_element_type=F32)
        qc = jnp.dot(qm, c_ref[p].astype(BF16), preferred_element_type=F32)
        qn = jnp.sum(qm.astype(F32) * n_ref[p], axis=-1, keepdims=True)
        prods.append((qk, qc, qn))

    for hd, ((qk, qc, qn), dc) in enumerate(zip(prods, decay)):
        sc = qk * dc['dw']
        num = jnp.dot(sc.astype(BF16), value(hd), preferred_element_type=F32) + dc['iw'] * qc
        den = jnp.sum(sc, axis=-1, keepdims=True) + dc['iw'] * qn
        h_ref[0, :, hd * ML_V_DIM:(hd + 1) * ML_V_DIM] = num / jnp.maximum(jnp.abs(den), dc['floor'])

    for p in range(ML_HEADS // 2):
        c_old, n_old, m_old = c_ref[p], n_ref[p], m_ref[p]
        c_new, n_new, m_new_pair = c_old, n_old, m_old
        for h2 in range(2):
            hd = 2 * p + h2
            dc = decay[hd]
            kw = jnp.where(half == h2, pair(k_ref, p), jnp.zeros((L, LANES), BF16)).astype(F32) * dc['kw']
            kv = lax.dot_general(kw.astype(BF16), value(hd), tn, preferred_element_type=F32)
            c_new = jnp.where(row_half == h2, dc['fa'] * c_old + dc['fb'] * kv, c_new)
            n_new = jnp.where(half == h2, dc['fa'] * n_old + dc['fb'] * jnp.sum(kw, axis=0, keepdims=True), n_new)
            m_new_pair = jnp.where(half == h2, dc['m_new'], m_new_pair)
        c_ref[p] = c_new
        n_ref[p] = n_new
        m_ref[p] = m_new_pair


def mlstm_pallas(q, k, v, g, gt, dims, reverse):
    B, C, N = dims
    S = C + N
    L = ML_CHUNK
    assert C % L == 0 and N % L == 0
    n_ctx, n_all = C // L, S // L
    if reverse:
        chunk = lambda j: jnp.where(j < n_ctx, n_ctx - 1 - j, n_all - 1 - (j - n_ctx))
    else:
        chunk = lambda j: j
    blk = lambda width: pl.BlockSpec((1, L, width), lambda b, j: (b, chunk(j), 0))
    n_pair = ML_HEADS // 2
    out = pl.pallas_call(
        functools.partial(_mlstm_kernel, reverse=reverse),
        grid=(B, n_all),
        in_specs=[blk(ML_QK_WIDTH), blk(ML_QK_WIDTH), blk(ML_WIDTH), blk(LANES),
                  pl.BlockSpec((LANES, L), lambda b, j: (0, b * n_all + chunk(j)))],
        out_specs=blk(ML_WIDTH),
        out_shape=jax.ShapeDtypeStruct((B, S, ML_WIDTH), F32),
        scratch_shapes=[pltpu.VMEM((n_pair, LANES, ML_V_DIM), F32), pltpu.VMEM((n_pair, 1, LANES), F32),
                        pltpu.VMEM((n_pair, 1, LANES), F32)],
        compiler_params=_cparams(2),
        name="mlstm_bwd" if reverse else "mlstm_fwd",
    )(q.reshape(B, S, -1), k.reshape(B, S, -1), v.reshape(B, S, -1), g.reshape(B, S, -1), gt)
    return out.reshape(B * S, ML_WIDTH)


def _odd_out_kernel(hf_ref, hb_ref, o_ref, h_ref, gate_ref, shift_ref, scale_ref, ng_ref, wout_ref,
                    lng_ref, lnb_ref, router_ref, h_out_ref, f_ref, s_ref):
    hs = hf_ref[...] + hb_ref[...]
    parts = []
    for hd in range(ML_HEADS):
        x = hs[:, hd * ML_V_DIM:(hd + 1) * ML_V_DIM]
        parts.append(x * lax.rsqrt(jnp.mean(x * x, axis=-1, keepdims=True) + ML_NORM_EPS))
    hn = jnp.concatenate(parts, axis=-1) * ng_ref[...] * jax.nn.sigmoid(o_ref[...])
    y = jnp.dot(hn.astype(BF16), wout_ref[...], preferred_element_type=F32)
    _mixer_tail(h_ref[...], y, gate_ref[...], lng_ref[...], lnb_ref[...], shift_ref[...], scale_ref[...],
                router_ref[...], h_out_ref, f_ref, s_ref)


def odd_out_pallas(h_f, h_b, o, h, mods, dims, norm_g, w_out, ln_g, ln_b, router_w):
    B, C, N = dims
    R, D = h.shape
    E = router_w.shape[1]
    tm = ROW_TILE
    mod_idx = _mod_index((C + N) // tm, C // tm, B)
    row = lambda width: pl.BlockSpec((tm, width), lambda i: (i, 0))
    vec = lambda z: z.reshape(1, -1)
    out_specs, out_shape = _tail_specs(R, D, E, tm)
    return pl.pallas_call(
        _odd_out_kernel,
        grid=(R // tm,),
        in_specs=[row(ML_WIDTH), row(ML_WIDTH), row(ML_WIDTH), row(D),
                  _mod_spec(2, mod_idx), _mod_spec(3, mod_idx), _mod_spec(4, mod_idx),
                  _full_spec((1, ML_WIDTH)), _full_spec((ML_WIDTH, D)),
                  _full_spec((1, D)), _full_spec((1, D)), _full_spec((D, E))],
        out_specs=out_specs,
        out_shape=out_shape,
        compiler_params=_cparams(1),
        name="odd_out",
    )(h_f, h_b, o, h, mods, mods, mods, vec(norm_g), w_out.astype(BF16), vec(ln_g), vec(ln_b), router_w)


def _offsets(layout, prefix=''):
    offs, o = {}, 0
    for name, width in layout:
        if name.startswith(prefix):
            offs[name] = (o, width)
            o += width
    return offs


def project(h, w, layout, names):
    offs = _offsets(layout)
    if len(names) == len(layout):
        y = jnp.einsum('btd,de->bte', h, w)
        return {n: y[..., offs[n][0]:offs[n][0] + offs[n][1]] for n in names}
    return {n: jnp.einsum('btd,de->bte', h, w[:, offs[n][0]:offs[n][0] + offs[n][1]]) for n in names}


def axial_rope(z):
    T, dh = z.shape[1], z.shape[-1]
    half = dh // 2
    nf = half // 2
    t = jnp.arange(T)
    row = (t // GRID_W).astype(F32)
    col = (t % GRID_W).astype(F32)
    inv = ROPE_BASE ** (-jnp.arange(nf, dtype=F32) / nf)

    def rot(u, pos):
        ang = pos[:, None] * inv[None, :]
        cos = jnp.cos(ang)[None, :, None, :]
        sin = jnp.sin(ang)[None, :, None, :]
        u1, u2 = u[..., :nf], u[..., nf:]
        return jnp.concatenate([u1 * cos - u2 * sin, u1 * sin + u2 * cos], -1)

    return jnp.concatenate([rot(z[..., :half], row), rot(z[..., half:], col)], -1).astype(z.dtype)


def ml_prep(t, gate_b, rope, need_q):
    B, T = t['ml_k'].shape[:2]
    heads = lambda z, dh: z.reshape(B, T, ML_HEADS, dh).astype(F32)
    k = heads(t['ml_k'], ML_QK_DIM)
    q = heads(t['ml_q'], ML_QK_DIM) * ML_QK_DIM ** -0.5 if need_q else None
    if rope:
        k = axial_rope(k)
        q = axial_rope(q)
    v = heads(t['ml_v'], ML_V_DIM)
    gb = gate_b.astype(F32)
    bht = lambda z: z.astype(F32).transpose(0, 2, 1)
    ig = (bht(t['ml_if'] + gb[0]), bht(t['ml_ib'] + gb[1]))
    lf = (jax.nn.log_sigmoid(bht(t['ml_ff'] + gb[2])), jax.nn.log_sigmoid(bht(t['ml_fb'] + gb[3])))
    bhtd = lambda z: None if z is None else z.transpose(0, 2, 1, 3)
    return bhtd(q), bhtd(k), bhtd(v), ig, lf


def ml_chunk_states(k, v, ig, lf, state0):
    B, H, T, dk = k.shape
    dv = v.shape[-1]
    L = min(ML_CHUNK, T)
    nc = T // L
    kc = k.reshape(B, H, nc, L, dk)
    vc = v.reshape(B, H, nc, L, dv)
    b = jnp.cumsum(lf.reshape(B, H, nc, L), -1)
    b_end = b[..., -1]
    g = b_end[..., None] - b + ig.reshape(B, H, nc, L)
    m_chunk = g.max(-1)
    wgt = jnp.exp(g - m_chunk[..., None])
    kv = jnp.einsum('bhnl,bhnlk,bhnlv->bhnkv', wgt, kc, vc)
    ks = jnp.einsum('bhnl,bhnlk->bhnk', wgt, kc)

    def step(state, inp):
        c_mem, n_mem, m = state
        be, mc, kv_n, ks_n = inp
        m_new = jnp.maximum(be + m, mc)
        fa = jnp.exp(be + m - m_new)
        fb = jnp.exp(mc - m_new)
        c_new = fa[..., None, None] * c_mem + fb[..., None, None] * kv_n
        n_new = fa[..., None] * n_mem + fb[..., None] * ks_n
        return (c_new, n_new, m_new), state

    xs = tuple(jnp.moveaxis(z, 2, 0) for z in (b_end, m_chunk, kv, ks))
    final, starts = lax.scan(step, state0, xs)
    return tuple(jnp.moveaxis(z, 0, 2) for z in starts), final


def ml_chunk_outputs(q, k, v, ig, lf, starts):
    B, H, T, dk = q.shape
    dv = v.shape[-1]
    L = min(ML_CHUNK, T)
    nc = T // L
    qc = q.reshape(B, H, nc, L, dk)
    kc = k.reshape(B, H, nc, L, dk)
    vc = v.reshape(B, H, nc, L, dv)
    b = jnp.cumsum(lf.reshape(B, H, nc, L), -1)
    c0, n0, m0 = starts
    causal = jnp.tril(jnp.ones((L, L), bool))
    dlog = jnp.where(causal, b[..., :, None] - b[..., None, :] + ig.reshape(B, H, nc, L)[..., None, :], NEG_INF)
    inter = b + m0[..., None]
    m = jnp.maximum(dlog.max(-1), inter)
    dw = jnp.exp(dlog - m[..., None])
    iw = jnp.exp(inter - m)
    s = jnp.einsum('bhntd,bhnsd->bhnts', qc, kc) * dw
    num = jnp.einsum('bhnts,bhnsv->bhntv', s, vc) + iw[..., None] * jnp.einsum('bhntd,bhndv->bhntv', qc, c0)
    den = s.sum(-1) + iw * jnp.einsum('bhntd,bhnd->bhnt', qc, n0)
    h = num / jnp.maximum(jnp.abs(den), jnp.exp(-m))[..., None]
    return h.reshape(B, H, T, dv)


def ml_readout(h, o, norm_g):
    B, H, T, dv = h.shape
    hn = h * lax.rsqrt(jnp.mean(h * h, -1, keepdims=True) + ML_NORM_EPS)
    hn = hn.transpose(0, 2, 1, 3).reshape(B, T, H * dv) * norm_g
    return hn * jax.nn.sigmoid(o.astype(F32))


def odd_mixer(a_lat, a_ctx, w_in, w_out, gate_b, norm_g, need_ctx):
    names = tuple(n for n, _ in ODD_LAYOUT)
    t_lat = project(a_lat, w_in, ODD_LAYOUT, names)
    t_ctx = project(a_ctx, w_in, ODD_LAYOUT, names if need_ctx else ODD_CTX_STATE_COLS)
    q_l, k_l, v_l, ig_l, lf_l = ml_prep(t_lat, gate_b, True, True)
    q_c, k_c, v_c, ig_c, lf_c = ml_prep(t_ctx, gate_b, False, need_ctx)
    B = a_lat.shape[0]
    zero = (jnp.zeros((B, ML_HEADS, ML_QK_DIM, ML_V_DIM), F32),
            jnp.zeros((B, ML_HEADS, ML_QK_DIM), F32),
            jnp.zeros((B, ML_HEADS), F32))
    h_l, h_c = [], []
    for d in range(2):
        f = (lambda z: jnp.flip(z, 2)) if d == 1 else (lambda z: z)
        starts_c, final_c = ml_chunk_states(f(k_c), f(v_c), f(ig_c[d]), f(lf_c[d]), zero)
        starts_l, _ = ml_chunk_states(f(k_l), f(v_l), f(ig_l[d]), f(lf_l[d]), final_c)
        h_l.append(f(ml_chunk_outputs(f(q_l), f(k_l), f(v_l), f(ig_l[d]), f(lf_l[d]), starts_l)))
        if need_ctx:
            h_c.append(f(ml_chunk_outputs(f(q_c), f(k_c), f(v_c), f(ig_c[d]), f(lf_c[d]), starts_c)))
    y_lat = jnp.einsum('btd,de->bte', ml_readout(h_l[0] + h_l[1], t_lat['ml_o'], norm_g), w_out).astype(a_lat.dtype)
    if not need_ctx:
        return y_lat, None
    y_ctx = jnp.einsum('btd,de->bte', ml_readout(h_c[0] + h_c[1], t_ctx['ml_o'], norm_g), w_out).astype(a_ctx.dtype)
    return y_lat, y_ctx


def kernel(x, c, ctx, c_ctx, ada_w, ada_b, ln_g, ln_b, ev_w_in, ev_w_out, na_rpb, rw_mu, rw_w0, rw_w_up,
           rw_a0, rw_a_up, rw_g_up, rw_k_k, rw_k_a, rw_r_k, rw_gn_g, rw_gn_b, od_w_in, od_w_out, ml_gate_b,
           ml_norm_g, moe_router, moe_bias, moe_w_gate, moe_w_up, moe_w_down, sh_w_gate, sh_w_up, sh_w_down):
    B, N, D = x.shape
    C = ctx.shape[1]
    S = C + N
    dims = (B, C, N)
    assert C % ROW_TILE == 0 and N % ROW_TILE == 0 and B + 1 <= SUBLANES
    h = jnp.concatenate([ctx, x], axis=1).reshape(B * S, D)
    cond = jnp.zeros((SUBLANES, D), F32).at[:B].set(c).at[B].set(c_ctx)
    for l in range(DEPTH):
        mods = ada_mods_pallas(cond, ada_w[l], ada_b[l])
        if l % 2 == 0:
            e = l // 2
            (q, k, v, dec_f, dec_b, beta_f, beta_b, kd_f, kd_b, nkk, rv, rr, glow) = proj_even_pallas(
                h, mods, dims, ev_w_in[e], rw_mu[e], rw_w0[e], rw_w_up[e], rw_a0[e], rw_a_up[e],
                rw_k_k[e], rw_k_a[e])
            y_f, y_b = rwkv_scan_pallas(dec_f, beta_f, kd_f, dec_b, beta_b, kd_b, nkk, rv, rr, dims)
            na = attention_pallas(q, k, v, na_rpb[e], dims)
            h, f, s = even_out_pallas(na, y_f, y_b, rr, rv, kd_f, kd_b, glow, h, mods, dims, rw_g_up[e],
                                      rw_r_k[e], rw_gn_g[e], rw_gn_b[e], ev_w_out[e], ln_g[l, 0], ln_b[l, 0],
                                      moe_router[l])
        else:
            o = l // 2
            q, k, v, og, g, gt = proj_odd_pallas(h, mods, dims, od_w_in[o], ml_gate_b[o])
            h_f = mlstm_pallas(q, k, v, g, gt, dims, False)
            h_b = mlstm_pallas(q, k, v, g, gt, dims, True)
            h, f, s = odd_out_pallas(h_f, h_b, og, h, mods, dims, ml_norm_g[o], od_w_out[o], ln_g[l, 0],
                                     ln_b[l, 0], moe_router[l])
        h = moe_layer(f, s, h, mods, dims, ln_g[l, 1], ln_b[l, 1], moe_bias[l], l, moe_w_gate, moe_w_up,
                      moe_w_down, sh_w_gate[l], sh_w_up[l], sh_w_down[l])
    return h.reshape(B, S, D)[:, C:]
```

```python
import functools

import jax
import jax.numpy as jnp
import numpy as np
from jax import lax
from jax.experimental import pallas as pl
from jax.experimental.pallas import tpu as pltpu

D_MODEL = 1024
DEPTH = 2
GRID_W = 64

DEEPNORM_ALPHA = (2.0 * DEPTH) ** 0.25
LN_EPS = 1e-5
NEG_INF = -1e30
F32 = jnp.float32
BF16 = jnp.bfloat16

NA_HEAD_DIM = 64
NA_WIDTH = D_MODEL // 2
NA_HEADS = NA_WIDTH // NA_HEAD_DIM
NA_WIN_ROWS = 8
NA_WIN_COLS = 16
NA_SCALE = NA_HEAD_DIM ** -0.5

RW_HEAD_DIM = 64
RW_WIDTH = D_MODEL // 2
RW_HEADS = RW_WIDTH // RW_HEAD_DIM
RW_DECAY_LORA = 32
RW_AAA_LORA = 32
RW_GATE_LORA = 96
RW_GN_EPS = 64e-5

ML_HEADS = 8
ML_V_DIM = D_MODEL // ML_HEADS
ML_QK_DIM = ML_V_DIM // 2
ML_WIDTH = ML_HEADS * ML_V_DIM
ML_CHUNK = 128
ML_NORM_EPS = 1e-6
ROPE_BASE = 10000.0

N_EXPERTS = 256
TOP_K = 8
N_GROUPS = 8
TOPK_GROUPS = 4
ROUTED_SCALE = 2.5
MOE_BLOCK = 256

ODD_LAYOUT = (
    ('ml_q', ML_HEADS * ML_QK_DIM), ('ml_k', ML_HEADS * ML_QK_DIM),
    ('ml_v', ML_WIDTH), ('ml_o', ML_WIDTH),
    ('ml_if', ML_HEADS), ('ml_ib', ML_HEADS), ('ml_ff', ML_HEADS), ('ml_fb', ML_HEADS),
)
ODD_CTX_STATE_COLS = ('ml_k', 'ml_v', 'ml_if', 'ml_ib', 'ml_ff', 'ml_fb')

SUBLANES = 8
LANES = 128
VMEM_LIMIT_BYTES = 56 * 1024 * 1024

ROW_TILE = 256
N_MODS = 6
RW_COLS = 3 * RW_WIDTH + 2 * LANES
NT_DIMS = (((1,), (1,)), ((), ()))


def _cparams(n_axes):
    return pltpu.CompilerParams(dimension_semantics=("arbitrary",) * n_axes, vmem_limit_bytes=VMEM_LIMIT_BYTES)


def _full_spec(shape):
    return pl.BlockSpec(shape, lambda *_: (0,) * len(shape))


def _split_bf16(x):
    hi = x.astype(BF16)
    lo = (x - hi.astype(F32)).astype(BF16)
    return jnp.concatenate([hi, lo], axis=-1)


def _block_ones(n_rows, n_cols, seg):
    row = lax.broadcasted_iota(jnp.int32, (n_rows, n_cols), 0)
    col = lax.broadcasted_iota(jnp.int32, (n_rows, n_cols), 1)
    return (((row % n_cols) // seg) == (col // seg)).astype(BF16)


def _seg_sum(x, ones2):
    return jnp.dot(_split_bf16(x), ones2, preferred_element_type=F32)


def _mod_index(tiles_per_batch, ctx_tiles, n_batch):
    def idx(i):
        return jnp.where(i % tiles_per_batch < ctx_tiles, n_batch, i // tiles_per_batch)
    return idx


def _mod_spec(chunk, mod_idx):
    return pl.BlockSpec((None, None, 1, D_MODEL), lambda i: (mod_idx(i), chunk, 0, 0))


def _ada_kernel(c_ref, w_ref, b_ref, o_ref):
    c = c_ref[...]
    x = (c * jax.nn.sigmoid(c)).astype(BF16)
    o_ref[...] = jnp.dot(x, w_ref[...].astype(BF16), preferred_element_type=F32) + b_ref[...]


def ada_mods_pallas(cond, w, b):
    n, D = cond.shape
    n_out = w.shape[1]
    tn = 512
    out = pl.pallas_call(
        _ada_kernel,
        grid=(n_out // tn,),
        in_specs=[_full_spec((n, D)), pl.BlockSpec((D, tn), lambda j: (0, j)), pl.BlockSpec((1, tn), lambda j: (0, j))],
        out_specs=pl.BlockSpec((n, tn), lambda j: (0, j)),
        out_shape=jax.ShapeDtypeStruct((n, n_out), F32),
        compiler_params=_cparams(1),
        name="ada_mods",
    )(cond, w, b.reshape(1, n_out))
    return out.reshape(n, N_MODS, 1, D)


def _softplus(x):
    return jnp.maximum(x, 0.0) + jnp.log(1.0 + jnp.exp(-jnp.abs(x)))


def _proj_even_kernel(h_ref, hp_ref, hn_ref, shift_ref, scale_ref, wna_ref, wrw_ref, mu_ref, ones_ref,
                      kk_ref, ka_ref, w0_ref, a0_ref, wup_ref, aup_ref,
                      q_ref, k_ref, v_ref, dec_f_ref, dec_b_ref, beta_f_ref, beta_b_ref, kd_f_ref, kd_b_ref,
                      nkk_ref, rv_ref, rr_ref, glow_ref, *, tiles_per_batch, ctx_tiles):
    i = pl.program_id(0)
    j = i % tiles_per_batch
    first = (j == 0) | (j == ctx_tiles)
    last = (j == ctx_tiles - 1) | (j == tiles_per_batch - 1)
    tm = h_ref.shape[0]
    gain = 1.0 + scale_ref[...]
    shift = shift_ref[...]
    a = h_ref[...] * gain + shift
    a_prev = jnp.where(first, 0.0, hp_ref[SUBLANES - 1:SUBLANES, :] * gain + shift)
    a_next = jnp.where(last, 0.0, hn_ref[0:1, :] * gain + shift)
    rid = lax.broadcasted_iota(jnp.int32, (tm, 1), 0)
    prev = jnp.where(rid == 0, a_prev, pltpu.roll(a, 1, 0))
    nxt = jnp.where(rid == tm - 1, a_next, pltpu.roll(a, tm - 1, 0))
    a16 = a.astype(BF16)
    nb16 = (0.5 * (prev + nxt)).astype(BF16)

    na = jnp.dot(a16, wna_ref[...], preferred_element_type=F32)
    q_ref[...] = (na[:, :NA_WIDTH] * NA_SCALE).astype(BF16)
    k_ref[...] = na[:, NA_WIDTH:2 * NA_WIDTH].astype(BF16)
    v_ref[...] = na[:, 2 * NA_WIDTH:].astype(BF16)

    pa = jnp.dot(a16, wrw_ref[...], preferred_element_type=F32)
    pn = jnp.dot(nb16, wrw_ref[...], preferred_element_type=F32)
    t = pa + mu_ref[...] * (pn - pa)
    r = t[:, :RW_WIDTH]
    k = t[:, RW_WIDTH:2 * RW_WIDTH]
    lora = t[:, 3 * RW_WIDTH:3 * RW_WIDTH + LANES]
    rr_ref[...] = r
    rv_ref[...] = t[:, 2 * RW_WIDTH:3 * RW_WIDTH]
    glow_ref[...] = t[:, 3 * RW_WIDTH + LANES:]

    kk = k * kk_ref[...]
    norm = jnp.sqrt(_seg_sum(kk * kk, ones_ref[...]))
    kk = kk / jnp.maximum(norm, 1e-12)
    nkk_ref[...] = -kk
    lora_t = jnp.tanh(lora).astype(BF16)
    lora16 = lora.astype(BF16)
    outs = ((dec_f_ref, beta_f_ref, kd_f_ref), (dec_b_ref, beta_b_ref, kd_b_ref))
    for d in range(2):
        w_log = -_softplus(-(w0_ref[d:d + 1, :] + jnp.dot(lora_t, wup_ref[d], preferred_element_type=F32))) - 0.5
        a_gate = jax.nn.sigmoid(a0_ref[d:d + 1, :] + jnp.dot(lora16, aup_ref[d], preferred_element_type=F32))
        outs[d][0][...] = jnp.exp(-jnp.exp(w_log))
        outs[d][1][...] = kk * a_gate
        outs[d][2][...] = k * (1.0 + (a_gate - 1.0) * ka_ref[...])


def proj_even_pallas(h, mods, dims, w_in, mu, w0, w_up, a0, a_up, k_k, k_a):
    B, C, N = dims
    R, D = h.shape
    tm = ROW_TILE
    tpb, ctx_tiles = (C + N) // tm, C // tm
    mod_idx = _mod_index(tpb, ctx_tiles, B)
    w_na = w_in[:, :3 * NA_WIDTH].astype(BF16)
    n_rw = w_in.shape[1] - 3 * NA_WIDTH
    w_rw = jnp.pad(w_in[:, 3 * NA_WIDTH:], ((0, 0), (0, RW_COLS - n_rw))).astype(BF16)
    mu_p = jnp.pad(mu, (0, RW_COLS - n_rw)).reshape(1, RW_COLS)
    ones2 = _block_ones(2 * RW_WIDTH, RW_WIDTH, RW_HEAD_DIM)
    lr = RW_DECAY_LORA

    def pad_up(m, first_row):
        out = jnp.zeros((2, LANES, RW_WIDTH), F32)
        for d in range(2):
            out = out.at[d, first_row + d * lr:first_row + (d + 1) * lr].set(m[d])
        return out.astype(BF16)

    row = lambda width: pl.BlockSpec((tm, width), lambda i: (i, 0))
    hb = tm // SUBLANES
    n_hb = R // SUBLANES
    wide = jax.ShapeDtypeStruct((R, RW_WIDTH), F32)
    half = jax.ShapeDtypeStruct((R, NA_WIDTH), BF16)
    return pl.pallas_call(
        functools.partial(_proj_even_kernel, tiles_per_batch=tpb, ctx_tiles=ctx_tiles),
        grid=(R // tm,),
        in_specs=[
            row(D),
            pl.BlockSpec((SUBLANES, D), lambda i: (jnp.maximum(i * hb - 1, 0), 0)),
            pl.BlockSpec((SUBLANES, D), lambda i: (jnp.minimum((i + 1) * hb, n_hb - 1), 0)),
            _mod_spec(0, mod_idx), _mod_spec(1, mod_idx),
            _full_spec((D, 3 * NA_WIDTH)), _full_spec((D, RW_COLS)), _full_spec((1, RW_COLS)),
            _full_spec((2 * RW_WIDTH, RW_WIDTH)),
            _full_spec((1, RW_WIDTH)), _full_spec((1, RW_WIDTH)),
            _full_spec((2, RW_WIDTH)), _full_spec((2, RW_WIDTH)),
            _full_spec((2, LANES, RW_WIDTH)), _full_spec((2, LANES, RW_WIDTH)),
        ],
        out_specs=[row(NA_WIDTH)] * 3 + [row(RW_WIDTH)] * 9 + [row(LANES)],
        out_shape=[half] * 3 + [wide] * 9 + [jax.ShapeDtypeStruct((R, LANES), F32)],
        compiler_params=_cparams(1),
        name="proj_even",
    )(h, h, h, mods, mods, w_na, w_rw, mu_p, ones2, k_k.reshape(1, -1), k_a.reshape(1, -1), w0, a0,
      pad_up(w_up, 0), pad_up(a_up, 2 * lr))


RW_SCAN_TIME = 256


def _rwkv_scan_kernel(wf_ref, bf_ref, kf_ref, nf_ref, vf_ref, rf_ref,
                      wb_ref, bb_ref, kb_ref, nb_ref, vb_ref, rb_ref, yf_ref, yb_ref, s_ref):
    @pl.when(pl.program_id(0) == 0)
    def _():
        s_ref[...] = jnp.zeros_like(s_ref)

    n_batch, n_time, width = wf_ref.shape
    n_pair = width // LANES
    n_dir_chain = n_batch * n_pair
    n_chain = 2 * n_dir_chain
    rows_all = n_chain * RW_HEAD_DIM
    ones = _block_ones(LANES, LANES, RW_HEAD_DIM)
    vi = lax.broadcasted_iota(jnp.int32, (1, RW_HEAD_DIM, LANES), 1)
    li = lax.broadcasted_iota(jnp.int32, (1, RW_HEAD_DIM, LANES), 2)
    diag = (li % RW_HEAD_DIM) == vi
    n_sub = n_time // SUBLANES

    def seg(x):
        out = jnp.dot(x.reshape(rows_all, LANES).astype(BF16), ones, preferred_element_type=F32)
        return out.reshape(n_chain, RW_HEAD_DIM, LANES)

    def chains(ref, rows):
        x = ref[:, rows, :]
        return [x[b, :, p * LANES:(p + 1) * LANES] for b in range(n_batch) for p in range(n_pair)]

    def sub(i, carry):
        rows_f = pl.ds(pl.multiple_of(i * SUBLANES, SUBLANES), SUBLANES)
        rows_b = pl.ds(pl.multiple_of((n_sub - 1 - i) * SUBLANES, SUBLANES), SUBLANES)
        load = lambda f_ref, b_ref: (jnp.stack(chains(f_ref, rows_f)), jnp.stack(chains(b_ref, rows_b)))
        w8, beta8, kd8 = load(wf_ref, wb_ref), load(bf_ref, bb_ref), load(kf_ref, kb_ref)
        nkk8, v8, r8 = load(nf_ref, nb_ref), load(vf_ref, vb_ref), load(rf_ref, rb_ref)

        def at(pair, t):
            tb = SUBLANES - 1 - t
            return jnp.concatenate([pair[0][:, t:t + 1, :], pair[1][:, tb:tb + 1, :]], axis=0)

        s = s_ref[...]
        rows = []
        value_col = lambda t: seg(jnp.where(diag, at(v8, t), 0.0))
        read_out = lambda sr: jnp.sum(jnp.where(diag, seg(sr), 0.0), axis=1, keepdims=True)
        vcol_next, pending = value_col(0), None
        for t in range(SUBLANES):
            sa = seg(s * at(nkk8, t))
            vcol = vcol_next
            if t + 1 < SUBLANES:
                vcol_next = value_col(t + 1)
            if pending is not None:
                rows.append(read_out(pending))
            s = s * at(w8, t) + sa * at(beta8, t) + vcol * at(kd8, t)
            pending = s * at(r8, t)
        rows.append(read_out(pending))
        s_ref[...] = s
        y_f = jnp.concatenate([row[:n_dir_chain] for row in rows], axis=1)
        y_b = jnp.concatenate([row[n_dir_chain:] for row in rows[::-1]], axis=1)
        for b in range(n_batch):
            for p in range(n_pair):
                c = b * n_pair + p
                yf_ref[b, rows_f, p * LANES:(p + 1) * LANES] = y_f[c]
                yb_ref[b, rows_b, p * LANES:(p + 1) * LANES] = y_b[c]
        return carry

    lax.fori_loop(0, n_sub, sub, 0)


def rwkv_scan_pallas(dec_f, beta_f, kd_f, dec_b, beta_b, kd_b, nkk, v, r, dims):
    B, C, N = dims
    S = C + N
    tc = RW_SCAN_TIME
    assert C % tc == 0 and N % tc == 0
    n_ctx, n_all = C // tc, S // tc
    as3 = lambda z: z.reshape(B, S, RW_WIDTH)
    fwd = pl.BlockSpec((B, tc, RW_WIDTH), lambda j: (0, j, 0))
    bwd = pl.BlockSpec((B, tc, RW_WIDTH),
                       lambda j: (0, jnp.where(j < n_ctx, n_ctx - 1 - j, n_all - 1 - (j - n_ctx)), 0))
    out = jax.ShapeDtypeStruct((B, S, RW_WIDTH), F32)
    y_f, y_b = pl.pallas_call(
        _rwkv_scan_kernel,
        grid=(n_all,),
        in_specs=[fwd] * 6 + [bwd] * 6,
        out_specs=[fwd, bwd],
        out_shape=[out, out],
        scratch_shapes=[pltpu.VMEM((2 * B * (RW_WIDTH // LANES), RW_HEAD_DIM, LANES), F32)],
        compiler_params=_cparams(1),
        name="rwkv_scan",
    )(as3(dec_f), as3(beta_f), as3(kd_f), as3(nkk), as3(v), as3(r),
      as3(dec_b), as3(beta_b), as3(kd_b), as3(nkk), as3(v), as3(r))
    return y_f.reshape(B * S, RW_WIDTH), y_b.reshape(B * S, RW_WIDTH)


NA_BAND = NA_WIN_ROWS * GRID_W


def _na_row_start(j, ctx_blocks, n_rows):
    r = jnp.maximum(j - ctx_blocks, 0)
    return r, jnp.clip(r - NA_WIN_ROWS // 2, 0, n_rows - NA_WIN_ROWS)


def _na_kernel(q_ref, k_ref, v_ref, bias_ref, o_ref, *, n_ctx):
    j = pl.program_id(1)
    ctx_blocks = n_ctx // GRID_W
    n_rows = pl.num_programs(1) - ctx_blocks
    _, row_start = _na_row_start(j, ctx_blocks, n_rows)
    start = pl.multiple_of(n_ctx + row_start * GRID_W, GRID_W)
    q = q_ref[0]
    kb = k_ref[0, pl.ds(start, NA_BAND), :]
    vb = v_ref[0, pl.ds(start, NA_BAND), :]
    kc = k_ref[0, pl.ds(0, n_ctx), :]
    vc = v_ref[0, pl.ds(0, n_ctx), :]
    head_of_lane = lax.broadcasted_iota(jnp.int32, (GRID_W, LANES), 1) // NA_HEAD_DIM
    heads = [(p, h2) for p in range(NA_WIDTH // LANES) for h2 in range(LANES // NA_HEAD_DIM)]
    cols = lambda p: slice(p * LANES, (p + 1) * LANES)
    scores = []
    for p, h2 in heads:
        qm = jnp.where(head_of_lane == h2, q[:, cols(p)], jnp.zeros((GRID_W, LANES), BF16))
        s_loc = lax.dot_general(qm, kb[:, cols(p)], NT_DIMS, preferred_element_type=F32)
        s_ctx = lax.dot_general(qm, kc[:, cols(p)], NT_DIMS, preferred_element_type=F32)
        scores.append((s_loc + bias_ref[0, 2 * p + h2], s_ctx))
    probs = []
    for s_loc, s_ctx in scores:
        m = jnp.maximum(jnp.max(s_loc, axis=-1, keepdims=True), jnp.max(s_ctx, axis=-1, keepdims=True))
        e_loc = jnp.exp(s_loc - m)
        e_ctx = jnp.exp(s_ctx - m)
        den = jnp.sum(e_loc, axis=-1, keepdims=True) + jnp.sum(e_ctx, axis=-1, keepdims=True)
        probs.append((e_loc.astype(BF16), e_ctx.astype(BF16), den))
    outs = []
    for (p, h2), (e_loc, e_ctx, den) in zip(heads, probs):
        o = (jnp.dot(e_loc, vb[:, cols(p)], preferred_element_type=F32)
             + jnp.dot(e_ctx, vc[:, cols(p)], preferred_element_type=F32))
        outs.append(o / den)
    for p in range(NA_WIDTH // LANES):
        o_ref[0, :, cols(p)] = jnp.where(head_of_lane == 0, outs[2 * p], outs[2 * p + 1])


def _na_bias_table(rpb):
    kw = NA_WIN_COLS
    n_col_off = 2 * kw - 1
    j = np.arange(GRID_W)
    col_start = np.clip(j - kw // 2, 0, GRID_W - kw)
    col_in = (j[None, :] >= col_start[:, None]) & (j[None, :] < col_start[:, None] + kw)
    col_off = np.clip(j[None, :] - j[:, None], -(kw - 1), kw - 1) + (kw - 1)
    pick = (col_off.reshape(1, -1) == np.arange(n_col_off)[:, None]).astype(np.float32)
    toep = jnp.dot(rpb.astype(F32).reshape(-1, n_col_off), pick, precision=lax.Precision.HIGHEST)
    toep = toep.reshape(NA_HEADS, 2 * NA_WIN_ROWS - 1, GRID_W, GRID_W)
    toep = jnp.where(col_in[None, None], toep, NEG_INF)
    tab = jnp.stack([toep[:, NA_WIN_ROWS - 1 - d:2 * NA_WIN_ROWS - 1 - d] for d in range(NA_WIN_ROWS)], 0)
    tab = tab.transpose(0, 1, 3, 2, 4).reshape(NA_WIN_ROWS, NA_HEADS, GRID_W, NA_BAND)
    return jnp.concatenate([tab, jnp.full((1,) + tab.shape[1:], NEG_INF, F32)], 0)


def attention_pallas(q, k, v, rpb, dims):
    B, C, N = dims
    S = C + N
    W = NA_WIDTH
    n_rows = N // GRID_W
    ctx_blocks = C // GRID_W
    assert n_rows >= NA_WIN_ROWS and N % GRID_W == 0 and C % GRID_W == 0
    bias = _na_bias_table(rpb)
    as3 = lambda z: z.reshape(B, S, W)

    def bias_idx(b, j):
        r, row_start = _na_row_start(j, ctx_blocks, n_rows)
        return (jnp.where(j < ctx_blocks, NA_WIN_ROWS, r - row_start), 0, 0, 0)

    out = pl.pallas_call(
        functools.partial(_na_kernel, n_ctx=C),
        grid=(B, S // GRID_W),
        in_specs=[
            pl.BlockSpec((1, GRID_W, W), lambda b, j: (b, j, 0)),
            pl.BlockSpec((1, S, W), lambda b, j: (b, 0, 0)),
            pl.BlockSpec((1, S, W), lambda b, j: (b, 0, 0)),
            pl.BlockSpec((1, NA_HEADS, GRID_W, NA_BAND), bias_idx),
        ],
        out_specs=pl.BlockSpec((1, GRID_W, W), lambda b, j: (b, j, 0)),
        out_shape=jax.ShapeDtypeStruct((B, S, W), F32),
        compiler_params=_cparams(2),
        name="na_attention",
    )(as3(q), as3(k), as3(v), bias)
    return out.reshape(B * S, W)


def _layer_norm(x, g, b):
    mu = jnp.mean(x, axis=-1, keepdims=True)
    xc = x - mu
    var = jnp.mean(xc * xc, axis=-1, keepdims=True)
    return xc * lax.rsqrt(var + LN_EPS) * g + b


def _mixer_tail(h, y, gate, ln_g, ln_b, shift, scale, router, h_out_ref, f_ref, s_ref):
    h1 = _layer_norm(DEEPNORM_ALPHA * h + gate * y, ln_g, ln_b)
    f = h1 * (1.0 + scale) + shift
    h_out_ref[...] = h1
    f_ref[...] = f
    s_ref[...] = jax.nn.sigmoid(jnp.dot(f, router, preferred_element_type=F32, precision=lax.Precision.HIGHEST))


def _even_out_kernel(na_ref, yf_ref, yb_ref, r_ref, v_ref, kdf_ref, kdb_ref, glow_ref, h_ref,
                     gate_ref, shift_ref, scale_ref, ones_ref, gng_ref, gnb_ref, rk_ref, gup_ref, wout_ref,
                     lng_ref, lnb_ref, router_ref, h_out_ref, f_ref, s_ref):
    ones2 = ones_ref[...]
    inv = 1.0 / RW_HEAD_DIM
    y = yf_ref[...] + yb_ref[...]
    mu = _seg_sum(y, ones2) * inv
    yc = y - mu
    var = _seg_sum(yc * yc, ones2) * inv
    yn = yc * lax.rsqrt(var + RW_GN_EPS) * gng_ref[...] + gnb_ref[...]
    r = r_ref[...]
    bonus = (_seg_sum(r * kdf_ref[...] * rk_ref[...], ones2) + _seg_sum(r * kdb_ref[...] * rk_ref[...], ones2))
    gate = jnp.dot(jax.nn.sigmoid(glow_ref[...]).astype(BF16), gup_ref[...], preferred_element_type=F32)
    rw = (yn + bonus * v_ref[...]) * gate
    mix = jnp.concatenate([na_ref[...], rw], axis=-1).astype(BF16)
    y_mix = jnp.dot(mix, wout_ref[...], preferred_element_type=F32)
    _mixer_tail(h_ref[...], y_mix, gate_ref[...], lng_ref[...], lnb_ref[...], shift_ref[...], scale_ref[...],
                router_ref[...], h_out_ref, f_ref, s_ref)


def _tail_specs(R, D, E, tm):
    row = lambda width: pl.BlockSpec((tm, width), lambda i: (i, 0))
    return ([row(D), row(D), row(E)],
            [jax.ShapeDtypeStruct((R, D), F32), jax.ShapeDtypeStruct((R, D), F32), jax.ShapeDtypeStruct((R, E), F32)])


def even_out_pallas(na, y_f, y_b, r, v, kd_f, kd_b, glow, h, mods, dims, g_up, r_k, gn_g, gn_b, w_out,
                    ln_g, ln_b, router_w):
    B, C, N = dims
    R, D = h.shape
    E = router_w.shape[1]
    tm = ROW_TILE
    mod_idx = _mod_index((C + N) // tm, C // tm, B)
    row = lambda width: pl.BlockSpec((tm, width), lambda i: (i, 0))
    ones2 = _block_ones(2 * RW_WIDTH, RW_WIDTH, RW_HEAD_DIM)
    g_up_p = jnp.pad(g_up, ((0, LANES - g_up.shape[0]), (0, 0))).astype(BF16)
    vec = lambda z: z.reshape(1, -1)
    out_specs, out_shape = _tail_specs(R, D, E, tm)
    return pl.pallas_call(
        _even_out_kernel,
        grid=(R // tm,),
        in_specs=[row(NA_WIDTH)] + [row(RW_WIDTH)] * 6 + [row(LANES), row(D),
                  _mod_spec(2, mod_idx), _mod_spec(3, mod_idx), _mod_spec(4, mod_idx),
                  _full_spec((2 * RW_WIDTH, RW_WIDTH)),
                  _full_spec((1, RW_WIDTH)), _full_spec((1, RW_WIDTH)), _full_spec((1, RW_WIDTH)),
                  _full_spec((LANES, RW_WIDTH)), _full_spec((D, D)),
                  _full_spec((1, D)), _full_spec((1, D)), _full_spec((D, E))],
        out_specs=out_specs,
        out_shape=out_shape,
        compiler_params=_cparams(1),
        name="even_out",
    )(na, y_f, y_b, r, v, kd_f, kd_b, glow, h, mods, mods, mods, ones2, vec(gn_g), vec(gn_b), vec(r_k),
      g_up_p, w_out.astype(BF16), vec(ln_g), vec(ln_b), router_w)


def _resid_tail_kernel(y_ref, h_ref, gate_ref, shift_ref, scale_ref, lng_ref, lnb_ref, router_ref,
                       h_out_ref, f_ref, s_ref):
    _mixer_tail(h_ref[...], y_ref[...], gate_ref[...], lng_ref[...], lnb_ref[...], shift_ref[...],
                scale_ref[...], router_ref[...], h_out_ref, f_ref, s_ref)


def resid_tail_pallas(y, h, mods, dims, ln_g, ln_b, router_w):
    B, C, N = dims
    R, D = h.shape
    E = router_w.shape[1]
    tm = ROW_TILE
    mod_idx = _mod_index((C + N) // tm, C // tm, B)
    row = lambda width: pl.BlockSpec((tm, width), lambda i: (i, 0))
    vec = lambda z: z.reshape(1, -1)
    out_specs, out_shape = _tail_specs(R, D, E, tm)
    return pl.pallas_call(
        _resid_tail_kernel,
        grid=(R // tm,),
        in_specs=[row(D), row(D), _mod_spec(2, mod_idx), _mod_spec(3, mod_idx), _mod_spec(4, mod_idx),
                  _full_spec((1, D)), _full_spec((1, D)), _full_spec((D, E))],
        out_specs=out_specs,
        out_shape=out_shape,
        compiler_params=_cparams(1),
        name="resid_tail",
    )(y, h, mods, mods, mods, vec(ln_g), vec(ln_b), router_w)


MOE_TOKEN_TILE = 256


def _swiglu_bf16(x, wg, wu, wd):
    g = jnp.dot(x, wg, preferred_element_type=F32)
    u = jnp.dot(x, wu, preferred_element_type=F32)
    mid = (g * jax.nn.sigmoid(g) * u).astype(BF16)
    return jnp.dot(mid, wd, preferred_element_type=F32)


def _row_copy(src_ref, src_row, dst_ref, dst_row, sem):
    return pltpu.make_async_copy(src_ref.at[pl.ds(src_row, 1), :], dst_ref.at[pl.ds(dst_row, 1), :], sem)


def _slot_kernel(e_ref, rank_ref, starts_ref, slot_ref):
    e = e_ref[...]
    tm = e.shape[0]
    lane = lax.broadcasted_iota(jnp.int32, (tm, starts_ref.shape[1]), 1)
    col8 = lax.broadcasted_iota(jnp.int32, (tm, TOP_K), 1)
    first = jnp.zeros((tm, TOP_K), F32)
    for k in range(TOP_K):
        hit = lane == e[:, k:k + 1]
        first = jnp.where(col8 == k, jnp.sum(jnp.where(hit, starts_ref[...], 0.0), axis=-1, keepdims=True), first)
    slot_ref[...] = first.astype(jnp.int32) + rank_ref[...]


def moe_slots_pallas(e_idx, rank, starts):
    T = e_idx.shape[0]
    E = starts.shape[0]
    tm = ROW_TILE
    row8 = pl.BlockSpec((tm, TOP_K), lambda i: (i, 0))
    return pl.pallas_call(
        _slot_kernel,
        grid=(T // tm,),
        in_specs=[row8, row8, _full_spec((1, E))],
        out_specs=row8,
        out_shape=jax.ShapeDtypeStruct((T, TOP_K), jnp.int32),
        compiler_params=_cparams(1),
        name="moe_slots",
    )(e_idx, rank, starts.astype(F32).reshape(1, E))


def _dispatch_kernel(slot_ref, f_ref, xs_ref, sem):
    n_tok = f_ref.shape[0]

    def issue(t, carry):
        for k in range(TOP_K):
            _row_copy(f_ref, t, xs_ref, slot_ref[t * TOP_K + k], sem).start()
        return carry

    lax.fori_loop(0, n_tok, issue, 0)

    def drain(t, carry):
        for k in range(TOP_K):
            _row_copy(f_ref, 0, xs_ref, 0, sem).wait()
        return carry

    lax.fori_loop(0, n_tok, drain, 0)


def _slot_specs(tm):
    return [pl.BlockSpec((tm * TOP_K,), lambda i: (i,), memory_space=pltpu.SMEM)]


def moe_dispatch_pallas(f, slot_flat):
    T, D = f.shape
    tm = MOE_TOKEN_TILE
    assert T % tm == 0
    return pl.pallas_call(
        _dispatch_kernel,
        grid=(T // tm,),
        in_specs=_slot_specs(tm) + [pl.BlockSpec((tm, D), lambda i: (i, 0))],
        out_specs=pl.BlockSpec(memory_space=pl.ANY),
        out_shape=jax.ShapeDtypeStruct((T * TOP_K, D), F32),
        scratch_shapes=[pltpu.SemaphoreType.DMA(())],
        compiler_params=_cparams(1),
        name="moe_dispatch",
    )(slot_flat, f)


def _expert_item_kernel(blk_ref, e_ref, lo_ref, hi_ref, first_ref, x_ref, wg_ref, wu_ref, wd_ref, o_ref,
                        wg16_ref, wu16_ref, wd16_ref):
    i = pl.program_id(0)
    lo, hi = lo_ref[i], hi_ref[i]

    @pl.when((i == 0) | (e_ref[i] != e_ref[jnp.maximum(i - 1, 0)]))
    def _():
        wg16_ref[...] = wg_ref[0, 0].astype(BF16)
        wu16_ref[...] = wu_ref[0, 0].astype(BF16)
        wd16_ref[...] = wd_ref[0, 0].astype(BF16)

    @pl.when(hi > lo)
    def _():
        y = _swiglu_bf16(x_ref[...].astype(BF16), wg16_ref[...], wu16_ref[...], wd16_ref[...])
        rows = blk_ref[i] * MOE_BLOCK + lax.broadcasted_iota(jnp.int32, (MOE_BLOCK, 1), 0)
        y = jnp.where((rows >= lo) & (rows < hi), y, 0.0)

        @pl.when(first_ref[i] == 1)
        def _():
            o_ref[...] = y

        @pl.when(first_ref[i] == 0)
        def _():
            o_ref[...] += y


def moe_experts_pallas(xs, items, layer, wg, wu, wd):
    n_rows, D = xs.shape
    F = wg.shape[-1]
    n_items = items[0].shape[0]
    grid_spec = pltpu.PrefetchScalarGridSpec(
        num_scalar_prefetch=5,
        grid=(n_items,),
        in_specs=[
            pl.BlockSpec((MOE_BLOCK, D), lambda i, blk, e, lo, hi, first: (blk[i], 0)),
            pl.BlockSpec((1, 1, D, F), lambda i, blk, e, lo, hi, first: (layer, e[i], 0, 0)),
            pl.BlockSpec((1, 1, D, F), lambda i, blk, e, lo, hi, first: (layer, e[i], 0, 0)),
            pl.BlockSpec((1, 1, F, D), lambda i, blk, e, lo, hi, first: (layer, e[i], 0, 0)),
        ],
        out_specs=pl.BlockSpec((MOE_BLOCK, D), lambda i, blk, e, lo, hi, first: (blk[i], 0)),
        scratch_shapes=[pltpu.VMEM((D, F), BF16), pltpu.VMEM((D, F), BF16), pltpu.VMEM((F, D), BF16)],
    )
    return pl.pallas_call(
        _expert_item_kernel,
        grid_spec=grid_spec,
        out_shape=jax.ShapeDtypeStruct((n_rows, D), F32),
        compiler_params=_cparams(1),
        name="moe_experts",
    )(*items, xs, wg, wu, wd)


def _combine_kernel(slot_ref, w_ref, f_ref, h_ref, gate_ref, lng_ref, lnb_ref,
                    sg_ref, su_ref, sd_ref, ys_ref, o_ref, buf_ref, sem):
    n_tok = f_ref.shape[0]

    def issue(t, carry):
        for k in range(TOP_K):
            pltpu.make_async_copy(ys_ref.at[pl.ds(slot_ref[t * TOP_K + k], 1), :],
                                  buf_ref.at[k, pl.ds(t, 1), :], sem).start()
        return carry

    lax.fori_loop(0, n_tok, issue, 0)
    acc = _swiglu_bf16(f_ref[...].astype(BF16), sg_ref[...], su_ref[...], sd_ref[...])

    def drain(t, carry):
        for k in range(TOP_K):
            pltpu.make_async_copy(ys_ref.at[pl.ds(0, 1), :], buf_ref.at[0, pl.ds(0, 1), :], sem).wait()
        return carry

    lax.fori_loop(0, n_tok, drain, 0)
    w = w_ref[...]
    for k in range(TOP_K):
        acc = acc + w[:, k:k + 1] * buf_ref[k]
    o_ref[...] = _layer_norm(DEEPNORM_ALPHA * h_ref[...] + gate_ref[...] * acc, lng_ref[...], lnb_ref[...])


def moe_combine_pallas(ys, slot_flat, w_sel, f, h, mods, dims, ln_g, ln_b, sg, su, sd):
    B, C, N = dims
    T, D = f.shape
    tm = MOE_TOKEN_TILE
    F = sg.shape[-1]
    mod_idx = _mod_index((C + N) // tm, C // tm, B)
    vec = lambda z: z.reshape(1, -1)
    return pl.pallas_call(
        _combine_kernel,
        grid=(T // tm,),
        in_specs=_slot_specs(tm) + [
            pl.BlockSpec((tm, TOP_K), lambda i: (i, 0)),
            pl.BlockSpec((tm, D), lambda i: (i, 0)),
            pl.BlockSpec((tm, D), lambda i: (i, 0)),
            _mod_spec(5, mod_idx), _full_spec((1, D)), _full_spec((1, D)),
            _full_spec((D, F)), _full_spec((D, F)), _full_spec((F, D)),
            pl.BlockSpec(memory_space=pl.ANY),
        ],
        out_specs=pl.BlockSpec((tm, D), lambda i: (i, 0)),
        out_shape=jax.ShapeDtypeStruct((T, D), F32),
        scratch_shapes=[pltpu.VMEM((TOP_K, tm, D), F32), pltpu.SemaphoreType.DMA(())],
        compiler_params=_cparams(1),
        name="moe_combine",
    )(slot_flat, w_sel, f, h, mods, vec(ln_g), vec(ln_b), sg, su, sd, ys)


REMOVED = -3e38


def _router_kernel(s_ref, bias_ref, e_ref, w_ref, rank_ref, cnt_ref, carry_ref):
    @pl.when(pl.program_id(0) == 0)
    def _():
        carry_ref[...] = jnp.zeros_like(carry_ref)

    s = s_ref[...]
    tm, n_exp = s.shape
    per_group = n_exp // N_GROUPS
    lane_i = lax.broadcasted_iota(jnp.int32, (tm, n_exp), 1)
    lane = lane_i.astype(F32)
    group_of_lane = lane_i // per_group
    big = float(n_exp)
    rmax = lambda z: jnp.max(z, axis=-1, keepdims=True)
    first_at = lambda z, m: jnp.min(jnp.where(z == m, lane, big), axis=-1, keepdims=True)

    grp = s + bias_ref[...]
    g_score = []
    for g in range(N_GROUPS):
        mg = jnp.where(group_of_lane == g, grp, REMOVED)
        m1 = rmax(mg)
        m2 = rmax(jnp.where(lane == first_at(mg, m1), REMOVED, mg))
        g_score.append(m1 + m2)
    choice = jnp.full_like(grp, NEG_INF)
    for g in range(N_GROUPS):
        ahead = jnp.zeros((tm, 1), F32)
        for g2 in range(N_GROUPS):
            if g2 != g:
                beats = (g_score[g2] > g_score[g]) | ((g_score[g2] == g_score[g]) & (g2 < g))
                ahead = ahead + beats.astype(F32)
        choice = jnp.where((group_of_lane == g) & (ahead < TOPK_GROUPS), grp, choice)

    col8 = lax.broadcasted_iota(jnp.int32, (tm, TOP_K), 1)
    e_out = jnp.zeros((tm, TOP_K), F32)
    w_out = jnp.zeros((tm, TOP_K), F32)
    picked = []
    onehot = jnp.zeros((tm, n_exp), F32)
    for k in range(TOP_K):
        idx = first_at(choice, rmax(choice))
        hit = lane == idx
        picked.append(hit)
        onehot = jnp.where(hit, 1.0, onehot)
        e_out = jnp.where(col8 == k, idx, e_out)
        w_out = jnp.where(col8 == k, jnp.sum(jnp.where(hit, s, 0.0), axis=-1, keepdims=True), w_out)
        choice = jnp.where(hit, REMOVED, choice)
    ri = lax.broadcasted_iota(jnp.int32, (tm, tm), 0)
    ci = lax.broadcasted_iota(jnp.int32, (tm, tm), 1)
    before = jnp.dot((ci < ri).astype(BF16), onehot.astype(BF16), preferred_element_type=F32) + carry_ref[0:1, :]
    rank = jnp.zeros((tm, TOP_K), F32)
    for k in range(TOP_K):
        rank = jnp.where(col8 == k, jnp.sum(jnp.where(picked[k], before, 0.0), axis=-1, keepdims=True), rank)
    total = carry_ref[0:1, :] + jnp.sum(onehot, axis=0, keepdims=True)
    carry_ref[...] = jnp.broadcast_to(total, carry_ref.shape)
    cnt_ref[...] = jnp.broadcast_to(total, cnt_ref.shape)
    e_ref[...] = e_out.astype(jnp.int32)
    w_ref[...] = w_out / jnp.sum(w_out, axis=-1, keepdims=True) * ROUTED_SCALE
    rank_ref[...] = rank.astype(jnp.int32)


def router_pallas(s, router_b):
    T, E = s.shape
    tm = ROW_TILE
    row8 = pl.BlockSpec((tm, TOP_K), lambda i: (i, 0))
    e_idx, w_sel, rank, cnt = pl.pallas_call(
        _router_kernel,
        grid=(T // tm,),
        in_specs=[pl.BlockSpec((tm, E), lambda i: (i, 0)), _full_spec((1, E))],
        out_specs=[row8, row8, row8, _full_spec((SUBLANES, E))],
        out_shape=[jax.ShapeDtypeStruct((T, TOP_K), jnp.int32), jax.ShapeDtypeStruct((T, TOP_K), F32),
                   jax.ShapeDtypeStruct((T, TOP_K), jnp.int32), jax.ShapeDtypeStruct((SUBLANES, E), F32)],
        scratch_shapes=[pltpu.VMEM((SUBLANES, E), F32)],
        compiler_params=_cparams(1),
        name="moe_router",
    )(s, router_b.astype(F32).reshape(1, E))
    return e_idx, w_sel, rank, cnt[0].astype(jnp.int32)


def moe_layer(f, s, h, mods, dims, ln_g, ln_b, router_b, layer, wg, wu, wd, sg, su, sd):
    T, D = f.shape
    E = s.shape[-1]
    e_idx, w_sel, rank, counts = router_pallas(s, router_b)
    n_asg = T * TOP_K
    assert n_asg % MOE_BLOCK == 0
    i32 = jnp.int32
    ends = jnp.cumsum(counts).astype(i32)
    starts = ends - counts
    slot_flat = moe_slots_pallas(e_idx, rank, starts).reshape(-1)
    nb = n_asg // MOE_BLOCK
    first_blk = starts // MOE_BLOCK
    nblk = jnp.where(counts > 0, (ends - 1) // MOE_BLOCK - first_blk + 1, 0)
    item_ends = jnp.cumsum(nblk).astype(i32)
    item_starts = item_ends - nblk
    n_items = nb + E
    it = jnp.arange(n_items, dtype=i32)
    real = it < item_ends[-1]
    e_of = jnp.sum((item_ends[None, :] <= jnp.where(real, it, item_ends[-1] - 1)[:, None]).astype(i32), axis=1)
    is_e = e_of[:, None] == jnp.arange(E, dtype=i32)[None, :]
    pick = lambda tab: jnp.sum(jnp.where(is_e, tab[None, :], 0), axis=1)
    blk = jnp.where(real, pick(first_blk) + it - pick(item_starts), nb - 1).astype(i32)
    lo = jnp.where(real, jnp.maximum(pick(starts), blk * MOE_BLOCK), 0).astype(i32)
    hi = jnp.where(real, jnp.minimum(pick(ends), (blk + 1) * MOE_BLOCK), 0).astype(i32)
    first = (real & (blk != jnp.concatenate([jnp.full((1,), -1, i32), blk[:-1]]))).astype(i32)
    xs = moe_dispatch_pallas(f, slot_flat)
    ys = moe_experts_pallas(xs, (blk, e_of, lo, hi, first), layer, wg, wu, wd)
    return moe_combine_pallas(ys, slot_flat, w_sel, f, h, mods, dims, ln_g, ln_b,
                              sg.astype(BF16), su.astype(BF16), sd.astype(BF16))


ML_QK_WIDTH = ML_HEADS * ML_QK_DIM
ROPE_GROUP = ML_QK_DIM // 4
GATE_IN, GATE_FORGET = 0, 2 * ML_HEADS


def _log_sigmoid(x):
    return -_softplus(-x)


def _proj_odd_kernel(h_ref, shift_ref, scale_ref, wqk_ref, wv_ref, wo_ref, wg_ref, wgt_ref, gb_ref, gbt_ref,
                     cos_ref, sin_ref, q_ref, k_ref, v_ref, o_ref, g_ref, gt_ref):
    a16 = (h_ref[...] * (1.0 + scale_ref[...]) + shift_ref[...]).astype(BF16)
    qk = jnp.dot(a16, wqk_ref[...], preferred_element_type=F32)
    lane = lax.broadcasted_iota(jnp.int32, (1, ML_QK_WIDTH), 1)
    first_of_pair = (lane % (2 * ROPE_GROUP)) < ROPE_GROUP
    cos, sin = cos_ref[...], sin_ref[...]

    def rope(z):
        partner = jnp.where(first_of_pair, pltpu.roll(z, ML_QK_WIDTH - ROPE_GROUP, 1), pltpu.roll(z, ROPE_GROUP, 1))
        return z * cos + partner * sin

    q_ref[...] = rope(qk[:, :ML_QK_WIDTH] * ML_QK_DIM ** -0.5).astype(BF16)
    k_ref[...] = rope(qk[:, ML_QK_WIDTH:]).astype(BF16)
    v_ref[...] = jnp.dot(a16, wv_ref[...], preferred_element_type=F32).astype(BF16)
    o_ref[...] = jnp.dot(a16, wo_ref[...], preferred_element_type=F32)
    g = jnp.dot(a16, wg_ref[...], preferred_element_type=F32) + gb_ref[...]
    gl = lax.broadcasted_iota(jnp.int32, g.shape, 1)
    g_ref[...] = jnp.where((gl >= GATE_FORGET) & (gl < 2 * GATE_FORGET), _log_sigmoid(g), g)
    gt = lax.dot_general(wgt_ref[...], a16, NT_DIMS, preferred_element_type=F32) + gbt_ref[...]
    gs = lax.broadcasted_iota(jnp.int32, gt.shape, 0)
    gt_ref[...] = jnp.where((gs >= GATE_FORGET) & (gs < 2 * GATE_FORGET), _log_sigmoid(gt), gt)


def _rope_tables(C, N):
    t = jnp.arange(N)
    pos = jnp.stack([(t // GRID_W).astype(F32), (t % GRID_W).astype(F32)], 0)
    lane = jnp.arange(ML_QK_WIDTH) % ML_QK_DIM
    inv = ROPE_BASE ** (-(lane % ROPE_GROUP).astype(F32) / ROPE_GROUP)
    ang = pos[lane // (2 * ROPE_GROUP)].T * inv[None, :]
    sign = jnp.where((lane % (2 * ROPE_GROUP)) < ROPE_GROUP, -1.0, 1.0)
    cos = jnp.concatenate([jnp.ones((C, ML_QK_WIDTH), F32), jnp.cos(ang)], 0)
    sin = jnp.concatenate([jnp.zeros((C, ML_QK_WIDTH), F32), jnp.sin(ang) * sign], 0)
    return cos, sin


def proj_odd_pallas(h, mods, dims, w_in, gate_b):
    B, C, N = dims
    R, D = h.shape
    tm = ROW_TILE
    tpb = (C + N) // tm
    mod_idx = _mod_index(tpb, C // tm, B)
    o_qk, o_v, o_o = 2 * ML_QK_WIDTH, 2 * ML_QK_WIDTH + ML_WIDTH, 2 * ML_QK_WIDTH + 2 * ML_WIDTH
    n_gate = w_in.shape[1] - o_o
    w16 = w_in.astype(BF16)
    w_g = jnp.pad(w16[:, o_o:], ((0, 0), (0, LANES - n_gate)))
    gb = jnp.pad(gate_b.astype(F32).reshape(-1), (0, LANES - n_gate))
    cos, sin = _rope_tables(C, N)
    row = lambda width: pl.BlockSpec((tm, width), lambda i: (i, 0))
    seg = pl.BlockSpec((tm, ML_QK_WIDTH), lambda i: (i % tpb, 0))
    return pl.pallas_call(
        _proj_odd_kernel,
        grid=(R // tm,),
        in_specs=[row(D), _mod_spec(0, mod_idx), _mod_spec(1, mod_idx),
                  _full_spec((D, 2 * ML_QK_WIDTH)), _full_spec((D, ML_WIDTH)), _full_spec((D, ML_WIDTH)),
                  _full_spec((D, LANES)), _full_spec((LANES, D)), _full_spec((1, LANES)), _full_spec((LANES, 1)),
                  seg, seg],
        out_specs=[row(ML_QK_WIDTH), row(ML_QK_WIDTH), row(ML_WIDTH), row(ML_WIDTH), row(LANES),
                   pl.BlockSpec((LANES, tm), lambda i: (0, i))],
        out_shape=[jax.ShapeDtypeStruct((R, ML_QK_WIDTH), BF16), jax.ShapeDtypeStruct((R, ML_QK_WIDTH), BF16),
                   jax.ShapeDtypeStruct((R, ML_WIDTH), BF16), jax.ShapeDtypeStruct((R, ML_WIDTH), F32),
                   jax.ShapeDtypeStruct((R, LANES), F32), jax.ShapeDtypeStruct((LANES, R), F32)],
        compiler_params=_cparams(1),
        name="proj_odd",
    )(h, mods, mods, w16[:, :o_qk], w16[:, o_qk:o_v], w16[:, o_v:o_o], w_g, w_g.T, gb.reshape(1, LANES),
      gb.reshape(LANES, 1), cos, sin)


def _split3_bf16(x, axis):
    x1 = x.astype(BF16)
    r1 = x - x1.astype(F32)
    x2 = r1.astype(BF16)
    x3 = (r1 - x2.astype(F32)).astype(BF16)
    return jnp.concatenate([x1, x2, x3], axis=axis)


def _mlstm_kernel(q_ref, k_ref, v_ref, g_ref, gt_ref, h_ref, c_ref, n_ref, m_ref, *, reverse):
    @pl.when(pl.program_id(1) == 0)
    def _():
        c_ref[...] = jnp.zeros_like(c_ref)
        n_ref[...] = jnp.zeros_like(n_ref)
        m_ref[...] = jnp.zeros_like(m_ref)

    L = q_ref.shape[1]
    ti = lax.broadcasted_iota(jnp.int32, (L, L), 0)
    si = lax.broadcasted_iota(jnp.int32, (L, L), 1)
    seen = (si >= ti) if reverse else (si <= ti)
    g = g_ref[0]
    gt = gt_ref[...]
    b_cols3 = jnp.dot(seen.astype(BF16), _split3_bf16(g, 1), preferred_element_type=F32)
    b_cols = b_cols3[:, :LANES] + b_cols3[:, LANES:2 * LANES] + b_cols3[:, 2 * LANES:]
    b_rows3 = lax.dot_general(_split3_bf16(gt, 0), seen.astype(BF16), NT_DIMS, preferred_element_type=F32)
    b_rows = b_rows3[:LANES] + b_rows3[LANES:2 * LANES] + b_rows3[2 * LANES:]
    half = lax.broadcasted_iota(jnp.int32, (1, LANES), 1) // ML_QK_DIM
    row_half = lax.broadcasted_iota(jnp.int32, (LANES, 1), 0) // ML_QK_DIM
    d_off = ML_HEADS if reverse else 0
    tn = (((0,), (0,)), ((), ()))
    heads = [(hd // 2, hd % 2) for hd in range(ML_HEADS)]
    pair = lambda ref, p: ref[0, :, p * LANES:(p + 1) * LANES]
    value = lambda hd: v_ref[0, :, hd * ML_V_DIM:(hd + 1) * ML_V_DIM]

    decay = []
    for hd, (p, h2) in enumerate(heads):
        gi, gf = GATE_IN + d_off + hd, GATE_FORGET + d_off + hd
        ig_col, ig_row = g[:, gi:gi + 1], gt[gi:gi + 1, :]
        b_col, b_row = b_cols[:, gf:gf + 1], b_rows[gf:gf + 1, :]
        m0 = m_ref[p][:, h2 * ML_QK_DIM:h2 * ML_QK_DIM + 1]
        dlog = jnp.where(seen, b_col - b_row + ig_row, NEG_INF)
        inter = b_col + m0
        m_t = jnp.maximum(jnp.max(dlog, axis=-1, keepdims=True), inter)
        b_end = jnp.sum(g[:, gf:gf + 1], axis=0, keepdims=True)
        g_col = b_end - b_col + ig_col
        m_chunk = jnp.max(g_col, axis=0, keepdims=True)
        m_new = jnp.maximum(b_end + m0, m_chunk)
        decay.append(dict(dw=jnp.exp(dlog - m_t), iw=jnp.exp(inter - m_t), floor=jnp.exp(-m_t),
                          kw=jnp.exp(g_col - m_chunk), m_new=m_new,
                          fa=jnp.exp(b_end + m0 - m_new), fb=jnp.exp(m_chunk - m_new)))

    prods = []
    for p, h2 in heads:
        qm = jnp.where(half == h2, pair(q_ref, p), jnp.zeros((L, LANES), BF16))
        qk = lax.dot_general(qm, pair(k_ref, p), NT_DIMS, preferred_element_type=F32)
        qc = jnp.dot(qm, c_ref[p].astype(BF16), preferred_element_type=F32)
        qn = jnp.sum(qm.astype(F32) * n_ref[p], axis=-1, keepdims=True)
        prods.append((qk, qc, qn))

    for hd, ((qk, qc, qn), dc) in enumerate(zip(prods, decay)):
        sc = qk * dc['dw']
        num = jnp.dot(sc.astype(BF16), value(hd), preferred_element_type=F32) + dc['iw'] * qc
        den = jnp.sum(sc, axis=-1, keepdims=True) + dc['iw'] * qn
        h_ref[0, :, hd * ML_V_DIM:(hd + 1) * ML_V_DIM] = num / jnp.maximum(jnp.abs(den), dc['floor'])

    for p in range(ML_HEADS // 2):
        c_old, n_old, m_old = c_ref[p], n_ref[p], m_ref[p]
        c_new, n_new, m_new_pair = c_old, n_old, m_old
        for h2 in range(2):
            hd = 2 * p + h2
            dc = decay[hd]
            kw = jnp.where(half == h2, pair(k_ref, p), jnp.zeros((L, LANES), BF16)).astype(F32) * dc['kw']
            kv = lax.dot_general(kw.astype(BF16), value(hd), tn, preferred_element_type=F32)
            c_new = jnp.where(row_half == h2, dc['fa'] * c_old + dc['fb'] * kv, c_new)
            n_new = jnp.where(half == h2, dc['fa'] * n_old + dc['fb'] * jnp.sum(kw, axis=0, keepdims=True), n_new)
            m_new_pair = jnp.where(half == h2, dc['m_new'], m_new_pair)
        c_ref[p] = c_new
        n_ref[p] = n_new
        m_ref[p] = m_new_pair


def mlstm_pallas(q, k, v, g, gt, dims, reverse):
    B, C, N = dims
    S = C + N
    L = ML_CHUNK
    assert C % L == 0 and N % L == 0
    n_ctx, n_all = C // L, S // L
    if reverse:
        chunk = lambda j: jnp.where(j < n_ctx, n_ctx - 1 - j, n_all - 1 - (j - n_ctx))
    else:
        chunk = lambda j: j
    blk = lambda width: pl.BlockSpec((1, L, width), lambda b, j: (b, chunk(j), 0))
    n_pair = ML_HEADS // 2
    out = pl.pallas_call(
        functools.partial(_mlstm_kernel, reverse=reverse),
        grid=(B, n_all),
        in_specs=[blk(ML_QK_WIDTH), blk(ML_QK_WIDTH), blk(ML_WIDTH), blk(LANES),
                  pl.BlockSpec((LANES, L), lambda b, j: (0, b * n_all + chunk(j)))],
        out_specs=blk(ML_WIDTH),
        out_shape=jax.ShapeDtypeStruct((B, S, ML_WIDTH), F32),
        scratch_shapes=[pltpu.VMEM((n_pair, LANES, ML_V_DIM), F32), pltpu.VMEM((n_pair, 1, LANES), F32),
                        pltpu.VMEM((n_pair, 1, LANES), F32)],
        compiler_params=_cparams(2),
        name="mlstm_bwd" if reverse else "mlstm_fwd",
    )(q.reshape(B, S, -1), k.reshape(B, S, -1), v.reshape(B, S, -1), g.reshape(B, S, -1), gt)
    return out.reshape(B * S, ML_WIDTH)


def _odd_out_kernel(hf_ref, hb_ref, o_ref, h_ref, gate_ref, shift_ref, scale_ref, ng_ref, wout_ref,
                    lng_ref, lnb_ref, router_ref, h_out_ref, f_ref, s_ref):
    hs = hf_ref[...] + hb_ref[...]
    parts = []
    for hd in range(ML_HEADS):
        x = hs[:, hd * ML_V_DIM:(hd + 1) * ML_V_DIM]
        parts.append(x * lax.rsqrt(jnp.mean(x * x, axis=-1, keepdims=True) + ML_NORM_EPS))
    hn = jnp.concatenate(parts, axis=-1) * ng_ref[...] * jax.nn.sigmoid(o_ref[...])
    y = jnp.dot(hn.astype(BF16), wout_ref[...], preferred_element_type=F32)
    _mixer_tail(h_ref[...], y, gate_ref[...], lng_ref[...], lnb_ref[...], shift_ref[...], scale_ref[...],
                router_ref[...], h_out_ref, f_ref, s_ref)


def odd_out_pallas(h_f, h_b, o, h, mods, dims, norm_g, w_out, ln_g, ln_b, router_w):
    B, C, N = dims
    R, D = h.shape
    E = router_w.shape[1]
    tm = ROW_TILE
    mod_idx = _mod_index((C + N) // tm, C // tm, B)
    row = lambda width: pl.BlockSpec((tm, width), lambda i: (i, 0))
    vec = lambda z: z.reshape(1, -1)
    out_specs, out_shape = _tail_specs(R, D, E, tm)
    return pl.pallas_call(
        _odd_out_kernel,
        grid=(R // tm,),
        in_specs=[row(ML_WIDTH), row(ML_WIDTH), row(ML_WIDTH), row(D),
                  _mod_spec(2, mod_idx), _mod_spec(3, mod_idx), _mod_spec(4, mod_idx),
                  _full_spec((1, ML_WIDTH)), _full_spec((ML_WIDTH, D)),
                  _full_spec((1, D)), _full_spec((1, D)), _full_spec((D, E))],
        out_specs=out_specs,
        out_shape=out_shape,
        compiler_params=_cparams(1),
        name="odd_out",
    )(h_f, h_b, o, h, mods, mods, mods, vec(norm_g), w_out.astype(BF16), vec(ln_g), vec(ln_b), router_w)


def _offsets(layout, prefix=''):
    offs, o = {}, 0
    for name, width in layout:
        if name.startswith(prefix):
            offs[name] = (o, width)
            o += width
    return offs


def project(h, w, layout, names):
    offs = _offsets(layout)
    if len(names) == len(layout):
        y = jnp.einsum('btd,de->bte', h, w)
        return {n: y[..., offs[n][0]:offs[n][0] + offs[n][1]] for n in names}
    return {n: jnp.einsum('btd,de->bte', h, w[:, offs[n][0]:offs[n][0] + offs[n][1]]) for n in names}


def axial_rope(z):
    T, dh = z.shape[1], z.shape[-1]
    half = dh // 2
    nf = half // 2
    t = jnp.arange(T)
    row = (t // GRID_W).astype(F32)
    col = (t % GRID_W).astype(F32)
    inv = ROPE_BASE ** (-jnp.arange(nf, dtype=F32) / nf)

    def rot(u, pos):
        ang = pos[:, None] * inv[None, :]
        cos = jnp.cos(ang)[None, :, None, :]
        sin = jnp.sin(ang)[None, :, None, :]
        u1, u2 = u[..., :nf], u[..., nf:]
        return jnp.concatenate([u1 * cos - u2 * sin, u1 * sin + u2 * cos], -1)

    return jnp.concatenate([rot(z[..., :half], row), rot(z[..., half:], col)], -1).astype(z.dtype)


def ml_prep(t, gate_b, rope, need_q):
    B, T = t['ml_k'].shape[:2]
    heads = lambda z, dh: z.reshape(B, T, ML_HEADS, dh).astype(F32)
    k = heads(t['ml_k'], ML_QK_DIM)
    q = heads(t['ml_q'], ML_QK_DIM) * ML_QK_DIM ** -0.5 if need_q else None
    if rope:
        k = axial_rope(k)
        q = axial_rope(q)
    v = heads(t['ml_v'], ML_V_DIM)
    gb = gate_b.astype(F32)
    bht = lambda z: z.astype(F32).transpose(0, 2, 1)
    ig = (bht(t['ml_if'] + gb[0]), bht(t['ml_ib'] + gb[1]))
    lf = (jax.nn.log_sigmoid(bht(t['ml_ff'] + gb[2])), jax.nn.log_sigmoid(bht(t['ml_fb'] + gb[3])))
    bhtd = lambda z: None if z is None else z.transpose(0, 2, 1, 3)
    return bhtd(q), bhtd(k), bhtd(v), ig, lf


def ml_chunk_states(k, v, ig, lf, state0):
    B, H, T, dk = k.shape
    dv = v.shape[-1]
    L = min(ML_CHUNK, T)
    nc = T // L
    kc = k.reshape(B, H, nc, L, dk)
    vc = v.reshape(B, H, nc, L, dv)
    b = jnp.cumsum(lf.reshape(B, H, nc, L), -1)
    b_end = b[..., -1]
    g = b_end[..., None] - b + ig.reshape(B, H, nc, L)
    m_chunk = g.max(-1)
    wgt = jnp.exp(g - m_chunk[..., None])
    kv = jnp.einsum('bhnl,bhnlk,bhnlv->bhnkv', wgt, kc, vc)
    ks = jnp.einsum('bhnl,bhnlk->bhnk', wgt, kc)

    def step(state, inp):
        c_mem, n_mem, m = state
        be, mc, kv_n, ks_n = inp
        m_new = jnp.maximum(be + m, mc)
        fa = jnp.exp(be + m - m_new)
        fb = jnp.exp(mc - m_new)
        c_new = fa[..., None, None] * c_mem + fb[..., None, None] * kv_n
        n_new = fa[..., None] * n_mem + fb[..., None] * ks_n
        return (c_new, n_new, m_new), state

    xs = tuple(jnp.moveaxis(z, 2, 0) for z in (b_end, m_chunk, kv, ks))
    final, starts = lax.scan(step, state0, xs)
    return tuple(jnp.moveaxis(z, 0, 2) for z in starts), final


def ml_chunk_outputs(q, k, v, ig, lf, starts):
    B, H, T, dk = q.shape
    dv = v.shape[-1]
    L = min(ML_CHUNK, T)
    nc = T // L
    qc = q.reshape(B, H, nc, L, dk)
    kc = k.reshape(B, H, nc, L, dk)
    vc = v.reshape(B, H, nc, L, dv)
    b = jnp.cumsum(lf.reshape(B, H, nc, L), -1)
    c0, n0, m0 = starts
    causal = jnp.tril(jnp.ones((L, L), bool))
    dlog = jnp.where(causal, b[..., :, None] - b[..., None, :] + ig.reshape(B, H, nc, L)[..., None, :], NEG_INF)
    inter = b + m0[..., None]
    m = jnp.maximum(dlog.max(-1), inter)
    dw = jnp.exp(dlog - m[..., None])
    iw = jnp.exp(inter - m)
    s = jnp.einsum('bhntd,bhnsd->bhnts', qc, kc) * dw
    num = jnp.einsum('bhnts,bhnsv->bhntv', s, vc) + iw[..., None] * jnp.einsum('bhntd,bhndv->bhntv', qc, c0)
    den = s.sum(-1) + iw * jnp.einsum('bhntd,bhnd->bhnt', qc, n0)
    h = num / jnp.maximum(jnp.abs(den), jnp.exp(-m))[..., None]
    return h.reshape(B, H, T, dv)


def ml_readout(h, o, norm_g):
    B, H, T, dv = h.shape
    hn = h * lax.rsqrt(jnp.mean(h * h, -1, keepdims=True) + ML_NORM_EPS)
    hn = hn.transpose(0, 2, 1, 3).reshape(B, T, H * dv) * norm_g
    return hn * jax.nn.sigmoid(o.astype(F32))


def odd_mixer(a_lat, a_ctx, w_in, w_out, gate_b, norm_g, need_ctx):
    names = tuple(n for n, _ in ODD_LAYOUT)
    t_lat = project(a_lat, w_in, ODD_LAYOUT, names)
    t_ctx = project(a_ctx, w_in, ODD_LAYOUT, names if need_ctx else ODD_CTX_STATE_COLS)
    q_l, k_l, v_l, ig_l, lf_l = ml_prep(t_lat, gate_b, True, True)
    q_c, k_c, v_c, ig_c, lf_c = ml_prep(t_ctx, gate_b, False, need_ctx)
    B = a_lat.shape[0]
    zero = (jnp.zeros((B, ML_HEADS, ML_QK_DIM, ML_V_DIM), F32),
            jnp.zeros((B, ML_HEADS, ML_QK_DIM), F32),
            jnp.zeros((B, ML_HEADS), F32))
    h_l, h_c = [], []
    for d in range(2):
        f = (lambda z: jnp.flip(z, 2)) if d == 1 else (lambda z: z)
        starts_c, final_c = ml_chunk_states(f(k_c), f(v_c), f(ig_c[d]), f(lf_c[d]), zero)
        starts_l, _ = ml_chunk_states(f(k_l), f(v_l), f(ig_l[d]), f(lf_l[d]), final_c)
        h_l.append(f(ml_chunk_outputs(f(q_l), f(k_l), f(v_l), f(ig_l[d]), f(lf_l[d]), starts_l)))
        if need_ctx:
            h_c.append(f(ml_chunk_outputs(f(q_c), f(k_c), f(v_c), f(ig_c[d]), f(lf_c[d]), starts_c)))
    y_lat = jnp.einsum('btd,de->bte', ml_readout(h_l[0] + h_l[1], t_lat['ml_o'], norm_g), w_out).astype(a_lat.dtype)
    if not need_ctx:
        return y_lat, None
    y_ctx = jnp.einsum('btd,de->bte', ml_readout(h_c[0] + h_c[1], t_ctx['ml_o'], norm_g), w_out).astype(a_ctx.dtype)
    return y_lat, y_ctx


def kernel(x, c, ctx, c_ctx, ada_w, ada_b, ln_g, ln_b, ev_w_in, ev_w_out, na_rpb, rw_mu, rw_w0, rw_w_up,
           rw_a0, rw_a_up, rw_g_up, rw_k_k, rw_k_a, rw_r_k, rw_gn_g, rw_gn_b, od_w_in, od_w_out, ml_gate_b,
           ml_norm_g, moe_router, moe_bias, moe_w_gate, moe_w_up, moe_w_down, sh_w_gate, sh_w_up, sh_w_down):
    B, N, D = x.shape
    C = ctx.shape[1]
    S = C + N
    dims = (B, C, N)
    assert C % ROW_TILE == 0 and N % ROW_TILE == 0 and B + 1 <= SUBLANES
    h = jnp.concatenate([ctx, x], axis=1).reshape(B * S, D)
    cond = jnp.zeros((SUBLANES, D), F32).at[:B].set(c).at[B].set(c_ctx)
    for l in range(DEPTH):
        mods = ada_mods_pallas(cond, ada_w[l], ada_b[l])
        if l % 2 == 0:
            e = l // 2
            (q, k, v, dec_f, dec_b, beta_f, beta_b, kd_f, kd_b, nkk, rv, rr, glow) = proj_even_pallas(
                h, mods, dims, ev_w_in[e], rw_mu[e], rw_w0[e], rw_w_up[e], rw_a0[e], rw_a_up[e],
                rw_k_k[e], rw_k_a[e])
            y_f, y_b = rwkv_scan_pallas(dec_f, beta_f, kd_f, dec_b, beta_b, kd_b, nkk, rv, rr, dims)
            na = attention_pallas(q, k, v, na_rpb[e], dims)
            h, f, s = even_out_pallas(na, y_f, y_b, rr, rv, kd_f, kd_b, glow, h, mods, dims, rw_g_up[e],
                                      rw_r_k[e], rw_gn_g[e], rw_gn_b[e], ev_w_out[e], ln_g[l, 0], ln_b[l, 0],
                                      moe_router[l])
        else:
            o = l // 2
            q, k, v, og, g, gt = proj_odd_pallas(h, mods, dims, od_w_in[o], ml_gate_b[o])
            h_f = mlstm_pallas(q, k, v, g, gt, dims, False)
            h_b = mlstm_pallas(q, k, v, g, gt, dims, True)
            h, f, s = odd_out_pallas(h_f, h_b, og, h, mods, dims, ml_norm_g[o], od_w_out[o], ln_g[l, 0],
                                     ln_b[l, 0], moe_router[l])
        h = moe_layer(f, s, h, mods, dims, ln_g[l, 1], ln_b[l, 1], moe_bias[l], l, moe_w_gate, moe_w_up,
                      moe_w_down, sh_w_gate[l], sh_w_up[l], sh_w_down[l])
    return h.reshape(B, S, D)[:, C:]
```

```python
import functools

import jax
import jax.numpy as jnp
import numpy as np
from jax import lax
from jax.experimental import pallas as pl
from jax.experimental.pallas import tpu as pltpu

D_MODEL = 1024
DEPTH = 2
GRID_W = 64

DEEPNORM_ALPHA = (2.0 * DEPTH) ** 0.25
LN_EPS = 1e-5
NEG_INF = -1e30
F32 = jnp.float32
BF16 = jnp.bfloat16

NA_HEAD_DIM = 64
NA_WIDTH = D_MODEL // 2
NA_HEADS = NA_WIDTH // NA_HEAD_DIM
NA_WIN_ROWS = 8
NA_WIN_COLS = 16
NA_SCALE = NA_HEAD_DIM ** -0.5

RW_HEAD_DIM = 64
RW_WIDTH = D_MODEL // 2
RW_DECAY_LORA = 32
RW_GN_EPS = 64e-5

ML_HEADS = 8
ML_V_DIM = D_MODEL // ML_HEADS
ML_QK_DIM = ML_V_DIM // 2
ML_WIDTH = ML_HEADS * ML_V_DIM
ML_CHUNK = 128
ML_NORM_EPS = 1e-6
ROPE_BASE = 10000.0

TOP_K = 8
N_GROUPS = 8
TOPK_GROUPS = 4
ROUTED_SCALE = 2.5
MOE_BLOCK = 256

SUBLANES = 8
LANES = 128
VMEM_LIMIT_BYTES = 56 * 1024 * 1024

ROW_TILE = 256
N_MODS = 6
RW_COLS = 3 * RW_WIDTH + 2 * LANES
NT_DIMS = (((1,), (1,)), ((), ()))


def _cparams(n_axes):
    return pltpu.CompilerParams(dimension_semantics=("arbitrary",) * n_axes, vmem_limit_bytes=VMEM_LIMIT_BYTES)


def _full_spec(shape):
    return pl.BlockSpec(shape, lambda *_: (0,) * len(shape))


def _split_bf16(x):
    hi = x.astype(BF16)
    lo = (x - hi.astype(F32)).astype(BF16)
    return jnp.concatenate([hi, lo], axis=-1)


def _block_ones(n_rows, n_cols, seg):
    row = lax.broadcasted_iota(jnp.int32, (n_rows, n_cols), 0)
    col = lax.broadcasted_iota(jnp.int32, (n_rows, n_cols), 1)
    return (((row % n_cols) // seg) == (col // seg)).astype(BF16)


def _seg_sum(x, ones2):
    return jnp.dot(_split_bf16(x), ones2, preferred_element_type=F32)


def _mod_index(tiles_per_batch, ctx_tiles, n_batch):
    def idx(i):
        return jnp.where(i % tiles_per_batch < ctx_tiles, n_batch, i // tiles_per_batch)
    return idx


def _mod_spec(chunk, mod_idx):
    return pl.BlockSpec((None, None, 1, D_MODEL), lambda i: (mod_idx(i), chunk, 0, 0))


def _ada_kernel(c_ref, w_ref, b_ref, o_ref):
    c = c_ref[...]
    x = (c * jax.nn.sigmoid(c)).astype(BF16)
    o_ref[...] = jnp.dot(x, w_ref[...].astype(BF16), preferred_element_type=F32) + b_ref[...]


def ada_mods_pallas(cond, w, b):
    n, D = cond.shape
    n_out = w.shape[1]
    tn = 512
    out = pl.pallas_call(
        _ada_kernel,
        grid=(n_out // tn,),
        in_specs=[_full_spec((n, D)), pl.BlockSpec((D, tn), lambda j: (0, j)), pl.BlockSpec((1, tn), lambda j: (0, j))],
        out_specs=pl.BlockSpec((n, tn), lambda j: (0, j)),
        out_shape=jax.ShapeDtypeStruct((n, n_out), F32),
        compiler_params=_cparams(1),
        name="ada_mods",
    )(cond, w, b.reshape(1, n_out))
    return out.reshape(n, N_MODS, 1, D)


def _softplus(x):
    return jnp.maximum(x, 0.0) + jnp.log(1.0 + jnp.exp(-jnp.abs(x)))


def _proj_even_kernel(h_ref, hp_ref, hn_ref, shift_ref, scale_ref, wna_ref, wrw_ref, mu_ref, ones_ref,
                      kk_ref, ka_ref, w0_ref, a0_ref, wup_ref, aup_ref,
                      q_ref, k_ref, v_ref, dec_f_ref, dec_b_ref, beta_f_ref, beta_b_ref, kd_f_ref, kd_b_ref,
                      nkk_ref, rv_ref, rr_ref, glow_ref, *, tiles_per_batch, ctx_tiles):
    i = pl.program_id(0)
    j = i % tiles_per_batch
    first = (j == 0) | (j == ctx_tiles)
    last = (j == ctx_tiles - 1) | (j == tiles_per_batch - 1)
    tm = h_ref.shape[0]
    gain = 1.0 + scale_ref[...]
    shift = shift_ref[...]
    a = h_ref[...] * gain + shift
    a_prev = jnp.where(first, 0.0, hp_ref[SUBLANES - 1:SUBLANES, :] * gain + shift)
    a_next = jnp.where(last, 0.0, hn_ref[0:1, :] * gain + shift)
    rid = lax.broadcasted_iota(jnp.int32, (tm, 1), 0)
    prev = jnp.where(rid == 0, a_prev, pltpu.roll(a, 1, 0))
    nxt = jnp.where(rid == tm - 1, a_next, pltpu.roll(a, tm - 1, 0))
    a16 = a.astype(BF16)
    nb16 = (0.5 * (prev + nxt)).astype(BF16)

    na = jnp.dot(a16, wna_ref[...], preferred_element_type=F32)
    q_ref[...] = (na[:, :NA_WIDTH] * NA_SCALE).astype(BF16)
    k_ref[...] = na[:, NA_WIDTH:2 * NA_WIDTH].astype(BF16)
    v_ref[...] = na[:, 2 * NA_WIDTH:].astype(BF16)

    pa = jnp.dot(a16, wrw_ref[...], preferred_element_type=F32)
    pn = jnp.dot(nb16, wrw_ref[...], preferred_element_type=F32)
    t = pa + mu_ref[...] * (pn - pa)
    r = t[:, :RW_WIDTH]
    k = t[:, RW_WIDTH:2 * RW_WIDTH]
    lora = t[:, 3 * RW_WIDTH:3 * RW_WIDTH + LANES]
    rr_ref[...] = r
    rv_ref[...] = t[:, 2 * RW_WIDTH:3 * RW_WIDTH]
    glow_ref[...] = t[:, 3 * RW_WIDTH + LANES:]

    kk = k * kk_ref[...]
    norm = jnp.sqrt(_seg_sum(kk * kk, ones_ref[...]))
    kk = kk / jnp.maximum(norm, 1e-12)
    nkk_ref[...] = -kk
    lora_t = jnp.tanh(lora).astype(BF16)
    lora16 = lora.astype(BF16)
    outs = ((dec_f_ref, beta_f_ref, kd_f_ref), (dec_b_ref, beta_b_ref, kd_b_ref))
    for d in range(2):
        w_log = -_softplus(-(w0_ref[d:d + 1, :] + jnp.dot(lora_t, wup_ref[d], preferred_element_type=F32))) - 0.5
        a_gate = jax.nn.sigmoid(a0_ref[d:d + 1, :] + jnp.dot(lora16, aup_ref[d], preferred_element_type=F32))
        outs[d][0][...] = jnp.exp(-jnp.exp(w_log))
        outs[d][1][...] = kk * a_gate
        outs[d][2][...] = k * (1.0 + (a_gate - 1.0) * ka_ref[...])


def proj_even_pallas(h, mods, dims, w_in, mu, w0, w_up, a0, a_up, k_k, k_a):
    B, C, N = dims
    R, D = h.shape
    tm = ROW_TILE
    tpb, ctx_tiles = (C + N) // tm, C // tm
    mod_idx = _mod_index(tpb, ctx_tiles, B)
    w_na = w_in[:, :3 * NA_WIDTH].astype(BF16)
    n_rw = w_in.shape[1] - 3 * NA_WIDTH
    w_rw = jnp.pad(w_in[:, 3 * NA_WIDTH:], ((0, 0), (0, RW_COLS - n_rw))).astype(BF16)
    mu_p = jnp.pad(mu, (0, RW_COLS - n_rw)).reshape(1, RW_COLS)
    ones2 = _block_ones(2 * RW_WIDTH, RW_WIDTH, RW_HEAD_DIM)
    lr = RW_DECAY_LORA

    def pad_up(m, first_row):
        out = jnp.zeros((2, LANES, RW_WIDTH), F32)
        for d in range(2):
            out = out.at[d, first_row + d * lr:first_row + (d + 1) * lr].set(m[d])
        return out.astype(BF16)

    row = lambda width: pl.BlockSpec((tm, width), lambda i: (i, 0))
    hb = tm // SUBLANES
    n_hb = R // SUBLANES
    wide = jax.ShapeDtypeStruct((R, RW_WIDTH), F32)
    half = jax.ShapeDtypeStruct((R, NA_WIDTH), BF16)
    return pl.pallas_call(
        functools.partial(_proj_even_kernel, tiles_per_batch=tpb, ctx_tiles=ctx_tiles),
        grid=(R // tm,),
        in_specs=[
            row(D),
            pl.BlockSpec((SUBLANES, D), lambda i: (jnp.maximum(i * hb - 1, 0), 0)),
            pl.BlockSpec((SUBLANES, D), lambda i: (jnp.minimum((i + 1) * hb, n_hb - 1), 0)),
            _mod_spec(0, mod_idx), _mod_spec(1, mod_idx),
            _full_spec((D, 3 * NA_WIDTH)), _full_spec((D, RW_COLS)), _full_spec((1, RW_COLS)),
            _full_spec((2 * RW_WIDTH, RW_WIDTH)),
            _full_spec((1, RW_WIDTH)), _full_spec((1, RW_WIDTH)),
            _full_spec((2, RW_WIDTH)), _full_spec((2, RW_WIDTH)),
            _full_spec((2, LANES, RW_WIDTH)), _full_spec((2, LANES, RW_WIDTH)),
        ],
        out_specs=[row(NA_WIDTH)] * 3 + [row(RW_WIDTH)] * 9 + [row(LANES)],
        out_shape=[half] * 3 + [wide] * 9 + [jax.ShapeDtypeStruct((R, LANES), F32)],
        compiler_params=_cparams(1),
        name="proj_even",
    )(h, h, h, mods, mods, w_na, w_rw, mu_p, ones2, k_k.reshape(1, -1), k_a.reshape(1, -1), w0, a0,
      pad_up(w_up, 0), pad_up(a_up, 2 * lr))


RW_SCAN_TIME = 256


def _rwkv_scan_kernel(wf_ref, bf_ref, kf_ref, nf_ref, vf_ref, rf_ref,
                      wb_ref, bb_ref, kb_ref, nb_ref, vb_ref, rb_ref, yf_ref, yb_ref, s_ref):
    @pl.when(pl.program_id(0) == 0)
    def _():
        s_ref[...] = jnp.zeros_like(s_ref)

    n_batch, n_time, width = wf_ref.shape
    n_pair = width // LANES
    n_dir_chain = n_batch * n_pair
    n_chain = 2 * n_dir_chain
    rows_all = n_chain * RW_HEAD_DIM
    ones = _block_ones(LANES, LANES, RW_HEAD_DIM)
    vi = lax.broadcasted_iota(jnp.int32, (1, RW_HEAD_DIM, LANES), 1)
    li = lax.broadcasted_iota(jnp.int32, (1, RW_HEAD_DIM, LANES), 2)
    diag = (li % RW_HEAD_DIM) == vi
    n_sub = n_time // SUBLANES

    def seg(x):
        out = jnp.dot(x.reshape(rows_all, LANES).astype(BF16), ones, preferred_element_type=F32)
        return out.reshape(n_chain, RW_HEAD_DIM, LANES)

    def chains(ref, rows):
        x = ref[:, rows, :]
        return [x[b, :, p * LANES:(p + 1) * LANES] for b in range(n_batch) for p in range(n_pair)]

    def sub(i, carry):
        rows_f = pl.ds(pl.multiple_of(i * SUBLANES, SUBLANES), SUBLANES)
        rows_b = pl.ds(pl.multiple_of((n_sub - 1 - i) * SUBLANES, SUBLANES), SUBLANES)
        load = lambda f_ref, b_ref: (jnp.stack(chains(f_ref, rows_f)), jnp.stack(chains(b_ref, rows_b)))
        w8, beta8, kd8 = load(wf_ref, wb_ref), load(bf_ref, bb_ref), load(kf_ref, kb_ref)
        nkk8, v8, r8 = load(nf_ref, nb_ref), load(vf_ref, vb_ref), load(rf_ref, rb_ref)

        def at(pair, t):
            tb = SUBLANES - 1 - t
            return jnp.concatenate([pair[0][:, t:t + 1, :], pair[1][:, tb:tb + 1, :]], axis=0)

        s = s_ref[...]
        rows = []
        value_col = lambda t: seg(jnp.where(diag, at(v8, t), 0.0))
        read_out = lambda sr: jnp.sum(jnp.where(diag, seg(sr), 0.0), axis=1, keepdims=True)
        vcol_next, pending = value_col(0), None
        for t in range(SUBLANES):
            sa = seg(s * at(nkk8, t))
            vcol = vcol_next
            if t + 1 < SUBLANES:
                vcol_next = value_col(t + 1)
            if pending is not None:
                rows.append(read_out(pending))
            s = s * at(w8, t) + sa * at(beta8, t) + vcol * at(kd8, t)
            pending = s * at(r8, t)
        rows.append(read_out(pending))
        s_ref[...] = s
        y_f = jnp.concatenate([row[:n_dir_chain] for row in rows], axis=1)
        y_b = jnp.concatenate([row[n_dir_chain:] for row in rows[::-1]], axis=1)
        for b in range(n_batch):
            for p in range(n_pair):
                c = b * n_pair + p
                yf_ref[b, rows_f, p * LANES:(p + 1) * LANES] = y_f[c]
                yb_ref[b, rows_b, p * LANES:(p + 1) * LANES] = y_b[c]
        return carry

    lax.fori_loop(0, n_sub, sub, 0)


def rwkv_scan_pallas(dec_f, beta_f, kd_f, dec_b, beta_b, kd_b, nkk, v, r, dims):
    B, C, N = dims
    S = C + N
    tc = RW_SCAN_TIME
    assert C % tc == 0 and N % tc == 0
    n_ctx, n_all = C // tc, S // tc
    as3 = lambda z: z.reshape(B, S, RW_WIDTH)
    fwd = pl.BlockSpec((B, tc, RW_WIDTH), lambda j: (0, j, 0))
    bwd = pl.BlockSpec((B, tc, RW_WIDTH),
                       lambda j: (0, jnp.where(j < n_ctx, n_ctx - 1 - j, n_all - 1 - (j - n_ctx)), 0))
    out = jax.ShapeDtypeStruct((B, S, RW_WIDTH), F32)
    y_f, y_b = pl.pallas_call(
        _rwkv_scan_kernel,
        grid=(n_all,),
        in_specs=[fwd] * 6 + [bwd] * 6,
        out_specs=[fwd, bwd],
        out_shape=[out, out],
        scratch_shapes=[pltpu.VMEM((2 * B * (RW_WIDTH // LANES), RW_HEAD_DIM, LANES), F32)],
        compiler_params=_cparams(1),
        name="rwkv_scan",
    )(as3(dec_f), as3(beta_f), as3(kd_f), as3(nkk), as3(v), as3(r),
      as3(dec_b), as3(beta_b), as3(kd_b), as3(nkk), as3(v), as3(r))
    return y_f.reshape(B * S, RW_WIDTH), y_b.reshape(B * S, RW_WIDTH)


NA_BAND = NA_WIN_ROWS * GRID_W


def _na_row_start(j, ctx_blocks, n_rows):
    r = jnp.maximum(j - ctx_blocks, 0)
    return r, jnp.clip(r - NA_WIN_ROWS // 2, 0, n_rows - NA_WIN_ROWS)


def _na_kernel(q_ref, k_ref, v_ref, bias_ref, o_ref, *, n_ctx):
    j = pl.program_id(1)
    ctx_blocks = n_ctx // GRID_W
    n_rows = pl.num_programs(1) - ctx_blocks
    _, row_start = _na_row_start(j, ctx_blocks, n_rows)
    start = pl.multiple_of(n_ctx + row_start * GRID_W, GRID_W)
    q = q_ref[0]
    kb = k_ref[0, pl.ds(start, NA_BAND), :]
    vb = v_ref[0, pl.ds(start, NA_BAND), :]
    kc = k_ref[0, pl.ds(0, n_ctx), :]
    vc = v_ref[0, pl.ds(0, n_ctx), :]
    head_of_lane = lax.broadcasted_iota(jnp.int32, (GRID_W, LANES), 1) // NA_HEAD_DIM
    heads = [(p, h2) for p in range(NA_WIDTH // LANES) for h2 in range(LANES // NA_HEAD_DIM)]
    cols = lambda p: slice(p * LANES, (p + 1) * LANES)
    scores = []
    for p, h2 in heads:
        qm = jnp.where(head_of_lane == h2, q[:, cols(p)], jnp.zeros((GRID_W, LANES), BF16))
        s_loc = lax.dot_general(qm, kb[:, cols(p)], NT_DIMS, preferred_element_type=F32)
        s_ctx = lax.dot_general(qm, kc[:, cols(p)], NT_DIMS, preferred_element_type=F32)
        scores.append((s_loc + bias_ref[0, 2 * p + h2], s_ctx))
    probs = []
    for s_loc, s_ctx in scores:
        m = jnp.maximum(jnp.max(s_loc, axis=-1, keepdims=True), jnp.max(s_ctx, axis=-1, keepdims=True))
        e_loc = jnp.exp(s_loc - m)
        e_ctx = jnp.exp(s_ctx - m)
        den = jnp.sum(e_loc, axis=-1, keepdims=True) + jnp.sum(e_ctx, axis=-1, keepdims=True)
        probs.append((e_loc.astype(BF16), e_ctx.astype(BF16), den))
    outs = []
    for (p, h2), (e_loc, e_ctx, den) in zip(heads, probs):
        o = (jnp.dot(e_loc, vb[:, cols(p)], preferred_element_type=F32)
             + jnp.dot(e_ctx, vc[:, cols(p)], preferred_element_type=F32))
        outs.append(o / den)
    for p in range(NA_WIDTH // LANES):
        o_ref[0, :, cols(p)] = jnp.where(head_of_lane == 0, outs[2 * p], outs[2 * p + 1])


def _na_bias_table(rpb):
    kw = NA_WIN_COLS
    n_col_off = 2 * kw - 1
    j = np.arange(GRID_W)
    col_start = np.clip(j - kw // 2, 0, GRID_W - kw)
    col_in = (j[None, :] >= col_start[:, None]) & (j[None, :] < col_start[:, None] + kw)
    col_off = np.clip(j[None, :] - j[:, None], -(kw - 1), kw - 1) + (kw - 1)
    pick = (col_off.reshape(1, -1) == np.arange(n_col_off)[:, None]).astype(np.float32)
    toep = jnp.dot(rpb.astype(F32).reshape(-1, n_col_off), pick, precision=lax.Precision.HIGHEST)
    toep = toep.reshape(NA_HEADS, 2 * NA_WIN_ROWS - 1, GRID_W, GRID_W)
    toep = jnp.where(col_in[None, None], toep, NEG_INF)
    tab = jnp.stack([toep[:, NA_WIN_ROWS - 1 - d:2 * NA_WIN_ROWS - 1 - d] for d in range(NA_WIN_ROWS)], 0)
    tab = tab.transpose(0, 1, 3, 2, 4).reshape(NA_WIN_ROWS, NA_HEADS, GRID_W, NA_BAND)
    return jnp.concatenate([tab, jnp.full((1,) + tab.shape[1:], NEG_INF, F32)], 0)


def attention_pallas(q, k, v, rpb, dims):
    B, C, N = dims
    S = C + N
    W = NA_WIDTH
    n_rows = N // GRID_W
    ctx_blocks = C // GRID_W
    assert n_rows >= NA_WIN_ROWS and N % GRID_W == 0 and C % GRID_W == 0
    bias = _na_bias_table(rpb)
    as3 = lambda z: z.reshape(B, S, W)

    def bias_idx(b, j):
        r, row_start = _na_row_start(j, ctx_blocks, n_rows)
        return (jnp.where(j < ctx_blocks, NA_WIN_ROWS, r - row_start), 0, 0, 0)

    out = pl.pallas_call(
        functools.partial(_na_kernel, n_ctx=C),
        grid=(B, S // GRID_W),
        in_specs=[
            pl.BlockSpec((1, GRID_W, W), lambda b, j: (b, j, 0)),
            pl.BlockSpec((1, S, W), lambda b, j: (b, 0, 0)),
            pl.BlockSpec((1, S, W), lambda b, j: (b, 0, 0)),
            pl.BlockSpec((1, NA_HEADS, GRID_W, NA_BAND), bias_idx),
        ],
        out_specs=pl.BlockSpec((1, GRID_W, W), lambda b, j: (b, j, 0)),
        out_shape=jax.ShapeDtypeStruct((B, S, W), F32),
        compiler_params=_cparams(2),
        name="na_attention",
    )(as3(q), as3(k), as3(v), bias)
    return out.reshape(B * S, W)


def _layer_norm(x, g, b):
    mu = jnp.mean(x, axis=-1, keepdims=True)
    xc = x - mu
    var = jnp.mean(xc * xc, axis=-1, keepdims=True)
    return xc * lax.rsqrt(var + LN_EPS) * g + b


def _mixer_tail(h, y, gate, ln_g, ln_b, shift, scale, router, h_out_ref, f_ref, s_ref):
    h1 = _layer_norm(DEEPNORM_ALPHA * h + gate * y, ln_g, ln_b)
    f = h1 * (1.0 + scale) + shift
    h_out_ref[...] = h1
    f_ref[...] = f
    s_ref[...] = jax.nn.sigmoid(jnp.dot(f, router, preferred_element_type=F32, precision=lax.Precision.HIGHEST))


def _even_out_kernel(na_ref, yf_ref, yb_ref, r_ref, v_ref, kdf_ref, kdb_ref, glow_ref, h_ref,
                     gate_ref, shift_ref, scale_ref, ones_ref, gng_ref, gnb_ref, rk_ref, gup_ref, wout_ref,
                     lng_ref, lnb_ref, router_ref, h_out_ref, f_ref, s_ref):
    ones2 = ones_ref[...]
    inv = 1.0 / RW_HEAD_DIM
    y = yf_ref[...] + yb_ref[...]
    mu = _seg_sum(y, ones2) * inv
    yc = y - mu
    var = _seg_sum(yc * yc, ones2) * inv
    yn = yc * lax.rsqrt(var + RW_GN_EPS) * gng_ref[...] + gnb_ref[...]
    r = r_ref[...]
    bonus = (_seg_sum(r * kdf_ref[...] * rk_ref[...], ones2) + _seg_sum(r * kdb_ref[...] * rk_ref[...], ones2))
    gate = jnp.dot(jax.nn.sigmoid(glow_ref[...]).astype(BF16), gup_ref[...], preferred_element_type=F32)
    rw = (yn + bonus * v_ref[...]) * gate
    mix = jnp.concatenate([na_ref[...], rw], axis=-1).astype(BF16)
    y_mix = jnp.dot(mix, wout_ref[...], preferred_element_type=F32)
    _mixer_tail(h_ref[...], y_mix, gate_ref[...], lng_ref[...], lnb_ref[...], shift_ref[...], scale_ref[...],
                router_ref[...], h_out_ref, f_ref, s_ref)


def _tail_specs(R, D, E, tm):
    row = lambda width: pl.BlockSpec((tm, width), lambda i: (i, 0))
    return ([row(D), row(D), row(E)],
            [jax.ShapeDtypeStruct((R, D), F32), jax.ShapeDtypeStruct((R, D), F32), jax.ShapeDtypeStruct((R, E), F32)])


def even_out_pallas(na, y_f, y_b, r, v, kd_f, kd_b, glow, h, mods, dims, g_up, r_k, gn_g, gn_b, w_out,
                    ln_g, ln_b, router_w):
    B, C, N = dims
    R, D = h.shape
    E = router_w.shape[1]
    tm = ROW_TILE
    mod_idx = _mod_index((C + N) // tm, C // tm, B)
    row = lambda width: pl.BlockSpec((tm, width), lambda i: (i, 0))
    ones2 = _block_ones(2 * RW_WIDTH, RW_WIDTH, RW_HEAD_DIM)
    g_up_p = jnp.pad(g_up, ((0, LANES - g_up.shape[0]), (0, 0))).astype(BF16)
    vec = lambda z: z.reshape(1, -1)
    out_specs, out_shape = _tail_specs(R, D, E, tm)
    return pl.pallas_call(
        _even_out_kernel,
        grid=(R // tm,),
        in_specs=[row(NA_WIDTH)] + [row(RW_WIDTH)] * 6 + [row(LANES), row(D),
                  _mod_spec(2, mod_idx), _mod_spec(3, mod_idx), _mod_spec(4, mod_idx),
                  _full_spec((2 * RW_WIDTH, RW_WIDTH)),
                  _full_spec((1, RW_WIDTH)), _full_spec((1, RW_WIDTH)), _full_spec((1, RW_WIDTH)),
                  _full_spec((LANES, RW_WIDTH)), _full_spec((D, D)),
                  _full_spec((1, D)), _full_spec((1, D)), _full_spec((D, E))],
        out_specs=out_specs,
        out_shape=out_shape,
        compiler_params=_cparams(1),
        name="even_out",
    )(na, y_f, y_b, r, v, kd_f, kd_b, glow, h, mods, mods, mods, ones2, vec(gn_g), vec(gn_b), vec(r_k),
      g_up_p, w_out.astype(BF16), vec(ln_g), vec(ln_b), router_w)


MOE_TOKEN_TILE = 256


def _swiglu_bf16(x, wg, wu, wd):
    g = jnp.dot(x, wg, preferred_element_type=F32)
    u = jnp.dot(x, wu, preferred_element_type=F32)
    mid = (g * jax.nn.sigmoid(g) * u).astype(BF16)
    return jnp.dot(mid, wd, preferred_element_type=F32)


def _row_copy(src_ref, src_row, dst_ref, dst_row, sem):
    return pltpu.make_async_copy(src_ref.at[pl.ds(src_row, 1), :], dst_ref.at[pl.ds(dst_row, 1), :], sem)


def _slot_kernel(e_ref, rank_ref, starts_ref, slot_ref):
    e = e_ref[...]
    tm = e.shape[0]
    lane = lax.broadcasted_iota(jnp.int32, (tm, starts_ref.shape[1]), 1)
    col8 = lax.broadcasted_iota(jnp.int32, (tm, TOP_K), 1)
    first = jnp.zeros((tm, TOP_K), F32)
    for k in range(TOP_K):
        hit = lane == e[:, k:k + 1]
        first = jnp.where(col8 == k, jnp.sum(jnp.where(hit, starts_ref[...], 0.0), axis=-1, keepdims=True), first)
    slot_ref[...] = first.astype(jnp.int32) + rank_ref[...]


def moe_slots_pallas(e_idx, rank, starts):
    T = e_idx.shape[0]
    E = starts.shape[0]
    tm = ROW_TILE
    row8 = pl.BlockSpec((tm, TOP_K), lambda i: (i, 0))
    return pl.pallas_call(
        _slot_kernel,
        grid=(T // tm,),
        in_specs=[row8, row8, _full_spec((1, E))],
        out_specs=row8,
        out_shape=jax.ShapeDtypeStruct((T, TOP_K), jnp.int32),
        compiler_params=_cparams(1),
        name="moe_slots",
    )(e_idx, rank, starts.astype(F32).reshape(1, E))


def _dispatch_kernel(slot_ref, f_ref, xs_ref, sem):
    n_tok = f_ref.shape[0]

    def issue(t, carry):
        for k in range(TOP_K):
            _row_copy(f_ref, t, xs_ref, slot_ref[t * TOP_K + k], sem).start()
        return carry

    lax.fori_loop(0, n_tok, issue, 0)

    def drain(t, carry):
        for k in range(TOP_K):
            _row_copy(f_ref, 0, xs_ref, 0, sem).wait()
        return carry

    lax.fori_loop(0, n_tok, drain, 0)


def _slot_specs(tm):
    return [pl.BlockSpec((tm * TOP_K,), lambda i: (i,), memory_space=pltpu.SMEM)]


def moe_dispatch_pallas(f, slot_flat):
    T, D = f.shape
    tm = MOE_TOKEN_TILE
    assert T % tm == 0
    return pl.pallas_call(
        _dispatch_kernel,
        grid=(T // tm,),
        in_specs=_slot_specs(tm) + [pl.BlockSpec((tm, D), lambda i: (i, 0))],
        out_specs=pl.BlockSpec(memory_space=pl.ANY),
        out_shape=jax.ShapeDtypeStruct((T * TOP_K, D), F32),
        scratch_shapes=[pltpu.SemaphoreType.DMA(())],
        compiler_params=_cparams(1),
        name="moe_dispatch",
    )(slot_flat, f)


def _expert_item_kernel(blk_ref, e_ref, lo_ref, hi_ref, first_ref, x_ref, wg_ref, wu_ref, wd_ref, o_ref,
                        wg16_ref, wu16_ref, wd16_ref):
    i = pl.program_id(0)
    lo, hi = lo_ref[i], hi_ref[i]

    @pl.when((i == 0) | (e_ref[i] != e_ref[jnp.maximum(i - 1, 0)]))
    def _():
        wg16_ref[...] = wg_ref[0, 0].astype(BF16)
        wu16_ref[...] = wu_ref[0, 0].astype(BF16)
        wd16_ref[...] = wd_ref[0, 0].astype(BF16)

    @pl.when(hi > lo)
    def _():
        y = _swiglu_bf16(x_ref[...].astype(BF16), wg16_ref[...], wu16_ref[...], wd16_ref[...])
        rows = blk_ref[i] * MOE_BLOCK + lax.broadcasted_iota(jnp.int32, (MOE_BLOCK, 1), 0)
        y = jnp.where((rows >= lo) & (rows < hi), y, 0.0)

        @pl.when(first_ref[i] == 1)
        def _():
            o_ref[...] = y

        @pl.when(first_ref[i] == 0)
        def _():
            o_ref[...] += y


def moe_experts_pallas(xs, items, layer, wg, wu, wd):
    n_rows, D = xs.shape
    F = wg.shape[-1]
    n_items = items[0].shape[0]
    grid_spec = pltpu.PrefetchScalarGridSpec(
        num_scalar_prefetch=5,
        grid=(n_items,),
        in_specs=[
            pl.BlockSpec((MOE_BLOCK, D), lambda i, blk, e, lo, hi, first: (blk[i], 0)),
            pl.BlockSpec((1, 1, D, F), lambda i, blk, e, lo, hi, first: (layer, e[i], 0, 0)),
            pl.BlockSpec((1, 1, D, F), lambda i, blk, e, lo, hi, first: (layer, e[i], 0, 0)),
            pl.BlockSpec((1, 1, F, D), lambda i, blk, e, lo, hi, first: (layer, e[i], 0, 0)),
        ],
        out_specs=pl.BlockSpec((MOE_BLOCK, D), lambda i, blk, e, lo, hi, first: (blk[i], 0)),
        scratch_shapes=[pltpu.VMEM((D, F), BF16), pltpu.VMEM((D, F), BF16), pltpu.VMEM((F, D), BF16)],
    )
    return pl.pallas_call(
        _expert_item_kernel,
        grid_spec=grid_spec,
        out_shape=jax.ShapeDtypeStruct((n_rows, D), F32),
        compiler_params=_cparams(1),
        name="moe_experts",
    )(*items, xs, wg, wu, wd)


def _combine_kernel(slot_ref, w_ref, f_ref, h_ref, gate_ref, lng_ref, lnb_ref,
                    sg_ref, su_ref, sd_ref, ys_ref, o_ref, buf_ref, sem):
    n_tok = f_ref.shape[0]

    def issue(t, carry):
        for k in range(TOP_K):
            pltpu.make_async_copy(ys_ref.at[pl.ds(slot_ref[t * TOP_K + k], 1), :],
                                  buf_ref.at[k, pl.ds(t, 1), :], sem).start()
        return carry

    lax.fori_loop(0, n_tok, issue, 0)
    acc = _swiglu_bf16(f_ref[...].astype(BF16), sg_ref[...], su_ref[...], sd_ref[...])

    def drain(t, carry):
        for k in range(TOP_K):
            pltpu.make_async_copy(ys_ref.at[pl.ds(0, 1), :], buf_ref.at[0, pl.ds(0, 1), :], sem).wait()
        return carry

    lax.fori_loop(0, n_tok, drain, 0)
    w = w_ref[...]
    for k in range(TOP_K):
        acc = acc + w[:, k:k + 1] * buf_ref[k]
    o_ref[...] = _layer_norm(DEEPNORM_ALPHA * h_ref[...] + gate_ref[...] * acc, lng_ref[...], lnb_ref[...])


def moe_combine_pallas(ys, slot_flat, w_sel, f, h, mods, dims, ln_g, ln_b, sg, su, sd):
    B, C, N = dims
    T, D = f.shape
    tm = MOE_TOKEN_TILE
    F = sg.shape[-1]
    mod_idx = _mod_index((C + N) // tm, C // tm, B)
    vec = lambda z: z.reshape(1, -1)
    return pl.pallas_call(
        _combine_kernel,
        grid=(T // tm,),
        in_specs=_slot_specs(tm) + [
            pl.BlockSpec((tm, TOP_K), lambda i: (i, 0)),
            pl.BlockSpec((tm, D), lambda i: (i, 0)),
            pl.BlockSpec((tm, D), lambda i: (i, 0)),
            _mod_spec(5, mod_idx), _full_spec((1, D)), _full_spec((1, D)),
            _full_spec((D, F)), _full_spec((D, F)), _full_spec((F, D)),
            pl.BlockSpec(memory_space=pl.ANY),
        ],
        out_specs=pl.BlockSpec((tm, D), lambda i: (i, 0)),
        out_shape=jax.ShapeDtypeStruct((T, D), F32),
        scratch_shapes=[pltpu.VMEM((TOP_K, tm, D), F32), pltpu.SemaphoreType.DMA(())],
        compiler_params=_cparams(1),
        name="moe_combine",
    )(slot_flat, w_sel, f, h, mods, vec(ln_g), vec(ln_b), sg, su, sd, ys)


REMOVED = -3e38


def _router_kernel(s_ref, bias_ref, e_ref, w_ref, rank_ref, cnt_ref, carry_ref):
    @pl.when(pl.program_id(0) == 0)
    def _():
        carry_ref[...] = jnp.zeros_like(carry_ref)

    s = s_ref[...]
    tm, n_exp = s.shape
    per_group = n_exp // N_GROUPS
    lane_i = lax.broadcasted_iota(jnp.int32, (tm, n_exp), 1)
    lane = lane_i.astype(F32)
    group_of_lane = lane_i // per_group
    big = float(n_exp)
    rmax = lambda z: jnp.max(z, axis=-1, keepdims=True)
    first_at = lambda z, m: jnp.min(jnp.where(z == m, lane, big), axis=-1, keepdims=True)

    grp = s + bias_ref[...]
    g_score = []
    for g in range(N_GROUPS):
        mg = jnp.where(group_of_lane == g, grp, REMOVED)
        m1 = rmax(mg)
        m2 = rmax(jnp.where(lane == first_at(mg, m1), REMOVED, mg))
        g_score.append(m1 + m2)
    choice = jnp.full_like(grp, NEG_INF)
    for g in range(N_GROUPS):
        ahead = jnp.zeros((tm, 1), F32)
        for g2 in range(N_GROUPS):
            if g2 != g:
                beats = (g_score[g2] > g_score[g]) | ((g_score[g2] == g_score[g]) & (g2 < g))
                ahead = ahead + beats.astype(F32)
        choice = jnp.where((group_of_lane == g) & (ahead < TOPK_GROUPS), grp, choice)

    col8 = lax.broadcasted_iota(jnp.int32, (tm, TOP_K), 1)
    e_out = jnp.zeros((tm, TOP_K), F32)
    w_out = jnp.zeros((tm, TOP_K), F32)
    picked = []
    onehot = jnp.zeros((tm, n_exp), F32)
    for k in range(TOP_K):
        idx = first_at(choice, rmax(choice))
        hit = lane == idx
        picked.append(hit)
        onehot = jnp.where(hit, 1.0, onehot)
        e_out = jnp.where(col8 == k, idx, e_out)
        w_out = jnp.where(col8 == k, jnp.sum(jnp.where(hit, s, 0.0), axis=-1, keepdims=True), w_out)
        choice = jnp.where(hit, REMOVED, choice)
    ri = lax.broadcasted_iota(jnp.int32, (tm, tm), 0)
    ci = lax.broadcasted_iota(jnp.int32, (tm, tm), 1)
    before = jnp.dot((ci < ri).astype(BF16), onehot.astype(BF16), preferred_element_type=F32) + carry_ref[0:1, :]
    rank = jnp.zeros((tm, TOP_K), F32)
    for k in range(TOP_K):
        rank = jnp.where(col8 == k, jnp.sum(jnp.where(picked[k], before, 0.0), axis=-1, keepdims=True), rank)
    total = carry_ref[0:1, :] + jnp.sum(onehot, axis=0, keepdims=True)
    carry_ref[...] = jnp.broadcast_to(total, carry_ref.shape)
    cnt_ref[...] = jnp.broadcast_to(total, cnt_ref.shape)
    e_ref[...] = e_out.astype(jnp.int32)
    w_ref[...] = w_out / jnp.sum(w_out, axis=-1, keepdims=True) * ROUTED_SCALE
    rank_ref[...] = rank.astype(jnp.int32)


def router_pallas(s, router_b):
    T, E = s.shape
    tm = ROW_TILE
    row8 = pl.BlockSpec((tm, TOP_K), lambda i: (i, 0))
    e_idx, w_sel, rank, cnt = pl.pallas_call(
        _router_kernel,
        grid=(T // tm,),
        in_specs=[pl.BlockSpec((tm, E), lambda i: (i, 0)), _full_spec((1, E))],
        out_specs=[row8, row8, row8, _full_spec((SUBLANES, E))],
        out_shape=[jax.ShapeDtypeStruct((T, TOP_K), jnp.int32), jax.ShapeDtypeStruct((T, TOP_K), F32),
                   jax.ShapeDtypeStruct((T, TOP_K), jnp.int32), jax.ShapeDtypeStruct((SUBLANES, E), F32)],
        scratch_shapes=[pltpu.VMEM((SUBLANES, E), F32)],
        compiler_params=_cparams(1),
        name="moe_router",
    )(s, router_b.astype(F32).reshape(1, E))
    return e_idx, w_sel, rank, cnt[0].astype(jnp.int32)


def moe_layer(f, s, h, mods, dims, ln_g, ln_b, router_b, layer, wg, wu, wd, sg, su, sd):
    T, D = f.shape
    E = s.shape[-1]
    e_idx, w_sel, rank, counts = router_pallas(s, router_b)
    n_asg = T * TOP_K
    assert n_asg % MOE_BLOCK == 0
    i32 = jnp.int32
    ends = jnp.cumsum(counts).astype(i32)
    starts = ends - counts
    slot_flat = moe_slots_pallas(e_idx, rank, starts).reshape(-1)
    nb = n_asg // MOE_BLOCK
    first_blk = starts // MOE_BLOCK
    nblk = jnp.where(counts > 0, (ends - 1) // MOE_BLOCK - first_blk + 1, 0)
    item_ends = jnp.cumsum(nblk).astype(i32)
    item_starts = item_ends - nblk
    n_items = nb + E
    it = jnp.arange(n_items, dtype=i32)
    real = it < item_ends[-1]
    e_of = jnp.sum((item_ends[None, :] <= jnp.where(real, it, item_ends[-1] - 1)[:, None]).astype(i32), axis=1)
    is_e = e_of[:, None] == jnp.arange(E, dtype=i32)[None, :]
    pick = lambda tab: jnp.sum(jnp.where(is_e, tab[None, :], 0), axis=1)
    blk = jnp.where(real, pick(first_blk) + it - pick(item_starts), nb - 1).astype(i32)
    lo = jnp.where(real, jnp.maximum(pick(starts), blk * MOE_BLOCK), 0).astype(i32)
    hi = jnp.where(real, jnp.minimum(pick(ends), (blk + 1) * MOE_BLOCK), 0).astype(i32)
    first = (real & (blk != jnp.concatenate([jnp.full((1,), -1, i32), blk[:-1]]))).astype(i32)
    xs = moe_dispatch_pallas(f, slot_flat)
    ys = moe_experts_pallas(xs, (blk, e_of, lo, hi, first), layer, wg, wu, wd)
    return moe_combine_pallas(ys, slot_flat, w_sel, f, h, mods, dims, ln_g, ln_b,
                              sg.astype(BF16), su.astype(BF16), sd.astype(BF16))


ML_QK_WIDTH = ML_HEADS * ML_QK_DIM
ROPE_GROUP = ML_QK_DIM // 4
GATE_IN, GATE_FORGET = 0, 2 * ML_HEADS


def _log_sigmoid(x):
    return -_softplus(-x)


def _proj_odd_kernel(h_ref, shift_ref, scale_ref, wqk_ref, wv_ref, wo_ref, wg_ref, wgt_ref, gb_ref, gbt_ref,
                     cos_ref, sin_ref, q_ref, k_ref, v_ref, o_ref, g_ref, gt_ref):
    a16 = (h_ref[...] * (1.0 + scale_ref[...]) + shift_ref[...]).astype(BF16)
    qk = jnp.dot(a16, wqk_ref[...], preferred_element_type=F32)
    lane = lax.broadcasted_iota(jnp.int32, (1, ML_QK_WIDTH), 1)
    first_of_pair = (lane % (2 * ROPE_GROUP)) < ROPE_GROUP
    cos, sin = cos_ref[...], sin_ref[...]

    def rope(z):
        partner = jnp.where(first_of_pair, pltpu.roll(z, ML_QK_WIDTH - ROPE_GROUP, 1), pltpu.roll(z, ROPE_GROUP, 1))
        return z * cos + partner * sin

    q_ref[...] = rope(qk[:, :ML_QK_WIDTH] * ML_QK_DIM ** -0.5).astype(BF16)
    k_ref[...] = rope(qk[:, ML_QK_WIDTH:]).astype(BF16)
    v_ref[...] = jnp.dot(a16, wv_ref[...], preferred_element_type=F32).astype(BF16)
    o_ref[...] = jnp.dot(a16, wo_ref[...], preferred_element_type=F32)
    g = jnp.dot(a16, wg_ref[...], preferred_element_type=F32) + gb_ref[...]
    gl = lax.broadcasted_iota(jnp.int32, g.shape, 1)
    g_ref[...] = jnp.where((gl >= GATE_FORGET) & (gl < 2 * GATE_FORGET), _log_sigmoid(g), g)
    gt = lax.dot_general(wgt_ref[...], a16, NT_DIMS, preferred_element_type=F32) + gbt_ref[...]
    gs = lax.broadcasted_iota(jnp.int32, gt.shape, 0)
    gt_ref[...] = jnp.where((gs >= GATE_FORGET) & (gs < 2 * GATE_FORGET), _log_sigmoid(gt), gt)


def _rope_tables(C, N):
    t = jnp.arange(N)
    pos = jnp.stack([(t // GRID_W).astype(F32), (t % GRID_W).astype(F32)], 0)
    lane = jnp.arange(ML_QK_DIM)
    inv = ROPE_BASE ** (-(lane % ROPE_GROUP).astype(F32) / ROPE_GROUP)
    ang = pos[lane // (2 * ROPE_GROUP)].T * inv[None, :]
    sign = jnp.where((lane % (2 * ROPE_GROUP)) < ROPE_GROUP, -1.0, 1.0)
    heads = lambda z: jnp.tile(z, (1, ML_HEADS))
    cos = jnp.concatenate([jnp.ones((C, ML_QK_WIDTH), F32), heads(jnp.cos(ang))], 0)
    sin = jnp.concatenate([jnp.zeros((C, ML_QK_WIDTH), F32), heads(jnp.sin(ang) * sign)], 0)
    return cos, sin


def proj_odd_pallas(h, mods, dims, w_in, gate_b):
    B, C, N = dims
    R, D = h.shape
    tm = ROW_TILE
    tpb = (C + N) // tm
    mod_idx = _mod_index(tpb, C // tm, B)
    o_qk, o_v, o_o = 2 * ML_QK_WIDTH, 2 * ML_QK_WIDTH + ML_WIDTH, 2 * ML_QK_WIDTH + 2 * ML_WIDTH
    n_gate = w_in.shape[1] - o_o
    w16 = w_in.astype(BF16)
    w_g = jnp.pad(w16[:, o_o:], ((0, 0), (0, LANES - n_gate)))
    gb = jnp.pad(gate_b.astype(F32).reshape(-1), (0, LANES - n_gate))
    cos, sin = _rope_tables(C, N)
    row = lambda width: pl.BlockSpec((tm, width), lambda i: (i, 0))
    seg = pl.BlockSpec((tm, ML_QK_WIDTH), lambda i: (i % tpb, 0))
    return pl.pallas_call(
        _proj_odd_kernel,
        grid=(R // tm,),
        in_specs=[row(D), _mod_spec(0, mod_idx), _mod_spec(1, mod_idx),
                  _full_spec((D, 2 * ML_QK_WIDTH)), _full_spec((D, ML_WIDTH)), _full_spec((D, ML_WIDTH)),
                  _full_spec((D, LANES)), _full_spec((LANES, D)), _full_spec((1, LANES)), _full_spec((LANES, 1)),
                  seg, seg],
        out_specs=[row(ML_QK_WIDTH), row(ML_QK_WIDTH), row(ML_WIDTH), row(ML_WIDTH), row(LANES),
                   pl.BlockSpec((LANES, tm), lambda i: (0, i))],
        out_shape=[jax.ShapeDtypeStruct((R, ML_QK_WIDTH), BF16), jax.ShapeDtypeStruct((R, ML_QK_WIDTH), BF16),
                   jax.ShapeDtypeStruct((R, ML_WIDTH), BF16), jax.ShapeDtypeStruct((R, ML_WIDTH), F32),
                   jax.ShapeDtypeStruct((R, LANES), F32), jax.ShapeDtypeStruct((LANES, R), F32)],
        compiler_params=_cparams(1),
        name="proj_odd",
    )(h, mods, mods, w16[:, :o_qk], w16[:, o_qk:o_v], w16[:, o_v:o_o], w_g, w_g.T, gb.reshape(1, LANES),
      gb.reshape(LANES, 1), cos, sin)


def _split3_bf16(x, axis):
    x1 = x.astype(BF16)
    r1 = x - x1.astype(F32)
    x2 = r1.astype(BF16)
    x3 = (r1 - x2.astype(F32)).astype(BF16)
    return jnp.concatenate([x1, x2, x3], axis=axis)


def _mlstm_kernel(q_ref, k_ref, v_ref, g_ref, gt_ref, h_ref, c_ref, n_ref, m_ref, *, reverse):
    @pl.when(pl.program_id(1) == 0)
    def _():
        c_ref[...] = jnp.zeros_like(c_ref)
        n_ref[...] = jnp.zeros_like(n_ref)
        m_ref[...] = jnp.zeros_like(m_ref)

    L = q_ref.shape[1]
    ti = lax.broadcasted_iota(jnp.int32, (L, L), 0)
    si = lax.broadcasted_iota(jnp.int32, (L, L), 1)
    seen = (si >= ti) if reverse else (si <= ti)
    g = g_ref[0]
    gt = gt_ref[...]
    b_cols3 = jnp.dot(seen.astype(BF16), _split3_bf16(g, 1), preferred_element_type=F32)
    b_cols = b_cols3[:, :LANES] + b_cols3[:, LANES:2 * LANES] + b_cols3[:, 2 * LANES:]
    b_rows3 = lax.dot_general(_split3_bf16(gt, 0), seen.astype(BF16), NT_DIMS, preferred_element_type=F32)
    b_rows = b_rows3[:LANES] + b_rows3[LANES:2 * LANES] + b_rows3[2 * LANES:]
    half = lax.broadcasted_iota(jnp.int32, (1, LANES), 1) // ML_QK_DIM
    row_half = lax.broadcasted_iota(jnp.int32, (LANES, 1), 0) // ML_QK_DIM
    d_off = ML_HEADS if reverse else 0
    tn = (((0,), (0,)), ((), ()))
    heads = [(hd // 2, hd % 2) for hd in range(ML_HEADS)]
    pair = lambda ref, p: ref[0, :, p * LANES:(p + 1) * LANES]
    value = lambda hd: v_ref[0, :, hd * ML_V_DIM:(hd + 1) * ML_V_DIM]

    decay = []
    for hd, (p, h2) in enumerate(heads):
        gi, gf = GATE_IN + d_off + hd, GATE_FORGET + d_off + hd
        ig_col, ig_row = g[:, gi:gi + 1], gt[gi:gi + 1, :]
        b_col, b_row = b_cols[:, gf:gf + 1], b_rows[gf:gf + 1, :]
        m0 = m_ref[p][:, h2 * ML_QK_DIM:h2 * ML_QK_DIM + 1]
        dlog = jnp.where(seen, b_col - b_row + ig_row, NEG_INF)
        inter = b_col + m0
        m_t = jnp.maximum(jnp.max(dlog, axis=-1, keepdims=True), inter)
        b_end = jnp.sum(g[:, gf:gf + 1], axis=0, keepdims=True)
        g_col = b_end - b_col + ig_col
        m_chunk = jnp.max(g_col, axis=0, keepdims=True)
        m_new = jnp.maximum(b_end + m0, m_chunk)
        decay.append(dict(dw=jnp.exp(dlog - m_t), iw=jnp.exp(inter - m_t), floor=jnp.exp(-m_t),
                          kw=jnp.exp(g_col - m_chunk), m_new=m_new,
                          fa=jnp.exp(b_end + m0 - m_new), fb=jnp.exp(m_chunk - m_new)))

    prods = []
    for p, h2 in heads:
        qm = jnp.where(half == h2, pair(q_ref, p), jnp.zeros((L, LANES), BF16))
        qk = lax.dot_general(qm, pair(k_ref, p), NT_DIMS, preferred_element_type=F32)
        qc = jnp.dot(qm, c_ref[p].astype(BF16), preferred_element_type=F32)
        qn = jnp.sum(qm.astype(F32) * n_ref[p], axis=-1, keepdims=True)
        prods.append((qk, qc, qn))

    for hd, ((qk, qc, qn), dc) in enumerate(zip(prods, decay)):
        sc = qk * dc['dw']
        num = jnp.dot(sc.astype(BF16), value(hd), preferred_element_type=F32) + dc['iw'] * qc
        den = jnp.sum(sc, axis=-1, keepdims=True) + dc['iw'] * qn
        h_ref[0, :, hd * ML_V_DIM:(hd + 1) * ML_V_DIM] = num / jnp.maximum(jnp.abs(den), dc['floor'])

    for p in range(ML_HEADS // 2):
        c_old, n_old, m_old = c_ref[p], n_ref[p], m_ref[p]
        c_new, n_new, m_new_pair = c_old, n_old, m_old
        for h2 in range(2):
            hd = 2 * p + h2
            dc = decay[hd]
            kw = jnp.where(half == h2, pair(k_ref, p), jnp.zeros((L, LANES), BF16)).astype(F32) * dc['kw']
            kv = lax.dot_general(kw.astype(BF16), value(hd), tn, preferred_element_type=F32)
            c_new = jnp.where(row_half == h2, dc['fa'] * c_old + dc['fb'] * kv, c_new)
            n_new = jnp.where(half == h2, dc['fa'] * n_old + dc['fb'] * jnp.sum(kw, axis=0, keepdims=True), n_new)
            m_new_pair = jnp.where(half == h2, dc['m_new'], m_new_pair)
        c_ref[p] = c_new
        n_ref[p] = n_new
        m_ref[p] = m_new_pair


def mlstm_pallas(q, k, v, g, gt, dims, reverse):
    B, C, N = dims
    S = C + N
    L = ML_CHUNK
    assert C % L == 0 and N % L == 0
    n_ctx, n_all = C // L, S // L
    if reverse:
        chunk = lambda j: jnp.where(j < n_ctx, n_ctx - 1 - j, n_all - 1 - (j - n_ctx))
    else:
        chunk = lambda j: j
    blk = lambda width: pl.BlockSpec((1, L, width), lambda b, j: (b, chunk(j), 0))
    n_pair = ML_HEADS // 2
    out = pl.pallas_call(
        functools.partial(_mlstm_kernel, reverse=reverse),
        grid=(B, n_all),
        in_specs=[blk(ML_QK_WIDTH), blk(ML_QK_WIDTH), blk(ML_WIDTH), blk(LANES),
                  pl.BlockSpec((LANES, L), lambda b, j: (0, b * n_all + chunk(j)))],
        out_specs=blk(ML_WIDTH),
        out_shape=jax.ShapeDtypeStruct((B, S, ML_WIDTH), F32),
        scratch_shapes=[pltpu.VMEM((n_pair, LANES, ML_V_DIM), F32), pltpu.VMEM((n_pair, 1, LANES), F32),
                        pltpu.VMEM((n_pair, 1, LANES), F32)],
        compiler_params=_cparams(2),
        name="mlstm_bwd" if reverse else "mlstm_fwd",
    )(q.reshape(B, S, -1), k.reshape(B, S, -1), v.reshape(B, S, -1), g.reshape(B, S, -1), gt)
    return out.reshape(B * S, ML_WIDTH)


def _odd_out_kernel(hf_ref, hb_ref, o_ref, h_ref, gate_ref, shift_ref, scale_ref, ng_ref, wout_ref,
                    lng_ref, lnb_ref, router_ref, h_out_ref, f_ref, s_ref):
    hs = hf_ref[...] + hb_ref[...]
    parts = []
    for hd in range(ML_HEADS):
        x = hs[:, hd * ML_V_DIM:(hd + 1) * ML_V_DIM]
        parts.append(x * lax.rsqrt(jnp.mean(x * x, axis=-1, keepdims=True) + ML_NORM_EPS))
    hn = jnp.concatenate(parts, axis=-1) * ng_ref[...] * jax.nn.sigmoid(o_ref[...])
    y = jnp.dot(hn.astype(BF16), wout_ref[...], preferred_element_type=F32)
    _mixer_tail(h_ref[...], y, gate_ref[...], lng_ref[...], lnb_ref[...], shift_ref[...], scale_ref[...],
                router_ref[...], h_out_ref, f_ref, s_ref)


def odd_out_pallas(h_f, h_b, o, h, mods, dims, norm_g, w_out, ln_g, ln_b, router_w):
    B, C, N = dims
    R, D = h.shape
    E = router_w.shape[1]
    tm = ROW_TILE
    mod_idx = _mod_index((C + N) // tm, C // tm, B)
    row = lambda width: pl.BlockSpec((tm, width), lambda i: (i, 0))
    vec = lambda z: z.reshape(1, -1)
    out_specs, out_shape = _tail_specs(R, D, E, tm)
    return pl.pallas_call(
        _odd_out_kernel,
        grid=(R // tm,),
        in_specs=[row(ML_WIDTH), row(ML_WIDTH), row(ML_WIDTH), row(D),
                  _mod_spec(2, mod_idx), _mod_spec(3, mod_idx), _mod_spec(4, mod_idx),
                  _full_spec((1, ML_WIDTH)), _full_spec((ML_WIDTH, D)),
                  _full_spec((1, D)), _full_spec((1, D)), _full_spec((D, E))],
        out_specs=out_specs,
        out_shape=out_shape,
        compiler_params=_cparams(1),
        name="odd_out",
    )(h_f, h_b, o, h, mods, mods, mods, vec(norm_g), w_out.astype(BF16), vec(ln_g), vec(ln_b), router_w)


def kernel(x, c, ctx, c_ctx, ada_w, ada_b, ln_g, ln_b, ev_w_in, ev_w_out, na_rpb, rw_mu, rw_w0, rw_w_up,
           rw_a0, rw_a_up, rw_g_up, rw_k_k, rw_k_a, rw_r_k, rw_gn_g, rw_gn_b, od_w_in, od_w_out, ml_gate_b,
           ml_norm_g, moe_router, moe_bias, moe_w_gate, moe_w_up, moe_w_down, sh_w_gate, sh_w_up, sh_w_down):
    B, N, D = x.shape
    C = ctx.shape[1]
    S = C + N
    dims = (B, C, N)
    assert C % ROW_TILE == 0 and N % ROW_TILE == 0 and B + 1 <= SUBLANES
    h = jnp.concatenate([ctx, x], axis=1).reshape(B * S, D)
    cond = jnp.zeros((SUBLANES, D), F32).at[:B].set(c).at[B].set(c_ctx)
    for l in range(DEPTH):
        mods = ada_mods_pallas(cond, ada_w[l], ada_b[l])
        if l % 2 == 0:
            e = l // 2
            (q, k, v, dec_f, dec_b, beta_f, beta_b, kd_f, kd_b, nkk, rv, rr, glow) = proj_even_pallas(
                h, mods, dims, ev_w_in[e], rw_mu[e], rw_w0[e], rw_w_up[e], rw_a0[e], rw_a_up[e],
                rw_k_k[e], rw_k_a[e])
            y_f, y_b = rwkv_scan_pallas(dec_f, beta_f, kd_f, dec_b, beta_b, kd_b, nkk, rv, rr, dims)
            na = attention_pallas(q, k, v, na_rpb[e], dims)
            h, f, s = even_out_pallas(na, y_f, y_b, rr, rv, kd_f, kd_b, glow, h, mods, dims, rw_g_up[e],
                                      rw_r_k[e], rw_gn_g[e], rw_gn_b[e], ev_w_out[e], ln_g[l, 0], ln_b[l, 0],
                                      moe_router[l])
        else:
            o = l // 2
            q, k, v, og, g, gt = proj_odd_pallas(h, mods, dims, od_w_in[o], ml_gate_b[o])
            h_f = mlstm_pallas(q, k, v, g, gt, dims, False)
            h_b = mlstm_pallas(q, k, v, g, gt, dims, True)
            h, f, s = odd_out_pallas(h_f, h_b, og, h, mods, dims, ml_norm_g[o], od_w_out[o], ln_g[l, 0],
                                     ln_b[l, 0], moe_router[l])
        h = moe_layer(f, s, h, mods, dims, ln_g[l, 1], ln_b[l, 1], moe_bias[l], l, moe_w_gate, moe_w_up,
                      moe_w_down, sh_w_gate[l], sh_w_up[l], sh_w_down[l])
    return h.reshape(B, S, D)[:, C:]
```

```python
import functools

import jax
import jax.numpy as jnp
import numpy as np
from jax import lax
from jax.experimental import pallas as pl
from jax.experimental.pallas import tpu as pltpu

D_MODEL = 1024
DEPTH = 2
GRID_W = 64

DEEPNORM_ALPHA = (2.0 * DEPTH) ** 0.25
LN_EPS = 1e-5
NEG_INF = -1e30
F32 = jnp.float32
BF16 = jnp.bfloat16

NA_HEAD_DIM = 64
NA_WIDTH = D_MODEL // 2
NA_HEADS = NA_WIDTH // NA_HEAD_DIM
NA_WIN_ROWS = 8
NA_WIN_COLS = 16
NA_SCALE = NA_HEAD_DIM ** -0.5

RW_HEAD_DIM = 64
RW_WIDTH = D_MODEL // 2
RW_DECAY_LORA = 32
RW_GN_EPS = 64e-5

ML_HEADS = 8
ML_V_DIM = D_MODEL // ML_HEADS
ML_QK_DIM = ML_V_DIM // 2
ML_WIDTH = ML_HEADS * ML_V_DIM
ML_CHUNK = 128
ML_NORM_EPS = 1e-6
ROPE_BASE = 10000.0

TOP_K = 8
N_GROUPS = 8
TOPK_GROUPS = 4
ROUTED_SCALE = 2.5
MOE_BLOCK = 512

SUBLANES = 8
LANES = 128
VMEM_LIMIT_BYTES = 56 * 1024 * 1024

ROW_TILE = 256
N_MODS = 6
RW_COLS = 3 * RW_WIDTH + 2 * LANES
NT_DIMS = (((1,), (1,)), ((), ()))


def _cparams(n_axes):
    return pltpu.CompilerParams(dimension_semantics=("arbitrary",) * n_axes, vmem_limit_bytes=VMEM_LIMIT_BYTES)


def _full_spec(shape):
    return pl.BlockSpec(shape, lambda *_: (0,) * len(shape))


def _split_bf16(x):
    hi = x.astype(BF16)
    lo = (x - hi.astype(F32)).astype(BF16)
    return jnp.concatenate([hi, lo], axis=-1)


def _block_ones(n_rows, n_cols, seg):
    row = lax.broadcasted_iota(jnp.int32, (n_rows, n_cols), 0)
    col = lax.broadcasted_iota(jnp.int32, (n_rows, n_cols), 1)
    return (((row % n_cols) // seg) == (col // seg)).astype(BF16)


def _seg_sum(x, ones2):
    return jnp.dot(_split_bf16(x), ones2, preferred_element_type=F32)


def _mod_index(tiles_per_batch, ctx_tiles, n_batch):
    def idx(i):
        return jnp.where(i % tiles_per_batch < ctx_tiles, n_batch, i // tiles_per_batch)
    return idx


def _mod_spec(chunk, mod_idx):
    return pl.BlockSpec((None, None, 1, D_MODEL), lambda i: (mod_idx(i), chunk, 0, 0))


def _ada_kernel(c_ref, w_ref, b_ref, o_ref):
    c = c_ref[...]
    x = (c * jax.nn.sigmoid(c)).astype(BF16)
    o_ref[...] = jnp.dot(x, w_ref[...].astype(BF16), preferred_element_type=F32) + b_ref[...]


def ada_mods_pallas(cond, w, b):
    n, D = cond.shape
    n_out = w.shape[1]
    tn = 512
    out = pl.pallas_call(
        _ada_kernel,
        grid=(n_out // tn,),
        in_specs=[_full_spec((n, D)), pl.BlockSpec((D, tn), lambda j: (0, j)), pl.BlockSpec((1, tn), lambda j: (0, j))],
        out_specs=pl.BlockSpec((n, tn), lambda j: (0, j)),
        out_shape=jax.ShapeDtypeStruct((n, n_out), F32),
        compiler_params=_cparams(1),
        name="ada_mods",
    )(cond, w, b.reshape(1, n_out))
    return out.reshape(n, N_MODS, 1, D)


def _softplus(x):
    return jnp.maximum(x, 0.0) + jnp.log(1.0 + jnp.exp(-jnp.abs(x)))


def _proj_even_kernel(h_ref, hp_ref, hn_ref, shift_ref, scale_ref, wna_ref, wrw_ref, mu_ref, ones_ref,
                      kk_ref, ka_ref, w0_ref, a0_ref, wup_ref, aup_ref,
                      q_ref, k_ref, v_ref, dec_f_ref, dec_b_ref, beta_f_ref, beta_b_ref, kd_f_ref, kd_b_ref,
                      nkk_ref, rv_ref, rr_ref, glow_ref, *, tiles_per_batch, ctx_tiles):
    i = pl.program_id(0)
    j = i % tiles_per_batch
    first = (j == 0) | (j == ctx_tiles)
    last = (j == ctx_tiles - 1) | (j == tiles_per_batch - 1)
    tm = h_ref.shape[0]
    gain = 1.0 + scale_ref[...]
    shift = shift_ref[...]
    a = h_ref[...] * gain + shift
    a_prev = jnp.where(first, 0.0, hp_ref[SUBLANES - 1:SUBLANES, :] * gain + shift)
    a_next = jnp.where(last, 0.0, hn_ref[0:1, :] * gain + shift)
    rid = lax.broadcasted_iota(jnp.int32, (tm, 1), 0)
    prev = jnp.where(rid == 0, a_prev, pltpu.roll(a, 1, 0))
    nxt = jnp.where(rid == tm - 1, a_next, pltpu.roll(a, tm - 1, 0))
    a16 = a.astype(BF16)
    nb16 = (0.5 * (prev + nxt)).astype(BF16)

    na = jnp.dot(a16, wna_ref[...], preferred_element_type=F32)
    q_ref[...] = (na[:, :NA_WIDTH] * NA_SCALE).astype(BF16)
    k_ref[...] = na[:, NA_WIDTH:2 * NA_WIDTH].astype(BF16)
    v_ref[...] = na[:, 2 * NA_WIDTH:].astype(BF16)

    pa = jnp.dot(a16, wrw_ref[...], preferred_element_type=F32)
    pn = jnp.dot(nb16, wrw_ref[...], preferred_element_type=F32)
    t = pa + mu_ref[...] * (pn - pa)
    r = t[:, :RW_WIDTH]
    k = t[:, RW_WIDTH:2 * RW_WIDTH]
    lora = t[:, 3 * RW_WIDTH:3 * RW_WIDTH + LANES]
    rr_ref[...] = r
    rv_ref[...] = t[:, 2 * RW_WIDTH:3 * RW_WIDTH]
    glow_ref[...] = t[:, 3 * RW_WIDTH + LANES:]

    kk = k * kk_ref[...]
    norm = jnp.sqrt(_seg_sum(kk * kk, ones_ref[...]))
    kk = kk / jnp.maximum(norm, 1e-12)
    nkk_ref[...] = -kk
    lora_t = jnp.tanh(lora).astype(BF16)
    lora16 = lora.astype(BF16)
    outs = ((dec_f_ref, beta_f_ref, kd_f_ref), (dec_b_ref, beta_b_ref, kd_b_ref))
    for d in range(2):
        w_log = -_softplus(-(w0_ref[d:d + 1, :] + jnp.dot(lora_t, wup_ref[d], preferred_element_type=F32))) - 0.5
        a_gate = jax.nn.sigmoid(a0_ref[d:d + 1, :] + jnp.dot(lora16, aup_ref[d], preferred_element_type=F32))
        outs[d][0][...] = jnp.exp(-jnp.exp(w_log))
        outs[d][1][...] = kk * a_gate
        outs[d][2][...] = k * (1.0 + (a_gate - 1.0) * ka_ref[...])


def proj_even_pallas(h, mods, dims, w_in, mu, w0, w_up, a0, a_up, k_k, k_a):
    B, C, N = dims
    R, D = h.shape
    tm = ROW_TILE
    tpb, ctx_tiles = (C + N) // tm, C // tm
    mod_idx = _mod_index(tpb, ctx_tiles, B)
    w_na = w_in[:, :3 * NA_WIDTH].astype(BF16)
    n_rw = w_in.shape[1] - 3 * NA_WIDTH
    w_rw = jnp.pad(w_in[:, 3 * NA_WIDTH:], ((0, 0), (0, RW_COLS - n_rw))).astype(BF16)
    mu_p = jnp.pad(mu, (0, RW_COLS - n_rw)).reshape(1, RW_COLS)
    ones2 = _block_ones(2 * RW_WIDTH, RW_WIDTH, RW_HEAD_DIM)
    lr = RW_DECAY_LORA

    def pad_up(m, first_row):
        out = jnp.zeros((2, LANES, RW_WIDTH), F32)
        for d in range(2):
            out = out.at[d, first_row + d * lr:first_row + (d + 1) * lr].set(m[d])
        return out.astype(BF16)

    row = lambda width: pl.BlockSpec((tm, width), lambda i: (i, 0))
    hb = tm // SUBLANES
    n_hb = R // SUBLANES
    wide = jax.ShapeDtypeStruct((R, RW_WIDTH), F32)
    half = jax.ShapeDtypeStruct((R, NA_WIDTH), BF16)
    return pl.pallas_call(
        functools.partial(_proj_even_kernel, tiles_per_batch=tpb, ctx_tiles=ctx_tiles),
        grid=(R // tm,),
        in_specs=[
            row(D),
            pl.BlockSpec((SUBLANES, D), lambda i: (jnp.maximum(i * hb - 1, 0), 0)),
            pl.BlockSpec((SUBLANES, D), lambda i: (jnp.minimum((i + 1) * hb, n_hb - 1), 0)),
            _mod_spec(0, mod_idx), _mod_spec(1, mod_idx),
            _full_spec((D, 3 * NA_WIDTH)), _full_spec((D, RW_COLS)), _full_spec((1, RW_COLS)),
            _full_spec((2 * RW_WIDTH, RW_WIDTH)),
            _full_spec((1, RW_WIDTH)), _full_spec((1, RW_WIDTH)),
            _full_spec((2, RW_WIDTH)), _full_spec((2, RW_WIDTH)),
            _full_spec((2, LANES, RW_WIDTH)), _full_spec((2, LANES, RW_WIDTH)),
        ],
        out_specs=[row(NA_WIDTH)] * 3 + [row(RW_WIDTH)] * 9 + [row(LANES)],
        out_shape=[half] * 3 + [wide] * 9 + [jax.ShapeDtypeStruct((R, LANES), F32)],
        compiler_params=_cparams(1),
        name="proj_even",
    )(h, h, h, mods, mods, w_na, w_rw, mu_p, ones2, k_k.reshape(1, -1), k_a.reshape(1, -1), w0, a0,
      pad_up(w_up, 0), pad_up(a_up, 2 * lr))


RW_SCAN_TIME = 256


def _rwkv_scan_kernel(wf_ref, bf_ref, kf_ref, nf_ref, vf_ref, rf_ref,
                      wb_ref, bb_ref, kb_ref, nb_ref, vb_ref, rb_ref, yf_ref, yb_ref, s_ref):
    @pl.when(pl.program_id(0) == 0)
    def _():
        s_ref[...] = jnp.zeros_like(s_ref)

    n_batch, n_time, width = wf_ref.shape
    n_pair = width // LANES
    n_dir_chain = n_batch * n_pair
    n_chain = 2 * n_dir_chain
    rows_all = n_chain * RW_HEAD_DIM
    ones = _block_ones(LANES, LANES, RW_HEAD_DIM)
    vi = lax.broadcasted_iota(jnp.int32, (1, RW_HEAD_DIM, LANES), 1)
    li = lax.broadcasted_iota(jnp.int32, (1, RW_HEAD_DIM, LANES), 2)
    diag = (li % RW_HEAD_DIM) == vi
    n_sub = n_time // SUBLANES

    def seg(x):
        out = jnp.dot(x.reshape(rows_all, LANES).astype(BF16), ones, preferred_element_type=F32)
        return out.reshape(n_chain, RW_HEAD_DIM, LANES)

    def chains(ref, rows):
        x = ref[:, rows, :]
        return [x[b, :, p * LANES:(p + 1) * LANES] for b in range(n_batch) for p in range(n_pair)]

    def sub(i, carry):
        rows_f = pl.ds(pl.multiple_of(i * SUBLANES, SUBLANES), SUBLANES)
        rows_b = pl.ds(pl.multiple_of((n_sub - 1 - i) * SUBLANES, SUBLANES), SUBLANES)
        load = lambda f_ref, b_ref: (jnp.stack(chains(f_ref, rows_f)), jnp.stack(chains(b_ref, rows_b)))
        w8, beta8, kd8 = load(wf_ref, wb_ref), load(bf_ref, bb_ref), load(kf_ref, kb_ref)
        nkk8, v8, r8 = load(nf_ref, nb_ref), load(vf_ref, vb_ref), load(rf_ref, rb_ref)

        def at(pair, t):
            tb = SUBLANES - 1 - t
            return jnp.concatenate([pair[0][:, t:t + 1, :], pair[1][:, tb:tb + 1, :]], axis=0)

        s = s_ref[...]
        rows = []
        value_col = lambda t: seg(jnp.where(diag, at(v8, t), 0.0))
        read_out = lambda sr: jnp.sum(jnp.where(diag, seg(sr), 0.0), axis=1, keepdims=True)
        vcol_next, pending = value_col(0), None
        for t in range(SUBLANES):
            sa = seg(s * at(nkk8, t))
            vcol = vcol_next
            if t + 1 < SUBLANES:
                vcol_next = value_col(t + 1)
            if pending is not None:
                rows.append(read_out(pending))
            s = s * at(w8, t) + sa * at(beta8, t) + vcol * at(kd8, t)
            pending = s * at(r8, t)
        rows.append(read_out(pending))
        s_ref[...] = s
        y_f = jnp.concatenate([row[:n_dir_chain] for row in rows], axis=1)
        y_b = jnp.concatenate([row[n_dir_chain:] for row in rows[::-1]], axis=1)
        for b in range(n_batch):
            for p in range(n_pair):
                c = b * n_pair + p
                yf_ref[b, rows_f, p * LANES:(p + 1) * LANES] = y_f[c]
                yb_ref[b, rows_b, p * LANES:(p + 1) * LANES] = y_b[c]
        return carry

    lax.fori_loop(0, n_sub, sub, 0)


def rwkv_scan_pallas(dec_f, beta_f, kd_f, dec_b, beta_b, kd_b, nkk, v, r, dims):
    B, C, N = dims
    S = C + N
    tc = RW_SCAN_TIME
    assert C % tc == 0 and N % tc == 0
    n_ctx, n_all = C // tc, S // tc
    as3 = lambda z: z.reshape(B, S, RW_WIDTH)
    fwd = pl.BlockSpec((B, tc, RW_WIDTH), lambda j: (0, j, 0))
    bwd = pl.BlockSpec((B, tc, RW_WIDTH),
                       lambda j: (0, jnp.where(j < n_ctx, n_ctx - 1 - j, n_all - 1 - (j - n_ctx)), 0))
    out = jax.ShapeDtypeStruct((B, S, RW_WIDTH), F32)
    y_f, y_b = pl.pallas_call(
        _rwkv_scan_kernel,
        grid=(n_all,),
        in_specs=[fwd] * 6 + [bwd] * 6,
        out_specs=[fwd, bwd],
        out_shape=[out, out],
        scratch_shapes=[pltpu.VMEM((2 * B * (RW_WIDTH // LANES), RW_HEAD_DIM, LANES), F32)],
        compiler_params=_cparams(1),
        name="rwkv_scan",
    )(as3(dec_f), as3(beta_f), as3(kd_f), as3(nkk), as3(v), as3(r),
      as3(dec_b), as3(beta_b), as3(kd_b), as3(nkk), as3(v), as3(r))
    return y_f.reshape(B * S, RW_WIDTH), y_b.reshape(B * S, RW_WIDTH)


NA_BAND = NA_WIN_ROWS * GRID_W


def _na_row_start(j, ctx_blocks, n_rows):
    r = jnp.maximum(j - ctx_blocks, 0)
    return r, jnp.clip(r - NA_WIN_ROWS // 2, 0, n_rows - NA_WIN_ROWS)


def _na_kernel(q_ref, k_ref, v_ref, bias_ref, o_ref, *, n_ctx):
    j = pl.program_id(1)
    ctx_blocks = n_ctx // GRID_W
    n_rows = pl.num_programs(1) - ctx_blocks
    _, row_start = _na_row_start(j, ctx_blocks, n_rows)
    start = pl.multiple_of(n_ctx + row_start * GRID_W, GRID_W)
    q = q_ref[0]
    kb = k_ref[0, pl.ds(start, NA_BAND), :]
    vb = v_ref[0, pl.ds(start, NA_BAND), :]
    kc = k_ref[0, pl.ds(0, n_ctx), :]
    vc = v_ref[0, pl.ds(0, n_ctx), :]
    head_of_lane = lax.broadcasted_iota(jnp.int32, (GRID_W, LANES), 1) // NA_HEAD_DIM
    heads = [(p, h2) for p in range(NA_WIDTH // LANES) for h2 in range(LANES // NA_HEAD_DIM)]
    cols = lambda p: slice(p * LANES, (p + 1) * LANES)
    scores = []
    for p, h2 in heads:
        qm = jnp.where(head_of_lane == h2, q[:, cols(p)], jnp.zeros((GRID_W, LANES), BF16))
        s_loc = lax.dot_general(qm, kb[:, cols(p)], NT_DIMS, preferred_element_type=F32)
        s_ctx = lax.dot_general(qm, kc[:, cols(p)], NT_DIMS, preferred_element_type=F32)
        scores.append((s_loc + bias_ref[0, 2 * p + h2], s_ctx))
    probs = []
    for s_loc, s_ctx in scores:
        m = jnp.maximum(jnp.max(s_loc, axis=-1, keepdims=True), jnp.max(s_ctx, axis=-1, keepdims=True))
        e_loc = jnp.exp(s_loc - m)
        e_ctx = jnp.exp(s_ctx - m)
        den = jnp.sum(e_loc, axis=-1, keepdims=True) + jnp.sum(e_ctx, axis=-1, keepdims=True)
        probs.append((e_loc.astype(BF16), e_ctx.astype(BF16), den))
    outs = []
    for (p, h2), (e_loc, e_ctx, den) in zip(heads, probs):
        o = (jnp.dot(e_loc, vb[:, cols(p)], preferred_element_type=F32)
             + jnp.dot(e_ctx, vc[:, cols(p)], preferred_element_type=F32))
        outs.append(o / den)
    for p in range(NA_WIDTH // LANES):
        o_ref[0, :, cols(p)] = jnp.where(head_of_lane == 0, outs[2 * p], outs[2 * p + 1])


def _na_bias_table(rpb):
    kw = NA_WIN_COLS
    n_col_off = 2 * kw - 1
    j = np.arange(GRID_W)
    col_start = np.clip(j - kw // 2, 0, GRID_W - kw)
    col_in = (j[None, :] >= col_start[:, None]) & (j[None, :] < col_start[:, None] + kw)
    col_off = np.clip(j[None, :] - j[:, None], -(kw - 1), kw - 1) + (kw - 1)
    pick = (col_off.reshape(1, -1) == np.arange(n_col_off)[:, None]).astype(np.float32)
    toep = jnp.dot(rpb.astype(F32).reshape(-1, n_col_off), pick, precision=lax.Precision.HIGHEST)
    toep = toep.reshape(NA_HEADS, 2 * NA_WIN_ROWS - 1, GRID_W, GRID_W)
    toep = jnp.where(col_in[None, None], toep, NEG_INF)
    tab = jnp.stack([toep[:, NA_WIN_ROWS - 1 - d:2 * NA_WIN_ROWS - 1 - d] for d in range(NA_WIN_ROWS)], 0)
    tab = tab.transpose(0, 1, 3, 2, 4).reshape(NA_WIN_ROWS, NA_HEADS, GRID_W, NA_BAND)
    return jnp.concatenate([tab, jnp.full((1,) + tab.shape[1:], NEG_INF, F32)], 0)


def attention_pallas(q, k, v, rpb, dims):
    B, C, N = dims
    S = C + N
    W = NA_WIDTH
    n_rows = N // GRID_W
    ctx_blocks = C // GRID_W
    assert n_rows >= NA_WIN_ROWS and N % GRID_W == 0 and C % GRID_W == 0
    bias = _na_bias_table(rpb)
    as3 = lambda z: z.reshape(B, S, W)

    def bias_idx(b, j):
        r, row_start = _na_row_start(j, ctx_blocks, n_rows)
        return (jnp.where(j < ctx_blocks, NA_WIN_ROWS, r - row_start), 0, 0, 0)

    out = pl.pallas_call(
        functools.partial(_na_kernel, n_ctx=C),
        grid=(B, S // GRID_W),
        in_specs=[
            pl.BlockSpec((1, GRID_W, W), lambda b, j: (b, j, 0)),
            pl.BlockSpec((1, S, W), lambda b, j: (b, 0, 0)),
            pl.BlockSpec((1, S, W), lambda b, j: (b, 0, 0)),
            pl.BlockSpec((1, NA_HEADS, GRID_W, NA_BAND), bias_idx),
        ],
        out_specs=pl.BlockSpec((1, GRID_W, W), lambda b, j: (b, j, 0)),
        out_shape=jax.ShapeDtypeStruct((B, S, W), F32),
        compiler_params=_cparams(2),
        name="na_attention",
    )(as3(q), as3(k), as3(v), bias)
    return out.reshape(B * S, W)


def _layer_norm(x, g, b):
    mu = jnp.mean(x, axis=-1, keepdims=True)
    xc = x - mu
    var = jnp.mean(xc * xc, axis=-1, keepdims=True)
    return xc * lax.rsqrt(var + LN_EPS) * g + b


def _mixer_tail(h, y, gate, ln_g, ln_b, shift, scale, router, h_out_ref, f_ref, s_ref):
    h1 = _layer_norm(DEEPNORM_ALPHA * h + gate * y, ln_g, ln_b)
    f = h1 * (1.0 + scale) + shift
    h_out_ref[...] = h1
    f_ref[...] = f
    s_ref[...] = jax.nn.sigmoid(jnp.dot(f, router, preferred_element_type=F32, precision=lax.Precision.HIGHEST))


def _even_out_kernel(na_ref, yf_ref, yb_ref, r_ref, v_ref, kdf_ref, kdb_ref, glow_ref, h_ref,
                     gate_ref, shift_ref, scale_ref, ones_ref, gng_ref, gnb_ref, rk_ref, gup_ref, wout_ref,
                     lng_ref, lnb_ref, router_ref, h_out_ref, f_ref, s_ref):
    ones2 = ones_ref[...]
    inv = 1.0 / RW_HEAD_DIM
    y = yf_ref[...] + yb_ref[...]
    mu = _seg_sum(y, ones2) * inv
    yc = y - mu
    var = _seg_sum(yc * yc, ones2) * inv
    yn = yc * lax.rsqrt(var + RW_GN_EPS) * gng_ref[...] + gnb_ref[...]
    r = r_ref[...]
    bonus = (_seg_sum(r * kdf_ref[...] * rk_ref[...], ones2) + _seg_sum(r * kdb_ref[...] * rk_ref[...], ones2))
    gate = jnp.dot(jax.nn.sigmoid(glow_ref[...]).astype(BF16), gup_ref[...], preferred_element_type=F32)
    rw = (yn + bonus * v_ref[...]) * gate
    mix = jnp.concatenate([na_ref[...], rw], axis=-1).astype(BF16)
    y_mix = jnp.dot(mix, wout_ref[...], preferred_element_type=F32)
    _mixer_tail(h_ref[...], y_mix, gate_ref[...], lng_ref[...], lnb_ref[...], shift_ref[...], scale_ref[...],
                router_ref[...], h_out_ref, f_ref, s_ref)


def _tail_specs(R, D, E, tm):
    row = lambda width: pl.BlockSpec((tm, width), lambda i: (i, 0))
    return ([row(D), row(D), row(E)],
            [jax.ShapeDtypeStruct((R, D), F32), jax.ShapeDtypeStruct((R, D), F32), jax.ShapeDtypeStruct((R, E), F32)])


def even_out_pallas(na, y_f, y_b, r, v, kd_f, kd_b, glow, h, mods, dims, g_up, r_k, gn_g, gn_b, w_out,
                    ln_g, ln_b, router_w):
    B, C, N = dims
    R, D = h.shape
    E = router_w.shape[1]
    tm = ROW_TILE
    mod_idx = _mod_index((C + N) // tm, C // tm, B)
    row = lambda width: pl.BlockSpec((tm, width), lambda i: (i, 0))
    ones2 = _block_ones(2 * RW_WIDTH, RW_WIDTH, RW_HEAD_DIM)
    g_up_p = jnp.pad(g_up, ((0, LANES - g_up.shape[0]), (0, 0))).astype(BF16)
    vec = lambda z: z.reshape(1, -1)
    out_specs, out_shape = _tail_specs(R, D, E, tm)
    return pl.pallas_call(
        _even_out_kernel,
        grid=(R // tm,),
        in_specs=[row(NA_WIDTH)] + [row(RW_WIDTH)] * 6 + [row(LANES), row(D),
                  _mod_spec(2, mod_idx), _mod_spec(3, mod_idx), _mod_spec(4, mod_idx),
                  _full_spec((2 * RW_WIDTH, RW_WIDTH)),
                  _full_spec((1, RW_WIDTH)), _full_spec((1, RW_WIDTH)), _full_spec((1, RW_WIDTH)),
                  _full_spec((LANES, RW_WIDTH)), _full_spec((D, D)),
                  _full_spec((1, D)), _full_spec((1, D)), _full_spec((D, E))],
        out_specs=out_specs,
        out_shape=out_shape,
        compiler_params=_cparams(1),
        name="even_out",
    )(na, y_f, y_b, r, v, kd_f, kd_b, glow, h, mods, mods, mods, ones2, vec(gn_g), vec(gn_b), vec(r_k),
      g_up_p, w_out.astype(BF16), vec(ln_g), vec(ln_b), router_w)


MOE_TOKEN_TILE = 256


def _swiglu_bf16(x, wg, wu, wd):
    g = jnp.dot(x, wg, preferred_element_type=F32)
    u = jnp.dot(x, wu, preferred_element_type=F32)
    mid = (g * jax.nn.sigmoid(g) * u).astype(BF16)
    return jnp.dot(mid, wd, preferred_element_type=F32)


def _row_copy(src_ref, src_row, dst_ref, dst_row, sem):
    return pltpu.make_async_copy(src_ref.at[pl.ds(src_row, 1), :], dst_ref.at[pl.ds(dst_row, 1), :], sem)


def _slot_kernel(e_ref, rank_ref, starts_ref, slot_ref):
    e = e_ref[...]
    tm = e.shape[0]
    lane = lax.broadcasted_iota(jnp.int32, (tm, starts_ref.shape[1]), 1)
    col8 = lax.broadcasted_iota(jnp.int32, (tm, TOP_K), 1)
    first = jnp.zeros((tm, TOP_K), F32)
    for k in range(TOP_K):
        hit = lane == e[:, k:k + 1]
        first = jnp.where(col8 == k, jnp.sum(jnp.where(hit, starts_ref[...], 0.0), axis=-1, keepdims=True), first)
    slot_ref[...] = first.astype(jnp.int32) + rank_ref[...]


def moe_slots_pallas(e_idx, rank, starts):
    T = e_idx.shape[0]
    E = starts.shape[0]
    tm = ROW_TILE
    row8 = pl.BlockSpec((tm, TOP_K), lambda i: (i, 0))
    return pl.pallas_call(
        _slot_kernel,
        grid=(T // tm,),
        in_specs=[row8, row8, _full_spec((1, E))],
        out_specs=row8,
        out_shape=jax.ShapeDtypeStruct((T, TOP_K), jnp.int32),
        compiler_params=_cparams(1),
        name="moe_slots",
    )(e_idx, rank, starts.astype(F32).reshape(1, E))


def _dispatch_kernel(slot_ref, f_ref, xs_ref, sem):
    n_tok = f_ref.shape[0]

    def issue(t, carry):
        for k in range(TOP_K):
            _row_copy(f_ref, t, xs_ref, slot_ref[t * TOP_K + k], sem).start()
        return carry

    lax.fori_loop(0, n_tok, issue, 0)

    def drain(t, carry):
        for k in range(TOP_K):
            _row_copy(f_ref, 0, xs_ref, 0, sem).wait()
        return carry

    lax.fori_loop(0, n_tok, drain, 0)


def _slot_specs(tm):
    return [pl.BlockSpec((tm * TOP_K,), lambda i: (i,), memory_space=pltpu.SMEM)]


def moe_dispatch_pallas(f, slot_flat):
    T, D = f.shape
    tm = MOE_TOKEN_TILE
    assert T % tm == 0
    return pl.pallas_call(
        _dispatch_kernel,
        grid=(T // tm,),
        in_specs=_slot_specs(tm) + [pl.BlockSpec((tm, D), lambda i: (i, 0))],
        out_specs=pl.BlockSpec(memory_space=pl.ANY),
        out_shape=jax.ShapeDtypeStruct((T * TOP_K, D), F32),
        scratch_shapes=[pltpu.SemaphoreType.DMA(())],
        compiler_params=_cparams(1),
        name="moe_dispatch",
    )(slot_flat, f)


def _expert_item_kernel(blk_ref, e_ref, lo_ref, hi_ref, first_ref, x_ref, wg_ref, wu_ref, wd_ref, o_ref,
                        wg16_ref, wu16_ref, wd16_ref):
    i = pl.program_id(0)
    lo, hi = lo_ref[i], hi_ref[i]

    @pl.when((i == 0) | (e_ref[i] != e_ref[jnp.maximum(i - 1, 0)]))
    def _():
        wg16_ref[...] = wg_ref[0, 0].astype(BF16)
        wu16_ref[...] = wu_ref[0, 0].astype(BF16)
        wd16_ref[...] = wd_ref[0, 0].astype(BF16)

    @pl.when(hi > lo)
    def _():
        y = _swiglu_bf16(x_ref[...].astype(BF16), wg16_ref[...], wu16_ref[...], wd16_ref[...])
        rows = blk_ref[i] * MOE_BLOCK + lax.broadcasted_iota(jnp.int32, (MOE_BLOCK, 1), 0)
        y = jnp.where((rows >= lo) & (rows < hi), y, 0.0)

        @pl.when(first_ref[i] == 1)
        def _():
            o_ref[...] = y

        @pl.when(first_ref[i] == 0)
        def _():
            o_ref[...] += y


def moe_experts_pallas(xs, items, layer, wg, wu, wd):
    n_rows, D = xs.shape
    F = wg.shape[-1]
    n_items = items[0].shape[0]
    grid_spec = pltpu.PrefetchScalarGridSpec(
        num_scalar_prefetch=5,
        grid=(n_items,),
        in_specs=[
            pl.BlockSpec((MOE_BLOCK, D), lambda i, blk, e, lo, hi, first: (blk[i], 0)),
            pl.BlockSpec((1, 1, D, F), lambda i, blk, e, lo, hi, first: (layer, e[i], 0, 0)),
            pl.BlockSpec((1, 1, D, F), lambda i, blk, e, lo, hi, first: (layer, e[i], 0, 0)),
            pl.BlockSpec((1, 1, F, D), lambda i, blk, e, lo, hi, first: (layer, e[i], 0, 0)),
        ],
        out_specs=pl.BlockSpec((MOE_BLOCK, D), lambda i, blk, e, lo, hi, first: (blk[i], 0)),
        scratch_shapes=[pltpu.VMEM((D, F), BF16), pltpu.VMEM((D, F), BF16), pltpu.VMEM((F, D), BF16)],
    )
    return pl.pallas_call(
        _expert_item_kernel,
        grid_spec=grid_spec,
        out_shape=jax.ShapeDtypeStruct((n_rows, D), F32),
        compiler_params=_cparams(1),
        name="moe_experts",
    )(*items, xs, wg, wu, wd)


def _combine_kernel(slot_ref, w_ref, f_ref, h_ref, gate_ref, lng_ref, lnb_ref,
                    sg_ref, su_ref, sd_ref, ys_ref, o_ref, buf_ref, sem):
    n_tok = f_ref.shape[0]

    def issue(t, carry):
        for k in range(TOP_K):
            pltpu.make_async_copy(ys_ref.at[pl.ds(slot_ref[t * TOP_K + k], 1), :],
                                  buf_ref.at[k, pl.ds(t, 1), :], sem).start()
        return carry

    lax.fori_loop(0, n_tok, issue, 0)
    acc = _swiglu_bf16(f_ref[...].astype(BF16), sg_ref[...], su_ref[...], sd_ref[...])

    def drain(t, carry):
        for k in range(TOP_K):
            pltpu.make_async_copy(ys_ref.at[pl.ds(0, 1), :], buf_ref.at[0, pl.ds(0, 1), :], sem).wait()
        return carry

    lax.fori_loop(0, n_tok, drain, 0)
    w = w_ref[...]
    for k in range(TOP_K):
        acc = acc + w[:, k:k + 1] * buf_ref[k]
    o_ref[...] = _layer_norm(DEEPNORM_ALPHA * h_ref[...] + gate_ref[...] * acc, lng_ref[...], lnb_ref[...])


def moe_combine_pallas(ys, slot_flat, w_sel, f, h, mods, dims, ln_g, ln_b, sg, su, sd):
    B, C, N = dims
    T, D = f.shape
    tm = MOE_TOKEN_TILE
    F = sg.shape[-1]
    mod_idx = _mod_index((C + N) // tm, C // tm, B)
    vec = lambda z: z.reshape(1, -1)
    return pl.pallas_call(
        _combine_kernel,
        grid=(T // tm,),
        in_specs=_slot_specs(tm) + [
            pl.BlockSpec((tm, TOP_K), lambda i: (i, 0)),
            pl.BlockSpec((tm, D), lambda i: (i, 0)),
            pl.BlockSpec((tm, D), lambda i: (i, 0)),
            _mod_spec(5, mod_idx), _full_spec((1, D)), _full_spec((1, D)),
            _full_spec((D, F)), _full_spec((D, F)), _full_spec((F, D)),
            pl.BlockSpec(memory_space=pl.ANY),
        ],
        out_specs=pl.BlockSpec((tm, D), lambda i: (i, 0)),
        out_shape=jax.ShapeDtypeStruct((T, D), F32),
        scratch_shapes=[pltpu.VMEM((TOP_K, tm, D), F32), pltpu.SemaphoreType.DMA(())],
        compiler_params=_cparams(1),
        name="moe_combine",
    )(slot_flat, w_sel, f, h, mods, vec(ln_g), vec(ln_b), sg, su, sd, ys)


REMOVED = -3e38


def _router_kernel(s_ref, bias_ref, e_ref, w_ref, rank_ref, cnt_ref, carry_ref):
    @pl.when(pl.program_id(0) == 0)
    def _():
        carry_ref[...] = jnp.zeros_like(carry_ref)

    s = s_ref[...]
    tm, n_exp = s.shape
    per_group = n_exp // N_GROUPS
    lane_i = lax.broadcasted_iota(jnp.int32, (tm, n_exp), 1)
    lane = lane_i.astype(F32)
    group_of_lane = lane_i // per_group
    big = float(n_exp)
    rmax = lambda z: jnp.max(z, axis=-1, keepdims=True)
    first_at = lambda z, m: jnp.min(jnp.where(z == m, lane, big), axis=-1, keepdims=True)

    grp = s + bias_ref[...]
    g_score = []
    for g in range(N_GROUPS):
        mg = jnp.where(group_of_lane == g, grp, REMOVED)
        m1 = rmax(mg)
        m2 = rmax(jnp.where(lane == first_at(mg, m1), REMOVED, mg))
        g_score.append(m1 + m2)
    choice = jnp.full_like(grp, NEG_INF)
    for g in range(N_GROUPS):
        ahead = jnp.zeros((tm, 1), F32)
        for g2 in range(N_GROUPS):
            if g2 != g:
                beats = (g_score[g2] > g_score[g]) | ((g_score[g2] == g_score[g]) & (g2 < g))
                ahead = ahead + beats.astype(F32)
        choice = jnp.where((group_of_lane == g) & (ahead < TOPK_GROUPS), grp, choice)

    col8 = lax.broadcasted_iota(jnp.int32, (tm, TOP_K), 1)
    e_out = jnp.zeros((tm, TOP_K), F32)
    w_out = jnp.zeros((tm, TOP_K), F32)
    picked = []
    onehot = jnp.zeros((tm, n_exp), F32)
    for k in range(TOP_K):
        idx = first_at(choice, rmax(choice))
        hit = lane == idx
        picked.append(hit)
        onehot = jnp.where(hit, 1.0, onehot)
        e_out = jnp.where(col8 == k, idx, e_out)
        w_out = jnp.where(col8 == k, jnp.sum(jnp.where(hit, s, 0.0), axis=-1, keepdims=True), w_out)
        choice = jnp.where(hit, REMOVED, choice)
    ri = lax.broadcasted_iota(jnp.int32, (tm, tm), 0)
    ci = lax.broadcasted_iota(jnp.int32, (tm, tm), 1)
    before = jnp.dot((ci < ri).astype(BF16), onehot.astype(BF16), preferred_element_type=F32) + carry_ref[0:1, :]
    rank = jnp.zeros((tm, TOP_K), F32)
    for k in range(TOP_K):
        rank = jnp.where(col8 == k, jnp.sum(jnp.where(picked[k], before, 0.0), axis=-1, keepdims=True), rank)
    total = carry_ref[0:1, :] + jnp.sum(onehot, axis=0, keepdims=True)
    carry_ref[...] = jnp.broadcast_to(total, carry_ref.shape)
    cnt_ref[...] = jnp.broadcast_to(total, cnt_ref.shape)
    e_ref[...] = e_out.astype(jnp.int32)
    w_ref[...] = w_out / jnp.sum(w_out, axis=-1, keepdims=True) * ROUTED_SCALE
    rank_ref[...] = rank.astype(jnp.int32)


def router_pallas(s, router_b):
    T, E = s.shape
    tm = ROW_TILE
    row8 = pl.BlockSpec((tm, TOP_K), lambda i: (i, 0))
    e_idx, w_sel, rank, cnt = pl.pallas_call(
        _router_kernel,
        grid=(T // tm,),
        in_specs=[pl.BlockSpec((tm, E), lambda i: (i, 0)), _full_spec((1, E))],
        out_specs=[row8, row8, row8, _full_spec((SUBLANES, E))],
        out_shape=[jax.ShapeDtypeStruct((T, TOP_K), jnp.int32), jax.ShapeDtypeStruct((T, TOP_K), F32),
                   jax.ShapeDtypeStruct((T, TOP_K), jnp.int32), jax.ShapeDtypeStruct((SUBLANES, E), F32)],
        scratch_shapes=[pltpu.VMEM((SUBLANES, E), F32)],
        compiler_params=_cparams(1),
        name="moe_router",
    )(s, router_b.astype(F32).reshape(1, E))
    return e_idx, w_sel, rank, cnt[0].astype(jnp.int32)


def moe_layer(f, s, h, mods, dims, ln_g, ln_b, router_b, layer, wg, wu, wd, sg, su, sd):
    T, D = f.shape
    E = s.shape[-1]
    e_idx, w_sel, rank, counts = router_pallas(s, router_b)
    n_asg = T * TOP_K
    assert n_asg % MOE_BLOCK == 0
    i32 = jnp.int32
    ends = jnp.cumsum(counts).astype(i32)
    starts = ends - counts
    slot_flat = moe_slots_pallas(e_idx, rank, starts).reshape(-1)
    nb = n_asg // MOE_BLOCK
    first_blk = starts // MOE_BLOCK
    nblk = jnp.where(counts > 0, (ends - 1) // MOE_BLOCK - first_blk + 1, 0)
    item_ends = jnp.cumsum(nblk).astype(i32)
    item_starts = item_ends - nblk
    n_items = nb + E
    it = jnp.arange(n_items, dtype=i32)
    real = it < item_ends[-1]
    e_of = jnp.sum((item_ends[None, :] <= jnp.where(real, it, item_ends[-1] - 1)[:, None]).astype(i32), axis=1)
    is_e = e_of[:, None] == jnp.arange(E, dtype=i32)[None, :]
    pick = lambda tab: jnp.sum(jnp.where(is_e, tab[None, :], 0), axis=1)
    blk = jnp.where(real, pick(first_blk) + it - pick(item_starts), nb - 1).astype(i32)
    lo = jnp.where(real, jnp.maximum(pick(starts), blk * MOE_BLOCK), 0).astype(i32)
    hi = jnp.where(real, jnp.minimum(pick(ends), (blk + 1) * MOE_BLOCK), 0).astype(i32)
    first = (real & (blk != jnp.concatenate([jnp.full((1,), -1, i32), blk[:-1]]))).astype(i32)
    xs = moe_dispatch_pallas(f, slot_flat)
    ys = moe_experts_pallas(xs, (blk, e_of, lo, hi, first), layer, wg, wu, wd)
    return moe_combine_pallas(ys, slot_flat, w_sel, f, h, mods, dims, ln_g, ln_b,
                              sg.astype(BF16), su.astype(BF16), sd.astype(BF16))


ML_QK_WIDTH = ML_HEADS * ML_QK_DIM
ROPE_GROUP = ML_QK_DIM // 4
GATE_IN, GATE_FORGET = 0, 2 * ML_HEADS


def _log_sigmoid(x):
    return -_softplus(-x)


def _proj_odd_kernel(h_ref, shift_ref, scale_ref, wqk_ref, wv_ref, wo_ref, wg_ref, wgt_ref, gb_ref, gbt_ref,
                     cos_ref, sin_ref, q_ref, k_ref, v_ref, o_ref, g_ref, gt_ref):
    a16 = (h_ref[...] * (1.0 + scale_ref[...]) + shift_ref[...]).astype(BF16)
    qk = jnp.dot(a16, wqk_ref[...], preferred_element_type=F32)
    lane = lax.broadcasted_iota(jnp.int32, (1, ML_QK_WIDTH), 1)
    first_of_pair = (lane % (2 * ROPE_GROUP)) < ROPE_GROUP
    cos, sin = cos_ref[...], sin_ref[...]

    def rope(z):
        partner = jnp.where(first_of_pair, pltpu.roll(z, ML_QK_WIDTH - ROPE_GROUP, 1), pltpu.roll(z, ROPE_GROUP, 1))
        return z * cos + partner * sin

    q_ref[...] = rope(qk[:, :ML_QK_WIDTH] * ML_QK_DIM ** -0.5).astype(BF16)
    k_ref[...] = rope(qk[:, ML_QK_WIDTH:]).astype(BF16)
    v_ref[...] = jnp.dot(a16, wv_ref[...], preferred_element_type=F32).astype(BF16)
    o_ref[...] = jnp.dot(a16, wo_ref[...], preferred_element_type=F32)
    g = jnp.dot(a16, wg_ref[...], preferred_element_type=F32) + gb_ref[...]
    gl = lax.broadcasted_iota(jnp.int32, g.shape, 1)
    g_ref[...] = jnp.where((gl >= GATE_FORGET) & (gl < 2 * GATE_FORGET), _log_sigmoid(g), g)
    gt = lax.dot_general(wgt_ref[...], a16, NT_DIMS, preferred_element_type=F32) + gbt_ref[...]
    gs = lax.broadcasted_iota(jnp.int32, gt.shape, 0)
    gt_ref[...] = jnp.where((gs >= GATE_FORGET) & (gs < 2 * GATE_FORGET), _log_sigmoid(gt), gt)


def _rope_tables(C, N):
    t = jnp.arange(N)
    pos = jnp.stack([(t // GRID_W).astype(F32), (t % GRID_W).astype(F32)], 0)
    lane = jnp.arange(ML_QK_DIM)
    inv = ROPE_BASE ** (-(lane % ROPE_GROUP).astype(F32) / ROPE_GROUP)
    ang = pos[lane // (2 * ROPE_GROUP)].T * inv[None, :]
    sign = jnp.where((lane % (2 * ROPE_GROUP)) < ROPE_GROUP, -1.0, 1.0)
    heads = lambda z: jnp.tile(z, (1, ML_HEADS))
    cos = jnp.concatenate([jnp.ones((C, ML_QK_WIDTH), F32), heads(jnp.cos(ang))], 0)
    sin = jnp.concatenate([jnp.zeros((C, ML_QK_WIDTH), F32), heads(jnp.sin(ang) * sign)], 0)
    return cos, sin


def proj_odd_pallas(h, mods, dims, w_in, gate_b):
    B, C, N = dims
    R, D = h.shape
    tm = ROW_TILE
    tpb = (C + N) // tm
    mod_idx = _mod_index(tpb, C // tm, B)
    o_qk, o_v, o_o = 2 * ML_QK_WIDTH, 2 * ML_QK_WIDTH + ML_WIDTH, 2 * ML_QK_WIDTH + 2 * ML_WIDTH
    n_gate = w_in.shape[1] - o_o
    w16 = w_in.astype(BF16)
    w_g = jnp.pad(w16[:, o_o:], ((0, 0), (0, LANES - n_gate)))
    gb = jnp.pad(gate_b.astype(F32).reshape(-1), (0, LANES - n_gate))
    cos, sin = _rope_tables(C, N)
    row = lambda width: pl.BlockSpec((tm, width), lambda i: (i, 0))
    seg = pl.BlockSpec((tm, ML_QK_WIDTH), lambda i: (i % tpb, 0))
    return pl.pallas_call(
        _proj_odd_kernel,
        grid=(R // tm,),
        in_specs=[row(D), _mod_spec(0, mod_idx), _mod_spec(1, mod_idx),
                  _full_spec((D, 2 * ML_QK_WIDTH)), _full_spec((D, ML_WIDTH)), _full_spec((D, ML_WIDTH)),
                  _full_spec((D, LANES)), _full_spec((LANES, D)), _full_spec((1, LANES)), _full_spec((LANES, 1)),
                  seg, seg],
        out_specs=[row(ML_QK_WIDTH), row(ML_QK_WIDTH), row(ML_WIDTH), row(ML_WIDTH), row(LANES),
                   pl.BlockSpec((LANES, tm), lambda i: (0, i))],
        out_shape=[jax.ShapeDtypeStruct((R, ML_QK_WIDTH), BF16), jax.ShapeDtypeStruct((R, ML_QK_WIDTH), BF16),
                   jax.ShapeDtypeStruct((R, ML_WIDTH), BF16), jax.ShapeDtypeStruct((R, ML_WIDTH), F32),
                   jax.ShapeDtypeStruct((R, LANES), F32), jax.ShapeDtypeStruct((LANES, R), F32)],
        compiler_params=_cparams(1),
        name="proj_odd",
    )(h, mods, mods, w16[:, :o_qk], w16[:, o_qk:o_v], w16[:, o_v:o_o], w_g, w_g.T, gb.reshape(1, LANES),
      gb.reshape(LANES, 1), cos, sin)


def _split3_bf16(x, axis):
    x1 = x.astype(BF16)
    r1 = x - x1.astype(F32)
    x2 = r1.astype(BF16)
    x3 = (r1 - x2.astype(F32)).astype(BF16)
    return jnp.concatenate([x1, x2, x3], axis=axis)


def _mlstm_kernel(q_ref, k_ref, v_ref, g_ref, gt_ref, h_ref, c_ref, n_ref, m_ref, *, reverse):
    @pl.when(pl.program_id(1) == 0)
    def _():
        c_ref[...] = jnp.zeros_like(c_ref)
        n_ref[...] = jnp.zeros_like(n_ref)
        m_ref[...] = jnp.zeros_like(m_ref)

    L = q_ref.shape[1]
    ti = lax.broadcasted_iota(jnp.int32, (L, L), 0)
    si = lax.broadcasted_iota(jnp.int32, (L, L), 1)
    seen = (si >= ti) if reverse else (si <= ti)
    g = g_ref[0]
    gt = gt_ref[...]
    b_cols3 = jnp.dot(seen.astype(BF16), _split3_bf16(g, 1), preferred_element_type=F32)
    b_cols = b_cols3[:, :LANES] + b_cols3[:, LANES:2 * LANES] + b_cols3[:, 2 * LANES:]
    b_rows3 = lax.dot_general(_split3_bf16(gt, 0), seen.astype(BF16), NT_DIMS, preferred_element_type=F32)
    b_rows = b_rows3[:LANES] + b_rows3[LANES:2 * LANES] + b_rows3[2 * LANES:]
    half = lax.broadcasted_iota(jnp.int32, (1, LANES), 1) // ML_QK_DIM
    row_half = lax.broadcasted_iota(jnp.int32, (LANES, 1), 0) // ML_QK_DIM
    d_off = ML_HEADS if reverse else 0
    tn = (((0,), (0,)), ((), ()))
    heads = [(hd // 2, hd % 2) for hd in range(ML_HEADS)]
    pair = lambda ref, p: ref[0, :, p * LANES:(p + 1) * LANES]
    value = lambda hd: v_ref[0, :, hd * ML_V_DIM:(hd + 1) * ML_V_DIM]

    decay = []
    for hd, (p, h2) in enumerate(heads):
        gi, gf = GATE_IN + d_off + hd, GATE_FORGET + d_off + hd
        ig_col, ig_row = g[:, gi:gi + 1], gt[gi:gi + 1, :]
        b_col, b_row = b_cols[:, gf:gf + 1], b_rows[gf:gf + 1, :]
        m0 = m_ref[p][:, h2 * ML_QK_DIM:h2 * ML_QK_DIM + 1]
        dlog = jnp.where(seen, b_col - b_row + ig_row, NEG_INF)
        inter = b_col + m0
        m_t = jnp.maximum(jnp.max(dlog, axis=-1, keepdims=True), inter)
        b_end = jnp.sum(g[:, gf:gf + 1], axis=0, keepdims=True)
        g_col = b_end - b_col + ig_col
        m_chunk = jnp.max(g_col, axis=0, keepdims=True)
        m_new = jnp.maximum(b_end + m0, m_chunk)
        decay.append(dict(dw=jnp.exp(dlog - m_t), iw=jnp.exp(inter - m_t), floor=jnp.exp(-m_t),
                          kw=jnp.exp(g_col - m_chunk), m_new=m_new,
                          fa=jnp.exp(b_end + m0 - m_new), fb=jnp.exp(m_chunk - m_new)))

    prods = []
    for p, h2 in heads:
        qm = jnp.where(half == h2, pair(q_ref, p), jnp.zeros((L, LANES), BF16))
        qk = lax.dot_general(qm, pair(k_ref, p), NT_DIMS, preferred_element_type=F32)
        qc = jnp.dot(qm, c_ref[p].astype(BF16), preferred_element_type=F32)
        qn = jnp.sum(qm.astype(F32) * n_ref[p], axis=-1, keepdims=True)
        prods.append((qk, qc, qn))

    for hd, ((qk, qc, qn), dc) in enumerate(zip(prods, decay)):
        sc = qk * dc['dw']
        num = jnp.dot(sc.astype(BF16), value(hd), preferred_element_type=F32) + dc['iw'] * qc
        den = jnp.sum(sc, axis=-1, keepdims=True) + dc['iw'] * qn
        h_ref[0, :, hd * ML_V_DIM:(hd + 1) * ML_V_DIM] = num / jnp.maximum(jnp.abs(den), dc['floor'])

    for p in range(ML_HEADS // 2):
        c_old, n_old, m_old = c_ref[p], n_ref[p], m_ref[p]
        c_new, n_new, m_new_pair = c_old, n_old, m_old
        for h2 in range(2):
            hd = 2 * p + h2
            dc = decay[hd]
            kw = jnp.where(half == h2, pair(k_ref, p), jnp.zeros((L, LANES), BF16)).astype(F32) * dc['kw']
            kv = lax.dot_general(kw.astype(BF16), value(hd), tn, preferred_element_type=F32)
            c_new = jnp.where(row_half == h2, dc['fa'] * c_old + dc['fb'] * kv, c_new)
            n_new = jnp.where(half == h2, dc['fa'] * n_old + dc['fb'] * jnp.sum(kw, axis=0, keepdims=True), n_new)
            m_new_pair = jnp.where(half == h2, dc['m_new'], m_new_pair)
        c_ref[p] = c_new
        n_ref[p] = n_new
        m_ref[p] = m_new_pair


def mlstm_pallas(q, k, v, g, gt, dims, reverse):
    B, C, N = dims
    S = C + N
    L = ML_CHUNK
    assert C % L == 0 and N % L == 0
    n_ctx, n_all = C // L, S // L
    if reverse:
        chunk = lambda j: jnp.where(j < n_ctx, n_ctx - 1 - j, n_all - 1 - (j - n_ctx))
    else:
        chunk = lambda j: j
    blk = lambda width: pl.BlockSpec((1, L, width), lambda b, j: (b, chunk(j), 0))
    n_pair = ML_HEADS // 2
    out = pl.pallas_call(
        functools.partial(_mlstm_kernel, reverse=reverse),
        grid=(B, n_all),
        in_specs=[blk(ML_QK_WIDTH), blk(ML_QK_WIDTH), blk(ML_WIDTH), blk(LANES),
                  pl.BlockSpec((LANES, L), lambda b, j: (0, b * n_all + chunk(j)))],
        out_specs=blk(ML_WIDTH),
        out_shape=jax.ShapeDtypeStruct((B, S, ML_WIDTH), F32),
        scratch_shapes=[pltpu.VMEM((n_pair, LANES, ML_V_DIM), F32), pltpu.VMEM((n_pair, 1, LANES), F32),
                        pltpu.VMEM((n_pair, 1, LANES), F32)],
        compiler_params=_cparams(2),
        name="mlstm_bwd" if reverse else "mlstm_fwd",
    )(q.reshape(B, S, -1), k.reshape(B, S, -1), v.reshape(B, S, -1), g.reshape(B, S, -1), gt)
    return out.reshape(B * S, ML_WIDTH)


def _odd_out_kernel(hf_ref, hb_ref, o_ref, h_ref, gate_ref, shift_ref, scale_ref, ng_ref, wout_ref,
                    lng_ref, lnb_ref, router_ref, h_out_ref, f_ref, s_ref):
    hs = hf_ref[...] + hb_ref[...]
    parts = []
    for hd in range(ML_HEADS):
        x = hs[:, hd * ML_V_DIM:(hd + 1) * ML_V_DIM]
        parts.append(x * lax.rsqrt(jnp.mean(x * x, axis=-1, keepdims=True) + ML_NORM_EPS))
    hn = jnp.concatenate(parts, axis=-1) * ng_ref[...] * jax.nn.sigmoid(o_ref[...])
    y = jnp.dot(hn.astype(BF16), wout_ref[...], preferred_element_type=F32)
    _mixer_tail(h_ref[...], y, gate_ref[...], lng_ref[...], lnb_ref[...], shift_ref[...], scale_ref[...],
                router_ref[...], h_out_ref, f_ref, s_ref)


def odd_out_pallas(h_f, h_b, o, h, mods, dims, norm_g, w_out, ln_g, ln_b, router_w):
    B, C, N = dims
    R, D = h.shape
    E = router_w.shape[1]
    tm = ROW_TILE
    mod_idx = _mod_index((C + N) // tm, C // tm, B)
    row = lambda width: pl.BlockSpec((tm, width), lambda i: (i, 0))
    vec = lambda z: z.reshape(1, -1)
    out_specs, out_shape = _tail_specs(R, D, E, tm)
    return pl.pallas_call(
        _odd_out_kernel,
        grid=(R // tm,),
        in_specs=[row(ML_WIDTH), row(ML_WIDTH), row(ML_WIDTH), row(D),
                  _mod_spec(2, mod_idx), _mod_spec(3, mod_idx), _mod_spec(4, mod_idx),
                  _full_spec((1, ML_WIDTH)), _full_spec((ML_WIDTH, D)),
                  _full_spec((1, D)), _full_spec((1, D)), _full_spec((D, E))],
        out_specs=out_specs,
        out_shape=out_shape,
        compiler_params=_cparams(1),
        name="odd_out",
    )(h_f, h_b, o, h, mods, mods, mods, vec(norm_g), w_out.astype(BF16), vec(ln_g), vec(ln_b), router_w)


def kernel(x, c, ctx, c_ctx, ada_w, ada_b, ln_g, ln_b, ev_w_in, ev_w_out, na_rpb, rw_mu, rw_w0, rw_w_up,
           rw_a0, rw_a_up, rw_g_up, rw_k_k, rw_k_a, rw_r_k, rw_gn_g, rw_gn_b, od_w_in, od_w_out, ml_gate_b,
           ml_norm_g, moe_router, moe_bias, moe_w_gate, moe_w_up, moe_w_down, sh_w_gate, sh_w_up, sh_w_down):
    B, N, D = x.shape
    C = ctx.shape[1]
    S = C + N
    dims = (B, C, N)
    assert C % ROW_TILE == 0 and N % ROW_TILE == 0 and B + 1 <= SUBLANES
    h = jnp.concatenate([ctx, x], axis=1).reshape(B * S, D)
    cond = jnp.zeros((SUBLANES, D), F32).at[:B].set(c).at[B].set(c_ctx)
    for l in range(DEPTH):
        mods = ada_mods_pallas(cond, ada_w[l], ada_b[l])
        if l % 2 == 0:
            e = l // 2
            (q, k, v, dec_f, dec_b, beta_f, beta_b, kd_f, kd_b, nkk, rv, rr, glow) = proj_even_pallas(
                h, mods, dims, ev_w_in[e], rw_mu[e], rw_w0[e], rw_w_up[e], rw_a0[e], rw_a_up[e],
                rw_k_k[e], rw_k_a[e])
            y_f, y_b = rwkv_scan_pallas(dec_f, beta_f, kd_f, dec_b, beta_b, kd_b, nkk, rv, rr, dims)
            na = attention_pallas(q, k, v, na_rpb[e], dims)
            h, f, s = even_out_pallas(na, y_f, y_b, rr, rv, kd_f, kd_b, glow, h, mods, dims, rw_g_up[e],
                                      rw_r_k[e], rw_gn_g[e], rw_gn_b[e], ev_w_out[e], ln_g[l, 0], ln_b[l, 0],
                                      moe_router[l])
        else:
            o = l // 2
            q, k, v, og, g, gt = proj_odd_pallas(h, mods, dims, od_w_in[o], ml_gate_b[o])
            h_f = mlstm_pallas(q, k, v, g, gt, dims, False)
            h_b = mlstm_pallas(q, k, v, g, gt, dims, True)
            h, f, s = odd_out_pallas(h_f, h_b, og, h, mods, dims, ml_norm_g[o], od_w_out[o], ln_g[l, 0],
                                     ln_b[l, 0], moe_router[l])
        h = moe_layer(f, s, h, mods, dims, ln_g[l, 1], ln_b[l, 1], moe_bias[l], l, moe_w_gate, moe_w_up,
                      moe_w_down, sh_w_gate[l], sh_w_up[l], sh_w_down[l])
    return h.reshape(B, S, D)[:, C:]
```

```python
import functools

import jax
import jax.numpy as jnp
import numpy as np
from jax import lax
from jax.experimental import pallas as pl
from jax.experimental.pallas import tpu as pltpu

D_MODEL = 1024
DEPTH = 2
GRID_W = 64

DEEPNORM_ALPHA = (2.0 * DEPTH) ** 0.25
LN_EPS = 1e-5
NEG_INF = -1e30
F32 = jnp.float32
BF16 = jnp.bfloat16

NA_HEAD_DIM = 64
NA_WIDTH = D_MODEL // 2
NA_HEADS = NA_WIDTH // NA_HEAD_DIM
NA_WIN_ROWS = 8
NA_WIN_COLS = 16
NA_SCALE = NA_HEAD_DIM ** -0.5

RW_HEAD_DIM = 64
RW_WIDTH = D_MODEL // 2
RW_DECAY_LORA = 32
RW_GN_EPS = 64e-5

ML_HEADS = 8
ML_V_DIM = D_MODEL // ML_HEADS
ML_QK_DIM = ML_V_DIM // 2
ML_WIDTH = ML_HEADS * ML_V_DIM
ML_CHUNK = 128
ML_NORM_EPS = 1e-6
ROPE_BASE = 10000.0

TOP_K = 8
N_GROUPS = 8
TOPK_GROUPS = 4
ROUTED_SCALE = 2.5
MOE_BLOCK = 512

SUBLANES = 8
LANES = 128
VMEM_LIMIT_BYTES = 56 * 1024 * 1024

ROW_TILE = 256
N_MODS = 6
RW_COLS = 3 * RW_WIDTH + 2 * LANES
NT_DIMS = (((1,), (1,)), ((), ()))


def _cparams(n_axes):
    return pltpu.CompilerParams(dimension_semantics=("arbitrary",) * n_axes, vmem_limit_bytes=VMEM_LIMIT_BYTES)


def _full_spec(shape):
    return pl.BlockSpec(shape, lambda *_: (0,) * len(shape))


def _split_bf16(x):
    hi = x.astype(BF16)
    lo = (x - hi.astype(F32)).astype(BF16)
    return jnp.concatenate([hi, lo], axis=-1)


def _block_ones(n_rows, n_cols, seg):
    row = lax.broadcasted_iota(jnp.int32, (n_rows, n_cols), 0)
    col = lax.broadcasted_iota(jnp.int32, (n_rows, n_cols), 1)
    return (((row % n_cols) // seg) == (col // seg)).astype(BF16)


def _seg_sum(x, ones2):
    return jnp.dot(_split_bf16(x), ones2, preferred_element_type=F32)


def _mod_index(tiles_per_batch, ctx_tiles, n_batch):
    def idx(i):
        return jnp.where(i % tiles_per_batch < ctx_tiles, n_batch, i // tiles_per_batch)
    return idx


def _mod_spec(chunk, mod_idx):
    return pl.BlockSpec((None, None, 1, D_MODEL), lambda i: (mod_idx(i), chunk, 0, 0))


def _ada_kernel(c_ref, w_ref, b_ref, o_ref):
    c = c_ref[...]
    x = (c * jax.nn.sigmoid(c)).astype(BF16)
    o_ref[...] = jnp.dot(x, w_ref[...].astype(BF16), preferred_element_type=F32) + b_ref[...]


def ada_mods_pallas(cond, w, b):
    n, D = cond.shape
    n_out = w.shape[1]
    tn = 512
    out = pl.pallas_call(
        _ada_kernel,
        grid=(n_out // tn,),
        in_specs=[_full_spec((n, D)), pl.BlockSpec((D, tn), lambda j: (0, j)), pl.BlockSpec((1, tn), lambda j: (0, j))],
        out_specs=pl.BlockSpec((n, tn), lambda j: (0, j)),
        out_shape=jax.ShapeDtypeStruct((n, n_out), F32),
        compiler_params=_cparams(1),
        name="ada_mods",
    )(cond, w, b.reshape(1, n_out))
    return out.reshape(n, N_MODS, 1, D)


def _softplus(x):
    return jnp.maximum(x, 0.0) + jnp.log(1.0 + jnp.exp(-jnp.abs(x)))


def _proj_even_kernel(h_ref, hp_ref, hn_ref, shift_ref, scale_ref, wna_ref, wrw_ref, mu_ref, ones_ref,
                      kk_ref, ka_ref, w0_ref, a0_ref, wup_ref, aup_ref,
                      q_ref, k_ref, v_ref, dec_f_ref, dec_b_ref, beta_f_ref, beta_b_ref, kd_f_ref, kd_b_ref,
                      nkk_ref, rv_ref, rr_ref, glow_ref, *, tiles_per_batch, ctx_tiles):
    i = pl.program_id(0)
    j = i % tiles_per_batch
    first = (j == 0) | (j == ctx_tiles)
    last = (j == ctx_tiles - 1) | (j == tiles_per_batch - 1)
    tm = h_ref.shape[0]
    gain = 1.0 + scale_ref[...]
    shift = shift_ref[...]
    a = h_ref[...] * gain + shift
    a_prev = jnp.where(first, 0.0, hp_ref[SUBLANES - 1:SUBLANES, :] * gain + shift)
    a_next = jnp.where(last, 0.0, hn_ref[0:1, :] * gain + shift)
    rid = lax.broadcasted_iota(jnp.int32, (tm, 1), 0)
    prev = jnp.where(rid == 0, a_prev, pltpu.roll(a, 1, 0))
    nxt = jnp.where(rid == tm - 1, a_next, pltpu.roll(a, tm - 1, 0))
    a16 = a.astype(BF16)
    nb16 = (0.5 * (prev + nxt)).astype(BF16)

    na = jnp.dot(a16, wna_ref[...], preferred_element_type=F32)
    q_ref[...] = (na[:, :NA_WIDTH] * NA_SCALE).astype(BF16)
    k_ref[...] = na[:, NA_WIDTH:2 * NA_WIDTH].astype(BF16)
    v_ref[...] = na[:, 2 * NA_WIDTH:].astype(BF16)

    pa = jnp.dot(a16, wrw_ref[...], preferred_element_type=F32)
    pn = jnp.dot(nb16, wrw_ref[...], preferred_element_type=F32)
    t = pa + mu_ref[...] * (pn - pa)
    r = t[:, :RW_WIDTH]
    k = t[:, RW_WIDTH:2 * RW_WIDTH]
    lora = t[:, 3 * RW_WIDTH:3 * RW_WIDTH + LANES]
    rr_ref[...] = r
    rv_ref[...] = t[:, 2 * RW_WIDTH:3 * RW_WIDTH]
    glow_ref[...] = t[:, 3 * RW_WIDTH + LANES:]

    kk = k * kk_ref[...]
    norm = jnp.sqrt(_seg_sum(kk * kk, ones_ref[...]))
    kk = kk / jnp.maximum(norm, 1e-12)
    nkk_ref[...] = -kk
    lora_t = jnp.tanh(lora).astype(BF16)
    lora16 = lora.astype(BF16)
    outs = ((dec_f_ref, beta_f_ref, kd_f_ref), (dec_b_ref, beta_b_ref, kd_b_ref))
    for d in range(2):
        w_log = -_softplus(-(w0_ref[d:d + 1, :] + jnp.dot(lora_t, wup_ref[d], preferred_element_type=F32))) - 0.5
        a_gate = jax.nn.sigmoid(a0_ref[d:d + 1, :] + jnp.dot(lora16, aup_ref[d], preferred_element_type=F32))
        outs[d][0][...] = jnp.exp(-jnp.exp(w_log))
        outs[d][1][...] = kk * a_gate
        outs[d][2][...] = k * (1.0 + (a_gate - 1.0) * ka_ref[...])


def proj_even_pallas(h, mods, dims, w_in, mu, w0, w_up, a0, a_up, k_k, k_a):
    B, C, N = dims
    R, D = h.shape
    tm = ROW_TILE
    tpb, ctx_tiles = (C + N) // tm, C // tm
    mod_idx = _mod_index(tpb, ctx_tiles, B)
    w_na = w_in[:, :3 * NA_WIDTH].astype(BF16)
    n_rw = w_in.shape[1] - 3 * NA_WIDTH
    w_rw = jnp.pad(w_in[:, 3 * NA_WIDTH:], ((0, 0), (0, RW_COLS - n_rw))).astype(BF16)
    mu_p = jnp.pad(mu, (0, RW_COLS - n_rw)).reshape(1, RW_COLS)
    ones2 = _block_ones(2 * RW_WIDTH, RW_WIDTH, RW_HEAD_DIM)
    lr = RW_DECAY_LORA

    def pad_up(m, first_row):
        out = jnp.zeros((2, LANES, RW_WIDTH), F32)
        for d in range(2):
            out = out.at[d, first_row + d * lr:first_row + (d + 1) * lr].set(m[d])
        return out.astype(BF16)

    row = lambda width: pl.BlockSpec((tm, width), lambda i: (i, 0))
    hb = tm // SUBLANES
    n_hb = R // SUBLANES
    wide = jax.ShapeDtypeStruct((R, RW_WIDTH), F32)
    half = jax.ShapeDtypeStruct((R, NA_WIDTH), BF16)
    return pl.pallas_call(
        functools.partial(_proj_even_kernel, tiles_per_batch=tpb, ctx_tiles=ctx_tiles),
        grid=(R // tm,),
        in_specs=[
            row(D),
            pl.BlockSpec((SUBLANES, D), lambda i: (jnp.maximum(i * hb - 1, 0), 0)),
            pl.BlockSpec((SUBLANES, D), lambda i: (jnp.minimum((i + 1) * hb, n_hb - 1), 0)),
            _mod_spec(0, mod_idx), _mod_spec(1, mod_idx),
            _full_spec((D, 3 * NA_WIDTH)), _full_spec((D, RW_COLS)), _full_spec((1, RW_COLS)),
            _full_spec((2 * RW_WIDTH, RW_WIDTH)),
            _full_spec((1, RW_WIDTH)), _full_spec((1, RW_WIDTH)),
            _full_spec((2, RW_WIDTH)), _full_spec((2, RW_WIDTH)),
            _full_spec((2, LANES, RW_WIDTH)), _full_spec((2, LANES, RW_WIDTH)),
        ],
        out_specs=[row(NA_WIDTH)] * 3 + [row(RW_WIDTH)] * 9 + [row(LANES)],
        out_shape=[half] * 3 + [wide] * 9 + [jax.ShapeDtypeStruct((R, LANES), F32)],
        compiler_params=_cparams(1),
        name="proj_even",
    )(h, h, h, mods, mods, w_na, w_rw, mu_p, ones2, k_k.reshape(1, -1), k_a.reshape(1, -1), w0, a0,
      pad_up(w_up, 0), pad_up(a_up, 2 * lr))


RW_SCAN_TIME = 256


def _rwkv_scan_kernel(wf_ref, bf_ref, kf_ref, nf_ref, vf_ref, rf_ref,
                      wb_ref, bb_ref, kb_ref, nb_ref, vb_ref, rb_ref, yf_ref, yb_ref, s_ref):
    @pl.when(pl.program_id(0) == 0)
    def _():
        s_ref[...] = jnp.zeros_like(s_ref)

    n_batch, n_time, width = wf_ref.shape
    n_pair = width // LANES
    n_dir_chain = n_batch * n_pair
    n_chain = 2 * n_dir_chain
    rows_all = n_chain * RW_HEAD_DIM
    ones = _block_ones(LANES, LANES, RW_HEAD_DIM)
    vi = lax.broadcasted_iota(jnp.int32, (1, RW_HEAD_DIM, LANES), 1)
    li = lax.broadcasted_iota(jnp.int32, (1, RW_HEAD_DIM, LANES), 2)
    diag = (li % RW_HEAD_DIM) == vi
    n_sub = n_time // SUBLANES

    def seg(x):
        out = jnp.dot(x.reshape(rows_all, LANES).astype(BF16), ones, preferred_element_type=F32)
        return out.reshape(n_chain, RW_HEAD_DIM, LANES)

    def chains(ref, rows):
        x = ref[:, rows, :]
        return [x[b, :, p * LANES:(p + 1) * LANES] for b in range(n_batch) for p in range(n_pair)]

    def sub(i, carry):
        rows_f = pl.ds(pl.multiple_of(i * SUBLANES, SUBLANES), SUBLANES)
        rows_b = pl.ds(pl.multiple_of((n_sub - 1 - i) * SUBLANES, SUBLANES), SUBLANES)
        load = lambda f_ref, b_ref: (jnp.stack(chains(f_ref, rows_f)), jnp.stack(chains(b_ref, rows_b)))
        w8, beta8, kd8 = load(wf_ref, wb_ref), load(bf_ref, bb_ref), load(kf_ref, kb_ref)
        nkk8, v8, r8 = load(nf_ref, nb_ref), load(vf_ref, vb_ref), load(rf_ref, rb_ref)

        def at(pair, t):
            tb = SUBLANES - 1 - t
            return jnp.concatenate([pair[0][:, t:t + 1, :], pair[1][:, tb:tb + 1, :]], axis=0)

        s = s_ref[...]
        rows = []
        value_col = lambda t: seg(jnp.where(diag, at(v8, t), 0.0))
        read_out = lambda sr: jnp.sum(jnp.where(diag, seg(sr), 0.0), axis=1, keepdims=True)
        vcol_next, pending = value_col(0), None
        for t in range(SUBLANES):
            sa = seg(s * at(nkk8, t))
            vcol = vcol_next
            if t + 1 < SUBLANES:
                vcol_next = value_col(t + 1)
            if pending is not None:
                rows.append(read_out(pending))
            s = s * at(w8, t) + sa * at(beta8, t) + vcol * at(kd8, t)
            pending = s * at(r8, t)
        rows.append(read_out(pending))
        s_ref[...] = s
        y_f = jnp.concatenate([row[:n_dir_chain] for row in rows], axis=1)
        y_b = jnp.concatenate([row[n_dir_chain:] for row in rows[::-1]], axis=1)
        for b in range(n_batch):
            for p in range(n_pair):
                c = b * n_pair + p
                yf_ref[b, rows_f, p * LANES:(p + 1) * LANES] = y_f[c]
                yb_ref[b, rows_b, p * LANES:(p + 1) * LANES] = y_b[c]
        return carry

    lax.fori_loop(0, n_sub, sub, 0)


def rwkv_scan_pallas(dec_f, beta_f, kd_f, dec_b, beta_b, kd_b, nkk, v, r, dims):
    B, C, N = dims
    S = C + N
    tc = RW_SCAN_TIME
    assert C % tc == 0 and N % tc == 0
    n_ctx, n_all = C // tc, S // tc
    as3 = lambda z: z.reshape(B, S, RW_WIDTH)
    fwd = pl.BlockSpec((B, tc, RW_WIDTH), lambda j: (0, j, 0))
    bwd = pl.BlockSpec((B, tc, RW_WIDTH),
                       lambda j: (0, jnp.where(j < n_ctx, n_ctx - 1 - j, n_all - 1 - (j - n_ctx)), 0))
    out = jax.ShapeDtypeStruct((B, S, RW_WIDTH), F32)
    y_f, y_b = pl.pallas_call(
        _rwkv_scan_kernel,
        grid=(n_all,),
        in_specs=[fwd] * 6 + [bwd] * 6,
        out_specs=[fwd, bwd],
        out_shape=[out, out],
        scratch_shapes=[pltpu.VMEM((2 * B * (RW_WIDTH // LANES), RW_HEAD_DIM, LANES), F32)],
        compiler_params=_cparams(1),
        name="rwkv_scan",
    )(as3(dec_f), as3(beta_f), as3(kd_f), as3(nkk), as3(v), as3(r),
      as3(dec_b), as3(beta_b), as3(kd_b), as3(nkk), as3(v), as3(r))
    return y_f.reshape(B * S, RW_WIDTH), y_b.reshape(B * S, RW_WIDTH)


NA_BAND = NA_WIN_ROWS * GRID_W


def _na_row_start(j, ctx_blocks, n_rows):
    r = jnp.maximum(j - ctx_blocks, 0)
    return r, jnp.clip(r - NA_WIN_ROWS // 2, 0, n_rows - NA_WIN_ROWS)


def _na_kernel(q_ref, k_ref, v_ref, bias_ref, o_ref, *, n_ctx):
    j = pl.program_id(1)
    ctx_blocks = n_ctx // GRID_W
    n_rows = pl.num_programs(1) - ctx_blocks
    _, row_start = _na_row_start(j, ctx_blocks, n_rows)
    start = pl.multiple_of(n_ctx + row_start * GRID_W, GRID_W)
    q = q_ref[0]
    kb = k_ref[0, pl.ds(start, NA_BAND), :]
    vb = v_ref[0, pl.ds(start, NA_BAND), :]
    kc = k_ref[0, pl.ds(0, n_ctx), :]
    vc = v_ref[0, pl.ds(0, n_ctx), :]
    head_of_lane = lax.broadcasted_iota(jnp.int32, (GRID_W, LANES), 1) // NA_HEAD_DIM
    heads = [(p, h2) for p in range(NA_WIDTH // LANES) for h2 in range(LANES // NA_HEAD_DIM)]
    cols = lambda p: slice(p * LANES, (p + 1) * LANES)
    scores = []
    for p, h2 in heads:
        qm = jnp.where(head_of_lane == h2, q[:, cols(p)], jnp.zeros((GRID_W, LANES), BF16))
        s_loc = lax.dot_general(qm, kb[:, cols(p)], NT_DIMS, preferred_element_type=F32)
        s_ctx = lax.dot_general(qm, kc[:, cols(p)], NT_DIMS, preferred_element_type=F32)
        scores.append((s_loc + bias_ref[0, 2 * p + h2], s_ctx))
    probs = []
    for s_loc, s_ctx in scores:
        m = jnp.maximum(jnp.max(s_loc, axis=-1, keepdims=True), jnp.max(s_ctx, axis=-1, keepdims=True))
        e_loc = jnp.exp(s_loc - m)
        e_ctx = jnp.exp(s_ctx - m)
        den = jnp.sum(e_loc, axis=-1, keepdims=True) + jnp.sum(e_ctx, axis=-1, keepdims=True)
        probs.append((e_loc.astype(BF16), e_ctx.astype(BF16), den))
    outs = []
    for (p, h2), (e_loc, e_ctx, den) in zip(heads, probs):
        o = (jnp.dot(e_loc, vb[:, cols(p)], preferred_element_type=F32)
             + jnp.dot(e_ctx, vc[:, cols(p)], preferred_element_type=F32))
        outs.append(o / den)
    for p in range(NA_WIDTH // LANES):
        o_ref[0, :, cols(p)] = jnp.where(head_of_lane == 0, outs[2 * p], outs[2 * p + 1])


def _na_bias_table(rpb):
    kw = NA_WIN_COLS
    n_col_off = 2 * kw - 1
    j = np.arange(GRID_W)
    col_start = np.clip(j - kw // 2, 0, GRID_W - kw)
    col_in = (j[None, :] >= col_start[:, None]) & (j[None, :] < col_start[:, None] + kw)
    col_off = np.clip(j[None, :] - j[:, None], -(kw - 1), kw - 1) + (kw - 1)
    pick = (col_off.reshape(1, -1) == np.arange(n_col_off)[:, None]).astype(np.float32)
    toep = jnp.dot(rpb.astype(F32).reshape(-1, n_col_off), pick, precision=lax.Precision.HIGHEST)
    toep = toep.reshape(NA_HEADS, 2 * NA_WIN_ROWS - 1, GRID_W, GRID_W)
    toep = jnp.where(col_in[None, None], toep, NEG_INF)
    tab = jnp.stack([toep[:, NA_WIN_ROWS - 1 - d:2 * NA_WIN_ROWS - 1 - d] for d in range(NA_WIN_ROWS)], 0)
    tab = tab.transpose(0, 1, 3, 2, 4).reshape(NA_WIN_ROWS, NA_HEADS, GRID_W, NA_BAND)
    return jnp.concatenate([tab, jnp.full((1,) + tab.shape[1:], NEG_INF, F32)], 0)


def attention_pallas(q, k, v, rpb, dims):
    B, C, N = dims
    S = C + N
    W = NA_WIDTH
    n_rows = N // GRID_W
    ctx_blocks = C // GRID_W
    assert n_rows >= NA_WIN_ROWS and N % GRID_W == 0 and C % GRID_W == 0
    bias = _na_bias_table(rpb)
    as3 = lambda z: z.reshape(B, S, W)

    def bias_idx(b, j):
        r, row_start = _na_row_start(j, ctx_blocks, n_rows)
        return (jnp.where(j < ctx_blocks, NA_WIN_ROWS, r - row_start), 0, 0, 0)

    out = pl.pallas_call(
        functools.partial(_na_kernel, n_ctx=C),
        grid=(B, S // GRID_W),
        in_specs=[
            pl.BlockSpec((1, GRID_W, W), lambda b, j: (b, j, 0)),
            pl.BlockSpec((1, S, W), lambda b, j: (b, 0, 0)),
            pl.BlockSpec((1, S, W), lambda b, j: (b, 0, 0)),
            pl.BlockSpec((1, NA_HEADS, GRID_W, NA_BAND), bias_idx),
        ],
        out_specs=pl.BlockSpec((1, GRID_W, W), lambda b, j: (b, j, 0)),
        out_shape=jax.ShapeDtypeStruct((B, S, W), F32),
        compiler_params=_cparams(2),
        name="na_attention",
    )(as3(q), as3(k), as3(v), bias)
    return out.reshape(B * S, W)


def _layer_norm(x, g, b):
    mu = jnp.mean(x, axis=-1, keepdims=True)
    xc = x - mu
    var = jnp.mean(xc * xc, axis=-1, keepdims=True)
    return xc * lax.rsqrt(var + LN_EPS) * g + b


def _mixer_tail(h, y, gate, ln_g, ln_b, shift, scale, router, h_out_ref, f_ref, s_ref):
    h1 = _layer_norm(DEEPNORM_ALPHA * h + gate * y, ln_g, ln_b)
    f = h1 * (1.0 + scale) + shift
    h_out_ref[...] = h1
    f_ref[...] = f
    s_ref[...] = jax.nn.sigmoid(jnp.dot(f, router, preferred_element_type=F32, precision=lax.Precision.HIGHEST))


def _even_out_kernel(na_ref, yf_ref, yb_ref, r_ref, v_ref, kdf_ref, kdb_ref, glow_ref, h_ref,
                     gate_ref, shift_ref, scale_ref, ones_ref, gng_ref, gnb_ref, rk_ref, gup_ref, wout_ref,
                     lng_ref, lnb_ref, router_ref, h_out_ref, f_ref, s_ref):
    ones2 = ones_ref[...]
    inv = 1.0 / RW_HEAD_DIM
    y = yf_ref[...] + yb_ref[...]
    mu = _seg_sum(y, ones2) * inv
    yc = y - mu
    var = _seg_sum(yc * yc, ones2) * inv
    yn = yc * lax.rsqrt(var + RW_GN_EPS) * gng_ref[...] + gnb_ref[...]
    r = r_ref[...]
    bonus = (_seg_sum(r * kdf_ref[...] * rk_ref[...], ones2) + _seg_sum(r * kdb_ref[...] * rk_ref[...], ones2))
    gate = jnp.dot(jax.nn.sigmoid(glow_ref[...]).astype(BF16), gup_ref[...], preferred_element_type=F32)
    rw = (yn + bonus * v_ref[...]) * gate
    mix = jnp.concatenate([na_ref[...], rw], axis=-1).astype(BF16)
    y_mix = jnp.dot(mix, wout_ref[...], preferred_element_type=F32)
    _mixer_tail(h_ref[...], y_mix, gate_ref[...], lng_ref[...], lnb_ref[...], shift_ref[...], scale_ref[...],
                router_ref[...], h_out_ref, f_ref, s_ref)


def _tail_specs(R, D, E, tm):
    row = lambda width: pl.BlockSpec((tm, width), lambda i: (i, 0))
    return ([row(D), row(D), row(E)],
            [jax.ShapeDtypeStruct((R, D), F32), jax.ShapeDtypeStruct((R, D), F32), jax.ShapeDtypeStruct((R, E), F32)])


def even_out_pallas(na, y_f, y_b, r, v, kd_f, kd_b, glow, h, mods, dims, g_up, r_k, gn_g, gn_b, w_out,
                    ln_g, ln_b, router_w):
    B, C, N = dims
    R, D = h.shape
    E = router_w.shape[1]
    tm = ROW_TILE
    mod_idx = _mod_index((C + N) // tm, C // tm, B)
    row = lambda width: pl.BlockSpec((tm, width), lambda i: (i, 0))
    ones2 = _block_ones(2 * RW_WIDTH, RW_WIDTH, RW_HEAD_DIM)
    g_up_p = jnp.pad(g_up, ((0, LANES - g_up.shape[0]), (0, 0))).astype(BF16)
    vec = lambda z: z.reshape(1, -1)
    out_specs, out_shape = _tail_specs(R, D, E, tm)
    return pl.pallas_call(
        _even_out_kernel,
        grid=(R // tm,),
        in_specs=[row(NA_WIDTH)] + [row(RW_WIDTH)] * 6 + [row(LANES), row(D),
                  _mod_spec(2, mod_idx), _mod_spec(3, mod_idx), _mod_spec(4, mod_idx),
                  _full_spec((2 * RW_WIDTH, RW_WIDTH)),
                  _full_spec((1, RW_WIDTH)), _full_spec((1, RW_WIDTH)), _full_spec((1, RW_WIDTH)),
                  _full_spec((LANES, RW_WIDTH)), _full_spec((D, D)),
                  _full_spec((1, D)), _full_spec((1, D)), _full_spec((D, E))],
        out_specs=out_specs,
        out_shape=out_shape,
        compiler_params=_cparams(1),
        name="even_out",
    )(na, y_f, y_b, r, v, kd_f, kd_b, glow, h, mods, mods, mods, ones2, vec(gn_g), vec(gn_b), vec(r_k),
      g_up_p, w_out.astype(BF16), vec(ln_g), vec(ln_b), router_w)


MOE_TOKEN_TILE = 256


def _swiglu_bf16(x, wg, wu, wd):
    g = jnp.dot(x, wg, preferred_element_type=F32)
    u = jnp.dot(x, wu, preferred_element_type=F32)
    mid = (g * jax.nn.sigmoid(g) * u).astype(BF16)
    return jnp.dot(mid, wd, preferred_element_type=F32)


def _row_copy(src_ref, src_row, dst_ref, dst_row, sem):
    return pltpu.make_async_copy(src_ref.at[pl.ds(src_row, 1), :], dst_ref.at[pl.ds(dst_row, 1), :], sem)


def _slot_kernel(e_ref, rank_ref, starts_ref, slot_ref):
    e = e_ref[...]
    tm = e.shape[0]
    lane = lax.broadcasted_iota(jnp.int32, (tm, starts_ref.shape[1]), 1)
    col8 = lax.broadcasted_iota(jnp.int32, (tm, TOP_K), 1)
    first = jnp.zeros((tm, TOP_K), F32)
    for k in range(TOP_K):
        hit = lane == e[:, k:k + 1]
        first = jnp.where(col8 == k, jnp.sum(jnp.where(hit, starts_ref[...], 0.0), axis=-1, keepdims=True), first)
    slot_ref[...] = first.astype(jnp.int32) + rank_ref[...]


def moe_slots_pallas(e_idx, rank, starts):
    T = e_idx.shape[0]
    E = starts.shape[0]
    tm = ROW_TILE
    row8 = pl.BlockSpec((tm, TOP_K), lambda i: (i, 0))
    return pl.pallas_call(
        _slot_kernel,
        grid=(T // tm,),
        in_specs=[row8, row8, _full_spec((1, E))],
        out_specs=row8,
        out_shape=jax.ShapeDtypeStruct((T, TOP_K), jnp.int32),
        compiler_params=_cparams(1),
        name="moe_slots",
    )(e_idx, rank, starts.astype(F32).reshape(1, E))


def _dispatch_kernel(slot_ref, f_ref, xs_ref, sem):
    n_tok = f_ref.shape[0]

    def issue(t, carry):
        for k in range(TOP_K):
            _row_copy(f_ref, t, xs_ref, slot_ref[t * TOP_K + k], sem).start(priority=k % 2)
        return carry

    lax.fori_loop(0, n_tok, issue, 0)

    def drain(t, carry):
        for k in range(TOP_K):
            _row_copy(f_ref, 0, xs_ref, 0, sem).wait()
        return carry

    lax.fori_loop(0, n_tok, drain, 0)


def _slot_specs(tm):
    return [pl.BlockSpec((tm * TOP_K,), lambda i: (i,), memory_space=pltpu.SMEM)]


def moe_dispatch_pallas(f, slot_flat):
    T, D = f.shape
    tm = MOE_TOKEN_TILE
    assert T % tm == 0
    return pl.pallas_call(
        _dispatch_kernel,
        grid=(T // tm,),
        in_specs=_slot_specs(tm) + [pl.BlockSpec((tm, D), lambda i: (i, 0))],
        out_specs=pl.BlockSpec(memory_space=pl.ANY),
        out_shape=jax.ShapeDtypeStruct((T * TOP_K, D), F32),
        scratch_shapes=[pltpu.SemaphoreType.DMA(())],
        compiler_params=_cparams(1),
        name="moe_dispatch",
    )(slot_flat, f)


def _expert_item_kernel(blk_ref, e_ref, lo_ref, hi_ref, first_ref, x_ref, wg_ref, wu_ref, wd_ref, o_ref,
                        wg16_ref, wu16_ref, wd16_ref):
    i = pl.program_id(0)
    lo, hi = lo_ref[i], hi_ref[i]

    @pl.when((i == 0) | (e_ref[i] != e_ref[jnp.maximum(i - 1, 0)]))
    def _():
        wg16_ref[...] = wg_ref[0, 0].astype(BF16)
        wu16_ref[...] = wu_ref[0, 0].astype(BF16)
        wd16_ref[...] = wd_ref[0, 0].astype(BF16)

    @pl.when(hi > lo)
    def _():
        y = _swiglu_bf16(x_ref[...].astype(BF16), wg16_ref[...], wu16_ref[...], wd16_ref[...])
        rows = blk_ref[i] * MOE_BLOCK + lax.broadcasted_iota(jnp.int32, (MOE_BLOCK, 1), 0)
        y = jnp.where((rows >= lo) & (rows < hi), y, 0.0)

        @pl.when(first_ref[i] == 1)
        def _():
            o_ref[...] = y

        @pl.when(first_ref[i] == 0)
        def _():
            o_ref[...] += y


def moe_experts_pallas(xs, items, layer, wg, wu, wd):
    n_rows, D = xs.shape
    F = wg.shape[-1]
    n_items = items[0].shape[0]
    grid_spec = pltpu.PrefetchScalarGridSpec(
        num_scalar_prefetch=5,
        grid=(n_items,),
        in_specs=[
            pl.BlockSpec((MOE_BLOCK, D), lambda i, blk, e, lo, hi, first: (blk[i], 0)),
            pl.BlockSpec((1, 1, D, F), lambda i, blk, e, lo, hi, first: (layer, e[i], 0, 0)),
            pl.BlockSpec((1, 1, D, F), lambda i, blk, e, lo, hi, first: (layer, e[i], 0, 0)),
            pl.BlockSpec((1, 1, F, D), lambda i, blk, e, lo, hi, first: (layer, e[i], 0, 0)),
        ],
        out_specs=pl.BlockSpec((MOE_BLOCK, D), lambda i, blk, e, lo, hi, first: (blk[i], 0)),
        scratch_shapes=[pltpu.VMEM((D, F), BF16), pltpu.VMEM((D, F), BF16), pltpu.VMEM((F, D), BF16)],
    )
    return pl.pallas_call(
        _expert_item_kernel,
        grid_spec=grid_spec,
        out_shape=jax.ShapeDtypeStruct((n_rows, D), F32),
        compiler_params=_cparams(1),
        name="moe_experts",
    )(*items, xs, wg, wu, wd)


def _combine_kernel(slot_ref, w_ref, f_ref, h_ref, gate_ref, lng_ref, lnb_ref,
                    sg_ref, su_ref, sd_ref, ys_ref, o_ref, buf_ref, sem):
    n_tok = f_ref.shape[0]

    def issue(t, carry):
        for k in range(TOP_K):
            pltpu.make_async_copy(ys_ref.at[pl.ds(slot_ref[t * TOP_K + k], 1), :],
                                  buf_ref.at[k, pl.ds(t, 1), :], sem).start(priority=k % 2)
        return carry

    lax.fori_loop(0, n_tok, issue, 0)
    acc = _swiglu_bf16(f_ref[...].astype(BF16), sg_ref[...], su_ref[...], sd_ref[...])

    def drain(t, carry):
        for k in range(TOP_K):
            pltpu.make_async_copy(ys_ref.at[pl.ds(0, 1), :], buf_ref.at[0, pl.ds(0, 1), :], sem).wait()
        return carry

    lax.fori_loop(0, n_tok, drain, 0)
    w = w_ref[...]
    for k in range(TOP_K):
        acc = acc + w[:, k:k + 1] * buf_ref[k]
    o_ref[...] = _layer_norm(DEEPNORM_ALPHA * h_ref[...] + gate_ref[...] * acc, lng_ref[...], lnb_ref[...])


def moe_combine_pallas(ys, slot_flat, w_sel, f, h, mods, dims, ln_g, ln_b, sg, su, sd):
    B, C, N = dims
    T, D = f.shape
    tm = MOE_TOKEN_TILE
    F = sg.shape[-1]
    mod_idx = _mod_index((C + N) // tm, C // tm, B)
    vec = lambda z: z.reshape(1, -1)
    return pl.pallas_call(
        _combine_kernel,
        grid=(T // tm,),
        in_specs=_slot_specs(tm) + [
            pl.BlockSpec((tm, TOP_K), lambda i: (i, 0)),
            pl.BlockSpec((tm, D), lambda i: (i, 0)),
            pl.BlockSpec((tm, D), lambda i: (i, 0)),
            _mod_spec(5, mod_idx), _full_spec((1, D)), _full_spec((1, D)),
            _full_spec((D, F)), _full_spec((D, F)), _full_spec((F, D)),
            pl.BlockSpec(memory_space=pl.ANY),
        ],
        out_specs=pl.BlockSpec((tm, D), lambda i: (i, 0)),
        out_shape=jax.ShapeDtypeStruct((T, D), F32),
        scratch_shapes=[pltpu.VMEM((TOP_K, tm, D), F32), pltpu.SemaphoreType.DMA(())],
        compiler_params=_cparams(1),
        name="moe_combine",
    )(slot_flat, w_sel, f, h, mods, vec(ln_g), vec(ln_b), sg, su, sd, ys)


REMOVED = -3e38


def _router_kernel(s_ref, bias_ref, e_ref, w_ref, rank_ref, cnt_ref, carry_ref):
    @pl.when(pl.program_id(0) == 0)
    def _():
        carry_ref[...] = jnp.zeros_like(carry_ref)

    s = s_ref[...]
    tm, n_exp = s.shape
    per_group = n_exp // N_GROUPS
    lane_i = lax.broadcasted_iota(jnp.int32, (tm, n_exp), 1)
    lane = lane_i.astype(F32)
    group_of_lane = lane_i // per_group
    big = float(n_exp)
    rmax = lambda z: jnp.max(z, axis=-1, keepdims=True)
    first_at = lambda z, m: jnp.min(jnp.where(z == m, lane, big), axis=-1, keepdims=True)

    grp = s + bias_ref[...]
    g_score = []
    for g in range(N_GROUPS):
        mg = jnp.where(group_of_lane == g, grp, REMOVED)
        m1 = rmax(mg)
        m2 = rmax(jnp.where(lane == first_at(mg, m1), REMOVED, mg))
        g_score.append(m1 + m2)
    choice = jnp.full_like(grp, NEG_INF)
    for g in range(N_GROUPS):
        ahead = jnp.zeros((tm, 1), F32)
        for g2 in range(N_GROUPS):
            if g2 != g:
                beats = (g_score[g2] > g_score[g]) | ((g_score[g2] == g_score[g]) & (g2 < g))
                ahead = ahead + beats.astype(F32)
        choice = jnp.where((group_of_lane == g) & (ahead < TOPK_GROUPS), grp, choice)

    col8 = lax.broadcasted_iota(jnp.int32, (tm, TOP_K), 1)
    e_out = jnp.zeros((tm, TOP_K), F32)
    w_out = jnp.zeros((tm, TOP_K), F32)
    picked = []
    onehot = jnp.zeros((tm, n_exp), F32)
    for k in range(TOP_K):
        idx = first_at(choice, rmax(choice))
        hit = lane == idx
        picked.append(hit)
        onehot = jnp.where(hit, 1.0, onehot)
        e_out = jnp.where(col8 == k, idx, e_out)
        w_out = jnp.where(col8 == k, jnp.sum(jnp.where(hit, s, 0.0), axis=-1, keepdims=True), w_out)
        choice = jnp.where(hit, REMOVED, choice)
    ri = lax.broadcasted_iota(jnp.int32, (tm, tm), 0)
    ci = lax.broadcasted_iota(jnp.int32, (tm, tm), 1)
    before = jnp.dot((ci < ri).astype(BF16), onehot.astype(BF16), preferred_element_type=F32) + carry_ref[0:1, :]
    rank = jnp.zeros((tm, TOP_K), F32)
    for k in range(TOP_K):
        rank = jnp.where(col8 == k, jnp.sum(jnp.where(picked[k], before, 0.0), axis=-1, keepdims=True), rank)
    total = carry_ref[0:1, :] + jnp.sum(onehot, axis=0, keepdims=True)
    carry_ref[...] = jnp.broadcast_to(total, carry_ref.shape)
    cnt_ref[...] = jnp.broadcast_to(total, cnt_ref.shape)
    e_ref[...] = e_out.astype(jnp.int32)
    w_ref[...] = w_out / jnp.sum(w_out, axis=-1, keepdims=True) * ROUTED_SCALE
    rank_ref[...] = rank.astype(jnp.int32)


def router_pallas(s, router_b):
    T, E = s.shape
    tm = ROW_TILE
    row8 = pl.BlockSpec((tm, TOP_K), lambda i: (i, 0))
    e_idx, w_sel, rank, cnt = pl.pallas_call(
        _router_kernel,
        grid=(T // tm,),
        in_specs=[pl.BlockSpec((tm, E), lambda i: (i, 0)), _full_spec((1, E))],
        out_specs=[row8, row8, row8, _full_spec((SUBLANES, E))],
        out_shape=[jax.ShapeDtypeStruct((T, TOP_K), jnp.int32), jax.ShapeDtypeStruct((T, TOP_K), F32),
                   jax.ShapeDtypeStruct((T, TOP_K), jnp.int32), jax.ShapeDtypeStruct((SUBLANES, E), F32)],
        scratch_shapes=[pltpu.VMEM((SUBLANES, E), F32)],
        compiler_params=_cparams(1),
        name="moe_router",
    )(s, router_b.astype(F32).reshape(1, E))
    return e_idx, w_sel, rank, cnt[0].astype(jnp.int32)


def moe_layer(f, s, h, mods, dims, ln_g, ln_b, router_b, layer, wg, wu, wd, sg, su, sd):
    T, D = f.shape
    E = s.shape[-1]
    e_idx, w_sel, rank, counts = router_pallas(s, router_b)
    n_asg = T * TOP_K
    assert n_asg % MOE_BLOCK == 0
    i32 = jnp.int32
    ends = jnp.cumsum(counts).astype(i32)
    starts = ends - counts
    slot_flat = moe_slots_pallas(e_idx, rank, starts).reshape(-1)
    nb = n_asg // MOE_BLOCK
    first_blk = starts // MOE_BLOCK
    nblk = jnp.where(counts > 0, (ends - 1) // MOE_BLOCK - first_blk + 1, 0)
    item_ends = jnp.cumsum(nblk).astype(i32)
    item_starts = item_ends - nblk
    n_items = nb + E
    it = jnp.arange(n_items, dtype=i32)
    real = it < item_ends[-1]
    e_of = jnp.sum((item_ends[None, :] <= jnp.where(real, it, item_ends[-1] - 1)[:, None]).astype(i32), axis=1)
    is_e = e_of[:, None] == jnp.arange(E, dtype=i32)[None, :]
    pick = lambda tab: jnp.sum(jnp.where(is_e, tab[None, :], 0), axis=1)
    blk = jnp.where(real, pick(first_blk) + it - pick(item_starts), nb - 1).astype(i32)
    lo = jnp.where(real, jnp.maximum(pick(starts), blk * MOE_BLOCK), 0).astype(i32)
    hi = jnp.where(real, jnp.minimum(pick(ends), (blk + 1) * MOE_BLOCK), 0).astype(i32)
    first = (real & (blk != jnp.concatenate([jnp.full((1,), -1, i32), blk[:-1]]))).astype(i32)
    xs = moe_dispatch_pallas(f, slot_flat)
    ys = moe_experts_pallas(xs, (blk, e_of, lo, hi, first), layer, wg, wu, wd)
    return moe_combine_pallas(ys, slot_flat, w_sel, f, h, mods, dims, ln_g, ln_b,
                              sg.astype(BF16), su.astype(BF16), sd.astype(BF16))


ML_QK_WIDTH = ML_HEADS * ML_QK_DIM
ROPE_GROUP = ML_QK_DIM // 4
GATE_IN, GATE_FORGET = 0, 2 * ML_HEADS


def _log_sigmoid(x):
    return -_softplus(-x)


def _proj_odd_kernel(h_ref, shift_ref, scale_ref, wqk_ref, wv_ref, wo_ref, wg_ref, wgt_ref, gb_ref, gbt_ref,
                     cos_ref, sin_ref, q_ref, k_ref, v_ref, o_ref, g_ref, gt_ref):
    a16 = (h_ref[...] * (1.0 + scale_ref[...]) + shift_ref[...]).astype(BF16)
    qk = jnp.dot(a16, wqk_ref[...], preferred_element_type=F32)
    lane = lax.broadcasted_iota(jnp.int32, (1, ML_QK_WIDTH), 1)
    first_of_pair = (lane % (2 * ROPE_GROUP)) < ROPE_GROUP
    cos, sin = cos_ref[...], sin_ref[...]

    def rope(z):
        partner = jnp.where(first_of_pair, pltpu.roll(z, ML_QK_WIDTH - ROPE_GROUP, 1), pltpu.roll(z, ROPE_GROUP, 1))
        return z * cos + partner * sin

    q_ref[...] = rope(qk[:, :ML_QK_WIDTH] * ML_QK_DIM ** -0.5).astype(BF16)
    k_ref[...] = rope(qk[:, ML_QK_WIDTH:]).astype(BF16)
    v_ref[...] = jnp.dot(a16, wv_ref[...], preferred_element_type=F32).astype(BF16)
    o_ref[...] = jnp.dot(a16, wo_ref[...], preferred_element_type=F32)
    g = jnp.dot(a16, wg_ref[...], preferred_element_type=F32) + gb_ref[...]
    gl = lax.broadcasted_iota(jnp.int32, g.shape, 1)
    g_ref[...] = jnp.where((gl >= GATE_FORGET) & (gl < 2 * GATE_FORGET), _log_sigmoid(g), g)
    gt = lax.dot_general(wgt_ref[...], a16, NT_DIMS, preferred_element_type=F32) + gbt_ref[...]
    gs = lax.broadcasted_iota(jnp.int32, gt.shape, 0)
    gt_ref[...] = jnp.where((gs >= GATE_FORGET) & (gs < 2 * GATE_FORGET), _log_sigmoid(gt), gt)


def _rope_tables(C, N):
    t = jnp.arange(N)
    pos = jnp.stack([(t // GRID_W).astype(F32), (t % GRID_W).astype(F32)], 0)
    lane = jnp.arange(ML_QK_DIM)
    inv = ROPE_BASE ** (-(lane % ROPE_GROUP).astype(F32) / ROPE_GROUP)
    ang = pos[lane // (2 * ROPE_GROUP)].T * inv[None, :]
    sign = jnp.where((lane % (2 * ROPE_GROUP)) < ROPE_GROUP, -1.0, 1.0)
    heads = lambda z: jnp.tile(z, (1, ML_HEADS))
    cos = jnp.concatenate([jnp.ones((C, ML_QK_WIDTH), F32), heads(jnp.cos(ang))], 0)
    sin = jnp.concatenate([jnp.zeros((C, ML_QK_WIDTH), F32), heads(jnp.sin(ang) * sign)], 0)
    return cos, sin


def proj_odd_pallas(h, mods, dims, w_in, gate_b):
    B, C, N = dims
    R, D = h.shape
    tm = ROW_TILE
    tpb = (C + N) // tm
    mod_idx = _mod_index(tpb, C // tm, B)
    o_qk, o_v, o_o = 2 * ML_QK_WIDTH, 2 * ML_QK_WIDTH + ML_WIDTH, 2 * ML_QK_WIDTH + 2 * ML_WIDTH
    n_gate = w_in.shape[1] - o_o
    w16 = w_in.astype(BF16)
    w_g = jnp.pad(w16[:, o_o:], ((0, 0), (0, LANES - n_gate)))
    gb = jnp.pad(gate_b.astype(F32).reshape(-1), (0, LANES - n_gate))
    cos, sin = _rope_tables(C, N)
    row = lambda width: pl.BlockSpec((tm, width), lambda i: (i, 0))
    seg = pl.BlockSpec((tm, ML_QK_WIDTH), lambda i: (i % tpb, 0))
    return pl.pallas_call(
        _proj_odd_kernel,
        grid=(R // tm,),
        in_specs=[row(D), _mod_spec(0, mod_idx), _mod_spec(1, mod_idx),
                  _full_spec((D, 2 * ML_QK_WIDTH)), _full_spec((D, ML_WIDTH)), _full_spec((D, ML_WIDTH)),
                  _full_spec((D, LANES)), _full_spec((LANES, D)), _full_spec((1, LANES)), _full_spec((LANES, 1)),
                  seg, seg],
        out_specs=[row(ML_QK_WIDTH), row(ML_QK_WIDTH), row(ML_WIDTH), row(ML_WIDTH), row(LANES),
                   pl.BlockSpec((LANES, tm), lambda i: (0, i))],
        out_shape=[jax.ShapeDtypeStruct((R, ML_QK_WIDTH), BF16), jax.ShapeDtypeStruct((R, ML_QK_WIDTH), BF16),
                   jax.ShapeDtypeStruct((R, ML_WIDTH), BF16), jax.ShapeDtypeStruct((R, ML_WIDTH), F32),
                   jax.ShapeDtypeStruct((R, LANES), F32), jax.ShapeDtypeStruct((LANES, R), F32)],
        compiler_params=_cparams(1),
        name="proj_odd",
    )(h, mods, mods, w16[:, :o_qk], w16[:, o_qk:o_v], w16[:, o_v:o_o], w_g, w_g.T, gb.reshape(1, LANES),
      gb.reshape(LANES, 1), cos, sin)


def _split3_bf16(x, axis):
    x1 = x.astype(BF16)
    r1 = x - x1.astype(F32)
    x2 = r1.astype(BF16)
    x3 = (r1 - x2.astype(F32)).astype(BF16)
    return jnp.concatenate([x1, x2, x3], axis=axis)


def _mlstm_kernel(q_ref, k_ref, v_ref, g_ref, gt_ref, h_ref, c_ref, n_ref, m_ref, *, reverse):
    @pl.when(pl.program_id(1) == 0)
    def _():
        c_ref[...] = jnp.zeros_like(c_ref)
        n_ref[...] = jnp.zeros_like(n_ref)
        m_ref[...] = jnp.zeros_like(m_ref)

    L = q_ref.shape[1]
    ti = lax.broadcasted_iota(jnp.int32, (L, L), 0)
    si = lax.broadcasted_iota(jnp.int32, (L, L), 1)
    seen = (si >= ti) if reverse else (si <= ti)
    g = g_ref[0]
    gt = gt_ref[...]
    b_cols3 = jnp.dot(seen.astype(BF16), _split3_bf16(g, 1), preferred_element_type=F32)
    b_cols = b_cols3[:, :LANES] + b_cols3[:, LANES:2 * LANES] + b_cols3[:, 2 * LANES:]
    b_rows3 = lax.dot_general(_split3_bf16(gt, 0), seen.astype(BF16), NT_DIMS, preferred_element_type=F32)
    b_rows = b_rows3[:LANES] + b_rows3[LANES:2 * LANES] + b_rows3[2 * LANES:]
    half = lax.broadcasted_iota(jnp.int32, (1, LANES), 1) // ML_QK_DIM
    row_half = lax.broadcasted_iota(jnp.int32, (LANES, 1), 0) // ML_QK_DIM
    d_off = ML_HEADS if reverse else 0
    tn = (((0,), (0,)), ((), ()))
    heads = [(hd // 2, hd % 2) for hd in range(ML_HEADS)]
    pair = lambda ref, p: ref[0, :, p * LANES:(p + 1) * LANES]
    value = lambda hd: v_ref[0, :, hd * ML_V_DIM:(hd + 1) * ML_V_DIM]

    decay = []
    for hd, (p, h2) in enumerate(heads):
        gi, gf = GATE_IN + d_off + hd, GATE_FORGET + d_off + hd
        ig_col, ig_row = g[:, gi:gi + 1], gt[gi:gi + 1, :]
        b_col, b_row = b_cols[:, gf:gf + 1], b_rows[gf:gf + 1, :]
        m0 = m_ref[p][:, h2 * ML_QK_DIM:h2 * ML_QK_DIM + 1]
        dlog = jnp.where(seen, b_col - b_row + ig_row, NEG_INF)
        inter = b_col + m0
        m_t = jnp.maximum(jnp.max(dlog, axis=-1, keepdims=True), inter)
        b_end = jnp.sum(g[:, gf:gf + 1], axis=0, keepdims=True)
        g_col = b_end - b_col + ig_col
        m_chunk = jnp.max(g_col, axis=0, keepdims=True)
        m_new = jnp.maximum(b_end + m0, m_chunk)
        decay.append(dict(dw=jnp.exp(dlog - m_t), iw=jnp.exp(inter - m_t), floor=jnp.exp(-m_t),
                          kw=jnp.exp(g_col - m_chunk), m_new=m_new,
                          fa=jnp.exp(b_end + m0 - m_new), fb=jnp.exp(m_chunk - m_new)))

    prods = []
    for p, h2 in heads:
        qm = jnp.where(half == h2, pair(q_ref, p), jnp.zeros((L, LANES), BF16))
        qk = lax.dot_general(qm, pair(k_ref, p), NT_DIMS, preferred_element_type=F32)
        qc = jnp.dot(qm, c_ref[p].astype(BF16), preferred_element_type=F32)
        qn = jnp.sum(qm.astype(F32) * n_ref[p], axis=-1, keepdims=True)
        prods.append((qk, qc, qn))

    for hd, ((qk, qc, qn), dc) in enumerate(zip(prods, decay)):
        sc = qk * dc['dw']
        num = jnp.dot(sc.astype(BF16), value(hd), preferred_element_type=F32) + dc['iw'] * qc
        den = jnp.sum(sc, axis=-1, keepdims=True) + dc['iw'] * qn
        h_ref[0, :, hd * ML_V_DIM:(hd + 1) * ML_V_DIM] = num / jnp.maximum(jnp.abs(den), dc['floor'])

    for p in range(ML_HEADS // 2):
        c_old, n_old, m_old = c_ref[p], n_ref[p], m_ref[p]
        c_new, n_new, m_new_pair = c_old, n_old, m_old
        for h2 in range(2):
            hd = 2 * p + h2
            dc = decay[hd]
            kw = jnp.where(half == h2, pair(k_ref, p), jnp.zeros((L, LANES), BF16)).astype(F32) * dc['kw']
            kv = lax.dot_general(kw.astype(BF16), value(hd), tn, preferred_element_type=F32)
            c_new = jnp.where(row_half == h2, dc['fa'] * c_old + dc['fb'] * kv, c_new)
            n_new = jnp.where(half == h2, dc['fa'] * n_old + dc['fb'] * jnp.sum(kw, axis=0, keepdims=True), n_new)
            m_new_pair = jnp.where(half == h2, dc['m_new'], m_new_pair)
        c_ref[p] = c_new
        n_ref[p] = n_new
        m_ref[p] = m_new_pair


def mlstm_pallas(q, k, v, g, gt, dims, reverse):
    B, C, N = dims
    S = C + N
    L = ML_CHUNK
    assert C % L == 0 and N % L == 0
    n_ctx, n_all = C // L, S // L
    if reverse:
        chunk = lambda j: jnp.where(j < n_ctx, n_ctx - 1 - j, n_all - 1 - (j - n_ctx))
    else:
        chunk = lambda j: j
    blk = lambda width: pl.BlockSpec((1, L, width), lambda b, j: (b, chunk(j), 0))
    n_pair = ML_HEADS // 2
    out = pl.pallas_call(
        functools.partial(_mlstm_kernel, reverse=reverse),
        grid=(B, n_all),
        in_specs=[blk(ML_QK_WIDTH), blk(ML_QK_WIDTH), blk(ML_WIDTH), blk(LANES),
                  pl.BlockSpec((LANES, L), lambda b, j: (0, b * n_all + chunk(j)))],
        out_specs=blk(ML_WIDTH),
        out_shape=jax.ShapeDtypeStruct((B, S, ML_WIDTH), F32),
        scratch_shapes=[pltpu.VMEM((n_pair, LANES, ML_V_DIM), F32), pltpu.VMEM((n_pair, 1, LANES), F32),
                        pltpu.VMEM((n_pair, 1, LANES), F32)],
        compiler_params=_cparams(2),
        name="mlstm_bwd" if reverse else "mlstm_fwd",
    )(q.reshape(B, S, -1), k.reshape(B, S, -1), v.reshape(B, S, -1), g.reshape(B, S, -1), gt)
    return out.reshape(B * S, ML_WIDTH)


def _odd_out_kernel(hf_ref, hb_ref, o_ref, h_ref, gate_ref, shift_ref, scale_ref, ng_ref, wout_ref,
                    lng_ref, lnb_ref, router_ref, h_out_ref, f_ref, s_ref):
    hs = hf_ref[...] + hb_ref[...]
    parts = []
    for hd in range(ML_HEADS):
        x = hs[:, hd * ML_V_DIM:(hd + 1) * ML_V_DIM]
        parts.append(x * lax.rsqrt(jnp.mean(x * x, axis=-1, keepdims=True) + ML_NORM_EPS))
    hn = jnp.concatenate(parts, axis=-1) * ng_ref[...] * jax.nn.sigmoid(o_ref[...])
    y = jnp.dot(hn.astype(BF16), wout_ref[...], preferred_element_type=F32)
    _mixer_tail(h_ref[...], y, gate_ref[...], lng_ref[...], lnb_ref[...], shift_ref[...], scale_ref[...],
                router_ref[...], h_out_ref, f_ref, s_ref)


def odd_out_pallas(h_f, h_b, o, h, mods, dims, norm_g, w_out, ln_g, ln_b, router_w):
    B, C, N = dims
    R, D = h.shape
    E = router_w.shape[1]
    tm = ROW_TILE
    mod_idx = _mod_index((C + N) // tm, C // tm, B)
    row = lambda width: pl.BlockSpec((tm, width), lambda i: (i, 0))
    vec = lambda z: z.reshape(1, -1)
    out_specs, out_shape = _tail_specs(R, D, E, tm)
    return pl.pallas_call(
        _odd_out_kernel,
        grid=(R // tm,),
        in_specs=[row(ML_WIDTH), row(ML_WIDTH), row(ML_WIDTH), row(D),
                  _mod_spec(2, mod_idx), _mod_spec(3, mod_idx), _mod_spec(4, mod_idx),
                  _full_spec((1, ML_WIDTH)), _full_spec((ML_WIDTH, D)),
                  _full_spec((1, D)), _full_spec((1, D)), _full_spec((D, E))],
        out_specs=out_specs,
        out_shape=out_shape,
        compiler_params=_cparams(1),
        name="odd_out",
    )(h_f, h_b, o, h, mods, mods, mods, vec(norm_g), w_out.astype(BF16), vec(ln_g), vec(ln_b), router_w)


def kernel(x, c, ctx, c_ctx, ada_w, ada_b, ln_g, ln_b, ev_w_in, ev_w_out, na_rpb, rw_mu, rw_w0, rw_w_up,
           rw_a0, rw_a_up, rw_g_up, rw_k_k, rw_k_a, rw_r_k, rw_gn_g, rw_gn_b, od_w_in, od_w_out, ml_gate_b,
           ml_norm_g, moe_router, moe_bias, moe_w_gate, moe_w_up, moe_w_down, sh_w_gate, sh_w_up, sh_w_down):
    B, N, D = x.shape
    C = ctx.shape[1]
    S = C + N
    dims = (B, C, N)
    assert C % ROW_TILE == 0 and N % ROW_TILE == 0 and B + 1 <= SUBLANES
    h = jnp.concatenate([ctx, x], axis=1).reshape(B * S, D)
    cond = jnp.zeros((SUBLANES, D), F32).at[:B].set(c).at[B].set(c_ctx)
    for l in range(DEPTH):
        mods = ada_mods_pallas(cond, ada_w[l], ada_b[l])
        if l % 2 == 0:
            e = l // 2
            (q, k, v, dec_f, dec_b, beta_f, beta_b, kd_f, kd_b, nkk, rv, rr, glow) = proj_even_pallas(
                h, mods, dims, ev_w_in[e], rw_mu[e], rw_w0[e], rw_w_up[e], rw_a0[e], rw_a_up[e],
                rw_k_k[e], rw_k_a[e])
            y_f, y_b = rwkv_scan_pallas(dec_f, beta_f, kd_f, dec_b, beta_b, kd_b, nkk, rv, rr, dims)
            na = attention_pallas(q, k, v, na_rpb[e], dims)
            h, f, s = even_out_pallas(na, y_f, y_b, rr, rv, kd_f, kd_b, glow, h, mods, dims, rw_g_up[e],
                                      rw_r_k[e], rw_gn_g[e], rw_gn_b[e], ev_w_out[e], ln_g[l, 0], ln_b[l, 0],
                                      moe_router[l])
        else:
            o = l // 2
            q, k, v, og, g, gt = proj_odd_pallas(h, mods, dims, od_w_in[o], ml_gate_b[o])
            h_f = mlstm_pallas(q, k, v, g, gt, dims, False)
            h_b = mlstm_pallas(q, k, v, g, gt, dims, True)
            h, f, s = odd_out_pallas(h_f, h_b, og, h, mods, dims, ml_norm_g[o], od_w_out[o], ln_g[l, 0],
                                     ln_b[l, 0], moe_router[l])
        h = moe_layer(f, s, h, mods, dims, ln_g[l, 1], ln_b[l, 1], moe_bias[l], l, moe_w_gate, moe_w_up,
                      moe_w_down, sh_w_gate[l], sh_w_up[l], sh_w_down[l])
    return h.reshape(B, S, D)[:, C:]
```
